```python
import jax, jax.numpy as jnp
from jax import lax
import numpy as np

D_MODEL = 2048
BATCH = 8
SEQ = 2048
DEPTH = 4

CHUNK = 64
Q_BLOCK = 128
N_MIXERS = 2
N_MLA_LAYERS = (DEPTH + 1) // 2
N_FOX_LAYERS = DEPTH // 2

MLA_HEADS = 16
MLA_Q_LORA = 512
MLA_KV_LORA = 512
MLA_NOPE_DIM = 128
MLA_ROPE_DIM = 64
MLA_V_DIM = 128
ROPE_THETA = 10000.0

FOX_HEADS = 16
FOX_HEAD_DIM = D_MODEL // FOX_HEADS

D_FF = -(-8 * D_MODEL // (3 * 256)) * 256

DEEPNORM_ALPHA = (2 * DEPTH) ** 0.25
DEEPNORM_BETA = (8 * DEPTH) ** -0.25
LN_EPS = 1e-5
RMS_EPS = 1e-6

kernel_name = "hybrid_mla_fox_deepnorm_adaln_trunk"


def _layer_norm(x, g, b):
    xf = x.astype(jnp.float32)
    mu = jnp.mean(xf, axis=-1, keepdims=True)
    var = jnp.mean(jnp.square(xf - mu), axis=-1, keepdims=True)
    y = (xf - mu) * lax.rsqrt(var + LN_EPS)
    return (y * g.astype(jnp.float32) + b.astype(jnp.float32)).astype(x.dtype)


def _rms_norm(x, g):
    xf = x.astype(jnp.float32)
    y = xf * lax.rsqrt(jnp.mean(jnp.square(xf), axis=-1, keepdims=True) + RMS_EPS)
    return (y * g.astype(jnp.float32)).astype(x.dtype)


def _rope_cos_sin(positions, dtype):
    inv_freq = ROPE_THETA ** (-jnp.arange(0, MLA_ROPE_DIM, 2, dtype=jnp.float32) / MLA_ROPE_DIM)
    ang = positions.astype(jnp.float32)[..., None] * inv_freq
    return jnp.cos(ang).astype(dtype), jnp.sin(ang).astype(dtype)


def _rope(x, cos, sin):
    x2 = x.reshape(*x.shape[:-1], MLA_ROPE_DIM // 2, 2)
    x0, x1 = x2[..., 0], x2[..., 1]
    out = jnp.stack([x0 * cos - x1 * sin, x0 * sin + x1 * cos], axis=-1)
    return out.reshape(x.shape)


def _mla(h, positions, w_down, q_norm, w_uq, kv_norm, w_uk, w_uv, w_o):
    B, S, _ = h.shape
    H = MLA_HEADS
    lat = h @ w_down
    q_lat, kv_lat, k_pe = jnp.split(lat, [MLA_Q_LORA, MLA_Q_LORA + MLA_KV_LORA], axis=-1)
    q = (_rms_norm(q_lat, q_norm) @ w_uq).reshape(B, S, H, MLA_NOPE_DIM + MLA_ROPE_DIM)
    q_nope, q_pe = q[..., :MLA_NOPE_DIM], q[..., MLA_NOPE_DIM:]
    cos, sin = _rope_cos_sin(positions, h.dtype)
    q_pe = _rope(q_pe, cos[:, :, None, :], sin[:, :, None, :])
    k_pe = _rope(k_pe, cos, sin)
    c_kv = _rms_norm(kv_lat, kv_norm)
    k_nope = (c_kv @ w_uk).reshape(B, S, H, MLA_NOPE_DIM)
    v = (c_kv @ w_uv).reshape(B, S, H, MLA_V_DIM)
    scale = (MLA_NOPE_DIM + MLA_ROPE_DIM) ** -0.5
    chunk_id = jnp.arange(S) // CHUNK
    outs = []
    for qs in range(0, S, Q_BLOCK):
        ke = qs + Q_BLOCK
        s = (jnp.einsum('bqhd,bkhd->bhqk', q_nope[:, qs:ke], k_nope[:, :ke])
             + jnp.einsum('bqhr,bkr->bhqk', q_pe[:, qs:ke], k_pe[:, :ke])).astype(jnp.float32) * scale
        mask = chunk_id[qs:ke, None] >= chunk_id[None, :ke]
        p = jax.nn.softmax(jnp.where(mask, s, -jnp.inf), axis=-1).astype(v.dtype)
        outs.append(jnp.einsum('bhqk,bkhd->bqhd', p, v[:, :ke]))
    o = jnp.concatenate(outs, axis=1).reshape(B, S, H * MLA_V_DIM)
    return o @ w_o


def _fox(h, w_in, b_f, w_o):
    B, S, D = h.shape
    H, Dh = FOX_HEADS, FOX_HEAD_DIM
    proj = h @ w_in
    q, k, v, f_logit = jnp.split(proj, [D, 2 * D, 3 * D], axis=-1)
    q = q.reshape(B, S, H, Dh)
    k = k.reshape(B, S, H, Dh)
    v = v.reshape(B, S, H, Dh)
    log_f = jax.nn.log_sigmoid((f_logit + b_f).astype(jnp.float32))
    cum = jnp.transpose(lax.cumsum(log_f, axis=1), (0, 2, 1))
    scale = Dh ** -0.5
    pos = jnp.arange(S)
    outs = []
    for qs in range(0, S, Q_BLOCK):
        ke = qs + Q_BLOCK
        s = jnp.einsum('bqhd,bkhd->bhqk', q[:, qs:ke], k[:, :ke]).astype(jnp.float32) * scale
        s = s + cum[:, :, qs:ke, None] - cum[:, :, None, :ke]
        mask = pos[qs:ke, None] >= pos[None, :ke]
        p = jax.nn.softmax(jnp.where(mask, s, -jnp.inf), axis=-1).astype(v.dtype)
        outs.append(jnp.einsum('bhqk,bkhd->bqhd', p, v[:, :ke]))
    o = jnp.concatenate(outs, axis=1).reshape(B, S, D)
    return o @ w_o


def _swiglu(h, w1, w3, w2):
    return (jax.nn.silu(h @ w1) * (h @ w3)) @ w2


def _dense(key, shape, fan_in, scale=1.0):
    return jax.random.normal(key, shape, jnp.float32) * (scale * fan_in ** -0.5)


def _fwd_setup_inputs(seed: int = 0) -> dict:
    key = jax.random.key(seed)
    ks = jax.random.split(key, 32)
    D, L, NA, NB = D_MODEL, DEPTH, N_MLA_LAYERS, N_FOX_LAYERS
    HA, HB = MLA_HEADS, FOX_HEADS
    x = jax.random.normal(ks[0], (BATCH, SEQ, D), jnp.float32)
    c = jax.random.normal(ks[1], (BATCH, D), jnp.float32)
    offset = jax.random.randint(ks[2], (BATCH, 1), 0, 16, dtype=jnp.int32) * CHUNK
    positions = offset + jnp.arange(SEQ, dtype=jnp.int32)[None, :]

    ada_w = _dense(ks[3], (L, D, 6 * D), D, 0.5)
    ada_b = 0.01 * jax.random.normal(ks[4], (L, 6 * D), jnp.float32)
    ln1_g = 1.0 + 0.02 * jax.random.normal(ks[5], (L, D), jnp.float32)
    ln1_b = 0.02 * jax.random.normal(ks[6], (L, D), jnp.float32)
    ln2_g = 1.0 + 0.02 * jax.random.normal(ks[7], (L, D), jnp.float32)
    ln2_b = 0.02 * jax.random.normal(ks[8], (L, D), jnp.float32)
    ffn_w1 = _dense(ks[9], (L, D, D_FF), D)
    ffn_w3 = _dense(ks[10], (L, D, D_FF), D)
    ffn_w2 = _dense(ks[11], (L, D_FF, D), D_FF, DEEPNORM_BETA)

    mla_w_down = _dense(ks[12], (NA, D, MLA_Q_LORA + MLA_KV_LORA + MLA_ROPE_DIM), D)
    mla_q_norm = 1.0 + 0.02 * jax.random.normal(ks[13], (NA, MLA_Q_LORA), jnp.float32)
    mla_w_uq = _dense(ks[14], (NA, MLA_Q_LORA, HA * (MLA_NOPE_DIM + MLA_ROPE_DIM)), MLA_Q_LORA)
    mla_kv_norm = 1.0 + 0.02 * jax.random.normal(ks[15], (NA, MLA_KV_LORA), jnp.float32)
    mla_w_uk = _dense(ks[16], (NA, MLA_KV_LORA, HA * MLA_NOPE_DIM), MLA_KV_LORA)
    mla_w_uv = _dense(ks[17], (NA, MLA_KV_LORA, HA * MLA_V_DIM), MLA_KV_LORA, DEEPNORM_BETA)
    mla_w_o = _dense(ks[18], (NA, HA * MLA_V_DIM, D), HA * MLA_V_DIM, DEEPNORM_BETA)

    fox_wq = _dense(ks[19], (NB, D, D), D)
    fox_wk = _dense(ks[20], (NB, D, D), D)
    fox_wv = _dense(ks[21], (NB, D, D), D, DEEPNORM_BETA)
    fox_wf = _dense(ks[22], (NB, D, HB), D)
    fox_w_in = jnp.concatenate([fox_wq, fox_wk, fox_wv, fox_wf], axis=-1)
    fox_b_f = jax.random.uniform(ks[23], (NB, HB), jnp.float32, 1.0, 4.0)
    fox_w_o = _dense(ks[24], (NB, D, D), D, DEEPNORM_BETA)

    return {
        "x": x, "c": c, "positions": positions,
        "ada_w": ada_w, "ada_b": ada_b,
        "ln1_g": ln1_g, "ln1_b": ln1_b, "ln2_g": ln2_g, "ln2_b": ln2_b,
        "ffn_w1": ffn_w1, "ffn_w3": ffn_w3, "ffn_w2": ffn_w2,
        "mla_w_down": mla_w_down, "mla_q_norm": mla_q_norm, "mla_w_uq": mla_w_uq,
        "mla_kv_norm": mla_kv_norm, "mla_w_uk": mla_w_uk, "mla_w_uv": mla_w_uv, "mla_w_o": mla_w_o,
        "fox_w_in": fox_w_in, "fox_b_f": fox_b_f, "fox_w_o": fox_w_o,
    }


def _fwd_reference(x, c, positions, ada_w, ada_b, ln1_g, ln1_b, ln2_g, ln2_b,
              ffn_w1, ffn_w3, ffn_w2,
              mla_w_down, mla_q_norm, mla_w_uq, mla_kv_norm, mla_w_uk, mla_w_uv, mla_w_o,
              fox_w_in, fox_b_f, fox_w_o):
    c_act = jax.nn.silu(c)
    for i in range(DEPTH):
        mod = c_act @ ada_w[i] + ada_b[i]
        sh_a, sc_a, g_a, sh_f, sc_f, g_f = [m[:, None, :] for m in jnp.split(mod, 6, axis=-1)]

        h = x * (1.0 + sc_a) + sh_a
        j = i // N_MIXERS
        if i % N_MIXERS == 0:
            y = _mla(h, positions, mla_w_down[j], mla_q_norm[j], mla_w_uq[j],
                     mla_kv_norm[j], mla_w_uk[j], mla_w_uv[j], mla_w_o[j])
        else:
            y = _fox(h, fox_w_in[j], fox_b_f[j], fox_w_o[j])
        x = _layer_norm(DEEPNORM_ALPHA * x + (1.0 + g_a) * y, ln1_g[i], ln1_b[i])

        h = x * (1.0 + sc_f) + sh_f
        y = _swiglu(h, ffn_w1[i], ffn_w3[i], ffn_w2[i])
        x = _layer_norm(DEEPNORM_ALPHA * x + (1.0 + g_f) * y, ln2_g[i], ln2_b[i])
    return x


import jax as _jax
import jax.numpy as _jnp

TWIN_FORMAT = 'train_step'
FWD_PARAMS = ['x', 'c', 'positions', 'ada_w', 'ada_b', 'ln1_g', 'ln1_b', 'ln2_g', 'ln2_b', 'ffn_w1', 'ffn_w3', 'ffn_w2', 'mla_w_down', 'mla_q_norm', 'mla_w_uq', 'mla_kv_norm', 'mla_w_uk', 'mla_w_uv', 'mla_w_o', 'fox_w_in', 'fox_b_f', 'fox_w_o']
TWIN_WEIGHTS = ['ada_w', 'ada_b', 'ln1_g', 'ln1_b', 'ln2_g', 'ln2_b', 'ffn_w1', 'ffn_w3', 'ffn_w2', 'mla_w_down', 'mla_q_norm', 'mla_w_uq', 'mla_kv_norm', 'mla_w_uk', 'mla_w_uv', 'mla_w_o', 'fox_w_in', 'fox_b_f', 'fox_w_o']
TWIN_DIFF_INPUT = 'x'
TWIN_INPUTS = ['x', 'c', 'positions', 'ada_w', 'ada_b', 'ln1_g', 'ln1_b', 'ln2_g', 'ln2_b', 'ffn_w1', 'ffn_w3', 'ffn_w2', 'mla_w_down', 'mla_q_norm', 'mla_w_uq', 'mla_kv_norm', 'mla_w_uk', 'mla_w_uv', 'mla_w_o', 'fox_w_in', 'fox_b_f', 'fox_w_o', 'loss_target', 'm_ada_w', 'm_ada_b', 'm_ln1_g', 'm_ln1_b', 'm_ln2_g', 'm_ln2_b', 'm_ffn_w1', 'm_ffn_w3', 'm_ffn_w2', 'm_mla_w_down', 'm_mla_q_norm', 'm_mla_w_uq', 'm_mla_kv_norm', 'm_mla_w_uk', 'm_mla_w_uv', 'm_mla_w_o', 'm_fox_w_in', 'm_fox_b_f', 'm_fox_w_o', 'v_ada_w', 'v_ada_b', 'v_ln1_g', 'v_ln1_b', 'v_ln2_g', 'v_ln2_b', 'v_ffn_w1', 'v_ffn_w3', 'v_ffn_w2', 'v_mla_w_down', 'v_mla_q_norm', 'v_mla_w_uq', 'v_mla_kv_norm', 'v_mla_w_uk', 'v_mla_w_uv', 'v_mla_w_o', 'v_fox_w_in', 'v_fox_b_f', 'v_fox_w_o']
TWIN_OUTPUTS = ['loss', 'grad_x', 'grad_ada_w', 'grad_ada_b', 'grad_ln1_g', 'grad_ln1_b', 'grad_ln2_g', 'grad_ln2_b', 'grad_ffn_w1', 'grad_ffn_w3', 'grad_ffn_w2', 'grad_mla_w_down', 'grad_mla_q_norm', 'grad_mla_w_uq', 'grad_mla_kv_norm', 'grad_mla_w_uk', 'grad_mla_w_uv', 'grad_mla_w_o', 'grad_fox_w_in', 'grad_fox_b_f', 'grad_fox_w_o', 'delta_ada_w', 'delta_ada_b', 'delta_ln1_g', 'delta_ln1_b', 'delta_ln2_g', 'delta_ln2_b', 'delta_ffn_w1', 'delta_ffn_w3', 'delta_ffn_w2', 'delta_mla_w_down', 'delta_mla_q_norm', 'delta_mla_w_uq', 'delta_mla_kv_norm', 'delta_mla_w_uk', 'delta_mla_w_uv', 'delta_mla_w_o', 'delta_fox_w_in', 'delta_fox_b_f', 'delta_fox_w_o', 'new_m_ada_w', 'new_m_ada_b', 'new_m_ln1_g', 'new_m_ln1_b', 'new_m_ln2_g', 'new_m_ln2_b', 'new_m_ffn_w1', 'new_m_ffn_w3', 'new_m_ffn_w2', 'new_m_mla_w_down', 'new_m_mla_q_norm', 'new_m_mla_w_uq', 'new_m_mla_kv_norm', 'new_m_mla_w_uk', 'new_m_mla_w_uv', 'new_m_mla_w_o', 'new_m_fox_w_in', 'new_m_fox_b_f', 'new_m_fox_w_o', 'new_v_ada_w', 'new_v_ada_b', 'new_v_ln1_g', 'new_v_ln1_b', 'new_v_ln2_g', 'new_v_ln2_b', 'new_v_ffn_w1', 'new_v_ffn_w3', 'new_v_ffn_w2', 'new_v_mla_w_down', 'new_v_mla_q_norm', 'new_v_mla_w_uq', 'new_v_mla_kv_norm', 'new_v_mla_w_uk', 'new_v_mla_w_uv', 'new_v_mla_w_o', 'new_v_fox_w_in', 'new_v_fox_b_f', 'new_v_fox_w_o']
TWIN_LEAF_KINDS = {'loss': 'loss', 'grad_x': 'grad_x', 'grad_ada_w': 'grad_w', 'grad_ada_b': 'grad_w', 'grad_ln1_g': 'grad_w', 'grad_ln1_b': 'grad_w', 'grad_ln2_g': 'grad_w', 'grad_ln2_b': 'grad_w', 'grad_ffn_w1': 'grad_w', 'grad_ffn_w3': 'grad_w', 'grad_ffn_w2': 'grad_w', 'grad_mla_w_down': 'grad_w', 'grad_mla_q_norm': 'grad_w', 'grad_mla_w_uq': 'grad_w', 'grad_mla_kv_norm': 'grad_w', 'grad_mla_w_uk': 'grad_w', 'grad_mla_w_uv': 'grad_w', 'grad_mla_w_o': 'grad_w', 'grad_fox_w_in': 'grad_w', 'grad_fox_b_f': 'grad_w', 'grad_fox_w_o': 'grad_w', 'delta_ada_w': 'delta_w', 'delta_ada_b': 'delta_w', 'delta_ln1_g': 'delta_w', 'delta_ln1_b': 'delta_w', 'delta_ln2_g': 'delta_w', 'delta_ln2_b': 'delta_w', 'delta_ffn_w1': 'delta_w', 'delta_ffn_w3': 'delta_w', 'delta_ffn_w2': 'delta_w', 'delta_mla_w_down': 'delta_w', 'delta_mla_q_norm': 'delta_w', 'delta_mla_w_uq': 'delta_w', 'delta_mla_kv_norm': 'delta_w', 'delta_mla_w_uk': 'delta_w', 'delta_mla_w_uv': 'delta_w', 'delta_mla_w_o': 'delta_w', 'delta_fox_w_in': 'delta_w', 'delta_fox_b_f': 'delta_w', 'delta_fox_w_o': 'delta_w', 'new_m_ada_w': 'new_m', 'new_m_ada_b': 'new_m', 'new_m_ln1_g': 'new_m', 'new_m_ln1_b': 'new_m', 'new_m_ln2_g': 'new_m', 'new_m_ln2_b': 'new_m', 'new_m_ffn_w1': 'new_m', 'new_m_ffn_w3': 'new_m', 'new_m_ffn_w2': 'new_m', 'new_m_mla_w_down': 'new_m', 'new_m_mla_q_norm': 'new_m', 'new_m_mla_w_uq': 'new_m', 'new_m_mla_kv_norm': 'new_m', 'new_m_mla_w_uk': 'new_m', 'new_m_mla_w_uv': 'new_m', 'new_m_mla_w_o': 'new_m', 'new_m_fox_w_in': 'new_m', 'new_m_fox_b_f': 'new_m', 'new_m_fox_w_o': 'new_m', 'new_v_ada_w': 'new_v', 'new_v_ada_b': 'new_v', 'new_v_ln1_g': 'new_v', 'new_v_ln1_b': 'new_v', 'new_v_ln2_g': 'new_v', 'new_v_ln2_b': 'new_v', 'new_v_ffn_w1': 'new_v', 'new_v_ffn_w3': 'new_v', 'new_v_ffn_w2': 'new_v', 'new_v_mla_w_down': 'new_v', 'new_v_mla_q_norm': 'new_v', 'new_v_mla_w_uq': 'new_v', 'new_v_mla_kv_norm': 'new_v', 'new_v_mla_w_uk': 'new_v', 'new_v_mla_w_uv': 'new_v', 'new_v_mla_w_o': 'new_v', 'new_v_fox_w_in': 'new_v', 'new_v_fox_b_f': 'new_v', 'new_v_fox_w_o': 'new_v'}


def _forward(args):
    return _fwd_reference(*[args[k] for k in FWD_PARAMS])


def _output_shape():
    out = _jax.eval_shape(lambda: _forward(_fwd_setup_inputs(0)))
    return out.shape, out.dtype

N_MICROBATCH = 1
ADAM_LR = 0.001
ADAM_B1 = 0.9
ADAM_B2 = 0.999
ADAM_EPS = 1e-08
ADAM_WD = 0.01
ADAM_STEP = 10
PER_EXAMPLE_BATCH_AXIS = {'x': 0, 'c': 0, 'positions': 0, 'loss_target': 0}
SHARED_INPUTS = []
_WEIGHT_DTYPES = {'ada_w': _jnp.float32, 'ada_b': _jnp.float32, 'ln1_g': _jnp.float32, 'ln1_b': _jnp.float32, 'ln2_g': _jnp.float32, 'ln2_b': _jnp.float32, 'ffn_w1': _jnp.float32, 'ffn_w3': _jnp.float32, 'ffn_w2': _jnp.float32, 'mla_w_down': _jnp.float32, 'mla_q_norm': _jnp.float32, 'mla_w_uq': _jnp.float32, 'mla_kv_norm': _jnp.float32, 'mla_w_uk': _jnp.float32, 'mla_w_uv': _jnp.float32, 'mla_w_o': _jnp.float32, 'fox_w_in': _jnp.float32, 'fox_b_f': _jnp.float32, 'fox_w_o': _jnp.float32}
MOMENT_SCALE = {'ada_w': 6.839471e-03, 'ada_b': 1.238720e-02, 'ln1_g': 2.639517e-01, 'ln1_b': 1.396238e-01, 'ln2_g': 4.024060e+00, 'ln2_b': 2.859903e-01, 'ffn_w1': 7.360037e-03, 'ffn_w3': 7.164444e-03, 'ffn_w2': 2.820388e-02, 'mla_w_down': 3.788804e-03, 'mla_q_norm': 1.904201e-03, 'mla_w_uq': 8.003651e-04, 'mla_kv_norm': 5.734781e-03, 'mla_w_uk': 7.870546e-04, 'mla_w_uv': 6.670029e-03, 'mla_w_o': 6.683670e-03, 'fox_w_in': 5.749420e-03, 'fox_b_f': 1.568333e-02, 'fox_w_o': 9.485568e-03}


def _to_microbatches(a, axis):
    t = _jnp.moveaxis(a, axis, 0)
    t = t.reshape((N_MICROBATCH, t.shape[0] // N_MICROBATCH) + t.shape[1:])
    return _jnp.moveaxis(t, 1, axis + 1)


def setup_inputs(seed: int = 0) -> dict:
    inp = _fwd_setup_inputs(seed)
    key = _jax.random.fold_in(_jax.random.key(seed), 7919)
    shape, _ = _output_shape()
    out = dict(inp)
    out["loss_target"] = _jax.random.normal(_jax.random.fold_in(key, 0), shape, _jnp.float32)
    for i, name in enumerate(TWIN_WEIGHTS):
        w = inp[name].astype(_jnp.float32)
        if MOMENT_SCALE is None:
            s = _jnp.sqrt(_jnp.mean(_jnp.square(w)) + 1e-30)
        else:
            s = MOMENT_SCALE[name]
        km, kv = _jax.random.split(_jax.random.fold_in(key, i + 1))
        out[name] = w
        out["m_" + name] = s * _jax.random.normal(km, w.shape, _jnp.float32)
        out["v_" + name] = (s * s) * _jax.random.uniform(kv, w.shape, _jnp.float32, 0.5, 1.5)
    if N_MICROBATCH > 1:
        for name, axis in PER_EXAMPLE_BATCH_AXIS.items():
            out[name] = _to_microbatches(out[name], axis)
    return {'x': out['x'], 'c': out['c'], 'positions': out['positions'], 'ada_w': out['ada_w'], 'ada_b': out['ada_b'], 'ln1_g': out['ln1_g'], 'ln1_b': out['ln1_b'], 'ln2_g': out['ln2_g'], 'ln2_b': out['ln2_b'], 'ffn_w1': out['ffn_w1'], 'ffn_w3': out['ffn_w3'], 'ffn_w2': out['ffn_w2'], 'mla_w_down': out['mla_w_down'], 'mla_q_norm': out['mla_q_norm'], 'mla_w_uq': out['mla_w_uq'], 'mla_kv_norm': out['mla_kv_norm'], 'mla_w_uk': out['mla_w_uk'], 'mla_w_uv': out['mla_w_uv'], 'mla_w_o': out['mla_w_o'], 'fox_w_in': out['fox_w_in'], 'fox_b_f': out['fox_b_f'], 'fox_w_o': out['fox_w_o'], 'loss_target': out['loss_target'], 'm_ada_w': out['m_ada_w'], 'm_ada_b': out['m_ada_b'], 'm_ln1_g': out['m_ln1_g'], 'm_ln1_b': out['m_ln1_b'], 'm_ln2_g': out['m_ln2_g'], 'm_ln2_b': out['m_ln2_b'], 'm_ffn_w1': out['m_ffn_w1'], 'm_ffn_w3': out['m_ffn_w3'], 'm_ffn_w2': out['m_ffn_w2'], 'm_mla_w_down': out['m_mla_w_down'], 'm_mla_q_norm': out['m_mla_q_norm'], 'm_mla_w_uq': out['m_mla_w_uq'], 'm_mla_kv_norm': out['m_mla_kv_norm'], 'm_mla_w_uk': out['m_mla_w_uk'], 'm_mla_w_uv': out['m_mla_w_uv'], 'm_mla_w_o': out['m_mla_w_o'], 'm_fox_w_in': out['m_fox_w_in'], 'm_fox_b_f': out['m_fox_b_f'], 'm_fox_w_o': out['m_fox_w_o'], 'v_ada_w': out['v_ada_w'], 'v_ada_b': out['v_ada_b'], 'v_ln1_g': out['v_ln1_g'], 'v_ln1_b': out['v_ln1_b'], 'v_ln2_g': out['v_ln2_g'], 'v_ln2_b': out['v_ln2_b'], 'v_ffn_w1': out['v_ffn_w1'], 'v_ffn_w3': out['v_ffn_w3'], 'v_ffn_w2': out['v_ffn_w2'], 'v_mla_w_down': out['v_mla_w_down'], 'v_mla_q_norm': out['v_mla_q_norm'], 'v_mla_w_uq': out['v_mla_w_uq'], 'v_mla_kv_norm': out['v_mla_kv_norm'], 'v_mla_w_uk': out['v_mla_w_uk'], 'v_mla_w_uv': out['v_mla_w_uv'], 'v_mla_w_o': out['v_mla_w_o'], 'v_fox_w_in': out['v_fox_w_in'], 'v_fox_b_f': out['v_fox_b_f'], 'v_fox_w_o': out['v_fox_w_o']}


def _loss(weights, diff, rest, loss_target):
    with _jax.named_scope("forward"):
        args = {**rest, TWIN_DIFF_INPUT: diff, **{k: w.astype(_WEIGHT_DTYPES[k]) for k, w in weights.items()}}
        y = _forward(args)
    with _jax.named_scope("loss_head"):
        err = _jnp.square(y.astype(_jnp.float32) - loss_target)
        return 0.5 * _jnp.sum(_jnp.mean(err, axis=-1)) if err.ndim else 0.5 * err


def _adamw(w, g, m, v):
    m = ADAM_B1 * m + (1.0 - ADAM_B1) * g
    v = ADAM_B2 * v + (1.0 - ADAM_B2) * _jnp.square(g)
    m_hat = m / (1.0 - ADAM_B1 ** ADAM_STEP)
    v_hat = v / (1.0 - ADAM_B2 ** ADAM_STEP)
    delta = -ADAM_LR * (m_hat / (_jnp.sqrt(v_hat) + ADAM_EPS) + ADAM_WD * w)
    return delta, m, v


def reference(x, c, positions, ada_w, ada_b, ln1_g, ln1_b, ln2_g, ln2_b, ffn_w1, ffn_w3, ffn_w2, mla_w_down, mla_q_norm, mla_w_uq, mla_kv_norm, mla_w_uk, mla_w_uv, mla_w_o, fox_w_in, fox_b_f, fox_w_o, loss_target, m_ada_w, m_ada_b, m_ln1_g, m_ln1_b, m_ln2_g, m_ln2_b, m_ffn_w1, m_ffn_w3, m_ffn_w2, m_mla_w_down, m_mla_q_norm, m_mla_w_uq, m_mla_kv_norm, m_mla_w_uk, m_mla_w_uv, m_mla_w_o, m_fox_w_in, m_fox_b_f, m_fox_w_o, v_ada_w, v_ada_b, v_ln1_g, v_ln1_b, v_ln2_g, v_ln2_b, v_ffn_w1, v_ffn_w3, v_ffn_w2, v_mla_w_down, v_mla_q_norm, v_mla_w_uq, v_mla_kv_norm, v_mla_w_uk, v_mla_w_uv, v_mla_w_o, v_fox_w_in, v_fox_b_f, v_fox_w_o):
    given = dict(x=x, c=c, positions=positions, ada_w=ada_w, ada_b=ada_b, ln1_g=ln1_g, ln1_b=ln1_b, ln2_g=ln2_g, ln2_b=ln2_b, ffn_w1=ffn_w1, ffn_w3=ffn_w3, ffn_w2=ffn_w2, mla_w_down=mla_w_down, mla_q_norm=mla_q_norm, mla_w_uq=mla_w_uq, mla_kv_norm=mla_kv_norm, mla_w_uk=mla_w_uk, mla_w_uv=mla_w_uv, mla_w_o=mla_w_o, fox_w_in=fox_w_in, fox_b_f=fox_b_f, fox_w_o=fox_w_o, loss_target=loss_target, m_ada_w=m_ada_w, m_ada_b=m_ada_b, m_ln1_g=m_ln1_g, m_ln1_b=m_ln1_b, m_ln2_g=m_ln2_g, m_ln2_b=m_ln2_b, m_ffn_w1=m_ffn_w1, m_ffn_w3=m_ffn_w3, m_ffn_w2=m_ffn_w2, m_mla_w_down=m_mla_w_down, m_mla_q_norm=m_mla_q_norm, m_mla_w_uq=m_mla_w_uq, m_mla_kv_norm=m_mla_kv_norm, m_mla_w_uk=m_mla_w_uk, m_mla_w_uv=m_mla_w_uv, m_mla_w_o=m_mla_w_o, m_fox_w_in=m_fox_w_in, m_fox_b_f=m_fox_b_f, m_fox_w_o=m_fox_w_o, v_ada_w=v_ada_w, v_ada_b=v_ada_b, v_ln1_g=v_ln1_g, v_ln1_b=v_ln1_b, v_ln2_g=v_ln2_g, v_ln2_b=v_ln2_b, v_ffn_w1=v_ffn_w1, v_ffn_w3=v_ffn_w3, v_ffn_w2=v_ffn_w2, v_mla_w_down=v_mla_w_down, v_mla_q_norm=v_mla_q_norm, v_mla_w_uq=v_mla_w_uq, v_mla_kv_norm=v_mla_kv_norm, v_mla_w_uk=v_mla_w_uk, v_mla_w_uv=v_mla_w_uv, v_mla_w_o=v_mla_w_o, v_fox_w_in=v_fox_w_in, v_fox_b_f=v_fox_b_f, v_fox_w_o=v_fox_w_o)
    weights = {n: given[n] for n in TWIN_WEIGHTS}
    shared = {n: given[n] for n in SHARED_INPUTS}
    per_example = {n: given[n] for n in ['x', 'c', 'positions']}
    grad_fn = _jax.value_and_grad(_loss, argnums=(0, 1))

    def one_microbatch(ex, loss_target):
        ex = dict(ex)
        diff = ex.pop(TWIN_DIFF_INPUT)
        return grad_fn(weights, diff, {**shared, **ex}, loss_target)

    if N_MICROBATCH == 1:
        loss, (grad_w, grad_x) = one_microbatch(per_example, given["loss_target"])
    else:
        def body(carry, xs):
            loss_sum, grad_sum = carry
            l_k, (gw_k, gx_k) = one_microbatch(xs[0], xs[1])
            with _jax.named_scope("update"):
                return (loss_sum + l_k, _jax.tree.map(_jnp.add, grad_sum, gw_k)), gx_k

        init = (_jnp.zeros((), _jnp.float32), _jax.tree.map(_jnp.zeros_like, weights))
        (loss, grad_w), grad_x = _jax.lax.scan(body, init, (per_example, given["loss_target"]))
    with _jax.named_scope("update"):
        delta_w, new_m, new_v = {}, {}, {}
        for n in TWIN_WEIGHTS:
            delta_w[n], new_m[n], new_v[n] = _adamw(weights[n], grad_w[n], given["m_" + n], given["v_" + n])
    return (loss, grad_x, *[grad_w[n] for n in TWIN_WEIGHTS], *[delta_w[n] for n in TWIN_WEIGHTS],
            *[new_m[n] for n in TWIN_WEIGHTS], *[new_v[n] for n in TWIN_WEIGHTS])
```

```python
import functools

import numpy as np
import jax
import jax.numpy as jnp
from jax import lax
from jax.experimental import pallas as pl
from jax.experimental.pallas import tpu as pltpu

F32 = jnp.float32
BF16 = jnp.bfloat16
MXU_DTYPE = jnp.bfloat16
WIRE_DTYPE = jnp.bfloat16

HEAD_DIM = 128
ROPE_DIM = 64
CHUNK = 64
ROPE_THETA = 10000.0
LN_EPS = 1e-5
RMS_EPS = 1e-6
ADAM_LR, ADAM_B1, ADAM_B2, ADAM_EPS, ADAM_WD, ADAM_STEP = 0.001, 0.9, 0.999, 1e-08, 0.01, 10

N_CHIPS = 4
N_DEV = 8
LANE = 128
VMEM_LIMIT = 56 * 1024 * 1024
MESH = pl.DeviceIdType.MESH
ANY = pl.BlockSpec(memory_space=pl.ANY)
NEG = -1e30


def _params(**kw):
    return pltpu.CompilerParams(vmem_limit_bytes=VMEM_LIMIT, **kw)


def _pick(n, cands):
    for c in cands:
        if n % c == 0:
            return c
    return n


def _sigmoid(x):
    return 1.0 / (1.0 + jnp.exp(-x))


def _mm(name, terms, M, N, out_dtypes, epilogue=None, extras=(), row_extras=(), tm=512, tn=512):
    tm = _pick(M, (tm, 256, 128))
    tn = _pick(N, (tn, 896, 768, 640, 384, 256, 128))
    extras = tuple(extras) + tuple(row_extras)
    n_row = len(row_extras)
    n_terms, n_ex, n_out = len(terms), len(extras), len(out_dtypes)
    n_acc = 1 + max(t[4] for t in terms)
    flags = [(t[2], t[3], t[4]) for t in terms]

    def body(*refs):
        accs = [None] * n_acc
        for k, (ta, tb, ai) in enumerate(flags):
            a = refs[2 * k][...].astype(MXU_DTYPE)
            b = refs[2 * k + 1][...].astype(MXU_DTYPE)
            dn = (((0 if ta else 1,), (1 if tb else 0,)), ((), ()))
            r = lax.dot_general(a, b, dn, preferred_element_type=F32)
            accs[ai] = r if accs[ai] is None else accs[ai] + r
        ex = [refs[2 * n_terms + k][...] for k in range(n_ex)]
        outs = epilogue(accs, *ex) if epilogue is not None else (accs[0],)
        for k in range(n_out):
            o_ref = refs[2 * n_terms + n_ex + k]
            o_ref[...] = outs[k].astype(o_ref.dtype)

    in_specs, args = [], []
    for (a, b, ta, tb, _, bcol) in terms:
        K = a.shape[0] if ta else a.shape[1]
        in_specs.append(pl.BlockSpec((K, tm), lambda i, j: (0, i)) if ta
                        else pl.BlockSpec((tm, K), lambda i, j: (i, 0)))
        in_specs.append(pl.BlockSpec((tn, K), lambda i, j, o=bcol: (j + o, 0)) if tb
                        else pl.BlockSpec((K, tn), lambda i, j, o=bcol: (0, j + o)))
        args += [a, b]
    for k, e in enumerate(extras):
        in_specs.append(pl.BlockSpec((tm, tn), (lambda i, j: (i, 0)) if k >= n_ex - n_row else (lambda i, j: (i, j))))
        args.append(e)
    outs = pl.pallas_call(
        body, name=name, grid=(M // tm, N // tn), in_specs=in_specs,
        out_specs=[pl.BlockSpec((tm, tn), lambda i, j: (i, j)) for _ in out_dtypes],
        out_shape=[jax.ShapeDtypeStruct((M, N), d) for d in out_dtypes],
        compiler_params=_params(),
    )(*args)
    return outs


def _mm1(name, a, b, ta=False, tb=False, out_dtype=F32, bcol=0, N=None, **kw):
    M = a.shape[1] if ta else a.shape[0]
    if N is None:
        N = b.shape[0] if tb else b.shape[1]
    return _mm(name, [(a, b, ta, tb, 0, bcol)], M, N, [out_dtype], **kw)[0]


def _rowwise(name, fn, tiled, vecs, outs, reds=(), tr=128):
    R = tiled[0].shape[0]
    tr = _pick(R, (tr, 64, 32, 16, 8))
    nt, nv, no, nr = len(tiled), len(vecs), len(outs), len(reds)

    def body(*refs):
        vals = [r[...] for r in refs[:nt + nv]]
        res = fn(*vals)
        for k in range(no):
            o_ref = refs[nt + nv + k]
            o_ref[...] = res[k].astype(o_ref.dtype)
        if nr:
            first = pl.program_id(0) == 0
            for k in range(nr):
                r_ref = refs[nt + nv + no + k]

                @pl.when(first)
                def _(r_ref=r_ref, v=res[no + k]):
                    r_ref[...] = v

                @pl.when(jnp.logical_not(first))
                def _(r_ref=r_ref, v=res[no + k]):
                    r_ref[...] += v

    in_specs = [pl.BlockSpec((tr, t.shape[1]), lambda i: (i, 0)) for t in tiled]
    in_specs += [pl.BlockSpec(v.shape, lambda i, n=v.ndim: (0,) * n) for v in vecs]
    out_specs = [pl.BlockSpec((tr, w), lambda i: (i, 0)) for (w, _) in outs]
    out_specs += [pl.BlockSpec((1, w), lambda i: (0, 0)) for w in reds]
    out_shape = [jax.ShapeDtypeStruct((R, w), d) for (w, d) in outs]
    out_shape += [jax.ShapeDtypeStruct((1, w), F32) for w in reds]
    return pl.pallas_call(
        body, name=name, grid=(R // tr,), in_specs=in_specs, out_specs=out_specs, out_shape=out_shape,
        compiler_params=_params(),
    )(*tiled, *vecs)


def _colsum(v):
    return jnp.sum(v, axis=0, keepdims=True)


def _ln_stats(z):
    mu = jnp.mean(z, axis=-1, keepdims=True)
    zc = z - mu
    var = jnp.mean(zc * zc, axis=-1, keepdims=True)
    rstd = lax.rsqrt(var + LN_EPS)
    return zc * rstd, rstd


def _ln_bwd(dout, xhat, rstd, lg):
    dxh = dout * lg
    m1 = jnp.mean(dxh, axis=-1, keepdims=True)
    m2 = jnp.mean(dxh * xhat, axis=-1, keepdims=True)
    return rstd * (dxh - m1 - xhat * m2)


def _modulate(x, sc, sh):
    D = x.shape[1]
    return _rowwise("modulate", lambda x, sc, sh: ((x * (1.0 + sc) + sh),), [x], [sc, sh], [(D, MXU_DTYPE)])[0]


def _resid_ln_mod(x, y, g, lg, lb, sc_n, sh_n, alpha):
    D = x.shape[1]

    def fn(x, y, g, lg, lb, sc, sh):
        z = alpha * x + (1.0 + g) * y
        xhat, _ = _ln_stats(z)
        xo = xhat * lg + lb
        return z, xo, xo * (1.0 + sc) + sh

    return _rowwise("resid_ln_mod", fn, [x, y], [g, lg, lb, sc_n, sh_n], [(D, F32), (D, F32), (D, MXU_DTYPE)])


def _final_ln_loss(x, y, tgt, g, lg, lb, alpha):
    D = x.shape[1]

    def fn(x, y, t, g, lg, lb):
        z = alpha * x + (1.0 + g) * y
        xhat, rstd = _ln_stats(z)
        out = xhat * lg + lb
        err = out - t
        loss = jnp.sum(jnp.sum(err * err, axis=-1, keepdims=True), axis=0, keepdims=True)
        dout = err * (1.0 / D)
        dz = _ln_bwd(dout, xhat, rstd, lg)
        return (alpha * dz, (1.0 + g) * dz, jnp.broadcast_to(loss, (1, LANE)),
                _colsum(dout * xhat), _colsum(dout), _colsum(dz * y))

    return _rowwise("final_ln_loss", fn, [x, y, tgt], [g, lg, lb], [(D, F32), (D, MXU_DTYPE)], [LANE, D, D, D])


def _bwd_boundary(dx_res, dh, z_p, y_p, sc, g_p, lg_p, lb_p, alpha):
    D = dh.shape[1]

    def fn(dxr, dh, z, y, sc, g, lg, lb):
        xhat, rstd = _ln_stats(z)
        x_in = xhat * lg + lb
        dx = dxr + dh * (1.0 + sc)
        dz = _ln_bwd(dx, xhat, rstd, lg)
        return (alpha * dz, (1.0 + g) * dz,
                _colsum(dh * x_in), _colsum(dh), _colsum(dx * xhat), _colsum(dx), _colsum(dz * y))

    return _rowwise("bwd_boundary", fn, [dx_res, dh, z_p, y_p], [sc, g_p, lg_p, lb_p],
                    [(D, F32), (D, MXU_DTYPE)], [D, D, D, D, D])


def _first_bwd(dx_res, dh, x, sc):
    D = dh.shape[1]

    def fn(dxr, dh, x, sc):
        return dxr + dh * (1.0 + sc), _colsum(dh * x), _colsum(dh)

    return _rowwise("first_bwd", fn, [dx_res, dh, x], [sc], [(D, F32)], [D, D])


def _ffn_fwd(h, w1, w3, w2):
    S, F = h.shape[0], w1.shape[1]

    def epi(accs):
        a, b = accs
        return a, b, a * _sigmoid(a) * b

    a, b, u = _mm("ffn_up", [(h, w1, False, False, 0, 0), (h, w3, False, False, 1, 0)], S, F,
                  [MXU_DTYPE, MXU_DTYPE, MXU_DTYPE], epilogue=epi)
    y = _mm1("ffn_down", u, w2)
    return y, (a, b, u)


def _ffn_bwd(dy, h, saved, w1, w3, w2):
    a, b, u = saved
    S, F = a.shape
    D = h.shape[1]

    def epi(accs, a, b):
        du = accs[0]
        a = a.astype(F32)
        b = b.astype(F32)
        sg = _sigmoid(a)
        return du * b * (sg * (1.0 + a * (1.0 - sg))), du * (a * sg)

    da, db = _mm("ffn_du", [(dy, w2, False, True, 0, 0)], S, F, [MXU_DTYPE, MXU_DTYPE], epilogue=epi, extras=(a, b))
    dw2 = _mm1("ffn_dw2", u, dy, ta=True, out_dtype=WIRE_DTYPE)
    dw1, dw3 = _mm("ffn_dw13", [(h, da, True, False, 0, 0), (h, db, True, False, 1, 0)], D, F,
                   [WIRE_DTYPE, WIRE_DTYPE], epilogue=lambda accs: (accs[0], accs[1]))
    dh = _mm("ffn_dh", [(da, w1, False, True, 0, 0), (db, w3, False, True, 0, 0)], S, D, [F32], tn=256)[0]
    return dh, dw1, dw3, dw2


def _rope_tables(pos):
    j = np.arange(LANE)
    invf = ROPE_THETA ** (-jnp.arange(0, ROPE_DIM, 2, dtype=F32) / ROPE_DIM)
    invf = invf[(j % ROPE_DIM) // 2].reshape(1, LANE)
    sgn = jnp.asarray(np.where(j % 2 == 0, -1.0, 1.0).reshape(1, LANE), F32)

    def fn(pos, invf, sgn):
        ang = pos.astype(F32) * invf
        return jnp.cos(ang), jnp.sin(ang) * sgn

    return _rowwise("rope_tables", fn, [pos], [invf, sgn], [(LANE, F32), (LANE, F32)], tr=256)


def _pair_swap(x):
    w = x.shape[1]
    even = (lax.broadcasted_iota(jnp.int32, x.shape, 1) % 2) == 0
    return jnp.where(even, pltpu.roll(x, w - 1, 1), pltpu.roll(x, 1, 1))


def _rope_fwd(x, c, s):
    return x * c + _pair_swap(x) * s


def _rope_bwd(d, c, s):
    return d * c + _pair_swap(d * s)


ATT_T = 256


def _dot_nt(a, b):
    return lax.dot_general(a, b, (((1,), (1,)), ((), ())), preferred_element_type=F32)


def _dot_tn(a, b):
    return lax.dot_general(a, b, (((0,), (0,)), ((), ())), preferred_element_type=F32)


def _dot_nn(a, b):
    return lax.dot_general(a, b, (((1,), (0,)), ((), ())), preferred_element_type=F32)


def _diag_mask(T, gran):
    r = lax.broadcasted_iota(jnp.int32, (T, T), 0)
    c = lax.broadcasted_iota(jnp.int32, (T, T), 1)
    if gran > 1:
        sh = int(np.log2(gran))
        r, c = lax.shift_right_logical(r, sh), lax.shift_right_logical(c, sh)
    return r >= c


def _attn_specs(S, H, T, mla, col_q, col_k, col_v):
    W = 2 * HEAD_DIM
    specs = [pl.BlockSpec((T, W), lambda p, i: (i, col_q + p))]
    if mla:
        specs.append(pl.BlockSpec((T, 2 * ROPE_DIM), lambda p, i: (i, p)))
    specs.append(pl.BlockSpec((S, W), lambda p, i: (0, col_k + p)))
    if mla:
        specs.append(pl.BlockSpec((S, ROPE_DIM), lambda p, i: (0, 0)))
    specs.append(pl.BlockSpec((S, W), lambda p, i: (0, col_v + p)))
    if not mla:
        specs.append(pl.BlockSpec((2, T, 1), lambda p, i: (p, i, 0)))
        specs.append(pl.BlockSpec((2, 1, S), lambda p, i: (p, 0, 0)))
    return specs


def _attn_fwd(name, S, H, mla, q, k, v, q_pe=None, k_pe=None, cum_col=None, cum_row=None, cols=(0, 0, 0)):
    T = _pick(S, (ATT_T, 128))
    nq = S // T
    scale = (HEAD_DIM + ROPE_DIM) ** -0.5 if mla else HEAD_DIM ** -0.5
    gran = CHUNK if mla else 1

    def body(*refs):
        if mla:
            q_ref, qpe_ref, k_ref, kpe_ref, v_ref, o_ref, lse_ref, m_s, l_s, acc_s = refs
        else:
            q_ref, k_ref, v_ref, cc_ref, cr_ref, o_ref, lse_ref, m_s, l_s, acc_s = refs
        qi = pl.program_id(1)
        for hh in range(2):
            hl = slice(hh * HEAD_DIM, (hh + 1) * HEAD_DIM)
            qf = q_ref[:, hl]
            if mla:
                qf = jnp.concatenate([qf, qpe_ref[:, hh * ROPE_DIM:(hh + 1) * ROPE_DIM]], axis=-1)
            cq = None if mla else cc_ref[hh]
            m_s[...] = jnp.full(m_s.shape, NEG, F32)
            l_s[...] = jnp.zeros(l_s.shape, F32)
            acc_s[...] = jnp.zeros(acc_s.shape, F32)

            def step(j, masked):
                rows = pl.ds(pl.multiple_of(j * T, T), T)
                kf = k_ref[rows, hl]
                if mla:
                    kf = jnp.concatenate([kf, kpe_ref[rows, :]], axis=-1)
                s = _dot_nt(qf, kf) * scale
                if not mla:
                    s = s + (cq - cr_ref[hh, :, rows])
                if masked:
                    s = jnp.where(_diag_mask(T, gran), s, NEG)
                m_old = m_s[:, 0:1]
                m_new = jnp.maximum(m_old, jnp.max(s, axis=-1, keepdims=True))
                p = jnp.exp(s - m_new)
                corr = jnp.exp(m_old - m_new)
                l_s[...] = jnp.broadcast_to(corr * l_s[:, 0:1] + jnp.sum(p, axis=-1, keepdims=True), l_s.shape)
                acc_s[...] = corr * acc_s[...] + _dot_nn(p.astype(MXU_DTYPE), v_ref[rows, hl])
                m_s[...] = jnp.broadcast_to(m_new, m_s.shape)

            lax.fori_loop(0, qi, lambda j, c: (step(j, False), c)[1], 0)
            step(qi, True)
            l = l_s[:, 0:1]
            o_ref[:, hl] = (acc_s[...] / l).astype(o_ref.dtype)
            lse_ref[hh] = jnp.broadcast_to(m_s[:, 0:1] + jnp.log(l), (T, LANE))

    args = [q] + ([q_pe] if mla else []) + [k] + ([k_pe] if mla else []) + [v]
    if not mla:
        args += [cum_col, cum_row]
    return pl.pallas_call(
        body, name=name, grid=(H // 2, nq),
        in_specs=_attn_specs(S, H, T, mla, *cols),
        out_specs=[pl.BlockSpec((T, 2 * HEAD_DIM), lambda p, i: (i, p)),
                   pl.BlockSpec((2, T, LANE), lambda p, i: (p, i, 0))],
        out_shape=[jax.ShapeDtypeStruct((S, H * HEAD_DIM), MXU_DTYPE), jax.ShapeDtypeStruct((H, S, LANE), F32)],
        scratch_shapes=[pltpu.VMEM((T, LANE), F32), pltpu.VMEM((T, LANE), F32), pltpu.VMEM((T, HEAD_DIM), F32)],
        compiler_params=_params(),
    )(*args)


def _attn_bwd(name, S, H, mla, q, k, v, do, lse, q_pe=None, k_pe=None, cum_col=None, cum_row=None,
              cols=(0, 0, 0)):
    T = _pick(S, (ATT_T, 128))
    nq = S // T
    scale = (HEAD_DIM + ROPE_DIM) ** -0.5 if mla else HEAD_DIM ** -0.5
    gran = CHUNK if mla else 1
    dqk = HEAD_DIM + (ROPE_DIM if mla else 0)

    def body(*refs):
        if mla:
            (q_ref, qpe_ref, k_ref, kpe_ref, v_ref, do_ref, lse_ref,
             dq_ref, dk_ref, dv_ref, dqpe_ref, dkpe_ref, dq_s, dl_s, r_s) = refs
        else:
            (q_ref, k_ref, v_ref, cc_ref, cr_ref, do_ref, lse_ref,
             dq_ref, dk_ref, dv_ref, dck_ref, dcq_ref, dq_s, dl_s, r_s) = refs
        hp, qi = pl.program_id(0), pl.program_id(1)

        @pl.when(qi == 0)
        def _():
            dk_ref[...] = jnp.zeros(dk_ref.shape, F32)
            dv_ref[...] = jnp.zeros(dv_ref.shape, F32)
            if not mla:
                dck_ref[...] = jnp.zeros(dck_ref.shape, F32)

        if mla:
            @pl.when(jnp.logical_and(qi == 0, hp == 0))
            def _():
                dkpe_ref[...] = jnp.zeros(dkpe_ref.shape, F32)

        for hh in range(2):
            hl = slice(hh * HEAD_DIM, (hh + 1) * HEAD_DIM)
            qf = q_ref[:, hl]
            if mla:
                qf = jnp.concatenate([qf, qpe_ref[:, hh * ROPE_DIM:(hh + 1) * ROPE_DIM]], axis=-1)
            dof = do_ref[:, hl]
            lse = lse_ref[hh][:, 0:1]
            cq = None if mla else cc_ref[hh]
            dq_s[...] = jnp.zeros(dq_s.shape, F32)
            dl_s[...] = jnp.zeros(dl_s.shape, F32)
            r_s[...] = jnp.zeros(r_s.shape, F32)

            def p_dp(j, masked):
                rows = pl.ds(pl.multiple_of(j * T, T), T)
                kf = k_ref[rows, hl]
                if mla:
                    kf = jnp.concatenate([kf, kpe_ref[rows, :]], axis=-1)
                s = _dot_nt(qf, kf) * scale
                if not mla:
                    s = s + (cq - cr_ref[hh, :, rows])
                if masked:
                    s = jnp.where(_diag_mask(T, gran), s, NEG)
                return rows, kf, jnp.exp(s - lse), _dot_nt(dof, v_ref[rows, hl])

            def sweep1(j, masked):
                rows, _, p, dp = p_dp(j, masked)
                dl_s[...] += jnp.broadcast_to(jnp.sum(p * dp, axis=-1, keepdims=True), dl_s.shape)
                dv_ref[rows, hl] += _dot_tn(p.astype(MXU_DTYPE), dof)

            def sweep2(j, masked):
                rows, kf, p, dp = p_dp(j, masked)
                ds = p * (dp - dl_s[:, 0:1])
                dsb = (ds * scale).astype(MXU_DTYPE)
                dkf = _dot_tn(dsb, qf)
                dk_ref[rows, hl] += dkf[:, :HEAD_DIM]
                if mla:
                    dkpe_ref[rows, :] += dkf[:, HEAD_DIM:]
                else:
                    dck_ref[hh, :, rows] -= jnp.sum(ds, axis=0, keepdims=True)
                    r_s[...] += jnp.broadcast_to(jnp.sum(ds, axis=-1, keepdims=True), r_s.shape)
                dq_s[...] += _dot_nn(dsb, kf)

            for sweep in (sweep1, sweep2):
                lax.fori_loop(0, qi, lambda j, c, sweep=sweep: (sweep(j, False), c)[1], 0)
                sweep(qi, True)
            dq_ref[:, hl] = dq_s[:, :HEAD_DIM].astype(dq_ref.dtype)
            if mla:
                dqpe_ref[:, hh * ROPE_DIM:(hh + 1) * ROPE_DIM] = dq_s[:, HEAD_DIM:]
            else:
                dcq_ref[hh] = r_s[:, 0:1]

    W = 2 * HEAD_DIM
    args = [q] + ([q_pe] if mla else []) + [k] + ([k_pe] if mla else []) + [v]
    if not mla:
        args += [cum_col, cum_row]
    args += [do, lse]
    in_specs = _attn_specs(S, H, T, mla, *cols)
    in_specs += [pl.BlockSpec((T, W), lambda p, i: (i, p)), pl.BlockSpec((2, T, LANE), lambda p, i: (p, i, 0))]
    out_specs = [pl.BlockSpec((T, W), lambda p, i: (i, p)), pl.BlockSpec((S, W), lambda p, i: (0, p)),
                 pl.BlockSpec((S, W), lambda p, i: (0, p))]
    out_shape = [jax.ShapeDtypeStruct((S, H * HEAD_DIM), MXU_DTYPE), jax.ShapeDtypeStruct((S, H * HEAD_DIM), F32),
                 jax.ShapeDtypeStruct((S, H * HEAD_DIM), F32)]
    if mla:
        out_specs += [pl.BlockSpec((T, 2 * ROPE_DIM), lambda p, i: (i, p)),
                      pl.BlockSpec((S, ROPE_DIM), lambda p, i: (0, 0))]
        out_shape += [jax.ShapeDtypeStruct((S, H * ROPE_DIM), F32), jax.ShapeDtypeStruct((S, ROPE_DIM), F32)]
    else:
        out_specs += [pl.BlockSpec((2, 1, S), lambda p, i: (p, 0, 0)), pl.BlockSpec((2, T, 1), lambda p, i: (p, i, 0))]
        out_shape += [jax.ShapeDtypeStruct((H, 1, S), F32), jax.ShapeDtypeStruct((H, S, 1), F32)]
    return pl.pallas_call(
        body, name=name, grid=(H // 2, nq), in_specs=in_specs, out_specs=out_specs, out_shape=out_shape,
        scratch_shapes=[pltpu.VMEM((T, dqk), F32), pltpu.VMEM((T, LANE), F32), pltpu.VMEM((T, LANE), F32)],
        compiler_params=_params(),
    )(*args)


def _mla_prep(lat, cos, sin, qn, kvn, ql, kvl):
    def fn(lat, c, s, qn, kvn):
        ql_ = lat[:, :ql]
        kv_ = lat[:, ql:ql + kvl]
        kp = lat[:, ql + kvl:]
        cq = ql_ * lax.rsqrt(jnp.mean(ql_ * ql_, axis=-1, keepdims=True) + RMS_EPS) * qn
        ckv = kv_ * lax.rsqrt(jnp.mean(kv_ * kv_, axis=-1, keepdims=True) + RMS_EPS) * kvn
        kp2 = jnp.concatenate([kp, jnp.zeros_like(kp)], axis=-1)
        kr = _rope_fwd(kp2, c, s)[:, :ROPE_DIM]
        return cq, ckv, kr

    return _rowwise("mla_prep", fn, [lat, cos, sin], [qn, kvn],
                    [(ql, MXU_DTYPE), (kvl, MXU_DTYPE), (ROPE_DIM, MXU_DTYPE)])


def _mla_prep_bwd(lat, cos, sin, qn, kvn, dcq, dckv, dkr, ql, kvl):
    def fn(lat, c, s, dcq, dckv, dkr, qn, kvn):
        outs, reds = [], []
        for (x, g, d) in ((lat[:, :ql], qn, dcq), (lat[:, ql:ql + kvl], kvn, dckv)):
            r = lax.rsqrt(jnp.mean(x * x, axis=-1, keepdims=True) + RMS_EPS)
            n = x * r
            dn = d * g
            outs.append(r * (dn - n * jnp.mean(dn * n, axis=-1, keepdims=True)))
            reds.append(_colsum(d * n))
        d2 = jnp.concatenate([dkr, jnp.zeros_like(dkr)], axis=-1)
        outs.append(_rope_bwd(d2, c, s)[:, :ROPE_DIM])
        return (jnp.concatenate(outs, axis=-1), *reds)

    return _rowwise("mla_prep_bwd", fn, [lat, cos, sin, dcq, dckv, dkr], [qn, kvn],
                    [(ql + kvl + ROPE_DIM, MXU_DTYPE)], [ql, kvl])


def _mla_fwd(h, w, rc):
    S = h.shape[0]
    ql, kvl = w["q_norm"].shape[1], w["kv_norm"].shape[1]
    H = w["w_uk"].shape[1] // HEAD_DIM
    n_nope, n_pe = H * HEAD_DIM, H * ROPE_DIM
    lat = _mm1("mla_down", h, w["w_down"])
    cq, ckv, kr = _mla_prep(lat, rc[0], rc[1], w["q_norm"], w["kv_norm"], ql, kvl)
    q_nope = _mm1("mla_uq_nope", cq, w["w_uq"], out_dtype=MXU_DTYPE, N=n_nope)
    q_pe = _mm("mla_uq_pe", [(cq, w["w_uq"], False, False, 0, n_nope // LANE)], S, n_pe, [MXU_DTYPE],
               epilogue=lambda accs, c, s: (_rope_fwd(accs[0], c, s),), row_extras=rc, tn=LANE)[0]
    k_nope, v = _mm("mla_ukv", [(ckv, w["w_uk"], False, False, 0, 0), (ckv, w["w_uv"], False, False, 1, 0)],
                    S, n_nope, [MXU_DTYPE, MXU_DTYPE], epilogue=lambda accs: (accs[0], accs[1]))
    o, lse = _attn_fwd("mla_attn_fwd", S, H, True, q_nope, k_nope, v, q_pe=q_pe, k_pe=kr)
    y = _mm1("mla_wo", o, w["w_o"])
    return y, (lat, cq, ckv, kr, q_nope, q_pe, k_nope, v, o, lse)


def _mla_bwd(dy, h, saved, w, rc):
    lat, cq, ckv, kr, q_nope, q_pe, k_nope, v, o, lse = saved
    S = h.shape[0]
    ql, kvl = w["q_norm"].shape[1], w["kv_norm"].shape[1]
    H = w["w_uk"].shape[1] // HEAD_DIM
    n_nope, n_pe = H * HEAD_DIM, H * ROPE_DIM
    do = _mm1("mla_do", dy, w["w_o"], tb=True, out_dtype=MXU_DTYPE)
    dw_o = _mm1("mla_dwo", o, dy, ta=True, out_dtype=WIRE_DTYPE)
    dq_nope, dk_nope, dv, dq_pe_r, dk_pe_r = _attn_bwd("mla_attn_bwd", S, H, True, q_nope, k_nope, v, do, lse,
                                                        q_pe=q_pe, k_pe=kr)

    def unrope(d, c, s):
        reps = (1, n_pe // LANE)
        return (_rope_bwd(d, jnp.tile(c, reps), jnp.tile(s, reps)),)

    dq_pe = _rowwise("mla_unrope_q", unrope, [dq_pe_r, rc[0], rc[1]], [], [(n_pe, MXU_DTYPE)])[0]
    dq = jnp.concatenate([dq_nope, dq_pe], axis=1)
    dw_uq = _mm1("mla_dwuq", cq, dq, ta=True, out_dtype=WIRE_DTYPE)
    dcq = _mm1("mla_dcq", dq, w["w_uq"], tb=True)
    dw_uk, dw_uv = _mm("mla_dwukv", [(ckv, dk_nope, True, False, 0, 0), (ckv, dv, True, False, 1, 0)], kvl, n_nope,
                       [WIRE_DTYPE, WIRE_DTYPE], epilogue=lambda accs: (accs[0], accs[1]))
    dckv = _mm("mla_dckv", [(dk_nope, w["w_uk"], False, True, 0, 0), (dv, w["w_uv"], False, True, 0, 0)],
               S, kvl, [F32])[0]
    dlat, dqn, dkvn = _mla_prep_bwd(lat, rc[0], rc[1], w["q_norm"], w["kv_norm"], dcq, dckv, dk_pe_r, ql, kvl)
    dw_down = _mm1("mla_dwdown", h, dlat, ta=True, out_dtype=WIRE_DTYPE)
    dh = _mm1("mla_dh", dlat, w["w_down"], tb=True)
    return dh, dict(w_down=dw_down, q_norm=dqn, w_uq=dw_uq, kv_norm=dkvn, w_uk=dw_uk, w_uv=dw_uv, w_o=dw_o)


def _log_sigmoid(z):
    return jnp.minimum(z, 0.0) - jnp.log(1.0 + jnp.exp(-jnp.abs(z)))


def _fox_gate_fwd(f, bf):
    S = f.shape[0]
    B = LANE

    def body(f_ref, b_ref, cum_ref):
        r = lax.broadcasted_iota(jnp.int32, (B, B), 0)
        c = lax.broadcasted_iota(jnp.int32, (B, B), 1)
        tri = (r >= c).astype(F32)
        carry = jnp.zeros((1, LANE), F32)
        for blk in range(S // B):
            rows = slice(blk * B, (blk + 1) * B)
            lf = _log_sigmoid(f_ref[rows, :] + b_ref[...])
            cs = jnp.dot(tri, lf, precision=lax.Precision.HIGHEST, preferred_element_type=F32) + carry
            cum_ref[rows, :] = cs
            carry = cs[B - 1:B, :]

    return pl.pallas_call(body, name="fox_gate_fwd", out_shape=jax.ShapeDtypeStruct((S, LANE), F32),
                          compiler_params=_params())(f, bf)


def _fox_gate_bwd(dcum, f, bf):
    S = f.shape[0]
    B = LANE

    def body(d_ref, f_ref, b_ref, df_ref, db_ref):
        r = lax.broadcasted_iota(jnp.int32, (B, B), 0)
        c = lax.broadcasted_iota(jnp.int32, (B, B), 1)
        tri = (r <= c).astype(F32)
        carry = jnp.zeros((1, LANE), F32)
        db = jnp.zeros((1, LANE), F32)
        for blk in reversed(range(S // B)):
            rows = slice(blk * B, (blk + 1) * B)
            dlf = jnp.dot(tri, d_ref[rows, :], precision=lax.Precision.HIGHEST, preferred_element_type=F32) + carry
            carry = dlf[0:1, :]
            z = f_ref[rows, :] + b_ref[...]
            dz = dlf * _sigmoid(-z)
            df_ref[rows, :] = dz.astype(df_ref.dtype)
            db = db + jnp.sum(dz, axis=0, keepdims=True)
        db_ref[...] = db

    return pl.pallas_call(body, name="fox_gate_bwd",
                          out_shape=[jax.ShapeDtypeStruct((S, LANE), MXU_DTYPE), jax.ShapeDtypeStruct((1, LANE), F32)],
                          compiler_params=_params())(dcum, f, bf)


def _fox_fwd(h, w):
    S, D = h.shape
    H = D // HEAD_DIM
    qkv = _mm1("fox_qkv", h, w["w_in"], out_dtype=MXU_DTYPE, N=3 * D)
    f = _mm1("fox_f", h, w["w_in"], bcol=3 * D // LANE, N=LANE, tn=LANE)
    cum = _fox_gate_fwd(f, w["b_f"])
    cumT = cum[:, :H].T
    cum_col, cum_row = cumT.reshape(H, S, 1), cumT.reshape(H, 1, S)
    nb = D // (2 * HEAD_DIM)
    o, lse = _attn_fwd("fox_attn_fwd", S, H, False, qkv, qkv, qkv, cum_col=cum_col, cum_row=cum_row,
                       cols=(0, nb, 2 * nb))
    y = _mm1("fox_wo", o, w["w_o"])
    return y, (qkv, f, cum_col, cum_row, o, lse)


def _fox_bwd(dy, h, saved, w):
    qkv, f, cum_col, cum_row, o, lse = saved
    S, D = h.shape
    H = D // HEAD_DIM
    nb = D // (2 * HEAD_DIM)
    do = _mm1("fox_do", dy, w["w_o"], tb=True, out_dtype=MXU_DTYPE)
    dw_o = _mm1("fox_dwo", o, dy, ta=True, out_dtype=WIRE_DTYPE)
    dq, dk, dv, dck, dcq = _attn_bwd("fox_attn_bwd", S, H, False, qkv, qkv, qkv, do, lse, cum_col=cum_col,
                                     cum_row=cum_row, cols=(0, nb, 2 * nb))
    dcum = jnp.pad((dck.reshape(H, S) + dcq.reshape(H, S)).T, ((0, 0), (0, LANE - H)))
    df, dbf = _fox_gate_bwd(dcum, f, w["b_f"])
    dproj = jnp.concatenate([dq, dk.astype(MXU_DTYPE), dv.astype(MXU_DTYPE), df], axis=1)
    dw_in = _mm1("fox_dwin", h, dproj, ta=True, out_dtype=WIRE_DTYPE)
    dh = _mm1("fox_dh", dproj, w["w_in"], tb=True, tn=256)
    return dh, dict(w_in=dw_in, b_f=dbf[:, :H], w_o=dw_o)


def _place():
    x, y, c = lax.axis_index("x"), lax.axis_index("y"), lax.axis_index("c")
    return x, y, c, [(1 - x, y), (x, 1 - y), (1 - x, 1 - y)]


def _ag_small(name, blk):
    m, n = blk.shape

    def body(x_ref, out_ref, send_sems, recv_sems, local_sem):
        x, y, c, chips = _place()
        me, sibling = (x, y, c), (x, y, 1 - c)

        def rows(px, py, pc):
            return out_ref.at[pl.ds((4 * px + 2 * py + pc) * m, m), :]

        def copy(k, block, to, src=None):
            return pltpu.make_async_remote_copy(
                src_ref=rows(*block) if src is None else src, dst_ref=rows(*block),
                send_sem=send_sems.at[k], recv_sem=recv_sems.at[k], device_id=to, device_id_type=MESH)

        mine = pltpu.make_async_copy(x_ref, rows(*me), local_sem)
        mine.start()
        first = [copy(0, me, sibling, src=x_ref)]
        first += [copy(1 + j, me, (*chip, c), src=x_ref) for j, chip in enumerate(chips)]
        for cp in first:
            cp.start()
        passed = [copy(4 + j, (*chip, c), sibling) for j, chip in enumerate(chips)]
        for j, chip in enumerate(chips):
            copy(1 + j, (*chip, c), me).wait_recv()
            passed[j].start()
        copy(0, sibling, me).wait_recv()
        for j, chip in enumerate(chips):
            copy(4 + j, (*chip, 1 - c), me).wait_recv()
        for cp in first + passed:
            cp.wait_send()
        mine.wait()

    return pl.pallas_call(
        body, name=name, out_shape=jax.ShapeDtypeStruct((N_DEV * m, n), blk.dtype),
        in_specs=[pl.BlockSpec(memory_space=pltpu.VMEM)], out_specs=pl.BlockSpec(memory_space=pltpu.VMEM),
        scratch_shapes=[pltpu.SemaphoreType.DMA((7,)), pltpu.SemaphoreType.DMA((7,)), pltpu.SemaphoreType.DMA],
        compiler_params=_params(),
    )(blk)


def _half(ref, row_axis, c, rows):
    idx = [slice(None)] * len(ref.shape)
    idx[row_axis] = pl.ds(pl.multiple_of(c * rows, 16), rows)
    return ref.at[tuple(idx)]


def _shard(ref, layout, k):
    if layout == "row":
        return ref.at[:, k]
    w = ref.shape[2] // N_CHIPS
    return ref.at[:, :, pl.ds(pl.multiple_of(k * w, LANE), w)]


def _full_shape(shape, layout):
    L, r, w = shape
    return (L, N_CHIPS, r, w) if layout == "row" else (L, r, N_CHIPS * w)


def _gather_weights(shards, layouts):
    n = len(shards)
    half_rows = [s.shape[1] // 2 for s in shards]

    def body(*refs):
        ins, outs = refs[:n], refs[n:2 * n]
        send_sems, recv_sems, loc_sems = refs[2 * n:]
        x, y, c, chips = _place()
        k_me = 2 * x + y
        sibling = (x, y, 1 - c)

        def window(i, kx, ky, half):
            return _half(_shard(outs[i], layouts[i], 2 * kx + ky), 1, half, half_rows[i])

        local, first, passed = [], [], []
        for i in range(n):
            cp = pltpu.make_async_copy(ins[i], _shard(outs[i], layouts[i], k_me), loc_sems.at[i])
            cp.start()
            local.append(cp)
            src = _half(ins[i], 1, c, half_rows[i])
            for j, chip in enumerate(chips):
                cp = pltpu.make_async_remote_copy(
                    src_ref=src, dst_ref=window(i, x, y, c), send_sem=send_sems.at[i, j], recv_sem=recv_sems.at[i, j],
                    device_id=(*chip, c), device_id_type=MESH)
                cp.start()
                first.append(cp)
        for i in range(n):
            for j, chip in enumerate(chips):
                got = window(i, *chip, c)
                pltpu.make_async_remote_copy(
                    src_ref=got, dst_ref=got, send_sem=send_sems.at[i, j], recv_sem=recv_sems.at[i, j],
                    device_id=(*chip, c), device_id_type=MESH).wait_recv()
                cp = pltpu.make_async_remote_copy(
                    src_ref=got, dst_ref=got, send_sem=send_sems.at[i, 3 + j], recv_sem=recv_sems.at[i, 3 + j],
                    device_id=sibling, device_id_type=MESH)
                cp.start()
                passed.append(cp)
        for i in range(n):
            for j, chip in enumerate(chips):
                got = window(i, *chip, 1 - c)
                pltpu.make_async_remote_copy(
                    src_ref=got, dst_ref=got, send_sem=send_sems.at[i, 3 + j], recv_sem=recv_sems.at[i, 3 + j],
                    device_id=sibling, device_id_type=MESH).wait_recv()
        for cp in first + passed:
            cp.wait_send()
        for cp in local:
            cp.wait()

    return pl.pallas_call(
        body, name="gather_weights", in_specs=[ANY] * n, out_specs=[ANY] * n,
        out_shape=[jax.ShapeDtypeStruct(_full_shape(s.shape, lay), s.dtype) for s, lay in zip(shards, layouts)],
        scratch_shapes=[pltpu.SemaphoreType.DMA((n, 6)), pltpu.SemaphoreType.DMA((n, 6)), pltpu.SemaphoreType.DMA((n,))],
        compiler_params=_params(),
    )(*shards)


def _half_shape(shape, layout):
    s = list(shape)
    s[2 if layout == "row" else 1] //= 2
    return tuple(s)


def _swap_halves(grads, layouts):
    n = len(grads)
    row_axis = [2 if lay == "row" else 1 for lay in layouts]
    half_rows = [g.shape[ra] // 2 for g, ra in zip(grads, row_axis)]

    def body(*refs):
        ins, outs = refs[:n], refs[n:2 * n]
        send_sems, recv_sems = refs[2 * n:]
        x, y, c, _ = _place()
        cps = []
        for i in range(n):
            cp = pltpu.make_async_remote_copy(
                src_ref=_half(ins[i], row_axis[i], 1 - c, half_rows[i]), dst_ref=outs[i],
                send_sem=send_sems.at[i], recv_sem=recv_sems.at[i], device_id=(x, y, 1 - c), device_id_type=MESH)
            cp.start()
            cps.append(cp)
        for cp in cps:
            cp.wait()

    return pl.pallas_call(
        body, name="swap_halves", in_specs=[ANY] * n, out_specs=[ANY] * n,
        out_shape=[jax.ShapeDtypeStruct(_half_shape(g.shape, lay), g.dtype) for g, lay in zip(grads, layouts)],
        scratch_shapes=[pltpu.SemaphoreType.DMA((n,)), pltpu.SemaphoreType.DMA((n,))],
        compiler_params=_params(),
    )(*grads)


def _add_half(name, g, r, layout, c_idx):
    L = g.shape[0]
    if layout == "row":
        A, rows, W = L * N_CHIPS, g.shape[2] // 2, g.shape[3]
    else:
        A, rows, W = L, g.shape[1] // 2, g.shape[2]
    g3 = g.reshape(A, 2 * rows, W)
    r3 = r.reshape(A, rows, W)
    tr = _pick(rows, (256, 128, 64, 32, 16))
    nb = rows // tr

    def body(c_ref, g_ref, r_ref, o_ref):
        o_ref[...] = (g_ref[...].astype(F32) + r_ref[...].astype(F32)).astype(o_ref.dtype)

    out = pl.pallas_call(
        body, name=name,
        grid_spec=pltpu.PrefetchScalarGridSpec(
            num_scalar_prefetch=1, grid=(A, nb),
            in_specs=[pl.BlockSpec((None, tr, W), lambda a, i, c: (a, c[0] * nb + i, 0)),
                      pl.BlockSpec((None, tr, W), lambda a, i, c: (a, i, 0))],
            out_specs=pl.BlockSpec((None, tr, W), lambda a, i, c: (a, i, 0))),
        out_shape=jax.ShapeDtypeStruct((A, rows, W), WIRE_DTYPE),
        compiler_params=_params(),
    )(c_idx, g3, r3)
    return out.reshape(r.shape)


def _exchange_shards(parts, layouts):
    n = len(parts)

    def shard_half_shape(p, lay):
        if lay == "row":
            return (p.shape[0],) + p.shape[2:]
        return (p.shape[0], p.shape[1], p.shape[2] // N_CHIPS)

    def body(*refs):
        ins, outs = refs[:n], refs[n:2 * n]
        send_sems, recv_sems = refs[2 * n:]
        x, y, c, chips = _place()
        cps = []
        for i in range(n):
            for j, (kx, ky) in enumerate(chips):
                cp = pltpu.make_async_remote_copy(
                    src_ref=_shard(ins[i], layouts[i], 2 * kx + ky), dst_ref=outs[i].at[j],
                    send_sem=send_sems.at[i, j], recv_sem=recv_sems.at[i, j], device_id=(kx, ky, c),
                    device_id_type=MESH)
                cp.start()
                cps.append(cp)
        for cp in cps:
            cp.wait()

    return pl.pallas_call(
        body, name="exchange_shards", in_specs=[ANY] * n, out_specs=[ANY] * n,
        out_shape=[jax.ShapeDtypeStruct((3,) + shard_half_shape(p, lay), p.dtype) for p, lay in zip(parts, layouts)],
        scratch_shapes=[pltpu.SemaphoreType.DMA((n, 3)), pltpu.SemaphoreType.DMA((n, 3))],
        compiler_params=_params(),
    )(*parts)


def _sum_shards(name, p, r, layout, k_idx):
    L = p.shape[0]
    rows, W = r.shape[2], r.shape[3]
    tr = _pick(rows, (256, 128, 64, 32, 16))
    nb = rows // tr

    def body(k_ref, p_ref, r_ref, o_ref):
        acc = p_ref[...].astype(F32)
        for j in range(3):
            acc = acc + r_ref[j].astype(F32)
        o_ref[...] = acc

    if layout == "row":
        p_spec = pl.BlockSpec((None, None, tr, W), lambda a, i, k: (a, k[0], i, 0))
    else:
        p_spec = pl.BlockSpec((None, tr, W), lambda a, i, k: (a, i, k[0]))
    return pl.pallas_call(
        body, name=name,
        grid_spec=pltpu.PrefetchScalarGridSpec(
            num_scalar_prefetch=1, grid=(L, nb),
            in_specs=[p_spec, pl.BlockSpec((3, None, tr, W), lambda a, i, k: (0, a, i, 0))],
            out_specs=pl.BlockSpec((None, tr, W), lambda a, i, k: (a, i, 0))),
        out_shape=jax.ShapeDtypeStruct((L, rows, W), F32),
        compiler_params=_params(),
    )(k_idx, p, r)


def _join_halves(halves):
    n = len(halves)

    def body(*refs):
        ins, outs = refs[:n], refs[n:2 * n]
        send_sems, recv_sems, loc_sems = refs[2 * n:]
        x, y, c, _ = _place()
        cps, loc = [], []
        for i in range(n):
            rows = ins[i].shape[1]
            dst = _half(outs[i], 1, c, rows)
            lc = pltpu.make_async_copy(ins[i], dst, loc_sems.at[i])
            lc.start()
            loc.append(lc)
            cp = pltpu.make_async_remote_copy(
                src_ref=ins[i], dst_ref=dst, send_sem=send_sems.at[i], recv_sem=recv_sems.at[i],
                device_id=(x, y, 1 - c), device_id_type=MESH)
            cp.start()
            cps.append(cp)
        for cp in cps:
            cp.wait()
        for lc in loc:
            lc.wait()

    return pl.pallas_call(
        body, name="join_halves", in_specs=[ANY] * n, out_specs=[ANY] * n,
        out_shape=[jax.ShapeDtypeStruct((h.shape[0], 2 * h.shape[1], h.shape[2]), h.dtype) for h in halves],
        scratch_shapes=[pltpu.SemaphoreType.DMA((n,)), pltpu.SemaphoreType.DMA((n,)), pltpu.SemaphoreType.DMA((n,))],
        compiler_params=_params(),
    )(*halves)


def _ada_fwd(c_all, ada_w, ada_b):
    L, D, w = ada_w.shape
    tn = _pick(w, (512, 256, 128))

    def body(c_ref, w_ref, b_ref, o_ref, a_ref):
        c = c_ref[...]
        act = (c * _sigmoid(c)).astype(MXU_DTYPE)
        a_ref[...] = act
        o_ref[...] = jnp.dot(act, w_ref[...].astype(MXU_DTYPE), preferred_element_type=F32) + b_ref[...]

    return pl.pallas_call(
        body, name="ada_fwd", grid=(L, w // tn),
        in_specs=[pl.BlockSpec((16, D), lambda l, j: (0, 0)), pl.BlockSpec((None, D, tn), lambda l, j: (l, 0, j)),
                  pl.BlockSpec((None, 1, tn), lambda l, j: (l, 0, j))],
        out_specs=[pl.BlockSpec((None, 16, tn), lambda l, j: (l, 0, j)), pl.BlockSpec((16, D), lambda l, j: (0, 0))],
        out_shape=[jax.ShapeDtypeStruct((L, 16, w), F32), jax.ShapeDtypeStruct((16, D), MXU_DTYPE)],
        compiler_params=_params(),
    )(c_all, ada_w, ada_b)


def _sum_devices(name, parts):
    n, R, W = parts.shape
    tw = _pick(W, (2048, 1024, 512, 256, 128))

    def body(p_ref, o_ref):
        acc = p_ref[0]
        for d in range(1, n):
            acc = acc + p_ref[d]
        o_ref[...] = acc

    return pl.pallas_call(
        body, name=name, grid=(W // tw,), in_specs=[pl.BlockSpec((n, R, tw), lambda j: (0, 0, j))],
        out_specs=pl.BlockSpec((R, tw), lambda j: (0, j)), out_shape=jax.ShapeDtypeStruct((R, W), F32),
        compiler_params=_params(),
    )(parts)


def _adamw(name, w, g, m, v):
    W = w.shape[1]
    bc1 = 1.0 - ADAM_B1 ** ADAM_STEP
    bc2 = 1.0 - ADAM_B2 ** ADAM_STEP

    def fn(w, g, m, v):
        m2 = ADAM_B1 * m + (1.0 - ADAM_B1) * g
        v2 = ADAM_B2 * v + (1.0 - ADAM_B2) * (g * g)
        delta = -ADAM_LR * ((m2 / bc1) / (jnp.sqrt(v2 / bc2) + ADAM_EPS) + ADAM_WD * w)
        return delta, m2, v2

    tr = 256 if W <= 1024 else (128 if W <= 2048 else 64)
    return _rowwise(name, fn, [w, g, m, v], [], [(W, F32)] * 3, tr=tr)


def _cast(name, a):
    W = a.shape[1]
    tr = 256 if W <= 2048 else 128
    return _rowwise(name, lambda a: (a,), [a], [], [(W, WIRE_DTYPE)], tr=tr)[0]


def _flat(a):
    return a.reshape(-1, a.shape[-1])


BIG = ("ffn_w1", "ffn_w3", "ffn_w2", "mla_w_down", "mla_w_uq", "mla_w_uk", "mla_w_uv", "mla_w_o", "fox_w_in",
       "fox_w_o")
LAYOUT = dict(ffn_w1="col", ffn_w3="col", ffn_w2="row", mla_w_down="row", mla_w_uq="col", mla_w_uk="col",
              mla_w_uv="col", mla_w_o="row", fox_w_in="row", fox_w_o="row")
SMALL = ("ln1_g", "ln1_b", "ln2_g", "ln2_b", "mla_q_norm", "mla_kv_norm", "fox_b_f")
WEIGHTS = ("ada_w", "ada_b", "ln1_g", "ln1_b", "ln2_g", "ln2_b", "ffn_w1", "ffn_w3", "ffn_w2", "mla_w_down",
           "mla_q_norm", "mla_w_uq", "mla_kv_norm", "mla_w_uk", "mla_w_uv", "mla_w_o", "fox_w_in", "fox_b_f",
           "fox_w_o")


def _uq_perm(H):
    d = HEAD_DIM + ROPE_DIM
    nope = (np.arange(H)[:, None] * d + np.arange(HEAD_DIM)[None, :]).reshape(-1)
    pe = (np.arange(H)[:, None] * d + HEAD_DIM + np.arange(ROPE_DIM)[None, :]).reshape(-1)
    return np.concatenate([nope, pe])


def kernel(x, c, positions, ada_w, ada_b, ln1_g, ln1_b, ln2_g, ln2_b, ffn_w1, ffn_w3, ffn_w2, mla_w_down, mla_q_norm, mla_w_uq, mla_kv_norm, mla_w_uk, mla_w_uv, mla_w_o, fox_w_in, fox_b_f, fox_w_o, loss_target, m_ada_w, m_ada_b, m_ln1_g, m_ln1_b, m_ln2_g, m_ln2_b, m_ffn_w1, m_ffn_w3, m_ffn_w2, m_mla_w_down, m_mla_q_norm, m_mla_w_uq, m_mla_kv_norm, m_mla_w_uk, m_mla_w_uv, m_mla_w_o, m_fox_w_in, m_fox_b_f, m_fox_w_o, v_ada_w, v_ada_b, v_ln1_g, v_ln1_b, v_ln2_g, v_ln2_b, v_ffn_w1, v_ffn_w3, v_ffn_w2, v_mla_w_down, v_mla_q_norm, v_mla_w_uq, v_mla_kv_norm, v_mla_w_uk, v_mla_w_uv, v_mla_w_o, v_fox_w_in, v_fox_b_f, v_fox_w_o):
    W = dict(ada_w=ada_w, ada_b=ada_b, ln1_g=ln1_g, ln1_b=ln1_b, ln2_g=ln2_g, ln2_b=ln2_b, ffn_w1=ffn_w1,
             ffn_w3=ffn_w3, ffn_w2=ffn_w2, mla_w_down=mla_w_down, mla_q_norm=mla_q_norm, mla_w_uq=mla_w_uq,
             mla_kv_norm=mla_kv_norm, mla_w_uk=mla_w_uk, mla_w_uv=mla_w_uv, mla_w_o=mla_w_o, fox_w_in=fox_w_in,
             fox_b_f=fox_b_f, fox_w_o=fox_w_o)
    Mo = dict(ada_w=m_ada_w, ada_b=m_ada_b, ln1_g=m_ln1_g, ln1_b=m_ln1_b, ln2_g=m_ln2_g, ln2_b=m_ln2_b,
              ffn_w1=m_ffn_w1, ffn_w3=m_ffn_w3, ffn_w2=m_ffn_w2, mla_w_down=m_mla_w_down, mla_q_norm=m_mla_q_norm,
              mla_w_uq=m_mla_w_uq, mla_kv_norm=m_mla_kv_norm, mla_w_uk=m_mla_w_uk, mla_w_uv=m_mla_w_uv,
              mla_w_o=m_mla_w_o, fox_w_in=m_fox_w_in, fox_b_f=m_fox_b_f, fox_w_o=m_fox_w_o)
    Vo = dict(ada_w=v_ada_w, ada_b=v_ada_b, ln1_g=v_ln1_g, ln1_b=v_ln1_b, ln2_g=v_ln2_g, ln2_b=v_ln2_b,
              ffn_w1=v_ffn_w1, ffn_w3=v_ffn_w3, ffn_w2=v_ffn_w2, mla_w_down=v_mla_w_down, mla_q_norm=v_mla_q_norm,
              mla_w_uq=v_mla_w_uq, mla_kv_norm=v_mla_kv_norm, mla_w_uk=v_mla_w_uk, mla_w_uv=v_mla_w_uv,
              mla_w_o=v_mla_w_o, fox_w_in=v_fox_w_in, fox_b_f=v_fox_b_f, fox_w_o=v_fox_w_o)

    S, D = x.shape[1], x.shape[2]
    L = ada_w.shape[0]
    n_mla, n_fox = mla_w_down.shape[0], fox_w_o.shape[0]
    alpha = float((2 * L) ** 0.25)
    H_mla = mla_w_uk.shape[2] * N_CHIPS // HEAD_DIM
    H_fox = D // HEAD_DIM
    xi, yi, ci = lax.axis_index("x"), lax.axis_index("y"), lax.axis_index("c")
    chip = 2 * xi + yi
    dev = 2 * chip + ci
    c_idx = jnp.reshape(ci, (1,)).astype(jnp.int32)
    k_idx = jnp.reshape(chip, (1,)).astype(jnp.int32)
    x0, tgt = x[0], loss_target[0]
    pos = positions.reshape(S, 1)

    c_all = _ag_small("gather_c", jnp.pad(c, ((0, 7), (0, 0)))).reshape(N_DEV, 8, D)[:, 0]
    w_ada = ada_w.shape[2]
    ada_b_sh = lax.dynamic_slice_in_dim(ada_b, chip * w_ada, w_ada, axis=1).reshape(L, 1, w_ada)
    mod_sh, c_act = _ada_fwd(jnp.pad(c_all, ((0, 8), (0, 0))), ada_w, ada_b_sh)
    mod_all = _ag_small("gather_mod", mod_sh.transpose(1, 0, 2).reshape(16, L * w_ada))
    mod_all = mod_all.reshape(N_CHIPS, 2, 16, L, w_ada)[:, 0]
    mod = lax.dynamic_index_in_dim(mod_all, dev, axis=1, keepdims=False)
    mod = mod.transpose(1, 0, 2).reshape(L, 6, 1, D)

    shards = []
    for n in BIG:
        a = W[n]
        shards.append(_cast("cast_" + n, _flat(a)).reshape(a.shape))
    full = dict(zip(BIG, _gather_weights(shards, [LAYOUT[n] for n in BIG])))
    for n in BIG:
        if LAYOUT[n] == "row":
            f = full[n]
            full[n] = f.reshape(f.shape[0], f.shape[1] * f.shape[2], f.shape[3])
    fw = full["fox_w_in"].reshape(n_fox, N_CHIPS, D, -1).transpose(0, 2, 1, 3).reshape(n_fox, D, -1)
    n_in = fw.shape[2]
    full["fox_w_in"] = jnp.pad(fw, ((0, 0), (0, 0), (0, 3 * D + LANE - n_in)))
    perm = _uq_perm(H_mla)
    full["mla_w_uq"] = full["mla_w_uq"][:, :, perm]

    rc = tuple(_rope_tables(pos))

    def mixer_w(i):
        j = i // 2
        if i % 2 == 0:
            return dict(w_down=full["mla_w_down"][j], q_norm=mla_q_norm[j:j + 1], w_uq=full["mla_w_uq"][j],
                        kv_norm=mla_kv_norm[j:j + 1], w_uk=full["mla_w_uk"][j], w_uv=full["mla_w_uv"][j],
                        w_o=full["mla_w_o"][j])
        return dict(w_in=full["fox_w_in"][j], b_f=jnp.pad(fox_b_f[j:j + 1], ((0, 0), (0, LANE - H_fox))),
                    w_o=full["fox_w_o"][j])

    saved = []
    xc = x0
    h = _modulate(x0, mod[0, 1], mod[0, 0])
    for i in range(L):
        mw = mixer_w(i)
        if i % 2 == 0:
            y1, ms = _mla_fwd(h, mw, rc)
        else:
            y1, ms = _fox_fwd(h, mw)
        z1, x1, h2 = _resid_ln_mod(xc, y1, mod[i, 2], ln1_g[i:i + 1], ln1_b[i:i + 1], mod[i, 4], mod[i, 3], alpha)
        y2, fs = _ffn_fwd(h2, full["ffn_w1"][i], full["ffn_w3"][i], full["ffn_w2"][i])
        rec = dict(h1=h, ms=ms, y1=y1, z1=z1, h2=h2, fs=fs, y2=y2)
        if i + 1 < L:
            z2, xc, h = _resid_ln_mod(x1, y2, mod[i, 5], ln2_g[i:i + 1], ln2_b[i:i + 1], mod[i + 1, 1],
                                      mod[i + 1, 0], alpha)
            rec["z2"] = z2
        else:
            dx_res, dy, loss_v, dlg, dlb, dgate = _final_ln_loss(x1, y2, tgt, mod[i, 5], ln2_g[i:i + 1],
                                                                 ln2_b[i:i + 1], alpha)
        saved.append(rec)
    loss = lax.psum(loss_v[0, 0] * (0.5 / D), ("x", "y", "c"))

    G = {n: [None] * W[n].shape[0] for n in WEIGHTS if n != "ada_w" and n != "ada_b"}
    dmod = [[None] * 6 for _ in range(L)]
    for i in reversed(range(L)):
        rec = saved[i]
        mw = mixer_w(i)
        j = i // 2
        G["ln2_g"][i], G["ln2_b"][i], dmod[i][5] = dlg, dlb, dgate
        dh2, G["ffn_w1"][i], G["ffn_w3"][i], G["ffn_w2"][i] = _ffn_bwd(
            dy, rec["h2"], rec["fs"], full["ffn_w1"][i], full["ffn_w3"][i], full["ffn_w2"][i])
        dx_res, dy, dmod[i][4], dmod[i][3], G["ln1_g"][i], G["ln1_b"][i], dmod[i][2] = _bwd_boundary(
            dx_res, dh2, rec["z1"], rec["y1"], mod[i, 4], mod[i, 2], ln1_g[i:i + 1], ln1_b[i:i + 1], alpha)
        if i % 2 == 0:
            dh1, gm = _mla_bwd(dy, rec["h1"], rec["ms"], mw, rc)
            for n, g in gm.items():
                G["mla_" + n][j] = g
        else:
            dh1, gm = _fox_bwd(dy, rec["h1"], rec["ms"], mw)
            for n, g in gm.items():
                G["fox_" + n][j] = g
        if i > 0:
            p = saved[i - 1]
            dx_res, dy, dmod[i][1], dmod[i][0], dlg, dlb, dgate = _bwd_boundary(
                dx_res, dh1, p["z2"], p["y2"], mod[i, 1], mod[i - 1, 5], ln2_g[i - 1:i], ln2_b[i - 1:i], alpha)
        else:
            grad_x, dmod[i][1], dmod[i][0] = _first_bwd(dx_res, dh1, x0, mod[i, 1])

    small = jnp.concatenate([jnp.concatenate([g.reshape(-1) for g in G[n]]) for n in SMALL])
    dmod_v = jnp.concatenate([jnp.concatenate([d.reshape(-1) for d in row]) for row in dmod])
    n_small, n_dmod = small.shape[0], dmod_v.shape[0]
    wblk = -(-(n_small + n_dmod) // (8 * LANE)) * LANE
    blk = jnp.pad(jnp.concatenate([dmod_v, small]), (0, 8 * wblk - n_small - n_dmod)).reshape(8, wblk)
    parts = _ag_small("gather_small", blk).reshape(N_DEV, 8, wblk)
    tot = _sum_devices("sum_small", parts).reshape(-1)
    g_ada_b = tot[:n_dmod].reshape(L, 6 * D)
    off = n_dmod
    Gs = {}
    for n in SMALL:
        Gs[n] = tot[off:off + W[n].size].reshape(W[n].shape)
        off += W[n].size
    dmod_all = parts.reshape(N_DEV, 8 * wblk)[:, :n_dmod].reshape(N_DEV, L, N_CHIPS, w_ada)
    dmod_sh = lax.dynamic_index_in_dim(dmod_all, chip, axis=2, keepdims=False)
    dmod_sh = jnp.pad(dmod_sh, ((0, 8), (0, 0), (0, 0)))
    g_ada_w = jnp.stack([_mm1("ada_dw", c_act, dmod_sh[:, l], ta=True) for l in range(L)])

    gfull, lays = [], []
    for n in BIG:
        g = jnp.stack(G[n])
        if n == "mla_w_uq":
            g = g[:, :, np.argsort(perm)]
        if n == "fox_w_in":
            g = g[:, :, :n_in].reshape(n_fox, D, N_CHIPS, n_in // N_CHIPS).transpose(0, 2, 1, 3)
        elif LAYOUT[n] == "row":
            g = g.reshape(g.shape[0], N_CHIPS, g.shape[1] // N_CHIPS, g.shape[2])
        gfull.append(g)
        lays.append(LAYOUT[n])
    recv = _swap_halves(gfull, lays)
    parts_b = [_add_half("add_half_" + n, g, r, lay, c_idx) for n, g, r, lay in zip(BIG, gfull, recv, lays)]
    recv2 = _exchange_shards(parts_b, lays)
    halves = [_sum_shards("sum_shards_" + n, p, r, lay, k_idx) for n, p, r, lay in zip(BIG, parts_b, recv2, lays)]
    Gb = dict(zip(BIG, _join_halves(halves)))
    fg = Gb["fox_w_in"]
    Gb["fox_w_in"] = fg

    grads = dict(Gb)
    grads.update(Gs)
    grads["ada_w"] = g_ada_w
    grads["ada_b"] = g_ada_b
    delta, new_m, new_v = {}, {}, {}
    for n in WEIGHTS:
        if n in SMALL or n == "ada_b":
            continue
        shp = W[n].shape
        delta[n], new_m[n], new_v[n] = [r.reshape(shp) for r in _adamw(
            "adamw_" + n, _flat(W[n]), _flat(grads[n]), _flat(Mo[n]), _flat(Vo[n]))]
    names_s = SMALL + ("ada_b",)
    cat = lambda d: jnp.concatenate([d[n].reshape(-1) for n in names_s])
    n_s = sum(W[n].size for n in names_s)
    ws = -(-n_s // (8 * LANE)) * LANE
    pk = lambda d: jnp.pad(cat(d), (0, 8 * ws - n_s)).reshape(8, ws)
    ds, ms_, vs = _adamw("adamw_small", pk(W), pk(grads), pk(Mo), pk(Vo))
    off = 0
    for n in names_s:
        sz, shp = W[n].size, W[n].shape
        delta[n] = ds.reshape(-1)[off:off + sz].reshape(shp)
        new_m[n] = ms_.reshape(-1)[off:off + sz].reshape(shp)
        new_v[n] = vs.reshape(-1)[off:off + sz].reshape(shp)
        off += sz

    return (loss, grad_x[None], *[grads[n].reshape(W[n].shape) for n in WEIGHTS], *[delta[n] for n in WEIGHTS],
            *[new_m[n] for n in WEIGHTS], *[new_v[n] for n in WEIGHTS])
```

```python
import functools

import numpy as np
import jax
import jax.numpy as jnp
from jax import lax
from jax.experimental import pallas as pl
from jax.experimental.pallas import tpu as pltpu

F32 = jnp.float32
BF16 = jnp.bfloat16
MXU_DTYPE = jnp.bfloat16
WIRE_DTYPE = jnp.bfloat16

HEAD_DIM = 128
ROPE_DIM = 64
CHUNK = 64
ROPE_THETA = 10000.0
LN_EPS = 1e-5
RMS_EPS = 1e-6
ADAM_LR, ADAM_B1, ADAM_B2, ADAM_EPS, ADAM_WD, ADAM_STEP = 0.001, 0.9, 0.999, 1e-08, 0.01, 10

N_CHIPS = 4
N_DEV = 8
LANE = 128
VMEM_LIMIT = 56 * 1024 * 1024
MESH = pl.DeviceIdType.MESH
ANY = pl.BlockSpec(memory_space=pl.ANY)
NEG = -1e30


def _params(**kw):
    return pltpu.CompilerParams(vmem_limit_bytes=VMEM_LIMIT, **kw)


def _pick(n, cands):
    for c in cands:
        if n % c == 0:
            return c
    return n


def _sigmoid(x):
    return 1.0 / (1.0 + jnp.exp(-x))


def _mm(name, terms, M, N, out_dtypes, epilogue=None, extras=(), row_extras=(), tm=512, tn=512):
    tm = _pick(M, (tm, 256, 128))
    tn = _pick(N, (tn, 896, 768, 640, 384, 256, 128))
    extras = tuple(extras) + tuple(row_extras)
    n_row = len(row_extras)
    n_terms, n_ex, n_out = len(terms), len(extras), len(out_dtypes)
    n_acc = 1 + max(t[4] for t in terms)
    flags = [(t[2], t[3], t[4]) for t in terms]

    def body(*refs):
        accs = [None] * n_acc
        for k, (ta, tb, ai) in enumerate(flags):
            a = refs[2 * k][...].astype(MXU_DTYPE)
            b = refs[2 * k + 1][...].astype(MXU_DTYPE)
            dn = (((0 if ta else 1,), (1 if tb else 0,)), ((), ()))
            r = lax.dot_general(a, b, dn, preferred_element_type=F32)
            accs[ai] = r if accs[ai] is None else accs[ai] + r
        ex = [refs[2 * n_terms + k][...] for k in range(n_ex)]
        outs = epilogue(accs, *ex) if epilogue is not None else (accs[0],)
        for k in range(n_out):
            o_ref = refs[2 * n_terms + n_ex + k]
            o_ref[...] = outs[k].astype(o_ref.dtype)

    in_specs, args = [], []
    for (a, b, ta, tb, _, bcol) in terms:
        K = a.shape[0] if ta else a.shape[1]
        in_specs.append(pl.BlockSpec((K, tm), lambda i, j: (0, i)) if ta
                        else pl.BlockSpec((tm, K), lambda i, j: (i, 0)))
        if isinstance(b, tuple):
            b, lyr = b
            in_specs.append(pl.BlockSpec((None, tn, K), lambda i, j, o=bcol, l=lyr: (l, j + o, 0)) if tb
                            else pl.BlockSpec((None, K, tn), lambda i, j, o=bcol, l=lyr: (l, 0, j + o)))
        else:
            in_specs.append(pl.BlockSpec((tn, K), lambda i, j, o=bcol: (j + o, 0)) if tb
                            else pl.BlockSpec((K, tn), lambda i, j, o=bcol: (0, j + o)))
        args += [a, b]
    for k, e in enumerate(extras):
        in_specs.append(pl.BlockSpec((tm, tn), (lambda i, j: (i, 0)) if k >= n_ex - n_row else (lambda i, j: (i, j))))
        args.append(e)
    outs = pl.pallas_call(
        body, name=name, grid=(M // tm, N // tn), in_specs=in_specs,
        out_specs=[pl.BlockSpec((tm, tn), lambda i, j: (i, j)) for _ in out_dtypes],
        out_shape=[jax.ShapeDtypeStruct((M, N), d) for d in out_dtypes],
        compiler_params=_params(),
    )(*args)
    return outs


def _wdim(b, axis):
    return b[0].shape[1 + axis] if isinstance(b, tuple) else b.shape[axis]


def _mm1(name, a, b, ta=False, tb=False, out_dtype=F32, bcol=0, N=None, **kw):
    M = a.shape[1] if ta else a.shape[0]
    if N is None:
        N = _wdim(b, 0 if tb else 1)
    return _mm(name, [(a, b, ta, tb, 0, bcol)], M, N, [out_dtype], **kw)[0]


def _rowwise(name, fn, tiled, vecs, outs, reds=(), tr=128):
    R = tiled[0].shape[0]
    tr = _pick(R, (tr, 64, 32, 16, 8))
    nt, nv, no, nr = len(tiled), len(vecs), len(outs), len(reds)

    def body(*refs):
        vals = [r[...] for r in refs[:nt + nv]]
        res = fn(*vals)
        for k in range(no):
            o_ref = refs[nt + nv + k]
            o_ref[...] = res[k].astype(o_ref.dtype)
        if nr:
            first = pl.program_id(0) == 0
            for k in range(nr):
                r_ref = refs[nt + nv + no + k]

                @pl.when(first)
                def _(r_ref=r_ref, v=res[no + k]):
                    r_ref[...] = v

                @pl.when(jnp.logical_not(first))
                def _(r_ref=r_ref, v=res[no + k]):
                    r_ref[...] += v

    in_specs = [pl.BlockSpec((tr, t.shape[1]), lambda i: (i, 0)) for t in tiled]
    in_specs += [pl.BlockSpec(v.shape, lambda i, n=v.ndim: (0,) * n) for v in vecs]
    out_specs = [pl.BlockSpec((tr, w), lambda i: (i, 0)) for (w, _) in outs]
    out_specs += [pl.BlockSpec((1, w), lambda i: (0, 0)) for w in reds]
    out_shape = [jax.ShapeDtypeStruct((R, w), d) for (w, d) in outs]
    out_shape += [jax.ShapeDtypeStruct((1, w), F32) for w in reds]
    return pl.pallas_call(
        body, name=name, grid=(R // tr,), in_specs=in_specs, out_specs=out_specs, out_shape=out_shape,
        compiler_params=_params(),
    )(*tiled, *vecs)


def _colsum(v):
    return jnp.sum(v, axis=0, keepdims=True)


def _ln_stats(z):
    mu = jnp.mean(z, axis=-1, keepdims=True)
    zc = z - mu
    var = jnp.mean(zc * zc, axis=-1, keepdims=True)
    rstd = lax.rsqrt(var + LN_EPS)
    return zc * rstd, rstd


def _ln_bwd(dout, xhat, rstd, lg):
    dxh = dout * lg
    m1 = jnp.mean(dxh, axis=-1, keepdims=True)
    m2 = jnp.mean(dxh * xhat, axis=-1, keepdims=True)
    return rstd * (dxh - m1 - xhat * m2)


def _modulate(x, sc, sh):
    D = x.shape[1]
    return _rowwise("modulate", lambda x, sc, sh: ((x * (1.0 + sc) + sh),), [x], [sc, sh], [(D, MXU_DTYPE)])[0]


def _resid_ln_mod(x, y, g, lg, lb, sc_n, sh_n, alpha):
    D = x.shape[1]

    def fn(x, y, g, lg, lb, sc, sh):
        z = alpha * x + (1.0 + g) * y
        xhat, _ = _ln_stats(z)
        xo = xhat * lg + lb
        return z, xo, xo * (1.0 + sc) + sh

    return _rowwise("resid_ln_mod", fn, [x, y], [g, lg, lb, sc_n, sh_n], [(D, F32), (D, F32), (D, MXU_DTYPE)])


def _final_ln_loss(x, y, tgt, g, lg, lb, alpha):
    D = x.shape[1]

    def fn(x, y, t, g, lg, lb):
        z = alpha * x + (1.0 + g) * y
        xhat, rstd = _ln_stats(z)
        out = xhat * lg + lb
        err = out - t
        loss = jnp.sum(jnp.sum(err * err, axis=-1, keepdims=True), axis=0, keepdims=True)
        dout = err * (1.0 / D)
        dz = _ln_bwd(dout, xhat, rstd, lg)
        return (alpha * dz, (1.0 + g) * dz, jnp.broadcast_to(loss, (1, LANE)),
                _colsum(dout * xhat), _colsum(dout), _colsum(dz * y))

    return _rowwise("final_ln_loss", fn, [x, y, tgt], [g, lg, lb], [(D, F32), (D, MXU_DTYPE)], [LANE, D, D, D])


def _bwd_boundary(dx_res, dh, z_p, y_p, sc, g_p, lg_p, lb_p, alpha):
    D = dh.shape[1]

    def fn(dxr, dh, z, y, sc, g, lg, lb):
        xhat, rstd = _ln_stats(z)
        x_in = xhat * lg + lb
        dx = dxr + dh * (1.0 + sc)
        dz = _ln_bwd(dx, xhat, rstd, lg)
        return (alpha * dz, (1.0 + g) * dz,
                _colsum(dh * x_in), _colsum(dh), _colsum(dx * xhat), _colsum(dx), _colsum(dz * y))

    return _rowwise("bwd_boundary", fn, [dx_res, dh, z_p, y_p], [sc, g_p, lg_p, lb_p],
                    [(D, F32), (D, MXU_DTYPE)], [D, D, D, D, D])


def _first_bwd(dx_res, dh, x, sc):
    D = dh.shape[1]

    def fn(dxr, dh, x, sc):
        return dxr + dh * (1.0 + sc), _colsum(dh * x), _colsum(dh)

    return _rowwise("first_bwd", fn, [dx_res, dh, x], [sc], [(D, F32)], [D, D])


def _ffn_fwd(h, w1, w3, w2):
    S, F = h.shape[0], _wdim(w1, 1)

    def epi(accs):
        a, b = accs
        return a, b, a * _sigmoid(a) * b

    a, b, u = _mm("ffn_up", [(h, w1, False, False, 0, 0), (h, w3, False, False, 1, 0)], S, F,
                  [MXU_DTYPE, MXU_DTYPE, MXU_DTYPE], epilogue=epi)
    y = _mm1("ffn_down", u, w2)
    return y, (a, b, u)


def _ffn_bwd(dy, h, saved, w1, w3, w2):
    a, b, u = saved
    S, F = a.shape
    D = h.shape[1]

    def epi(accs, a, b):
        du = accs[0]
        a = a.astype(F32)
        b = b.astype(F32)
        sg = _sigmoid(a)
        return du * b * (sg * (1.0 + a * (1.0 - sg))), du * (a * sg)

    da, db = _mm("ffn_du", [(dy, w2, False, True, 0, 0)], S, F, [MXU_DTYPE, MXU_DTYPE], epilogue=epi, extras=(a, b))
    dw2 = _mm1("ffn_dw2", u, dy, ta=True, out_dtype=WIRE_DTYPE)
    dw1, dw3 = _mm("ffn_dw13", [(h, da, True, False, 0, 0), (h, db, True, False, 1, 0)], D, F,
                   [WIRE_DTYPE, WIRE_DTYPE], epilogue=lambda accs: (accs[0], accs[1]))
    dh = _mm("ffn_dh", [(da, w1, False, True, 0, 0), (db, w3, False, True, 0, 0)], S, D, [F32], tn=256)[0]
    return dh, dw1, dw3, dw2


def _rope_tables(pos):
    j = np.arange(LANE)
    invf = ROPE_THETA ** (-jnp.arange(0, ROPE_DIM, 2, dtype=F32) / ROPE_DIM)
    invf = invf[(j % ROPE_DIM) // 2].reshape(1, LANE)
    sgn = jnp.asarray(np.where(j % 2 == 0, -1.0, 1.0).reshape(1, LANE), F32)

    def fn(pos, invf, sgn):
        ang = pos.astype(F32) * invf
        return jnp.cos(ang), jnp.sin(ang) * sgn

    return _rowwise("rope_tables", fn, [pos], [invf, sgn], [(LANE, F32), (LANE, F32)], tr=256)


def _pair_swap(x):
    w = x.shape[1]
    even = (lax.broadcasted_iota(jnp.int32, x.shape, 1) % 2) == 0
    return jnp.where(even, pltpu.roll(x, w - 1, 1), pltpu.roll(x, 1, 1))


def _rope_fwd(x, c, s):
    return x * c + _pair_swap(x) * s


def _rope_bwd(d, c, s):
    return d * c + _pair_swap(d * s)


ATT_T = 256


def _dot_nt(a, b):
    return lax.dot_general(a, b, (((1,), (1,)), ((), ())), preferred_element_type=F32)


def _dot_tn(a, b):
    return lax.dot_general(a, b, (((0,), (0,)), ((), ())), preferred_element_type=F32)


def _dot_nn(a, b):
    return lax.dot_general(a, b, (((1,), (0,)), ((), ())), preferred_element_type=F32)


def _diag_mask(T, gran):
    r = lax.broadcasted_iota(jnp.int32, (T, T), 0)
    c = lax.broadcasted_iota(jnp.int32, (T, T), 1)
    if gran > 1:
        sh = int(np.log2(gran))
        r, c = lax.shift_right_logical(r, sh), lax.shift_right_logical(c, sh)
    return r >= c


def _attn_specs(S, H, T, mla, col_q, col_k, col_v):
    W = 2 * HEAD_DIM
    specs = [pl.BlockSpec((T, W), lambda p, i: (i, col_q + p))]
    if mla:
        specs.append(pl.BlockSpec((T, 2 * ROPE_DIM), lambda p, i: (i, p)))
    specs.append(pl.BlockSpec((S, W), lambda p, i: (0, col_k + p)))
    if mla:
        specs.append(pl.BlockSpec((S, ROPE_DIM), lambda p, i: (0, 0)))
    specs.append(pl.BlockSpec((S, W), lambda p, i: (0, col_v + p)))
    if not mla:
        specs.append(pl.BlockSpec((2, T, 1), lambda p, i: (p, i, 0)))
        specs.append(pl.BlockSpec((2, 1, S), lambda p, i: (p, 0, 0)))
    return specs


def _attn_fwd(name, S, H, mla, q, k, v, q_pe=None, k_pe=None, cum_col=None, cum_row=None, cols=(0, 0, 0)):
    T = _pick(S, (ATT_T, 128))
    nq = S // T
    scale = (HEAD_DIM + ROPE_DIM) ** -0.5 if mla else HEAD_DIM ** -0.5
    gran = CHUNK if mla else 1

    def body(*refs):
        if mla:
            q_ref, qpe_ref, k_ref, kpe_ref, v_ref, o_ref, lse_ref, m_s, l_s, acc_s = refs
        else:
            q_ref, k_ref, v_ref, cc_ref, cr_ref, o_ref, lse_ref, m_s, l_s, acc_s = refs
        qi = pl.program_id(1)
        for hh in range(2):
            hl = slice(hh * HEAD_DIM, (hh + 1) * HEAD_DIM)
            qf = q_ref[:, hl]
            if mla:
                qf = jnp.concatenate([qf, qpe_ref[:, hh * ROPE_DIM:(hh + 1) * ROPE_DIM]], axis=-1)
            cq = None if mla else cc_ref[hh]
            m_s[...] = jnp.full(m_s.shape, NEG, F32)
            l_s[...] = jnp.zeros(l_s.shape, F32)
            acc_s[...] = jnp.zeros(acc_s.shape, F32)

            def step(j, masked):
                rows = pl.ds(pl.multiple_of(j * T, T), T)
                kf = k_ref[rows, hl]
                if mla:
                    kf = jnp.concatenate([kf, kpe_ref[rows, :]], axis=-1)
                s = _dot_nt(qf, kf) * scale
                if not mla:
                    s = s + (cq - cr_ref[hh, :, rows])
                if masked:
                    s = jnp.where(_diag_mask(T, gran), s, NEG)
                m_old = m_s[:, 0:1]
                m_new = jnp.maximum(m_old, jnp.max(s, axis=-1, keepdims=True))
                p = jnp.exp(s - m_new)
                corr = jnp.exp(m_old - m_new)
                l_s[...] = jnp.broadcast_to(corr * l_s[:, 0:1] + jnp.sum(p, axis=-1, keepdims=True), l_s.shape)
                acc_s[...] = corr * acc_s[...] + _dot_nn(p.astype(MXU_DTYPE), v_ref[rows, hl])
                m_s[...] = jnp.broadcast_to(m_new, m_s.shape)

            lax.fori_loop(0, qi, lambda j, c: (step(j, False), c)[1], 0)
            step(qi, True)
            l = l_s[:, 0:1]
            o_ref[:, hl] = (acc_s[...] / l).astype(o_ref.dtype)
            lse_ref[hh] = jnp.broadcast_to(m_s[:, 0:1] + jnp.log(l), (T, LANE))

    args = [q] + ([q_pe] if mla else []) + [k] + ([k_pe] if mla else []) + [v]
    if not mla:
        args += [cum_col, cum_row]
    return pl.pallas_call(
        body, name=name, grid=(H // 2, nq),
        in_specs=_attn_specs(S, H, T, mla, *cols),
        out_specs=[pl.BlockSpec((T, 2 * HEAD_DIM), lambda p, i: (i, p)),
                   pl.BlockSpec((2, T, LANE), lambda p, i: (p, i, 0))],
        out_shape=[jax.ShapeDtypeStruct((S, H * HEAD_DIM), MXU_DTYPE), jax.ShapeDtypeStruct((H, S, LANE), F32)],
        scratch_shapes=[pltpu.VMEM((T, LANE), F32), pltpu.VMEM((T, LANE), F32), pltpu.VMEM((T, HEAD_DIM), F32)],
        compiler_params=_params(),
    )(*args)


def _attn_bwd(name, S, H, mla, q, k, v, do, lse, q_pe=None, k_pe=None, cum_col=None, cum_row=None,
              cols=(0, 0, 0)):
    T = _pick(S, (ATT_T, 128))
    nq = S // T
    scale = (HEAD_DIM + ROPE_DIM) ** -0.5 if mla else HEAD_DIM ** -0.5
    gran = CHUNK if mla else 1
    dqk = HEAD_DIM + (ROPE_DIM if mla else 0)

    def body(*refs):
        if mla:
            (q_ref, qpe_ref, k_ref, kpe_ref, v_ref, do_ref, lse_ref,
             dq_ref, dk_ref, dv_ref, dqpe_ref, dkpe_ref, dq_s, dl_s, r_s) = refs
        else:
            (q_ref, k_ref, v_ref, cc_ref, cr_ref, do_ref, lse_ref,
             dq_ref, dk_ref, dv_ref, dck_ref, dcq_ref, dq_s, dl_s, r_s) = refs
        hp, qi = pl.program_id(0), pl.program_id(1)

        @pl.when(qi == 0)
        def _():
            dk_ref[...] = jnp.zeros(dk_ref.shape, F32)
            dv_ref[...] = jnp.zeros(dv_ref.shape, F32)
            if not mla:
                dck_ref[...] = jnp.zeros(dck_ref.shape, F32)

        if mla:
            @pl.when(jnp.logical_and(qi == 0, hp == 0))
            def _():
                dkpe_ref[...] = jnp.zeros(dkpe_ref.shape, F32)

        for hh in range(2):
            hl = slice(hh * HEAD_DIM, (hh + 1) * HEAD_DIM)
            qf = q_ref[:, hl]
            if mla:
                qf = jnp.concatenate([qf, qpe_ref[:, hh * ROPE_DIM:(hh + 1) * ROPE_DIM]], axis=-1)
            dof = do_ref[:, hl]
            lse = lse_ref[hh][:, 0:1]
            cq = None if mla else cc_ref[hh]
            dq_s[...] = jnp.zeros(dq_s.shape, F32)
            dl_s[...] = jnp.zeros(dl_s.shape, F32)
            r_s[...] = jnp.zeros(r_s.shape, F32)

            def p_dp(j, masked):
                rows = pl.ds(pl.multiple_of(j * T, T), T)
                kf = k_ref[rows, hl]
                if mla:
                    kf = jnp.concatenate([kf, kpe_ref[rows, :]], axis=-1)
                s = _dot_nt(qf, kf) * scale
                if not mla:
                    s = s + (cq - cr_ref[hh, :, rows])
                if masked:
                    s = jnp.where(_diag_mask(T, gran), s, NEG)
                return rows, kf, jnp.exp(s - lse), _dot_nt(dof, v_ref[rows, hl])

            def sweep1(j, masked):
                rows, _, p, dp = p_dp(j, masked)
                dl_s[...] += jnp.broadcast_to(jnp.sum(p * dp, axis=-1, keepdims=True), dl_s.shape)
                dv_ref[rows, hl] += _dot_tn(p.astype(MXU_DTYPE), dof)

            def sweep2(j, masked):
                rows, kf, p, dp = p_dp(j, masked)
                ds = p * (dp - dl_s[:, 0:1])
                dsb = (ds * scale).astype(MXU_DTYPE)
                dkf = _dot_tn(dsb, qf)
                dk_ref[rows, hl] += dkf[:, :HEAD_DIM]
                if mla:
                    dkpe_ref[rows, :] += dkf[:, HEAD_DIM:]
                else:
                    dck_ref[hh, :, rows] -= jnp.sum(ds, axis=0, keepdims=True)
                    r_s[...] += jnp.broadcast_to(jnp.sum(ds, axis=-1, keepdims=True), r_s.shape)
                dq_s[...] += _dot_nn(dsb, kf)

            for sweep in (sweep1, sweep2):
                lax.fori_loop(0, qi, lambda j, c, sweep=sweep: (sweep(j, False), c)[1], 0)
                sweep(qi, True)
            dq_ref[:, hl] = dq_s[:, :HEAD_DIM].astype(dq_ref.dtype)
            if mla:
                dqpe_ref[:, hh * ROPE_DIM:(hh + 1) * ROPE_DIM] = dq_s[:, HEAD_DIM:]
            else:
                dcq_ref[hh] = r_s[:, 0:1]

    W = 2 * HEAD_DIM
    args = [q] + ([q_pe] if mla else []) + [k] + ([k_pe] if mla else []) + [v]
    if not mla:
        args += [cum_col, cum_row]
    args += [do, lse]
    in_specs = _attn_specs(S, H, T, mla, *cols)
    in_specs += [pl.BlockSpec((T, W), lambda p, i: (i, p)), pl.BlockSpec((2, T, LANE), lambda p, i: (p, i, 0))]
    out_specs = [pl.BlockSpec((T, W), lambda p, i: (i, p)), pl.BlockSpec((S, W), lambda p, i: (0, p)),
                 pl.BlockSpec((S, W), lambda p, i: (0, p))]
    out_shape = [jax.ShapeDtypeStruct((S, H * HEAD_DIM), MXU_DTYPE), jax.ShapeDtypeStruct((S, H * HEAD_DIM), F32),
                 jax.ShapeDtypeStruct((S, H * HEAD_DIM), F32)]
    if mla:
        out_specs += [pl.BlockSpec((T, 2 * ROPE_DIM), lambda p, i: (i, p)),
                      pl.BlockSpec((S, ROPE_DIM), lambda p, i: (0, 0))]
        out_shape += [jax.ShapeDtypeStruct((S, H * ROPE_DIM), F32), jax.ShapeDtypeStruct((S, ROPE_DIM), F32)]
    else:
        out_specs += [pl.BlockSpec((2, 1, S), lambda p, i: (p, 0, 0)), pl.BlockSpec((2, T, 1), lambda p, i: (p, i, 0))]
        out_shape += [jax.ShapeDtypeStruct((H, 1, S), F32), jax.ShapeDtypeStruct((H, S, 1), F32)]
    return pl.pallas_call(
        body, name=name, grid=(H // 2, nq), in_specs=in_specs, out_specs=out_specs, out_shape=out_shape,
        scratch_shapes=[pltpu.VMEM((T, dqk), F32), pltpu.VMEM((T, LANE), F32), pltpu.VMEM((T, LANE), F32)],
        compiler_params=_params(),
    )(*args)


def _mla_prep(lat, cos, sin, qn, kvn, ql, kvl):
    def fn(lat, c, s, qn, kvn):
        ql_ = lat[:, :ql]
        kv_ = lat[:, ql:ql + kvl]
        kp = lat[:, ql + kvl:]
        cq = ql_ * lax.rsqrt(jnp.mean(ql_ * ql_, axis=-1, keepdims=True) + RMS_EPS) * qn
        ckv = kv_ * lax.rsqrt(jnp.mean(kv_ * kv_, axis=-1, keepdims=True) + RMS_EPS) * kvn
        kp2 = jnp.concatenate([kp, jnp.zeros_like(kp)], axis=-1)
        kr = _rope_fwd(kp2, c, s)[:, :ROPE_DIM]
        return cq, ckv, kr

    return _rowwise("mla_prep", fn, [lat, cos, sin], [qn, kvn],
                    [(ql, MXU_DTYPE), (kvl, MXU_DTYPE), (ROPE_DIM, MXU_DTYPE)])


def _mla_prep_bwd(lat, cos, sin, qn, kvn, dcq, dckv, dkr, ql, kvl):
    def fn(lat, c, s, dcq, dckv, dkr, qn, kvn):
        outs, reds = [], []
        for (x, g, d) in ((lat[:, :ql], qn, dcq), (lat[:, ql:ql + kvl], kvn, dckv)):
            r = lax.rsqrt(jnp.mean(x * x, axis=-1, keepdims=True) + RMS_EPS)
            n = x * r
            dn = d * g
            outs.append(r * (dn - n * jnp.mean(dn * n, axis=-1, keepdims=True)))
            reds.append(_colsum(d * n))
        d2 = jnp.concatenate([dkr, jnp.zeros_like(dkr)], axis=-1)
        outs.append(_rope_bwd(d2, c, s)[:, :ROPE_DIM])
        return (jnp.concatenate(outs, axis=-1), *reds)

    return _rowwise("mla_prep_bwd", fn, [lat, cos, sin, dcq, dckv, dkr], [qn, kvn],
                    [(ql + kvl + ROPE_DIM, MXU_DTYPE)], [ql, kvl])


def _mla_fwd(h, w, rc):
    S = h.shape[0]
    ql, kvl = w["q_norm"].shape[1], w["kv_norm"].shape[1]
    H = _wdim(w["w_uk"], 1) // HEAD_DIM
    n_nope, n_pe = H * HEAD_DIM, H * ROPE_DIM
    lat = _mm1("mla_down", h, w["w_down"])
    cq, ckv, kr = _mla_prep(lat, rc[0], rc[1], w["q_norm"], w["kv_norm"], ql, kvl)
    q_nope = _mm1("mla_uq_nope", cq, w["w_uq"], out_dtype=MXU_DTYPE, N=n_nope)
    q_pe = _mm("mla_uq_pe", [(cq, w["w_uq"], False, False, 0, n_nope // LANE)], S, n_pe, [MXU_DTYPE],
               epilogue=lambda accs, c, s: (_rope_fwd(accs[0], c, s),), row_extras=rc, tn=LANE)[0]
    k_nope, v = _mm("mla_ukv", [(ckv, w["w_uk"], False, False, 0, 0), (ckv, w["w_uv"], False, False, 1, 0)],
                    S, n_nope, [MXU_DTYPE, MXU_DTYPE], epilogue=lambda accs: (accs[0], accs[1]))
    o, lse = _attn_fwd("mla_attn_fwd", S, H, True, q_nope, k_nope, v, q_pe=q_pe, k_pe=kr)
    y = _mm1("mla_wo", o, w["w_o"])
    return y, (lat, cq, ckv, kr, q_nope, q_pe, k_nope, v, o, lse)


def _mla_bwd(dy, h, saved, w, rc):
    lat, cq, ckv, kr, q_nope, q_pe, k_nope, v, o, lse = saved
    S = h.shape[0]
    ql, kvl = w["q_norm"].shape[1], w["kv_norm"].shape[1]
    H = _wdim(w["w_uk"], 1) // HEAD_DIM
    n_nope, n_pe = H * HEAD_DIM, H * ROPE_DIM
    do = _mm1("mla_do", dy, w["w_o"], tb=True, out_dtype=MXU_DTYPE)
    dw_o = _mm1("mla_dwo", o, dy, ta=True, out_dtype=WIRE_DTYPE)
    dq_nope, dk_nope, dv, dq_pe_r, dk_pe_r = _attn_bwd("mla_attn_bwd", S, H, True, q_nope, k_nope, v, do, lse,
                                                        q_pe=q_pe, k_pe=kr)

    def unrope(d, c, s):
        reps = (1, n_pe // LANE)
        return (_rope_bwd(d, jnp.tile(c, reps), jnp.tile(s, reps)),)

    dq_pe = _rowwise("mla_unrope_q", unrope, [dq_pe_r, rc[0], rc[1]], [], [(n_pe, MXU_DTYPE)])[0]
    dq = jnp.concatenate([dq_nope, dq_pe], axis=1)
    dw_uq = _mm1("mla_dwuq", cq, dq, ta=True, out_dtype=WIRE_DTYPE)
    dcq = _mm1("mla_dcq", dq, w["w_uq"], tb=True)
    dw_uk, dw_uv = _mm("mla_dwukv", [(ckv, dk_nope, True, False, 0, 0), (ckv, dv, True, False, 1, 0)], kvl, n_nope,
                       [WIRE_DTYPE, WIRE_DTYPE], epilogue=lambda accs: (accs[0], accs[1]))
    dckv = _mm("mla_dckv", [(dk_nope, w["w_uk"], False, True, 0, 0), (dv, w["w_uv"], False, True, 0, 0)],
               S, kvl, [F32])[0]
    dlat, dqn, dkvn = _mla_prep_bwd(lat, rc[0], rc[1], w["q_norm"], w["kv_norm"], dcq, dckv, dk_pe_r, ql, kvl)
    dw_down = _mm1("mla_dwdown", h, dlat, ta=True, out_dtype=WIRE_DTYPE)
    dh = _mm1("mla_dh", dlat, w["w_down"], tb=True)
    return dh, dict(w_down=dw_down, q_norm=dqn, w_uq=dw_uq, kv_norm=dkvn, w_uk=dw_uk, w_uv=dw_uv, w_o=dw_o)


def _log_sigmoid(z):
    return jnp.minimum(z, 0.0) - jnp.log(1.0 + jnp.exp(-jnp.abs(z)))


def _fox_gate_fwd(f, bf):
    S = f.shape[0]
    B = LANE

    def body(f_ref, b_ref, cum_ref):
        r = lax.broadcasted_iota(jnp.int32, (B, B), 0)
        c = lax.broadcasted_iota(jnp.int32, (B, B), 1)
        tri = (r >= c).astype(F32)
        carry = jnp.zeros((1, LANE), F32)
        for blk in range(S // B):
            rows = slice(blk * B, (blk + 1) * B)
            lf = _log_sigmoid(f_ref[rows, :] + b_ref[...])
            cs = jnp.dot(tri, lf, precision=lax.Precision.HIGHEST, preferred_element_type=F32) + carry
            cum_ref[rows, :] = cs
            carry = cs[B - 1:B, :]

    return pl.pallas_call(body, name="fox_gate_fwd", out_shape=jax.ShapeDtypeStruct((S, LANE), F32),
                          compiler_params=_params())(f, bf)


def _fox_gate_bwd(dcum, f, bf):
    S = f.shape[0]
    B = LANE

    def body(d_ref, f_ref, b_ref, df_ref, db_ref):
        r = lax.broadcasted_iota(jnp.int32, (B, B), 0)
        c = lax.broadcasted_iota(jnp.int32, (B, B), 1)
        tri = (r <= c).astype(F32)
        carry = jnp.zeros((1, LANE), F32)
        db = jnp.zeros((1, LANE), F32)
        for blk in reversed(range(S // B)):
            rows = slice(blk * B, (blk + 1) * B)
            dlf = jnp.dot(tri, d_ref[rows, :], precision=lax.Precision.HIGHEST, preferred_element_type=F32) + carry
            carry = dlf[0:1, :]
            z = f_ref[rows, :] + b_ref[...]
            dz = dlf * _sigmoid(-z)
            df_ref[rows, :] = dz.astype(df_ref.dtype)
            db = db + jnp.sum(dz, axis=0, keepdims=True)
        db_ref[...] = db

    return pl.pallas_call(body, name="fox_gate_bwd",
                          out_shape=[jax.ShapeDtypeStruct((S, LANE), MXU_DTYPE), jax.ShapeDtypeStruct((1, LANE), F32)],
                          compiler_params=_params())(dcum, f, bf)


def _fox_fwd(h, w):
    S, D = h.shape
    H = D // HEAD_DIM
    qkv = _mm1("fox_qkv", h, w["w_in"], out_dtype=MXU_DTYPE, N=3 * D)
    f = _mm1("fox_f", h, w["w_in"], bcol=3 * D // LANE, N=LANE, tn=LANE)
    cum = _fox_gate_fwd(f, w["b_f"])
    cumT = cum[:, :H].T
    cum_col, cum_row = cumT.reshape(H, S, 1), cumT.reshape(H, 1, S)
    nb = D // (2 * HEAD_DIM)
    o, lse = _attn_fwd("fox_attn_fwd", S, H, False, qkv, qkv, qkv, cum_col=cum_col, cum_row=cum_row,
                       cols=(0, nb, 2 * nb))
    y = _mm1("fox_wo", o, w["w_o"])
    return y, (qkv, f, cum_col, cum_row, o, lse)


def _fox_bwd(dy, h, saved, w):
    qkv, f, cum_col, cum_row, o, lse = saved
    S, D = h.shape
    H = D // HEAD_DIM
    nb = D // (2 * HEAD_DIM)
    do = _mm1("fox_do", dy, w["w_o"], tb=True, out_dtype=MXU_DTYPE)
    dw_o = _mm1("fox_dwo", o, dy, ta=True, out_dtype=WIRE_DTYPE)
    dq, dk, dv, dck, dcq = _attn_bwd("fox_attn_bwd", S, H, False, qkv, qkv, qkv, do, lse, cum_col=cum_col,
                                     cum_row=cum_row, cols=(0, nb, 2 * nb))
    dcum = jnp.pad((dck.reshape(H, S) + dcq.reshape(H, S)).T, ((0, 0), (0, LANE - H)))
    df, dbf = _fox_gate_bwd(dcum, f, w["b_f"])
    dproj = jnp.concatenate([dq, dk.astype(MXU_DTYPE), dv.astype(MXU_DTYPE), df], axis=1)
    dw_in = _mm1("fox_dwin", h, dproj, ta=True, out_dtype=WIRE_DTYPE)
    dh = _mm1("fox_dh", dproj, w["w_in"], tb=True, tn=256)
    return dh, dict(w_in=dw_in, b_f=dbf[:, :H], w_o=dw_o)


def _place():
    x, y, c = lax.axis_index("x"), lax.axis_index("y"), lax.axis_index("c")
    return x, y, c, [(1 - x, y), (x, 1 - y), (1 - x, 1 - y)]


def _ag_small(name, blk):
    m, n = blk.shape

    def body(x_ref, out_ref, send_sems, recv_sems, local_sem):
        x, y, c, chips = _place()
        me, sibling = (x, y, c), (x, y, 1 - c)

        def rows(px, py, pc):
            return out_ref.at[pl.ds((4 * px + 2 * py + pc) * m, m), :]

        def copy(k, block, to, src=None):
            return pltpu.make_async_remote_copy(
                src_ref=rows(*block) if src is None else src, dst_ref=rows(*block),
                send_sem=send_sems.at[k], recv_sem=recv_sems.at[k], device_id=to, device_id_type=MESH)

        mine = pltpu.make_async_copy(x_ref, rows(*me), local_sem)
        mine.start()
        first = [copy(0, me, sibling, src=x_ref)]
        first += [copy(1 + j, me, (*chip, c), src=x_ref) for j, chip in enumerate(chips)]
        for cp in first:
            cp.start()
        passed = [copy(4 + j, (*chip, c), sibling) for j, chip in enumerate(chips)]
        for j, chip in enumerate(chips):
            copy(1 + j, (*chip, c), me).wait_recv()
            passed[j].start()
        copy(0, sibling, me).wait_recv()
        for j, chip in enumerate(chips):
            copy(4 + j, (*chip, 1 - c), me).wait_recv()
        for cp in first + passed:
            cp.wait_send()
        mine.wait()

    return pl.pallas_call(
        body, name=name, out_shape=jax.ShapeDtypeStruct((N_DEV * m, n), blk.dtype),
        in_specs=[pl.BlockSpec(memory_space=pltpu.VMEM)], out_specs=pl.BlockSpec(memory_space=pltpu.VMEM),
        scratch_shapes=[pltpu.SemaphoreType.DMA((7,)), pltpu.SemaphoreType.DMA((7,)), pltpu.SemaphoreType.DMA],
        compiler_params=_params(),
    )(blk)


def _half(ref, row_axis, c, rows):
    idx = [slice(None)] * len(ref.shape)
    idx[row_axis] = pl.ds(pl.multiple_of(c * rows, 16), rows)
    return ref.at[tuple(idx)]


def _shard(ref, layout, k):
    if layout == "row":
        return ref.at[:, k]
    w = ref.shape[2] // N_CHIPS
    return ref.at[:, :, pl.ds(pl.multiple_of(k * w, LANE), w)]


def _full_shape(shape, layout):
    L, r, w = shape
    return (L, N_CHIPS, r, w) if layout == "row" else (L, r, N_CHIPS * w)


def _cast_full(name, a, layout, k_idx):
    L, r, C = a.shape
    tr = _pick(r, (256, 128, 64, 32, 16))

    def body(k_ref, a_ref, o_ref):
        o_ref[...] = a_ref[...].astype(o_ref.dtype)

    if layout == "row":
        o_spec = pl.BlockSpec((None, None, tr, C), lambda l, i, k: (l, k[0], i, 0))
    else:
        o_spec = pl.BlockSpec((None, tr, C), lambda l, i, k: (l, i, k[0]))
    return pl.pallas_call(
        body, name=name,
        grid_spec=pltpu.PrefetchScalarGridSpec(
            num_scalar_prefetch=1, grid=(L, r // tr),
            in_specs=[pl.BlockSpec((None, tr, C), lambda l, i, k: (l, i, 0))], out_specs=o_spec),
        out_shape=jax.ShapeDtypeStruct(_full_shape(a.shape, layout), WIRE_DTYPE),
        compiler_params=_params(),
    )(k_idx, a)


def _gather_weights(fulls, layouts):
    n = len(fulls)
    half_rows = [f.shape[2 if lay == "row" else 1] // 2 for f, lay in zip(fulls, layouts)]

    def body(*refs):
        outs = refs[n:2 * n]
        send_sems, recv_sems = refs[2 * n:]
        x, y, c, chips = _place()
        sibling = (x, y, 1 - c)

        def window(i, kx, ky, half):
            return _half(_shard(outs[i], layouts[i], 2 * kx + ky), 1, half, half_rows[i])

        first, passed = [], []
        for i in range(n):
            mine = window(i, x, y, c)
            for j, chip in enumerate(chips):
                cp = pltpu.make_async_remote_copy(
                    src_ref=mine, dst_ref=mine, send_sem=send_sems.at[i, j], recv_sem=recv_sems.at[i, j],
                    device_id=(*chip, c), device_id_type=MESH)
                cp.start()
                first.append(cp)
        for i in range(n):
            for j, chip in enumerate(chips):
                got = window(i, *chip, c)
                pltpu.make_async_remote_copy(
                    src_ref=got, dst_ref=got, send_sem=send_sems.at[i, j], recv_sem=recv_sems.at[i, j],
                    device_id=(*chip, c), device_id_type=MESH).wait_recv()
                cp = pltpu.make_async_remote_copy(
                    src_ref=got, dst_ref=got, send_sem=send_sems.at[i, 3 + j], recv_sem=recv_sems.at[i, 3 + j],
                    device_id=sibling, device_id_type=MESH)
                cp.start()
                passed.append(cp)
        for i in range(n):
            for j, chip in enumerate(chips):
                got = window(i, *chip, 1 - c)
                pltpu.make_async_remote_copy(
                    src_ref=got, dst_ref=got, send_sem=send_sems.at[i, 3 + j], recv_sem=recv_sems.at[i, 3 + j],
                    device_id=sibling, device_id_type=MESH).wait_recv()
        for cp in first + passed:
            cp.wait_send()

    return pl.pallas_call(
        body, name="gather_weights", in_specs=[ANY] * n, out_specs=[ANY] * n,
        out_shape=[jax.ShapeDtypeStruct(f.shape, f.dtype) for f in fulls],
        input_output_aliases={i: i for i in range(n)},
        scratch_shapes=[pltpu.SemaphoreType.DMA((n, 6)), pltpu.SemaphoreType.DMA((n, 6))],
        compiler_params=_params(),
    )(*fulls)


def _half_shape(shape, layout):
    s = list(shape)
    s[2 if layout == "row" else 1] //= 2
    return tuple(s)


def _swap_halves(grads, layouts):
    n = len(grads)
    row_axis = [2 if lay == "row" else 1 for lay in layouts]
    half_rows = [g.shape[ra] // 2 for g, ra in zip(grads, row_axis)]

    def body(*refs):
        ins, outs = refs[:n], refs[n:2 * n]
        send_sems, recv_sems = refs[2 * n:]
        x, y, c, _ = _place()
        cps = []
        for i in range(n):
            cp = pltpu.make_async_remote_copy(
                src_ref=_half(ins[i], row_axis[i], 1 - c, half_rows[i]), dst_ref=outs[i],
                send_sem=send_sems.at[i], recv_sem=recv_sems.at[i], device_id=(x, y, 1 - c), device_id_type=MESH)
            cp.start()
            cps.append(cp)
        for cp in cps:
            cp.wait()

    return pl.pallas_call(
        body, name="swap_halves", in_specs=[ANY] * n, out_specs=[ANY] * n,
        out_shape=[jax.ShapeDtypeStruct(_half_shape(g.shape, lay), g.dtype) for g, lay in zip(grads, layouts)],
        scratch_shapes=[pltpu.SemaphoreType.DMA((n,)), pltpu.SemaphoreType.DMA((n,))],
        compiler_params=_params(),
    )(*grads)


def _add_half(name, g, r, layout, c_idx):
    L = g.shape[0]
    if layout == "row":
        A, rows, W = L * N_CHIPS, g.shape[2] // 2, g.shape[3]
    else:
        A, rows, W = L, g.shape[1] // 2, g.shape[2]
    g3 = g.reshape(A, 2 * rows, W)
    r3 = r.reshape(A, rows, W)
    tr = _pick(rows, (256, 128, 64, 32, 16))
    nb = rows // tr

    def body(c_ref, g_ref, r_ref, o_ref):
        o_ref[...] = (g_ref[...].astype(F32) + r_ref[...].astype(F32)).astype(o_ref.dtype)

    out = pl.pallas_call(
        body, name=name,
        grid_spec=pltpu.PrefetchScalarGridSpec(
            num_scalar_prefetch=1, grid=(A, nb),
            in_specs=[pl.BlockSpec((None, tr, W), lambda a, i, c: (a, c[0] * nb + i, 0)),
                      pl.BlockSpec((None, tr, W), lambda a, i, c: (a, i, 0))],
            out_specs=pl.BlockSpec((None, tr, W), lambda a, i, c: (a, i, 0))),
        out_shape=jax.ShapeDtypeStruct((A, rows, W), WIRE_DTYPE),
        compiler_params=_params(),
    )(c_idx, g3, r3)
    return out.reshape(r.shape)


def _exchange_shards(parts, layouts):
    n = len(parts)

    def shard_half_shape(p, lay):
        if lay == "row":
            return (p.shape[0],) + p.shape[2:]
        return (p.shape[0], p.shape[1], p.shape[2] // N_CHIPS)

    def body(*refs):
        ins, outs = refs[:n], refs[n:2 * n]
        send_sems, recv_sems = refs[2 * n:]
        x, y, c, chips = _place()
        cps = []
        for i in range(n):
            for j, (kx, ky) in enumerate(chips):
                cp = pltpu.make_async_remote_copy(
                    src_ref=_shard(ins[i], layouts[i], 2 * kx + ky), dst_ref=outs[i].at[j],
                    send_sem=send_sems.at[i, j], recv_sem=recv_sems.at[i, j], device_id=(kx, ky, c),
                    device_id_type=MESH)
                cp.start()
                cps.append(cp)
        for cp in cps:
            cp.wait()

    return pl.pallas_call(
        body, name="exchange_shards", in_specs=[ANY] * n, out_specs=[ANY] * n,
        out_shape=[jax.ShapeDtypeStruct((3,) + shard_half_shape(p, lay), p.dtype) for p, lay in zip(parts, layouts)],
        scratch_shapes=[pltpu.SemaphoreType.DMA((n, 3)), pltpu.SemaphoreType.DMA((n, 3))],
        compiler_params=_params(),
    )(*parts)


def _sum_shards(name, p, r, layout, kc_idx):
    L = p.shape[0]
    rows, W = r.shape[2], r.shape[3]
    tr = _pick(rows, (256, 128, 64, 32, 16))
    nb = rows // tr

    def body(kc_ref, p_ref, r_ref, o_ref):
        acc = p_ref[...].astype(F32)
        for j in range(3):
            acc = acc + r_ref[j].astype(F32)
        o_ref[...] = acc

    if layout == "row":
        p_spec = pl.BlockSpec((None, None, tr, W), lambda a, i, kc: (a, kc[0], i, 0))
    else:
        p_spec = pl.BlockSpec((None, tr, W), lambda a, i, kc: (a, i, kc[0]))
    return pl.pallas_call(
        body, name=name,
        grid_spec=pltpu.PrefetchScalarGridSpec(
            num_scalar_prefetch=1, grid=(L, nb),
            in_specs=[p_spec, pl.BlockSpec((3, None, tr, W), lambda a, i, kc: (0, a, i, 0))],
            out_specs=pl.BlockSpec((None, tr, W), lambda a, i, kc: (a, kc[1] * nb + i, 0))),
        out_shape=jax.ShapeDtypeStruct((L, 2 * rows, W), F32),
        compiler_params=_params(),
    )(kc_idx, p, r)


def _join_halves(shards):
    n = len(shards)

    def body(*refs):
        outs = refs[n:2 * n]
        send_sems, recv_sems = refs[2 * n:]
        x, y, c, _ = _place()
        cps = []
        for i in range(n):
            mine = _half(outs[i], 1, c, outs[i].shape[1] // 2)
            cp = pltpu.make_async_remote_copy(
                src_ref=mine, dst_ref=mine, send_sem=send_sems.at[i], recv_sem=recv_sems.at[i],
                device_id=(x, y, 1 - c), device_id_type=MESH)
            cp.start()
            cps.append(cp)
        for cp in cps:
            cp.wait()

    return pl.pallas_call(
        body, name="join_halves", in_specs=[ANY] * n, out_specs=[ANY] * n,
        out_shape=[jax.ShapeDtypeStruct(s.shape, s.dtype) for s in shards],
        input_output_aliases={i: i for i in range(n)},
        scratch_shapes=[pltpu.SemaphoreType.DMA((n,)), pltpu.SemaphoreType.DMA((n,))],
        compiler_params=_params(),
    )(*shards)


def _ada_fwd(c_all, ada_w, ada_b):
    L, D, w = ada_w.shape
    tn = _pick(w, (512, 256, 128))

    def body(c_ref, w_ref, b_ref, o_ref, a_ref):
        c = c_ref[...]
        act = (c * _sigmoid(c)).astype(MXU_DTYPE)
        a_ref[...] = act
        o_ref[...] = jnp.dot(act, w_ref[...].astype(MXU_DTYPE), preferred_element_type=F32) + b_ref[...]

    return pl.pallas_call(
        body, name="ada_fwd", grid=(L, w // tn),
        in_specs=[pl.BlockSpec((16, D), lambda l, j: (0, 0)), pl.BlockSpec((None, D, tn), lambda l, j: (l, 0, j)),
                  pl.BlockSpec((None, 1, tn), lambda l, j: (l, 0, j))],
        out_specs=[pl.BlockSpec((None, 16, tn), lambda l, j: (l, 0, j)), pl.BlockSpec((16, D), lambda l, j: (0, 0))],
        out_shape=[jax.ShapeDtypeStruct((L, 16, w), F32), jax.ShapeDtypeStruct((16, D), MXU_DTYPE)],
        compiler_params=_params(),
    )(c_all, ada_w, ada_b)


def _sum_devices(name, parts):
    n, R, W = parts.shape
    tw = _pick(W, (2048, 1024, 512, 256, 128))

    def body(p_ref, o_ref):
        acc = p_ref[0]
        for d in range(1, n):
            acc = acc + p_ref[d]
        o_ref[...] = acc

    return pl.pallas_call(
        body, name=name, grid=(W // tw,), in_specs=[pl.BlockSpec((n, R, tw), lambda j: (0, 0, j))],
        out_specs=pl.BlockSpec((R, tw), lambda j: (0, j)), out_shape=jax.ShapeDtypeStruct((R, W), F32),
        compiler_params=_params(),
    )(parts)


def _adamw(name, w, g, m, v):
    W = w.shape[1]
    bc1 = 1.0 - ADAM_B1 ** ADAM_STEP
    bc2 = 1.0 - ADAM_B2 ** ADAM_STEP

    def fn(w, g, m, v):
        m2 = ADAM_B1 * m + (1.0 - ADAM_B1) * g
        v2 = ADAM_B2 * v + (1.0 - ADAM_B2) * (g * g)
        delta = -ADAM_LR * ((m2 / bc1) / (jnp.sqrt(v2 / bc2) + ADAM_EPS) + ADAM_WD * w)
        return delta, m2, v2

    tr = 256 if W <= 1024 else (128 if W <= 2048 else 64)
    return _rowwise(name, fn, [w, g, m, v], [], [(W, F32)] * 3, tr=tr)


def _cast(name, a):
    W = a.shape[1]
    tr = 256 if W <= 2048 else 128
    return _rowwise(name, lambda a: (a,), [a], [], [(W, WIRE_DTYPE)], tr=tr)[0]


def _flat(a):
    return a.reshape(-1, a.shape[-1])


BIG = ("ffn_w1", "ffn_w3", "ffn_w2", "mla_w_down", "mla_w_uq", "mla_w_uk", "mla_w_uv", "mla_w_o", "fox_w_in",
       "fox_w_o")
LAYOUT = dict(ffn_w1="col", ffn_w3="col", ffn_w2="row", mla_w_down="row", mla_w_uq="col", mla_w_uk="col",
              mla_w_uv="col", mla_w_o="row", fox_w_in="row", fox_w_o="row")
SMALL = ("ln1_g", "ln1_b", "ln2_g", "ln2_b", "mla_q_norm", "mla_kv_norm", "fox_b_f")
WEIGHTS = ("ada_w", "ada_b", "ln1_g", "ln1_b", "ln2_g", "ln2_b", "ffn_w1", "ffn_w3", "ffn_w2", "mla_w_down",
           "mla_q_norm", "mla_w_uq", "mla_kv_norm", "mla_w_uk", "mla_w_uv", "mla_w_o", "fox_w_in", "fox_b_f",
           "fox_w_o")


def _uq_perm(H):
    d = HEAD_DIM + ROPE_DIM
    nope = (np.arange(H)[:, None] * d + np.arange(HEAD_DIM)[None, :]).reshape(-1)
    pe = (np.arange(H)[:, None] * d + HEAD_DIM + np.arange(ROPE_DIM)[None, :]).reshape(-1)
    return np.concatenate([nope, pe])


def kernel(x, c, positions, ada_w, ada_b, ln1_g, ln1_b, ln2_g, ln2_b, ffn_w1, ffn_w3, ffn_w2, mla_w_down, mla_q_norm, mla_w_uq, mla_kv_norm, mla_w_uk, mla_w_uv, mla_w_o, fox_w_in, fox_b_f, fox_w_o, loss_target, m_ada_w, m_ada_b, m_ln1_g, m_ln1_b, m_ln2_g, m_ln2_b, m_ffn_w1, m_ffn_w3, m_ffn_w2, m_mla_w_down, m_mla_q_norm, m_mla_w_uq, m_mla_kv_norm, m_mla_w_uk, m_mla_w_uv, m_mla_w_o, m_fox_w_in, m_fox_b_f, m_fox_w_o, v_ada_w, v_ada_b, v_ln1_g, v_ln1_b, v_ln2_g, v_ln2_b, v_ffn_w1, v_ffn_w3, v_ffn_w2, v_mla_w_down, v_mla_q_norm, v_mla_w_uq, v_mla_kv_norm, v_mla_w_uk, v_mla_w_uv, v_mla_w_o, v_fox_w_in, v_fox_b_f, v_fox_w_o):
    W = dict(ada_w=ada_w, ada_b=ada_b, ln1_g=ln1_g, ln1_b=ln1_b, ln2_g=ln2_g, ln2_b=ln2_b, ffn_w1=ffn_w1,
             ffn_w3=ffn_w3, ffn_w2=ffn_w2, mla_w_down=mla_w_down, mla_q_norm=mla_q_norm, mla_w_uq=mla_w_uq,
             mla_kv_norm=mla_kv_norm, mla_w_uk=mla_w_uk, mla_w_uv=mla_w_uv, mla_w_o=mla_w_o, fox_w_in=fox_w_in,
             fox_b_f=fox_b_f, fox_w_o=fox_w_o)
    Mo = dict(ada_w=m_ada_w, ada_b=m_ada_b, ln1_g=m_ln1_g, ln1_b=m_ln1_b, ln2_g=m_ln2_g, ln2_b=m_ln2_b,
              ffn_w1=m_ffn_w1, ffn_w3=m_ffn_w3, ffn_w2=m_ffn_w2, mla_w_down=m_mla_w_down, mla_q_norm=m_mla_q_norm,
              mla_w_uq=m_mla_w_uq, mla_kv_norm=m_mla_kv_norm, mla_w_uk=m_mla_w_uk, mla_w_uv=m_mla_w_uv,
              mla_w_o=m_mla_w_o, fox_w_in=m_fox_w_in, fox_b_f=m_fox_b_f, fox_w_o=m_fox_w_o)
    Vo = dict(ada_w=v_ada_w, ada_b=v_ada_b, ln1_g=v_ln1_g, ln1_b=v_ln1_b, ln2_g=v_ln2_g, ln2_b=v_ln2_b,
              ffn_w1=v_ffn_w1, ffn_w3=v_ffn_w3, ffn_w2=v_ffn_w2, mla_w_down=v_mla_w_down, mla_q_norm=v_mla_q_norm,
              mla_w_uq=v_mla_w_uq, mla_kv_norm=v_mla_kv_norm, mla_w_uk=v_mla_w_uk, mla_w_uv=v_mla_w_uv,
              mla_w_o=v_mla_w_o, fox_w_in=v_fox_w_in, fox_b_f=v_fox_b_f, fox_w_o=v_fox_w_o)

    S, D = x.shape[1], x.shape[2]
    L = ada_w.shape[0]
    n_mla, n_fox = mla_w_down.shape[0], fox_w_o.shape[0]
    alpha = float((2 * L) ** 0.25)
    H_mla = mla_w_uk.shape[2] * N_CHIPS // HEAD_DIM
    H_fox = D // HEAD_DIM
    xi, yi, ci = lax.axis_index("x"), lax.axis_index("y"), lax.axis_index("c")
    chip = 2 * xi + yi
    dev = 2 * chip + ci
    c_idx = jnp.reshape(ci, (1,)).astype(jnp.int32)
    k_idx = jnp.reshape(chip, (1,)).astype(jnp.int32)
    x0, tgt = x[0], loss_target[0]
    pos = positions.reshape(S, 1)

    c_all = _ag_small("gather_c", jnp.pad(c, ((0, 7), (0, 0)))).reshape(N_DEV, 8, D)[:, 0]
    w_ada = ada_w.shape[2]
    ada_b_sh = lax.dynamic_slice_in_dim(ada_b, chip * w_ada, w_ada, axis=1).reshape(L, 1, w_ada)
    mod_sh, c_act = _ada_fwd(jnp.pad(c_all, ((0, 8), (0, 0))), ada_w, ada_b_sh)
    mod_all = _ag_small("gather_mod", mod_sh.transpose(1, 0, 2).reshape(16, L * w_ada))
    mod_all = mod_all.reshape(N_CHIPS, 2, 16, L, w_ada)[:, 0]
    mod = lax.dynamic_index_in_dim(mod_all, dev, axis=1, keepdims=False)
    mod = mod.transpose(1, 0, 2).reshape(L, 6, 1, D)

    kc_idx = jnp.stack([chip, ci]).astype(jnp.int32)
    fulls = [_cast_full("cast_" + n, W[n], LAYOUT[n], kc_idx) for n in BIG]
    full = dict(zip(BIG, _gather_weights(fulls, [LAYOUT[n] for n in BIG])))
    for n in BIG:
        if LAYOUT[n] == "row":
            f = full[n]
            full[n] = f.reshape(f.shape[0], f.shape[1] * f.shape[2], f.shape[3])
    fw = full["fox_w_in"].reshape(n_fox, N_CHIPS, D, -1).transpose(0, 2, 1, 3).reshape(n_fox, D, -1)
    n_in = fw.shape[2]
    full["fox_w_in"] = jnp.pad(fw, ((0, 0), (0, 0), (0, 3 * D + LANE - n_in)))
    perm = _uq_perm(H_mla)
    full["mla_w_uq"] = full["mla_w_uq"][:, :, perm]

    rc = tuple(_rope_tables(pos))

    def mixer_w(i):
        j = i // 2
        if i % 2 == 0:
            return dict(w_down=(full["mla_w_down"], j), q_norm=mla_q_norm[j:j + 1], w_uq=(full["mla_w_uq"], j),
                        kv_norm=mla_kv_norm[j:j + 1], w_uk=(full["mla_w_uk"], j), w_uv=(full["mla_w_uv"], j),
                        w_o=(full["mla_w_o"], j))
        return dict(w_in=(full["fox_w_in"], j), b_f=jnp.pad(fox_b_f[j:j + 1], ((0, 0), (0, LANE - H_fox))),
                    w_o=(full["fox_w_o"], j))

    saved = []
    xc = x0
    h = _modulate(x0, mod[0, 1], mod[0, 0])
    for i in range(L):
        mw = mixer_w(i)
        if i % 2 == 0:
            y1, ms = _mla_fwd(h, mw, rc)
        else:
            y1, ms = _fox_fwd(h, mw)
        z1, x1, h2 = _resid_ln_mod(xc, y1, mod[i, 2], ln1_g[i:i + 1], ln1_b[i:i + 1], mod[i, 4], mod[i, 3], alpha)
        y2, fs = _ffn_fwd(h2, (full["ffn_w1"], i), (full["ffn_w3"], i), (full["ffn_w2"], i))
        rec = dict(h1=h, ms=ms, y1=y1, z1=z1, h2=h2, fs=fs, y2=y2)
        if i + 1 < L:
            z2, xc, h = _resid_ln_mod(x1, y2, mod[i, 5], ln2_g[i:i + 1], ln2_b[i:i + 1], mod[i + 1, 1],
                                      mod[i + 1, 0], alpha)
            rec["z2"] = z2
        else:
            dx_res, dy, loss_v, dlg, dlb, dgate = _final_ln_loss(x1, y2, tgt, mod[i, 5], ln2_g[i:i + 1],
                                                                 ln2_b[i:i + 1], alpha)
        saved.append(rec)
    loss = lax.psum(loss_v[0, 0] * (0.5 / D), ("x", "y", "c"))

    G = {n: [None] * W[n].shape[0] for n in WEIGHTS if n != "ada_w" and n != "ada_b"}
    dmod = [[None] * 6 for _ in range(L)]
    for i in reversed(range(L)):
        rec = saved[i]
        mw = mixer_w(i)
        j = i // 2
        G["ln2_g"][i], G["ln2_b"][i], dmod[i][5] = dlg, dlb, dgate
        dh2, G["ffn_w1"][i], G["ffn_w3"][i], G["ffn_w2"][i] = _ffn_bwd(
            dy, rec["h2"], rec["fs"], (full["ffn_w1"], i), (full["ffn_w3"], i), (full["ffn_w2"], i))
        dx_res, dy, dmod[i][4], dmod[i][3], G["ln1_g"][i], G["ln1_b"][i], dmod[i][2] = _bwd_boundary(
            dx_res, dh2, rec["z1"], rec["y1"], mod[i, 4], mod[i, 2], ln1_g[i:i + 1], ln1_b[i:i + 1], alpha)
        if i % 2 == 0:
            dh1, gm = _mla_bwd(dy, rec["h1"], rec["ms"], mw, rc)
            for n, g in gm.items():
                G["mla_" + n][j] = g
        else:
            dh1, gm = _fox_bwd(dy, rec["h1"], rec["ms"], mw)
            for n, g in gm.items():
                G["fox_" + n][j] = g
        if i > 0:
            p = saved[i - 1]
            dx_res, dy, dmod[i][1], dmod[i][0], dlg, dlb, dgate = _bwd_boundary(
                dx_res, dh1, p["z2"], p["y2"], mod[i, 1], mod[i - 1, 5], ln2_g[i - 1:i], ln2_b[i - 1:i], alpha)
        else:
            grad_x, dmod[i][1], dmod[i][0] = _first_bwd(dx_res, dh1, x0, mod[i, 1])

    small = jnp.concatenate([jnp.concatenate([g.reshape(-1) for g in G[n]]) for n in SMALL])
    dmod_v = jnp.concatenate([jnp.concatenate([d.reshape(-1) for d in row]) for row in dmod])
    n_small, n_dmod = small.shape[0], dmod_v.shape[0]
    wblk = -(-(n_small + n_dmod) // (8 * LANE)) * LANE
    blk = jnp.pad(jnp.concatenate([dmod_v, small]), (0, 8 * wblk - n_small - n_dmod)).reshape(8, wblk)
    parts = _ag_small("gather_small", blk).reshape(N_DEV, 8, wblk)
    tot = _sum_devices("sum_small", parts).reshape(-1)
    g_ada_b = tot[:n_dmod].reshape(L, 6 * D)
    off = n_dmod
    Gs = {}
    for n in SMALL:
        Gs[n] = tot[off:off + W[n].size].reshape(W[n].shape)
        off += W[n].size
    dmod_all = parts.reshape(N_DEV, 8 * wblk)[:, :n_dmod].reshape(N_DEV, L, N_CHIPS, w_ada)
    dmod_sh = lax.dynamic_index_in_dim(dmod_all, chip, axis=2, keepdims=False)
    dmod_sh = jnp.pad(dmod_sh, ((0, 8), (0, 0), (0, 0)))
    g_ada_w = jnp.stack([_mm1("ada_dw", c_act, dmod_sh[:, l], ta=True) for l in range(L)])

    gfull, lays = [], []
    for n in BIG:
        g = jnp.stack(G[n])
        if n == "mla_w_uq":
            g = g[:, :, np.argsort(perm)]
        if n == "fox_w_in":
            g = g[:, :, :n_in].reshape(n_fox, D, N_CHIPS, n_in // N_CHIPS).transpose(0, 2, 1, 3)
        elif LAYOUT[n] == "row":
            g = g.reshape(g.shape[0], N_CHIPS, g.shape[1] // N_CHIPS, g.shape[2])
        gfull.append(g)
        lays.append(LAYOUT[n])
    recv = _swap_halves(gfull, lays)
    parts_b = [_add_half("add_half_" + n, g, r, lay, c_idx) for n, g, r, lay in zip(BIG, gfull, recv, lays)]
    recv2 = _exchange_shards(parts_b, lays)
    halves = [_sum_shards("sum_shards_" + n, p, r, lay, kc_idx) for n, p, r, lay in zip(BIG, parts_b, recv2, lays)]
    Gb = dict(zip(BIG, _join_halves(halves)))

    grads = dict(Gb)
    grads.update(Gs)
    grads["ada_w"] = g_ada_w
    grads["ada_b"] = g_ada_b
    delta, new_m, new_v = {}, {}, {}
    for n in WEIGHTS:
        if n in SMALL or n == "ada_b":
            continue
        shp = W[n].shape
        delta[n], new_m[n], new_v[n] = [r.reshape(shp) for r in _adamw(
            "adamw_" + n, _flat(W[n]), _flat(grads[n]), _flat(Mo[n]), _flat(Vo[n]))]
    names_s = SMALL + ("ada_b",)
    cat = lambda d: jnp.concatenate([d[n].reshape(-1) for n in names_s])
    n_s = sum(W[n].size for n in names_s)
    ws = -(-n_s // (8 * LANE)) * LANE
    pk = lambda d: jnp.pad(cat(d), (0, 8 * ws - n_s)).reshape(8, ws)
    ds, ms_, vs = _adamw("adamw_small", pk(W), pk(grads), pk(Mo), pk(Vo))
    off = 0
    for n in names_s:
        sz, shp = W[n].size, W[n].shape
        delta[n] = ds.reshape(-1)[off:off + sz].reshape(shp)
        new_m[n] = ms_.reshape(-1)[off:off + sz].reshape(shp)
        new_v[n] = vs.reshape(-1)[off:off + sz].reshape(shp)
        off += sz

    return (loss, grad_x[None], *[grads[n].reshape(W[n].shape) for n in WEIGHTS], *[delta[n] for n in WEIGHTS],
            *[new_m[n] for n in WEIGHTS], *[new_v[n] for n in WEIGHTS])
```

```python
import functools

import numpy as np
import jax
import jax.numpy as jnp
from jax import lax
from jax.experimental import pallas as pl
from jax.experimental.pallas import tpu as pltpu

F32 = jnp.float32
BF16 = jnp.bfloat16
MXU_DTYPE = jnp.bfloat16
WIRE_DTYPE = jnp.bfloat16

HEAD_DIM = 128
ROPE_DIM = 64
CHUNK = 64
ROPE_THETA = 10000.0
LN_EPS = 1e-5
RMS_EPS = 1e-6
ADAM_LR, ADAM_B1, ADAM_B2, ADAM_EPS, ADAM_WD, ADAM_STEP = 0.001, 0.9, 0.999, 1e-08, 0.01, 10

N_CHIPS = 4
N_DEV = 8
LANE = 128
VMEM_LIMIT = 56 * 1024 * 1024
MESH = pl.DeviceIdType.MESH
ANY = pl.BlockSpec(memory_space=pl.ANY)
NEG = -1e30


def _params(**kw):
    return pltpu.CompilerParams(vmem_limit_bytes=VMEM_LIMIT, **kw)


def _pick(n, cands):
    for c in cands:
        if n % c == 0:
            return c
    return n


def _sigmoid(x):
    return 1.0 / (1.0 + jnp.exp(-x))


def _mm(name, terms, M, N, out_dtypes, epilogue=None, extras=(), row_extras=(), tm=512, tn=512):
    tm = _pick(M, (tm, 256, 128))
    tn = _pick(N, (tn, 896, 768, 640, 384, 256, 128))
    extras = tuple(extras) + tuple(row_extras)
    n_row = len(row_extras)
    n_terms, n_ex, n_out = len(terms), len(extras), len(out_dtypes)
    n_acc = 1 + max(t[4] for t in terms)
    flags = [(t[2], t[3], t[4]) for t in terms]

    def body(*refs):
        accs = [None] * n_acc
        for k, (ta, tb, ai) in enumerate(flags):
            a = refs[2 * k][...].astype(MXU_DTYPE)
            b = refs[2 * k + 1][...].astype(MXU_DTYPE)
            dn = (((0 if ta else 1,), (1 if tb else 0,)), ((), ()))
            r = lax.dot_general(a, b, dn, preferred_element_type=F32)
            accs[ai] = r if accs[ai] is None else accs[ai] + r
        ex = [refs[2 * n_terms + k][...] for k in range(n_ex)]
        outs = epilogue(accs, *ex) if epilogue is not None else (accs[0],)
        for k in range(n_out):
            o_ref = refs[2 * n_terms + n_ex + k]
            o_ref[...] = outs[k].astype(o_ref.dtype)

    in_specs, args = [], []
    for (a, b, ta, tb, _, bcol) in terms:
        K = a.shape[0] if ta else a.shape[1]
        in_specs.append(pl.BlockSpec((K, tm), lambda i, j: (0, i)) if ta
                        else pl.BlockSpec((tm, K), lambda i, j: (i, 0)))
        if isinstance(b, tuple):
            b, lyr = b
            in_specs.append(pl.BlockSpec((None, tn, K), lambda i, j, o=bcol, l=lyr: (l, j + o, 0)) if tb
                            else pl.BlockSpec((None, K, tn), lambda i, j, o=bcol, l=lyr: (l, 0, j + o)))
        else:
            in_specs.append(pl.BlockSpec((tn, K), lambda i, j, o=bcol: (j + o, 0)) if tb
                            else pl.BlockSpec((K, tn), lambda i, j, o=bcol: (0, j + o)))
        args += [a, b]
    for k, e in enumerate(extras):
        in_specs.append(pl.BlockSpec((tm, tn), (lambda i, j: (i, 0)) if k >= n_ex - n_row else (lambda i, j: (i, j))))
        args.append(e)
    outs = pl.pallas_call(
        body, name=name, grid=(M // tm, N // tn), in_specs=in_specs,
        out_specs=[pl.BlockSpec((tm, tn), lambda i, j: (i, j)) for _ in out_dtypes],
        out_shape=[jax.ShapeDtypeStruct((M, N), d) for d in out_dtypes],
        compiler_params=_params(),
    )(*args)
    return outs


def _wdim(b, axis):
    return b[0].shape[1 + axis] if isinstance(b, tuple) else b.shape[axis]


def _mm1(name, a, b, ta=False, tb=False, out_dtype=F32, bcol=0, N=None, **kw):
    M = a.shape[1] if ta else a.shape[0]
    if N is None:
        N = _wdim(b, 0 if tb else 1)
    return _mm(name, [(a, b, ta, tb, 0, bcol)], M, N, [out_dtype], **kw)[0]


def _rowwise(name, fn, tiled, vecs, outs, reds=(), tr=128):
    R = tiled[0].shape[0]
    tr = _pick(R, (tr, 64, 32, 16, 8))
    nt, nv, no, nr = len(tiled), len(vecs), len(outs), len(reds)

    def body(*refs):
        vals = [r[...] for r in refs[:nt + nv]]
        res = fn(*vals)
        for k in range(no):
            o_ref = refs[nt + nv + k]
            o_ref[...] = res[k].astype(o_ref.dtype)
        if nr:
            first = pl.program_id(0) == 0
            for k in range(nr):
                r_ref = refs[nt + nv + no + k]

                @pl.when(first)
                def _(r_ref=r_ref, v=res[no + k]):
                    r_ref[...] = v

                @pl.when(jnp.logical_not(first))
                def _(r_ref=r_ref, v=res[no + k]):
                    r_ref[...] += v

    in_specs = [pl.BlockSpec((tr, t.shape[1]), lambda i: (i, 0)) for t in tiled]
    in_specs += [pl.BlockSpec(v.shape, lambda i, n=v.ndim: (0,) * n) for v in vecs]
    out_specs = [pl.BlockSpec((tr, w), lambda i: (i, 0)) for (w, _) in outs]
    out_specs += [pl.BlockSpec((1, w), lambda i: (0, 0)) for w in reds]
    out_shape = [jax.ShapeDtypeStruct((R, w), d) for (w, d) in outs]
    out_shape += [jax.ShapeDtypeStruct((1, w), F32) for w in reds]
    return pl.pallas_call(
        body, name=name, grid=(R // tr,), in_specs=in_specs, out_specs=out_specs, out_shape=out_shape,
        compiler_params=_params(),
    )(*tiled, *vecs)


def _colsum(v):
    return jnp.sum(v, axis=0, keepdims=True)


def _ln_stats(z):
    mu = jnp.mean(z, axis=-1, keepdims=True)
    zc = z - mu
    var = jnp.mean(zc * zc, axis=-1, keepdims=True)
    rstd = lax.rsqrt(var + LN_EPS)
    return zc * rstd, rstd


def _ln_bwd(dout, xhat, rstd, lg):
    dxh = dout * lg
    m1 = jnp.mean(dxh, axis=-1, keepdims=True)
    m2 = jnp.mean(dxh * xhat, axis=-1, keepdims=True)
    return rstd * (dxh - m1 - xhat * m2)


def _modulate(x, sc, sh):
    D = x.shape[1]
    return _rowwise("modulate", lambda x, sc, sh: ((x * (1.0 + sc) + sh),), [x], [sc, sh], [(D, MXU_DTYPE)])[0]


def _resid_ln_mod(x, y, g, lg, lb, sc_n, sh_n, alpha):
    D = x.shape[1]

    def fn(x, y, g, lg, lb, sc, sh):
        z = alpha * x + (1.0 + g) * y
        xhat, _ = _ln_stats(z)
        xo = xhat * lg + lb
        return z, xo, xo * (1.0 + sc) + sh

    return _rowwise("resid_ln_mod", fn, [x, y], [g, lg, lb, sc_n, sh_n], [(D, F32), (D, F32), (D, MXU_DTYPE)])


def _final_ln_loss(x, y, tgt, g, lg, lb, alpha):
    D = x.shape[1]

    def fn(x, y, t, g, lg, lb):
        z = alpha * x + (1.0 + g) * y
        xhat, rstd = _ln_stats(z)
        out = xhat * lg + lb
        err = out - t
        loss = jnp.sum(jnp.sum(err * err, axis=-1, keepdims=True), axis=0, keepdims=True)
        dout = err * (1.0 / D)
        dz = _ln_bwd(dout, xhat, rstd, lg)
        return (alpha * dz, (1.0 + g) * dz, jnp.broadcast_to(loss, (1, LANE)),
                _colsum(dout * xhat), _colsum(dout), _colsum(dz * y))

    return _rowwise("final_ln_loss", fn, [x, y, tgt], [g, lg, lb], [(D, F32), (D, MXU_DTYPE)], [LANE, D, D, D])


def _bwd_boundary(dx_res, dh, z_p, y_p, sc, g_p, lg_p, lb_p, alpha):
    D = dh.shape[1]

    def fn(dxr, dh, z, y, sc, g, lg, lb):
        xhat, rstd = _ln_stats(z)
        x_in = xhat * lg + lb
        dx = dxr + dh * (1.0 + sc)
        dz = _ln_bwd(dx, xhat, rstd, lg)
        return (alpha * dz, (1.0 + g) * dz,
                _colsum(dh * x_in), _colsum(dh), _colsum(dx * xhat), _colsum(dx), _colsum(dz * y))

    return _rowwise("bwd_boundary", fn, [dx_res, dh, z_p, y_p], [sc, g_p, lg_p, lb_p],
                    [(D, F32), (D, MXU_DTYPE)], [D, D, D, D, D])


def _first_bwd(dx_res, dh, x, sc):
    D = dh.shape[1]

    def fn(dxr, dh, x, sc):
        return dxr + dh * (1.0 + sc), _colsum(dh * x), _colsum(dh)

    return _rowwise("first_bwd", fn, [dx_res, dh, x], [sc], [(D, F32)], [D, D])


def _ffn_fwd(h, w1, w3, w2):
    S, F = h.shape[0], _wdim(w1, 1)

    def epi(accs):
        a, b = accs
        return a, b, a * _sigmoid(a) * b

    a, b, u = _mm("ffn_up", [(h, w1, False, False, 0, 0), (h, w3, False, False, 1, 0)], S, F,
                  [MXU_DTYPE, MXU_DTYPE, MXU_DTYPE], epilogue=epi)
    y = _mm1("ffn_down", u, w2)
    return y, (a, b, u)


def _ffn_bwd(dy, h, saved, w1, w3, w2):
    a, b, u = saved
    S, F = a.shape
    D = h.shape[1]

    def epi(accs, a, b):
        du = accs[0]
        a = a.astype(F32)
        b = b.astype(F32)
        sg = _sigmoid(a)
        return du * b * (sg * (1.0 + a * (1.0 - sg))), du * (a * sg)

    da, db = _mm("ffn_du", [(dy, w2, False, True, 0, 0)], S, F, [MXU_DTYPE, MXU_DTYPE], epilogue=epi, extras=(a, b))
    dw2 = _mm1("ffn_dw2", u, dy, ta=True, out_dtype=WIRE_DTYPE)
    dw1, dw3 = _mm("ffn_dw13", [(h, da, True, False, 0, 0), (h, db, True, False, 1, 0)], D, F,
                   [WIRE_DTYPE, WIRE_DTYPE], epilogue=lambda accs: (accs[0], accs[1]))
    dh = _mm("ffn_dh", [(da, w1, False, True, 0, 0), (db, w3, False, True, 0, 0)], S, D, [F32], tn=256)[0]
    return dh, dw1, dw3, dw2


def _rope_tables(pos):
    j = np.arange(LANE)
    invf = ROPE_THETA ** (-jnp.arange(0, ROPE_DIM, 2, dtype=F32) / ROPE_DIM)
    invf = invf[(j % ROPE_DIM) // 2].reshape(1, LANE)
    sgn = jnp.asarray(np.where(j % 2 == 0, -1.0, 1.0).reshape(1, LANE), F32)

    def fn(pos, invf, sgn):
        ang = pos.astype(F32) * invf
        return jnp.cos(ang), jnp.sin(ang) * sgn

    return _rowwise("rope_tables", fn, [pos], [invf, sgn], [(LANE, F32), (LANE, F32)], tr=256)


def _pair_swap(x):
    w = x.shape[1]
    even = (lax.broadcasted_iota(jnp.int32, x.shape, 1) % 2) == 0
    return jnp.where(even, pltpu.roll(x, w - 1, 1), pltpu.roll(x, 1, 1))


def _rope_fwd(x, c, s):
    return x * c + _pair_swap(x) * s


def _rope_bwd(d, c, s):
    return d * c + _pair_swap(d * s)


ATT_T = 256


def _dot_nt(a, b):
    return lax.dot_general(a, b, (((1,), (1,)), ((), ())), preferred_element_type=F32)


def _dot_tn(a, b):
    return lax.dot_general(a, b, (((0,), (0,)), ((), ())), preferred_element_type=F32)


def _dot_nn(a, b):
    return lax.dot_general(a, b, (((1,), (0,)), ((), ())), preferred_element_type=F32)


def _diag_mask(T, gran):
    r = lax.broadcasted_iota(jnp.int32, (T, T), 0)
    c = lax.broadcasted_iota(jnp.int32, (T, T), 1)
    if gran > 1:
        sh = int(np.log2(gran))
        r, c = lax.shift_right_logical(r, sh), lax.shift_right_logical(c, sh)
    return r >= c


def _attn_specs(S, H, T, mla, col_q, col_k, col_v):
    W = 2 * HEAD_DIM
    specs = [pl.BlockSpec((T, W), lambda p, i: (i, col_q + p))]
    if mla:
        specs.append(pl.BlockSpec((T, 2 * ROPE_DIM), lambda p, i: (i, p)))
    specs.append(pl.BlockSpec((S, W), lambda p, i: (0, col_k + p)))
    if mla:
        specs.append(pl.BlockSpec((S, ROPE_DIM), lambda p, i: (0, 0)))
    specs.append(pl.BlockSpec((S, W), lambda p, i: (0, col_v + p)))
    if not mla:
        specs.append(pl.BlockSpec((2, T, 1), lambda p, i: (p, i, 0)))
        specs.append(pl.BlockSpec((2, 1, S), lambda p, i: (p, 0, 0)))
    return specs


def _attn_fwd(name, S, H, mla, q, k, v, q_pe=None, k_pe=None, cum_col=None, cum_row=None, cols=(0, 0, 0)):
    T = _pick(S, (ATT_T, 128))
    nq = S // T
    scale = (HEAD_DIM + ROPE_DIM) ** -0.5 if mla else HEAD_DIM ** -0.5
    gran = CHUNK if mla else 1

    def body(*refs):
        if mla:
            q_ref, qpe_ref, k_ref, kpe_ref, v_ref, o_ref, lse_ref, m_s, l_s, acc_s = refs
        else:
            q_ref, k_ref, v_ref, cc_ref, cr_ref, o_ref, lse_ref, m_s, l_s, acc_s = refs
        qi = pl.program_id(1)
        for hh in range(2):
            hl = slice(hh * HEAD_DIM, (hh + 1) * HEAD_DIM)
            qf = q_ref[:, hl]
            if mla:
                qf = jnp.concatenate([qf, qpe_ref[:, hh * ROPE_DIM:(hh + 1) * ROPE_DIM]], axis=-1)
            cq = None if mla else cc_ref[hh]
            m_s[...] = jnp.full(m_s.shape, NEG, F32)
            l_s[...] = jnp.zeros(l_s.shape, F32)
            acc_s[...] = jnp.zeros(acc_s.shape, F32)

            def step(j, masked):
                rows = pl.ds(pl.multiple_of(j * T, T), T)
                kf = k_ref[rows, hl]
                if mla:
                    kf = jnp.concatenate([kf, kpe_ref[rows, :]], axis=-1)
                s = _dot_nt(qf, kf) * scale
                if not mla:
                    s = s + (cq - cr_ref[hh, :, rows])
                if masked:
                    s = jnp.where(_diag_mask(T, gran), s, NEG)
                m_old = m_s[:, 0:1]
                m_new = jnp.maximum(m_old, jnp.max(s, axis=-1, keepdims=True))
                p = jnp.exp(s - m_new)
                corr = jnp.exp(m_old - m_new)
                l_s[...] = jnp.broadcast_to(corr * l_s[:, 0:1] + jnp.sum(p, axis=-1, keepdims=True), l_s.shape)
                acc_s[...] = corr * acc_s[...] + _dot_nn(p.astype(MXU_DTYPE), v_ref[rows, hl])
                m_s[...] = jnp.broadcast_to(m_new, m_s.shape)

            lax.fori_loop(0, qi, lambda j, c: (step(j, False), c)[1], 0)
            step(qi, True)
            l = l_s[:, 0:1]
            o_ref[:, hl] = (acc_s[...] / l).astype(o_ref.dtype)
            lse_ref[hh] = jnp.broadcast_to(m_s[:, 0:1] + jnp.log(l), (T, LANE))

    args = [q] + ([q_pe] if mla else []) + [k] + ([k_pe] if mla else []) + [v]
    if not mla:
        args += [cum_col, cum_row]
    return pl.pallas_call(
        body, name=name, grid=(H // 2, nq),
        in_specs=_attn_specs(S, H, T, mla, *cols),
        out_specs=[pl.BlockSpec((T, 2 * HEAD_DIM), lambda p, i: (i, p)),
                   pl.BlockSpec((2, T, LANE), lambda p, i: (p, i, 0))],
        out_shape=[jax.ShapeDtypeStruct((S, H * HEAD_DIM), MXU_DTYPE), jax.ShapeDtypeStruct((H, S, LANE), F32)],
        scratch_shapes=[pltpu.VMEM((T, LANE), F32), pltpu.VMEM((T, LANE), F32), pltpu.VMEM((T, HEAD_DIM), F32)],
        compiler_params=_params(),
    )(*args)


def _carry_split(refs, n_in, n_out, n_scr, carry):
    if carry is None:
        return refs, (), (), ()
    ci, co = len(carry["ins"]), len(carry["out_shape"])
    own = refs[:n_in] + refs[n_in + ci:n_in + ci + n_out] + refs[n_in + ci + n_out + co:n_in + ci + n_out + co + n_scr]
    return (own, refs[n_in:n_in + ci], refs[n_in + ci + n_out:n_in + ci + n_out + co],
            refs[n_in + ci + n_out + co + n_scr:])


def _carry_run(carry, c_in, c_out, c_sems, when, start):
    if carry is None:
        return

    @pl.when(when)
    def _():
        for cp in carry["copies"](c_in, c_out, *c_sems):
            if start:
                cp.start()
            else:
                cp.wait()


def _carry_call_args(carry, n_in, n_out):
    if carry is None:
        return [], [], [], [], {}
    sems = [pltpu.SemaphoreType.DMA(carry["sems"]), pltpu.SemaphoreType.DMA(carry["sems"])]
    alias = {n_in + i: n_out + o for i, o in carry.get("alias", {}).items()}
    return [ANY] * len(carry["ins"]), [ANY] * len(carry["out_shape"]), list(carry["out_shape"]), sems, alias


def _attn_bwd(name, S, H, mla, q, k, v, do, lse, q_pe=None, k_pe=None, cum_col=None, cum_row=None,
              cols=(0, 0, 0), carry=None):
    T = _pick(S, (ATT_T, 128))
    nq = S // T
    scale = (HEAD_DIM + ROPE_DIM) ** -0.5 if mla else HEAD_DIM ** -0.5
    gran = CHUNK if mla else 1
    dqk = HEAD_DIM + (ROPE_DIM if mla else 0)

    def body(*refs):
        refs, c_in, c_out, c_sems = _carry_split(refs, 7, 5, 3, carry)
        if mla:
            (q_ref, qpe_ref, k_ref, kpe_ref, v_ref, do_ref, lse_ref,
             dq_ref, dk_ref, dv_ref, dqpe_ref, dkpe_ref, dq_s, dl_s, r_s) = refs
        else:
            (q_ref, k_ref, v_ref, cc_ref, cr_ref, do_ref, lse_ref,
             dq_ref, dk_ref, dv_ref, dck_ref, dcq_ref, dq_s, dl_s, r_s) = refs
        hp, qi = pl.program_id(0), pl.program_id(1)
        _carry_run(carry, c_in, c_out, c_sems, jnp.logical_and(hp == 0, qi == 0), True)

        @pl.when(qi == 0)
        def _():
            dk_ref[...] = jnp.zeros(dk_ref.shape, F32)
            dv_ref[...] = jnp.zeros(dv_ref.shape, F32)
            if not mla:
                dck_ref[...] = jnp.zeros(dck_ref.shape, F32)

        if mla:
            @pl.when(jnp.logical_and(qi == 0, hp == 0))
            def _():
                dkpe_ref[...] = jnp.zeros(dkpe_ref.shape, F32)

        for hh in range(2):
            hl = slice(hh * HEAD_DIM, (hh + 1) * HEAD_DIM)
            qf = q_ref[:, hl]
            if mla:
                qf = jnp.concatenate([qf, qpe_ref[:, hh * ROPE_DIM:(hh + 1) * ROPE_DIM]], axis=-1)
            dof = do_ref[:, hl]
            lse = lse_ref[hh][:, 0:1]
            cq = None if mla else cc_ref[hh]
            dq_s[...] = jnp.zeros(dq_s.shape, F32)
            dl_s[...] = jnp.zeros(dl_s.shape, F32)
            r_s[...] = jnp.zeros(r_s.shape, F32)

            def p_dp(j, masked):
                rows = pl.ds(pl.multiple_of(j * T, T), T)
                kf = k_ref[rows, hl]
                if mla:
                    kf = jnp.concatenate([kf, kpe_ref[rows, :]], axis=-1)
                s = _dot_nt(qf, kf) * scale
                if not mla:
                    s = s + (cq - cr_ref[hh, :, rows])
                if masked:
                    s = jnp.where(_diag_mask(T, gran), s, NEG)
                return rows, kf, jnp.exp(s - lse), _dot_nt(dof, v_ref[rows, hl])

            def sweep1(j, masked):
                rows, _, p, dp = p_dp(j, masked)
                dl_s[...] += jnp.broadcast_to(jnp.sum(p * dp, axis=-1, keepdims=True), dl_s.shape)
                dv_ref[rows, hl] += _dot_tn(p.astype(MXU_DTYPE), dof)

            def sweep2(j, masked):
                rows, kf, p, dp = p_dp(j, masked)
                ds = p * (dp - dl_s[:, 0:1])
                dsb = (ds * scale).astype(MXU_DTYPE)
                dkf = _dot_tn(dsb, qf)
                dk_ref[rows, hl] += dkf[:, :HEAD_DIM]
                if mla:
                    dkpe_ref[rows, :] += dkf[:, HEAD_DIM:]
                else:
                    dck_ref[hh, :, rows] -= jnp.sum(ds, axis=0, keepdims=True)
                    r_s[...] += jnp.broadcast_to(jnp.sum(ds, axis=-1, keepdims=True), r_s.shape)
                dq_s[...] += _dot_nn(dsb, kf)

            for sweep in (sweep1, sweep2):
                lax.fori_loop(0, qi, lambda j, c, sweep=sweep: (sweep(j, False), c)[1], 0)
                sweep(qi, True)
            dq_ref[:, hl] = dq_s[:, :HEAD_DIM].astype(dq_ref.dtype)
            if mla:
                dqpe_ref[:, hh * ROPE_DIM:(hh + 1) * ROPE_DIM] = dq_s[:, HEAD_DIM:]
            else:
                dcq_ref[hh] = r_s[:, 0:1]
        _carry_run(carry, c_in, c_out, c_sems, jnp.logical_and(hp == H // 2 - 1, qi == nq - 1), False)

    W = 2 * HEAD_DIM
    args = [q] + ([q_pe] if mla else []) + [k] + ([k_pe] if mla else []) + [v]
    if not mla:
        args += [cum_col, cum_row]
    args += [do, lse]
    in_specs = _attn_specs(S, H, T, mla, *cols)
    in_specs += [pl.BlockSpec((T, W), lambda p, i: (i, p)), pl.BlockSpec((2, T, LANE), lambda p, i: (p, i, 0))]
    out_specs = [pl.BlockSpec((T, W), lambda p, i: (i, p)), pl.BlockSpec((S, W), lambda p, i: (0, p)),
                 pl.BlockSpec((S, W), lambda p, i: (0, p))]
    out_shape = [jax.ShapeDtypeStruct((S, H * HEAD_DIM), MXU_DTYPE), jax.ShapeDtypeStruct((S, H * HEAD_DIM), F32),
                 jax.ShapeDtypeStruct((S, H * HEAD_DIM), F32)]
    if mla:
        out_specs += [pl.BlockSpec((T, 2 * ROPE_DIM), lambda p, i: (i, p)),
                      pl.BlockSpec((S, ROPE_DIM), lambda p, i: (0, 0))]
        out_shape += [jax.ShapeDtypeStruct((S, H * ROPE_DIM), F32), jax.ShapeDtypeStruct((S, ROPE_DIM), F32)]
    else:
        out_specs += [pl.BlockSpec((2, 1, S), lambda p, i: (p, 0, 0)), pl.BlockSpec((2, T, 1), lambda p, i: (p, i, 0))]
        out_shape += [jax.ShapeDtypeStruct((H, 1, S), F32), jax.ShapeDtypeStruct((H, S, 1), F32)]
    assert len(args) == 7 and len(out_shape) == 5
    c_is, c_os, c_shape, c_sems, c_alias = _carry_call_args(carry, 7, 5)
    res = pl.pallas_call(
        body, name=name, grid=(H // 2, nq), in_specs=in_specs + c_is, out_specs=out_specs + c_os,
        out_shape=out_shape + c_shape, input_output_aliases=c_alias,
        scratch_shapes=[pltpu.VMEM((T, dqk), F32), pltpu.VMEM((T, LANE), F32), pltpu.VMEM((T, LANE), F32)] + c_sems,
        compiler_params=_params(),
    )(*args, *(carry["ins"] if carry else ()))
    return (res[:5], res[5:]) if carry else res


def _mla_prep(lat, cos, sin, qn, kvn, ql, kvl):
    def fn(lat, c, s, qn, kvn):
        ql_ = lat[:, :ql]
        kv_ = lat[:, ql:ql + kvl]
        kp = lat[:, ql + kvl:]
        cq = ql_ * lax.rsqrt(jnp.mean(ql_ * ql_, axis=-1, keepdims=True) + RMS_EPS) * qn
        ckv = kv_ * lax.rsqrt(jnp.mean(kv_ * kv_, axis=-1, keepdims=True) + RMS_EPS) * kvn
        kp2 = jnp.concatenate([kp, jnp.zeros_like(kp)], axis=-1)
        kr = _rope_fwd(kp2, c, s)[:, :ROPE_DIM]
        return cq, ckv, kr

    return _rowwise("mla_prep", fn, [lat, cos, sin], [qn, kvn],
                    [(ql, MXU_DTYPE), (kvl, MXU_DTYPE), (ROPE_DIM, MXU_DTYPE)])


def _mla_prep_bwd(lat, cos, sin, qn, kvn, dcq, dckv, dkr, ql, kvl):
    def fn(lat, c, s, dcq, dckv, dkr, qn, kvn):
        outs, reds = [], []
        for (x, g, d) in ((lat[:, :ql], qn, dcq), (lat[:, ql:ql + kvl], kvn, dckv)):
            r = lax.rsqrt(jnp.mean(x * x, axis=-1, keepdims=True) + RMS_EPS)
            n = x * r
            dn = d * g
            outs.append(r * (dn - n * jnp.mean(dn * n, axis=-1, keepdims=True)))
            reds.append(_colsum(d * n))
        d2 = jnp.concatenate([dkr, jnp.zeros_like(dkr)], axis=-1)
        outs.append(_rope_bwd(d2, c, s)[:, :ROPE_DIM])
        return (jnp.concatenate(outs, axis=-1), *reds)

    return _rowwise("mla_prep_bwd", fn, [lat, cos, sin, dcq, dckv, dkr], [qn, kvn],
                    [(ql + kvl + ROPE_DIM, MXU_DTYPE)], [ql, kvl])


def _mla_fwd(h, w, rc):
    S = h.shape[0]
    ql, kvl = w["q_norm"].shape[1], w["kv_norm"].shape[1]
    H = _wdim(w["w_uk"], 1) // HEAD_DIM
    n_nope, n_pe = H * HEAD_DIM, H * ROPE_DIM
    lat = _mm1("mla_down", h, w["w_down"])
    cq, ckv, kr = _mla_prep(lat, rc[0], rc[1], w["q_norm"], w["kv_norm"], ql, kvl)
    q_nope = _mm1("mla_uq_nope", cq, w["w_uq"], out_dtype=MXU_DTYPE, N=n_nope)
    q_pe = _mm("mla_uq_pe", [(cq, w["w_uq"], False, False, 0, n_nope // LANE)], S, n_pe, [MXU_DTYPE],
               epilogue=lambda accs, c, s: (_rope_fwd(accs[0], c, s),), row_extras=rc, tn=LANE)[0]
    k_nope, v = _mm("mla_ukv", [(ckv, w["w_uk"], False, False, 0, 0), (ckv, w["w_uv"], False, False, 1, 0)],
                    S, n_nope, [MXU_DTYPE, MXU_DTYPE], epilogue=lambda accs: (accs[0], accs[1]))
    o, lse = _attn_fwd("mla_attn_fwd", S, H, True, q_nope, k_nope, v, q_pe=q_pe, k_pe=kr)
    y = _mm1("mla_wo", o, w["w_o"])
    return y, (lat, cq, ckv, kr, q_nope, q_pe, k_nope, v, o, lse)


def _mla_bwd(dy, h, saved, w, rc, carry):
    lat, cq, ckv, kr, q_nope, q_pe, k_nope, v, o, lse = saved
    S = h.shape[0]
    ql, kvl = w["q_norm"].shape[1], w["kv_norm"].shape[1]
    H = _wdim(w["w_uk"], 1) // HEAD_DIM
    n_nope, n_pe = H * HEAD_DIM, H * ROPE_DIM
    do = _mm1("mla_do", dy, w["w_o"], tb=True, out_dtype=MXU_DTYPE)
    dw_o = _mm1("mla_dwo", o, dy, ta=True, out_dtype=WIRE_DTYPE)
    (dq_nope, dk_nope, dv, dq_pe_r, dk_pe_r), carried = _attn_bwd(
        "mla_attn_bwd", S, H, True, q_nope, k_nope, v, do, lse, q_pe=q_pe, k_pe=kr, carry=carry)

    def unrope(d, c, s):
        reps = (1, n_pe // LANE)
        return (_rope_bwd(d, jnp.tile(c, reps), jnp.tile(s, reps)),)

    dq_pe = _rowwise("mla_unrope_q", unrope, [dq_pe_r, rc[0], rc[1]], [], [(n_pe, MXU_DTYPE)])[0]
    dq = jnp.concatenate([dq_nope, dq_pe], axis=1)
    dw_uq = _mm1("mla_dwuq", cq, dq, ta=True, out_dtype=WIRE_DTYPE)
    dcq = _mm1("mla_dcq", dq, w["w_uq"], tb=True)
    dw_uk, dw_uv = _mm("mla_dwukv", [(ckv, dk_nope, True, False, 0, 0), (ckv, dv, True, False, 1, 0)], kvl, n_nope,
                       [WIRE_DTYPE, WIRE_DTYPE], epilogue=lambda accs: (accs[0], accs[1]))
    dckv = _mm("mla_dckv", [(dk_nope, w["w_uk"], False, True, 0, 0), (dv, w["w_uv"], False, True, 0, 0)],
               S, kvl, [F32])[0]
    dlat, dqn, dkvn = _mla_prep_bwd(lat, rc[0], rc[1], w["q_norm"], w["kv_norm"], dcq, dckv, dk_pe_r, ql, kvl)
    dw_down = _mm1("mla_dwdown", h, dlat, ta=True, out_dtype=WIRE_DTYPE)
    dh = _mm1("mla_dh", dlat, w["w_down"], tb=True)
    return dh, dict(w_down=dw_down, q_norm=dqn, w_uq=dw_uq, kv_norm=dkvn, w_uk=dw_uk, w_uv=dw_uv, w_o=dw_o), carried


def _log_sigmoid(z):
    return jnp.minimum(z, 0.0) - jnp.log(1.0 + jnp.exp(-jnp.abs(z)))


def _fox_gate_fwd(f, bf):
    S = f.shape[0]
    B = LANE

    def body(f_ref, b_ref, cum_ref):
        r = lax.broadcasted_iota(jnp.int32, (B, B), 0)
        c = lax.broadcasted_iota(jnp.int32, (B, B), 1)
        tri = (r >= c).astype(F32)
        carry = jnp.zeros((1, LANE), F32)
        for blk in range(S // B):
            rows = slice(blk * B, (blk + 1) * B)
            lf = _log_sigmoid(f_ref[rows, :] + b_ref[...])
            cs = jnp.dot(tri, lf, precision=lax.Precision.HIGHEST, preferred_element_type=F32) + carry
            cum_ref[rows, :] = cs
            carry = cs[B - 1:B, :]

    return pl.pallas_call(body, name="fox_gate_fwd", out_shape=jax.ShapeDtypeStruct((S, LANE), F32),
                          compiler_params=_params())(f, bf)


def _fox_gate_bwd(dcum, f, bf):
    S = f.shape[0]
    B = LANE

    def body(d_ref, f_ref, b_ref, df_ref, db_ref):
        r = lax.broadcasted_iota(jnp.int32, (B, B), 0)
        c = lax.broadcasted_iota(jnp.int32, (B, B), 1)
        tri = (r <= c).astype(F32)
        carry = jnp.zeros((1, LANE), F32)
        db = jnp.zeros((1, LANE), F32)
        for blk in reversed(range(S // B)):
            rows = slice(blk * B, (blk + 1) * B)
            dlf = jnp.dot(tri, d_ref[rows, :], precision=lax.Precision.HIGHEST, preferred_element_type=F32) + carry
            carry = dlf[0:1, :]
            z = f_ref[rows, :] + b_ref[...]
            dz = dlf * _sigmoid(-z)
            df_ref[rows, :] = dz.astype(df_ref.dtype)
            db = db + jnp.sum(dz, axis=0, keepdims=True)
        db_ref[...] = db

    return pl.pallas_call(body, name="fox_gate_bwd",
                          out_shape=[jax.ShapeDtypeStruct((S, LANE), MXU_DTYPE), jax.ShapeDtypeStruct((1, LANE), F32)],
                          compiler_params=_params())(dcum, f, bf)


def _fox_fwd(h, w):
    S, D = h.shape
    H = D // HEAD_DIM
    qkv = _mm1("fox_qkv", h, w["w_in"], out_dtype=MXU_DTYPE, N=3 * D)
    f = _mm1("fox_f", h, w["w_in"], bcol=3 * D // LANE, N=LANE, tn=LANE)
    cum = _fox_gate_fwd(f, w["b_f"])
    cumT = cum[:, :H].T
    cum_col, cum_row = cumT.reshape(H, S, 1), cumT.reshape(H, 1, S)
    nb = D // (2 * HEAD_DIM)
    o, lse = _attn_fwd("fox_attn_fwd", S, H, False, qkv, qkv, qkv, cum_col=cum_col, cum_row=cum_row,
                       cols=(0, nb, 2 * nb))
    y = _mm1("fox_wo", o, w["w_o"])
    return y, (qkv, f, cum_col, cum_row, o, lse)


def _fox_bwd(dy, h, saved, w, carry):
    qkv, f, cum_col, cum_row, o, lse = saved
    S, D = h.shape
    H = D // HEAD_DIM
    nb = D // (2 * HEAD_DIM)
    do = _mm1("fox_do", dy, w["w_o"], tb=True, out_dtype=MXU_DTYPE)
    dw_o = _mm1("fox_dwo", o, dy, ta=True, out_dtype=WIRE_DTYPE)
    (dq, dk, dv, dck, dcq), carried = _attn_bwd("fox_attn_bwd", S, H, False, qkv, qkv, qkv, do, lse, cum_col=cum_col,
                                                cum_row=cum_row, cols=(0, nb, 2 * nb), carry=carry)
    dcum = jnp.pad((dck.reshape(H, S) + dcq.reshape(H, S)).T, ((0, 0), (0, LANE - H)))
    df, dbf = _fox_gate_bwd(dcum, f, w["b_f"])
    dproj = jnp.concatenate([dq, dk.astype(MXU_DTYPE), dv.astype(MXU_DTYPE), df], axis=1)
    dw_in = _mm1("fox_dwin", h, dproj, ta=True, out_dtype=WIRE_DTYPE)
    dh = _mm1("fox_dh", dproj, w["w_in"], tb=True, tn=256)
    return dh, dict(w_in=dw_in, b_f=dbf[:, :H], w_o=dw_o), carried


def _place():
    x, y, c = lax.axis_index("x"), lax.axis_index("y"), lax.axis_index("c")
    return x, y, c, [(1 - x, y), (x, 1 - y), (1 - x, 1 - y)]


def _ag_small(name, blk):
    m, n = blk.shape

    def body(x_ref, out_ref, send_sems, recv_sems, local_sem):
        x, y, c, chips = _place()
        me, sibling = (x, y, c), (x, y, 1 - c)

        def rows(px, py, pc):
            return out_ref.at[pl.ds((4 * px + 2 * py + pc) * m, m), :]

        def copy(k, block, to, src=None):
            return pltpu.make_async_remote_copy(
                src_ref=rows(*block) if src is None else src, dst_ref=rows(*block),
                send_sem=send_sems.at[k], recv_sem=recv_sems.at[k], device_id=to, device_id_type=MESH)

        mine = pltpu.make_async_copy(x_ref, rows(*me), local_sem)
        mine.start()
        first = [copy(0, me, sibling, src=x_ref)]
        first += [copy(1 + j, me, (*chip, c), src=x_ref) for j, chip in enumerate(chips)]
        for cp in first:
            cp.start()
        passed = [copy(4 + j, (*chip, c), sibling) for j, chip in enumerate(chips)]
        for j, chip in enumerate(chips):
            copy(1 + j, (*chip, c), me).wait_recv()
            passed[j].start()
        copy(0, sibling, me).wait_recv()
        for j, chip in enumerate(chips):
            copy(4 + j, (*chip, 1 - c), me).wait_recv()
        for cp in first + passed:
            cp.wait_send()
        mine.wait()

    return pl.pallas_call(
        body, name=name, out_shape=jax.ShapeDtypeStruct((N_DEV * m, n), blk.dtype),
        in_specs=[pl.BlockSpec(memory_space=pltpu.VMEM)], out_specs=pl.BlockSpec(memory_space=pltpu.VMEM),
        scratch_shapes=[pltpu.SemaphoreType.DMA((7,)), pltpu.SemaphoreType.DMA((7,)), pltpu.SemaphoreType.DMA],
        compiler_params=_params(),
    )(blk)


def _half(ref, row_axis, c, rows):
    idx = [slice(None)] * len(ref.shape)
    idx[row_axis] = pl.ds(pl.multiple_of(c * rows, 16), rows)
    return ref.at[tuple(idx)]


def _shard(ref, layout, k):
    if layout == "row":
        return ref.at[:, k]
    w = ref.shape[2] // N_CHIPS
    return ref.at[:, :, pl.ds(pl.multiple_of(k * w, LANE), w)]


def _full_shape(shape, layout):
    L, r, w = shape
    return (L, N_CHIPS, r, w) if layout == "row" else (L, r, N_CHIPS * w)


def _cast_full(name, a, layout, k_idx):
    L, r, C = a.shape
    tr = _pick(r, (256, 128, 64, 32, 16))

    def body(k_ref, a_ref, o_ref):
        o_ref[...] = a_ref[...].astype(o_ref.dtype)

    if layout == "row":
        o_spec = pl.BlockSpec((None, None, tr, C), lambda l, i, k: (l, k[0], i, 0))
    else:
        o_spec = pl.BlockSpec((None, tr, C), lambda l, i, k: (l, i, k[0]))
    return pl.pallas_call(
        body, name=name,
        grid_spec=pltpu.PrefetchScalarGridSpec(
            num_scalar_prefetch=1, grid=(L, r // tr),
            in_specs=[pl.BlockSpec((None, tr, C), lambda l, i, k: (l, i, 0))], out_specs=o_spec),
        out_shape=jax.ShapeDtypeStruct(_full_shape(a.shape, layout), WIRE_DTYPE),
        compiler_params=_params(),
    )(k_idx, a)


def _gather_weights(fulls, layouts):
    n = len(fulls)
    half_rows = [f.shape[2 if lay == "row" else 1] // 2 for f, lay in zip(fulls, layouts)]

    def body(*refs):
        outs = refs[n:2 * n]
        send_sems, recv_sems = refs[2 * n:]
        x, y, c, chips = _place()
        sibling = (x, y, 1 - c)

        def window(i, kx, ky, half):
            return _half(_shard(outs[i], layouts[i], 2 * kx + ky), 1, half, half_rows[i])

        first, passed = [], []
        for i in range(n):
            mine = window(i, x, y, c)
            for j, chip in enumerate(chips):
                cp = pltpu.make_async_remote_copy(
                    src_ref=mine, dst_ref=mine, send_sem=send_sems.at[i, j], recv_sem=recv_sems.at[i, j],
                    device_id=(*chip, c), device_id_type=MESH)
                cp.start()
                first.append(cp)
        for i in range(n):
            for j, chip in enumerate(chips):
                got = window(i, *chip, c)
                pltpu.make_async_remote_copy(
                    src_ref=got, dst_ref=got, send_sem=send_sems.at[i, j], recv_sem=recv_sems.at[i, j],
                    device_id=(*chip, c), device_id_type=MESH).wait_recv()
                cp = pltpu.make_async_remote_copy(
                    src_ref=got, dst_ref=got, send_sem=send_sems.at[i, 3 + j], recv_sem=recv_sems.at[i, 3 + j],
                    device_id=sibling, device_id_type=MESH)
                cp.start()
                passed.append(cp)
        for i in range(n):
            for j, chip in enumerate(chips):
                got = window(i, *chip, 1 - c)
                pltpu.make_async_remote_copy(
                    src_ref=got, dst_ref=got, send_sem=send_sems.at[i, 3 + j], recv_sem=recv_sems.at[i, 3 + j],
                    device_id=sibling, device_id_type=MESH).wait_recv()
        for cp in first + passed:
            cp.wait_send()

    return pl.pallas_call(
        body, name="gather_weights", in_specs=[ANY] * n, out_specs=[ANY] * n,
        out_shape=[jax.ShapeDtypeStruct(f.shape, f.dtype) for f in fulls],
        input_output_aliases={i: i for i in range(n)},
        scratch_shapes=[pltpu.SemaphoreType.DMA((n, 6)), pltpu.SemaphoreType.DMA((n, 6))],
        compiler_params=_params(),
    )(*fulls)


def _half_shape(shape, layout):
    s = list(shape)
    s[2 if layout == "row" else 1] //= 2
    return tuple(s)


def _swap_halves(name, grads, layouts):
    n = len(grads)
    row_axis = [2 if lay == "row" else 1 for lay in layouts]
    half_rows = [g.shape[ra] // 2 for g, ra in zip(grads, row_axis)]

    def body(*refs):
        ins, outs = refs[:n], refs[n:2 * n]
        send_sems, recv_sems = refs[2 * n:]
        x, y, c, _ = _place()
        cps = []
        for i in range(n):
            cp = pltpu.make_async_remote_copy(
                src_ref=_half(ins[i], row_axis[i], 1 - c, half_rows[i]), dst_ref=outs[i],
                send_sem=send_sems.at[i], recv_sem=recv_sems.at[i], device_id=(x, y, 1 - c), device_id_type=MESH)
            cp.start()
            cps.append(cp)
        for cp in cps:
            cp.wait()

    return pl.pallas_call(
        body, name=name, in_specs=[ANY] * n, out_specs=[ANY] * n,
        out_shape=[jax.ShapeDtypeStruct(_half_shape(g.shape, lay), g.dtype) for g, lay in zip(grads, layouts)],
        scratch_shapes=[pltpu.SemaphoreType.DMA((n,)), pltpu.SemaphoreType.DMA((n,))],
        compiler_params=_params(),
    )(*grads)


def _add_half(name, g, r, layout, c_idx):
    L = g.shape[0]
    if layout == "row":
        A, rows, W = L * N_CHIPS, g.shape[2] // 2, g.shape[3]
    else:
        A, rows, W = L, g.shape[1] // 2, g.shape[2]
    g3 = g.reshape(A, 2 * rows, W)
    r3 = r.reshape(A, rows, W)
    tr = _pick(rows, (256, 128, 64, 32, 16))
    nb = rows // tr

    def body(c_ref, g_ref, r_ref, o_ref):
        o_ref[...] = (g_ref[...].astype(F32) + r_ref[...].astype(F32)).astype(o_ref.dtype)

    out = pl.pallas_call(
        body, name=name,
        grid_spec=pltpu.PrefetchScalarGridSpec(
            num_scalar_prefetch=1, grid=(A, nb),
            in_specs=[pl.BlockSpec((None, tr, W), lambda a, i, c: (a, c[0] * nb + i, 0)),
                      pl.BlockSpec((None, tr, W), lambda a, i, c: (a, i, 0))],
            out_specs=pl.BlockSpec((None, tr, W), lambda a, i, c: (a, i, 0))),
        out_shape=jax.ShapeDtypeStruct((A, rows, W), WIRE_DTYPE),
        compiler_params=_params(),
    )(c_idx, g3, r3)
    return out.reshape(r.shape)


def _exchange_carry(parts, layouts):
    n = len(parts)

    def shard_half_shape(p, lay):
        if lay == "row":
            return (p.shape[0],) + p.shape[2:]
        return (p.shape[0], p.shape[1], p.shape[2] // N_CHIPS)

    def copies(ins, outs, send_sems, recv_sems):
        x, y, c, chips = _place()
        return [pltpu.make_async_remote_copy(
            src_ref=_shard(ins[i], layouts[i], 2 * kx + ky), dst_ref=outs[i].at[j], send_sem=send_sems.at[i, j],
            recv_sem=recv_sems.at[i, j], device_id=(kx, ky, c), device_id_type=MESH)
            for i in range(n) for j, (kx, ky) in enumerate(chips)]

    return dict(ins=list(parts), sems=(n, 3), copies=copies,
                out_shape=[jax.ShapeDtypeStruct((3,) + shard_half_shape(p, lay), p.dtype)
                           for p, lay in zip(parts, layouts)])


def _exchange_shards(parts, layouts):
    n = len(parts)
    carry = _exchange_carry(parts, layouts)

    def body(*refs):
        cps = carry["copies"](refs[:n], refs[n:2 * n], *refs[2 * n:])
        for cp in cps:
            cp.start()
        for cp in cps:
            cp.wait()

    return pl.pallas_call(
        body, name="exchange_shards", in_specs=[ANY] * n, out_specs=[ANY] * n, out_shape=carry["out_shape"],
        scratch_shapes=[pltpu.SemaphoreType.DMA((n, 3)), pltpu.SemaphoreType.DMA((n, 3))],
        compiler_params=_params(),
    )(*parts)


def _sum_shards(name, p, r, layout, kc_idx, dst, lyr, n_lyr):
    rows, W = r.shape[2], r.shape[3]
    tr = _pick(rows, (256, 128, 64, 32, 16))
    nb = rows // tr

    def body(kc_ref, p_ref, r_ref, *rest):
        acc = p_ref[...].astype(F32)
        for j in range(3):
            acc = acc + r_ref[j].astype(F32)
        rest[-1][...] = acc

    if layout == "row":
        p_spec = pl.BlockSpec((None, None, tr, W), lambda a, i, kc: (0, kc[0], i, 0))
    else:
        p_spec = pl.BlockSpec((None, tr, W), lambda a, i, kc: (0, i, kc[0]))
    in_specs = [p_spec, pl.BlockSpec((3, None, tr, W), lambda a, i, kc: (0, 0, i, 0))]
    args = [kc_idx, p, r]
    if dst is not None:
        in_specs.append(ANY)
        args.append(dst)
    return pl.pallas_call(
        body, name=name,
        grid_spec=pltpu.PrefetchScalarGridSpec(
            num_scalar_prefetch=1, grid=(1, nb), in_specs=in_specs,
            out_specs=pl.BlockSpec((None, tr, W), lambda a, i, kc: (lyr, kc[1] * nb + i, 0))),
        out_shape=jax.ShapeDtypeStruct((n_lyr, 2 * rows, W), F32),
        input_output_aliases={3: 0} if dst is not None else {},
        compiler_params=_params(),
    )(*args)


def _join_halves(shards):
    n = len(shards)

    def body(*refs):
        outs = refs[n:2 * n]
        send_sems, recv_sems = refs[2 * n:]
        x, y, c, _ = _place()
        cps = []
        for i in range(n):
            mine = _half(outs[i], 1, c, outs[i].shape[1] // 2)
            cp = pltpu.make_async_remote_copy(
                src_ref=mine, dst_ref=mine, send_sem=send_sems.at[i], recv_sem=recv_sems.at[i],
                device_id=(x, y, 1 - c), device_id_type=MESH)
            cp.start()
            cps.append(cp)
        for cp in cps:
            cp.wait()

    return pl.pallas_call(
        body, name="join_halves", in_specs=[ANY] * n, out_specs=[ANY] * n,
        out_shape=[jax.ShapeDtypeStruct(s.shape, s.dtype) for s in shards],
        input_output_aliases={i: i for i in range(n)},
        scratch_shapes=[pltpu.SemaphoreType.DMA((n,)), pltpu.SemaphoreType.DMA((n,))],
        compiler_params=_params(),
    )(*shards)


def _ada_fwd(c_all, ada_w, ada_b):
    L, D, w = ada_w.shape
    tn = _pick(w, (512, 256, 128))

    def body(c_ref, w_ref, b_ref, o_ref, a_ref):
        c = c_ref[...]
        act = (c * _sigmoid(c)).astype(MXU_DTYPE)
        a_ref[...] = act
        o_ref[...] = jnp.dot(act, w_ref[...].astype(MXU_DTYPE), preferred_element_type=F32) + b_ref[...]

    return pl.pallas_call(
        body, name="ada_fwd", grid=(L, w // tn),
        in_specs=[pl.BlockSpec((16, D), lambda l, j: (0, 0)), pl.BlockSpec((None, D, tn), lambda l, j: (l, 0, j)),
                  pl.BlockSpec((None, 1, tn), lambda l, j: (l, 0, j))],
        out_specs=[pl.BlockSpec((None, 16, tn), lambda l, j: (l, 0, j)), pl.BlockSpec((16, D), lambda l, j: (0, 0))],
        out_shape=[jax.ShapeDtypeStruct((L, 16, w), F32), jax.ShapeDtypeStruct((16, D), MXU_DTYPE)],
        compiler_params=_params(),
    )(c_all, ada_w, ada_b)


def _sum_devices(name, parts):
    n, R, W = parts.shape
    tw = _pick(W, (2048, 1024, 512, 256, 128))

    def body(p_ref, o_ref):
        acc = p_ref[0]
        for d in range(1, n):
            acc = acc + p_ref[d]
        o_ref[...] = acc

    return pl.pallas_call(
        body, name=name, grid=(W // tw,), in_specs=[pl.BlockSpec((n, R, tw), lambda j: (0, 0, j))],
        out_specs=pl.BlockSpec((R, tw), lambda j: (0, j)), out_shape=jax.ShapeDtypeStruct((R, W), F32),
        compiler_params=_params(),
    )(parts)


def _adamw(name, w, g, m, v):
    W = w.shape[1]
    bc1 = 1.0 - ADAM_B1 ** ADAM_STEP
    bc2 = 1.0 - ADAM_B2 ** ADAM_STEP

    def fn(w, g, m, v):
        m2 = ADAM_B1 * m + (1.0 - ADAM_B1) * g
        v2 = ADAM_B2 * v + (1.0 - ADAM_B2) * (g * g)
        delta = -ADAM_LR * ((m2 / bc1) / (jnp.sqrt(v2 / bc2) + ADAM_EPS) + ADAM_WD * w)
        return delta, m2, v2

    tr = 256 if W <= 1024 else (128 if W <= 2048 else 64)
    return _rowwise(name, fn, [w, g, m, v], [], [(W, F32)] * 3, tr=tr)


def _cast(name, a):
    W = a.shape[1]
    tr = 256 if W <= 2048 else 128
    return _rowwise(name, lambda a: (a,), [a], [], [(W, WIRE_DTYPE)], tr=tr)[0]


def _flat(a):
    return a.reshape(-1, a.shape[-1])


BIG = ("ffn_w1", "ffn_w3", "ffn_w2", "mla_w_down", "mla_w_uq", "mla_w_uk", "mla_w_uv", "mla_w_o", "fox_w_in",
       "fox_w_o")
LAYOUT = dict(ffn_w1="col", ffn_w3="col", ffn_w2="row", mla_w_down="row", mla_w_uq="col", mla_w_uk="col",
              mla_w_uv="col", mla_w_o="row", fox_w_in="row", fox_w_o="row")
SMALL = ("ln1_g", "ln1_b", "ln2_g", "ln2_b", "mla_q_norm", "mla_kv_norm", "fox_b_f")
WEIGHTS = ("ada_w", "ada_b", "ln1_g", "ln1_b", "ln2_g", "ln2_b", "ffn_w1", "ffn_w3", "ffn_w2", "mla_w_down",
           "mla_q_norm", "mla_w_uq", "mla_kv_norm", "mla_w_uk", "mla_w_uv", "mla_w_o", "fox_w_in", "fox_b_f",
           "fox_w_o")


def _uq_perm(H):
    d = HEAD_DIM + ROPE_DIM
    nope = (np.arange(H)[:, None] * d + np.arange(HEAD_DIM)[None, :]).reshape(-1)
    pe = (np.arange(H)[:, None] * d + HEAD_DIM + np.arange(ROPE_DIM)[None, :]).reshape(-1)
    return np.concatenate([nope, pe])


def kernel(x, c, positions, ada_w, ada_b, ln1_g, ln1_b, ln2_g, ln2_b, ffn_w1, ffn_w3, ffn_w2, mla_w_down, mla_q_norm, mla_w_uq, mla_kv_norm, mla_w_uk, mla_w_uv, mla_w_o, fox_w_in, fox_b_f, fox_w_o, loss_target, m_ada_w, m_ada_b, m_ln1_g, m_ln1_b, m_ln2_g, m_ln2_b, m_ffn_w1, m_ffn_w3, m_ffn_w2, m_mla_w_down, m_mla_q_norm, m_mla_w_uq, m_mla_kv_norm, m_mla_w_uk, m_mla_w_uv, m_mla_w_o, m_fox_w_in, m_fox_b_f, m_fox_w_o, v_ada_w, v_ada_b, v_ln1_g, v_ln1_b, v_ln2_g, v_ln2_b, v_ffn_w1, v_ffn_w3, v_ffn_w2, v_mla_w_down, v_mla_q_norm, v_mla_w_uq, v_mla_kv_norm, v_mla_w_uk, v_mla_w_uv, v_mla_w_o, v_fox_w_in, v_fox_b_f, v_fox_w_o):
    W = dict(ada_w=ada_w, ada_b=ada_b, ln1_g=ln1_g, ln1_b=ln1_b, ln2_g=ln2_g, ln2_b=ln2_b, ffn_w1=ffn_w1,
             ffn_w3=ffn_w3, ffn_w2=ffn_w2, mla_w_down=mla_w_down, mla_q_norm=mla_q_norm, mla_w_uq=mla_w_uq,
             mla_kv_norm=mla_kv_norm, mla_w_uk=mla_w_uk, mla_w_uv=mla_w_uv, mla_w_o=mla_w_o, fox_w_in=fox_w_in,
             fox_b_f=fox_b_f, fox_w_o=fox_w_o)
    Mo = dict(ada_w=m_ada_w, ada_b=m_ada_b, ln1_g=m_ln1_g, ln1_b=m_ln1_b, ln2_g=m_ln2_g, ln2_b=m_ln2_b,
              ffn_w1=m_ffn_w1, ffn_w3=m_ffn_w3, ffn_w2=m_ffn_w2, mla_w_down=m_mla_w_down, mla_q_norm=m_mla_q_norm,
              mla_w_uq=m_mla_w_uq, mla_kv_norm=m_mla_kv_norm, mla_w_uk=m_mla_w_uk, mla_w_uv=m_mla_w_uv,
              mla_w_o=m_mla_w_o, fox_w_in=m_fox_w_in, fox_b_f=m_fox_b_f, fox_w_o=m_fox_w_o)
    Vo = dict(ada_w=v_ada_w, ada_b=v_ada_b, ln1_g=v_ln1_g, ln1_b=v_ln1_b, ln2_g=v_ln2_g, ln2_b=v_ln2_b,
              ffn_w1=v_ffn_w1, ffn_w3=v_ffn_w3, ffn_w2=v_ffn_w2, mla_w_down=v_mla_w_down, mla_q_norm=v_mla_q_norm,
              mla_w_uq=v_mla_w_uq, mla_kv_norm=v_mla_kv_norm, mla_w_uk=v_mla_w_uk, mla_w_uv=v_mla_w_uv,
              mla_w_o=v_mla_w_o, fox_w_in=v_fox_w_in, fox_b_f=v_fox_b_f, fox_w_o=v_fox_w_o)

    S, D = x.shape[1], x.shape[2]
    L = ada_w.shape[0]
    n_mla, n_fox = mla_w_down.shape[0], fox_w_o.shape[0]
    alpha = float((2 * L) ** 0.25)
    H_mla = mla_w_uk.shape[2] * N_CHIPS // HEAD_DIM
    H_fox = D // HEAD_DIM
    xi, yi, ci = lax.axis_index("x"), lax.axis_index("y"), lax.axis_index("c")
    chip = 2 * xi + yi
    dev = 2 * chip + ci
    c_idx = jnp.reshape(ci, (1,)).astype(jnp.int32)
    k_idx = jnp.reshape(chip, (1,)).astype(jnp.int32)
    x0, tgt = x[0], loss_target[0]
    pos = positions.reshape(S, 1)

    c_all = _ag_small("gather_c", jnp.pad(c, ((0, 7), (0, 0)))).reshape(N_DEV, 8, D)[:, 0]
    w_ada = ada_w.shape[2]
    ada_b_sh = lax.dynamic_slice_in_dim(ada_b, chip * w_ada, w_ada, axis=1).reshape(L, 1, w_ada)
    mod_sh, c_act = _ada_fwd(jnp.pad(c_all, ((0, 8), (0, 0))), ada_w, ada_b_sh)
    mod_all = _ag_small("gather_mod", mod_sh.transpose(1, 0, 2).reshape(16, L * w_ada))
    mod_all = mod_all.reshape(N_CHIPS, 2, 16, L, w_ada)[:, 0]
    mod = lax.dynamic_index_in_dim(mod_all, dev, axis=1, keepdims=False)
    mod = mod.transpose(1, 0, 2).reshape(L, 6, 1, D)

    kc_idx = jnp.stack([chip, ci]).astype(jnp.int32)
    fulls = [_cast_full("cast_" + n, W[n], LAYOUT[n], kc_idx) for n in BIG]
    full = dict(zip(BIG, _gather_weights(fulls, [LAYOUT[n] for n in BIG])))
    for n in BIG:
        if LAYOUT[n] == "row":
            f = full[n]
            full[n] = f.reshape(f.shape[0], f.shape[1] * f.shape[2], f.shape[3])
    fw = full["fox_w_in"].reshape(n_fox, N_CHIPS, D, -1).transpose(0, 2, 1, 3).reshape(n_fox, D, -1)
    n_in = fw.shape[2]
    full["fox_w_in"] = jnp.pad(fw, ((0, 0), (0, 0), (0, 3 * D + LANE - n_in)))
    perm = _uq_perm(H_mla)
    full["mla_w_uq"] = full["mla_w_uq"][:, :, perm]

    rc = tuple(_rope_tables(pos))

    def mixer_w(i):
        j = i // 2
        if i % 2 == 0:
            return dict(w_down=(full["mla_w_down"], j), q_norm=mla_q_norm[j:j + 1], w_uq=(full["mla_w_uq"], j),
                        kv_norm=mla_kv_norm[j:j + 1], w_uk=(full["mla_w_uk"], j), w_uv=(full["mla_w_uv"], j),
                        w_o=(full["mla_w_o"], j))
        return dict(w_in=(full["fox_w_in"], j), b_f=jnp.pad(fox_b_f[j:j + 1], ((0, 0), (0, LANE - H_fox))),
                    w_o=(full["fox_w_o"], j))

    saved = []
    xc = x0
    h = _modulate(x0, mod[0, 1], mod[0, 0])
    for i in range(L):
        mw = mixer_w(i)
        if i % 2 == 0:
            y1, ms = _mla_fwd(h, mw, rc)
        else:
            y1, ms = _fox_fwd(h, mw)
        z1, x1, h2 = _resid_ln_mod(xc, y1, mod[i, 2], ln1_g[i:i + 1], ln1_b[i:i + 1], mod[i, 4], mod[i, 3], alpha)
        y2, fs = _ffn_fwd(h2, (full["ffn_w1"], i), (full["ffn_w3"], i), (full["ffn_w2"], i))
        rec = dict(h1=h, ms=ms, y1=y1, z1=z1, h2=h2, fs=fs, y2=y2)
        if i + 1 < L:
            z2, xc, h = _resid_ln_mod(x1, y2, mod[i, 5], ln2_g[i:i + 1], ln2_b[i:i + 1], mod[i + 1, 1],
                                      mod[i + 1, 0], alpha)
            rec["z2"] = z2
        else:
            dx_res, dy, loss_v, dlg, dlb, dgate = _final_ln_loss(x1, y2, tgt, mod[i, 5], ln2_g[i:i + 1],
                                                                 ln2_b[i:i + 1], alpha)
        saved.append(rec)
    loss = lax.psum(loss_v[0, 0] * (0.5 / D), ("x", "y", "c"))

    G = {n: [None] * W[n].shape[0] for n in SMALL}
    dmod = [[None] * 6 for _ in range(L)]
    red = {n: None for n in BIG}
    inv_perm = np.argsort(perm)

    def rs_view(n, g):
        if n == "mla_w_uq":
            g = g[:, inv_perm]
        if n == "fox_w_in":
            g = g[:, :n_in].reshape(D, N_CHIPS, n_in // N_CHIPS).transpose(1, 0, 2)
        elif LAYOUT[n] == "row":
            g = g.reshape(N_CHIPS, g.shape[0] // N_CHIPS, g.shape[1])
        return g[None]

    def half_sums(tag, names, grads):
        lays = [LAYOUT[n] for n in names]
        gs = [rs_view(n, g) for n, g in zip(names, grads)]
        recv = _swap_halves("swap_" + tag, gs, lays)
        return [_add_half("add_half_" + n, g, r, lay, c_idx) for n, g, r, lay in zip(names, gs, recv, lays)]

    def finish(names, parts, recv2, lyr):
        for n, p, r in zip(names, parts, recv2):
            red[n] = _sum_shards("sum_shards_" + n, p, r, LAYOUT[n], kc_idx, red[n], lyr, W[n].shape[0])

    FFN = ("ffn_w1", "ffn_w3", "ffn_w2")
    MLA = ("mla_w_down", "mla_w_uq", "mla_w_uk", "mla_w_uv", "mla_w_o")
    FOX = ("fox_w_in", "fox_w_o")
    pending = None
    for i in reversed(range(L)):
        rec = saved[i]
        mw = mixer_w(i)
        j = i // 2
        G["ln2_g"][i], G["ln2_b"][i], dmod[i][5] = dlg, dlb, dgate
        dh2, dw1, dw3, dw2 = _ffn_bwd(dy, rec["h2"], rec["fs"], (full["ffn_w1"], i), (full["ffn_w3"], i),
                                      (full["ffn_w2"], i))
        ffn_parts = half_sums("ffn", FFN, (dw1, dw3, dw2))
        dx_res, dy, dmod[i][4], dmod[i][3], G["ln1_g"][i], G["ln1_b"][i], dmod[i][2] = _bwd_boundary(
            dx_res, dh2, rec["z1"], rec["y1"], mod[i, 4], mod[i, 2], ln1_g[i:i + 1], ln1_b[i:i + 1], alpha)
        ride_names = FFN + (pending[0] if pending else ())
        ride_parts = ffn_parts + (pending[1] if pending else [])
        carry = _exchange_carry(ride_parts, [LAYOUT[n] for n in ride_names])
        if i % 2 == 0:
            dh1, gm, recv2 = _mla_bwd(dy, rec["h1"], rec["ms"], mw, rc, carry)
            names, pre = MLA, "mla_"
            G["mla_q_norm"][j], G["mla_kv_norm"][j] = gm["q_norm"], gm["kv_norm"]
        else:
            dh1, gm, recv2 = _fox_bwd(dy, rec["h1"], rec["ms"], mw, carry)
            names, pre = FOX, "fox_"
            G["fox_b_f"][j] = gm["b_f"]
        finish(FFN, ffn_parts, recv2[:3], i)
        if pending:
            finish(pending[0], pending[1], recv2[3:], pending[2])
        pending = (names, half_sums(pre[:-1], names, [gm[n[len(pre):]] for n in names]), j)
        if i > 0:
            p = saved[i - 1]
            dx_res, dy, dmod[i][1], dmod[i][0], dlg, dlb, dgate = _bwd_boundary(
                dx_res, dh1, p["z2"], p["y2"], mod[i, 1], mod[i - 1, 5], ln2_g[i - 1:i], ln2_b[i - 1:i], alpha)
        else:
            grad_x, dmod[i][1], dmod[i][0] = _first_bwd(dx_res, dh1, x0, mod[i, 1])
    finish(pending[0], pending[1], _exchange_shards(pending[1], [LAYOUT[n] for n in pending[0]]), pending[2])
    Gb = dict(zip(BIG, _join_halves([red[n] for n in BIG])))

    small = jnp.concatenate([jnp.concatenate([g.reshape(-1) for g in G[n]]) for n in SMALL])
    dmod_v = jnp.concatenate([jnp.concatenate([d.reshape(-1) for d in row]) for row in dmod])
    n_small, n_dmod = small.shape[0], dmod_v.shape[0]
    wblk = -(-(n_small + n_dmod) // (8 * LANE)) * LANE
    blk = jnp.pad(jnp.concatenate([dmod_v, small]), (0, 8 * wblk - n_small - n_dmod)).reshape(8, wblk)
    parts = _ag_small("gather_small", blk).reshape(N_DEV, 8, wblk)
    tot = _sum_devices("sum_small", parts).reshape(-1)
    g_ada_b = tot[:n_dmod].reshape(L, 6 * D)
    off = n_dmod
    Gs = {}
    for n in SMALL:
        Gs[n] = tot[off:off + W[n].size].reshape(W[n].shape)
        off += W[n].size
    dmod_all = parts.reshape(N_DEV, 8 * wblk)[:, :n_dmod].reshape(N_DEV, L, N_CHIPS, w_ada)
    dmod_sh = lax.dynamic_index_in_dim(dmod_all, chip, axis=2, keepdims=False)
    dmod_sh = jnp.pad(dmod_sh, ((0, 8), (0, 0), (0, 0)))
    g_ada_w = jnp.stack([_mm1("ada_dw", c_act, dmod_sh[:, l], ta=True) for l in range(L)])

    grads = dict(Gb)
    grads.update(Gs)
    grads["ada_w"] = g_ada_w
    grads["ada_b"] = g_ada_b
    delta, new_m, new_v = {}, {}, {}
    for n in WEIGHTS:
        if n in SMALL or n == "ada_b":
            continue
        shp = W[n].shape
        delta[n], new_m[n], new_v[n] = [r.reshape(shp) for r in _adamw(
            "adamw_" + n, _flat(W[n]), _flat(grads[n]), _flat(Mo[n]), _flat(Vo[n]))]
    names_s = SMALL + ("ada_b",)
    cat = lambda d: jnp.concatenate([d[n].reshape(-1) for n in names_s])
    n_s = sum(W[n].size for n in names_s)
    ws = -(-n_s // (8 * LANE)) * LANE
    pk = lambda d: jnp.pad(cat(d), (0, 8 * ws - n_s)).reshape(8, ws)
    ds, ms_, vs = _adamw("adamw_small", pk(W), pk(grads), pk(Mo), pk(Vo))
    off = 0
    for n in names_s:
        sz, shp = W[n].size, W[n].shape
        delta[n] = ds.reshape(-1)[off:off + sz].reshape(shp)
        new_m[n] = ms_.reshape(-1)[off:off + sz].reshape(shp)
        new_v[n] = vs.reshape(-1)[off:off + sz].reshape(shp)
        off += sz

    return (loss, grad_x[None], *[grads[n].reshape(W[n].shape) for n in WEIGHTS], *[delta[n] for n in WEIGHTS],
            *[new_m[n] for n in WEIGHTS], *[new_v[n] for n in WEIGHTS])
```

```python
import functools

import numpy as np
import jax
import jax.numpy as jnp
from jax import lax
from jax.experimental import pallas as pl
from jax.experimental.pallas import tpu as pltpu

F32 = jnp.float32
BF16 = jnp.bfloat16
MXU_DTYPE = jnp.bfloat16
WIRE_DTYPE = jnp.bfloat16

HEAD_DIM = 128
ROPE_DIM = 64
CHUNK = 64
ROPE_THETA = 10000.0
LN_EPS = 1e-5
RMS_EPS = 1e-6
ADAM_LR, ADAM_B1, ADAM_B2, ADAM_EPS, ADAM_WD, ADAM_STEP = 0.001, 0.9, 0.999, 1e-08, 0.01, 10

N_CHIPS = 4
N_DEV = 8
LANE = 128
VMEM_LIMIT = 56 * 1024 * 1024
MESH = pl.DeviceIdType.MESH
ANY = pl.BlockSpec(memory_space=pl.ANY)
NEG = -1e30


def _params(**kw):
    return pltpu.CompilerParams(vmem_limit_bytes=VMEM_LIMIT, **kw)


def _pick(n, cands):
    for c in cands:
        if n % c == 0:
            return c
    return n


def _sigmoid(x):
    return 1.0 / (1.0 + jnp.exp(-x))


def _mm(name, terms, M, N, out_dtypes, epilogue=None, extras=(), row_extras=(), tm=512, tn=512, carry=None):
    tm = _pick(M, (tm, 256, 128))
    tn = _pick(N, (tn, 896, 768, 640, 384, 256, 128))
    extras = tuple(extras) + tuple(row_extras)
    n_row = len(row_extras)
    n_terms, n_ex, n_out = len(terms), len(extras), len(out_dtypes)
    n_acc = 1 + max(t[4] for t in terms)
    flags = [(t[2], t[3], t[4]) for t in terms]

    gi, gj = M // tm, N // tn

    def body(*refs):
        refs, c_in, c_out, c_sems = _carry_split(refs, 2 * n_terms + n_ex, n_out, 0, carry)
        pi, pj = pl.program_id(0), pl.program_id(1)
        _carry_run(carry, c_in, c_out, c_sems, jnp.logical_and(pi == 0, pj == 0), True)
        accs = [None] * n_acc
        for k, (ta, tb, ai) in enumerate(flags):
            a = refs[2 * k][...].astype(MXU_DTYPE)
            b = refs[2 * k + 1][...].astype(MXU_DTYPE)
            dn = (((0 if ta else 1,), (1 if tb else 0,)), ((), ()))
            r = lax.dot_general(a, b, dn, preferred_element_type=F32)
            accs[ai] = r if accs[ai] is None else accs[ai] + r
        ex = [refs[2 * n_terms + k][...] for k in range(n_ex)]
        outs = epilogue(accs, *ex) if epilogue is not None else (accs[0],)
        for k in range(n_out):
            o_ref = refs[2 * n_terms + n_ex + k]
            o_ref[...] = outs[k].astype(o_ref.dtype)
        _carry_run(carry, c_in, c_out, c_sems, jnp.logical_and(pi == gi - 1, pj == gj - 1), False)

    in_specs, args = [], []
    for (a, b, ta, tb, _, bcol) in terms:
        K = a.shape[0] if ta else a.shape[1]
        in_specs.append(pl.BlockSpec((K, tm), lambda i, j: (0, i)) if ta
                        else pl.BlockSpec((tm, K), lambda i, j: (i, 0)))
        if isinstance(b, tuple):
            b, lyr = b
            in_specs.append(pl.BlockSpec((None, tn, K), lambda i, j, o=bcol, l=lyr: (l, j + o, 0)) if tb
                            else pl.BlockSpec((None, K, tn), lambda i, j, o=bcol, l=lyr: (l, 0, j + o)))
        else:
            in_specs.append(pl.BlockSpec((tn, K), lambda i, j, o=bcol: (j + o, 0)) if tb
                            else pl.BlockSpec((K, tn), lambda i, j, o=bcol: (0, j + o)))
        args += [a, b]
    for k, e in enumerate(extras):
        in_specs.append(pl.BlockSpec((tm, tn), (lambda i, j: (i, 0)) if k >= n_ex - n_row else (lambda i, j: (i, j))))
        args.append(e)
    c_is, c_os, c_shape, c_sems, c_alias = _carry_call_args(carry, len(args), n_out)
    outs = pl.pallas_call(
        body, name=name, grid=(gi, gj), in_specs=in_specs + c_is,
        out_specs=[pl.BlockSpec((tm, tn), lambda i, j: (i, j)) for _ in out_dtypes] + c_os,
        out_shape=[jax.ShapeDtypeStruct((M, N), d) for d in out_dtypes] + c_shape,
        input_output_aliases=c_alias, scratch_shapes=c_sems,
        compiler_params=_params(),
    )(*args, *(carry["ins"] if carry else ()))
    return (outs[:n_out], list(outs[n_out:])) if carry else outs


def _wdim(b, axis):
    return b[0].shape[1 + axis] if isinstance(b, tuple) else b.shape[axis]


def _mm1(name, a, b, ta=False, tb=False, out_dtype=F32, bcol=0, N=None, **kw):
    M = a.shape[1] if ta else a.shape[0]
    if N is None:
        N = _wdim(b, 0 if tb else 1)
    res = _mm(name, [(a, b, ta, tb, 0, bcol)], M, N, [out_dtype], **kw)
    return (res[0][0], res[1]) if kw.get("carry") else res[0]


def _rowwise(name, fn, tiled, vecs, outs, reds=(), tr=128):
    R = tiled[0].shape[0]
    tr = _pick(R, (tr, 64, 32, 16, 8))
    nt, nv, no, nr = len(tiled), len(vecs), len(outs), len(reds)

    def body(*refs):
        vals = [r[...] for r in refs[:nt + nv]]
        res = fn(*vals)
        for k in range(no):
            o_ref = refs[nt + nv + k]
            o_ref[...] = res[k].astype(o_ref.dtype)
        if nr:
            first = pl.program_id(0) == 0
            for k in range(nr):
                r_ref = refs[nt + nv + no + k]

                @pl.when(first)
                def _(r_ref=r_ref, v=res[no + k]):
                    r_ref[...] = v

                @pl.when(jnp.logical_not(first))
                def _(r_ref=r_ref, v=res[no + k]):
                    r_ref[...] += v

    in_specs = [pl.BlockSpec((tr, t.shape[1]), lambda i: (i, 0)) for t in tiled]
    in_specs += [pl.BlockSpec(v.shape, lambda i, n=v.ndim: (0,) * n) for v in vecs]
    out_specs = [pl.BlockSpec((tr, w), lambda i: (i, 0)) for (w, _) in outs]
    out_specs += [pl.BlockSpec((1, w), lambda i: (0, 0)) for w in reds]
    out_shape = [jax.ShapeDtypeStruct((R, w), d) for (w, d) in outs]
    out_shape += [jax.ShapeDtypeStruct((1, w), F32) for w in reds]
    return pl.pallas_call(
        body, name=name, grid=(R // tr,), in_specs=in_specs, out_specs=out_specs, out_shape=out_shape,
        compiler_params=_params(),
    )(*tiled, *vecs)


def _colsum(v):
    return jnp.sum(v, axis=0, keepdims=True)


def _ln_stats(z):
    mu = jnp.mean(z, axis=-1, keepdims=True)
    zc = z - mu
    var = jnp.mean(zc * zc, axis=-1, keepdims=True)
    rstd = lax.rsqrt(var + LN_EPS)
    return zc * rstd, rstd


def _ln_bwd(dout, xhat, rstd, lg):
    dxh = dout * lg
    m1 = jnp.mean(dxh, axis=-1, keepdims=True)
    m2 = jnp.mean(dxh * xhat, axis=-1, keepdims=True)
    return rstd * (dxh - m1 - xhat * m2)


def _modulate(x, sc, sh):
    D = x.shape[1]
    return _rowwise("modulate", lambda x, sc, sh: ((x * (1.0 + sc) + sh),), [x], [sc, sh], [(D, MXU_DTYPE)])[0]


def _resid_ln_mod(x, y, g, lg, lb, sc_n, sh_n, alpha):
    D = x.shape[1]

    def fn(x, y, g, lg, lb, sc, sh):
        z = alpha * x + (1.0 + g) * y
        xhat, _ = _ln_stats(z)
        xo = xhat * lg + lb
        return z, xo, xo * (1.0 + sc) + sh

    return _rowwise("resid_ln_mod", fn, [x, y], [g, lg, lb, sc_n, sh_n], [(D, F32), (D, F32), (D, MXU_DTYPE)])


def _final_ln_loss(x, y, tgt, g, lg, lb, alpha):
    D = x.shape[1]

    def fn(x, y, t, g, lg, lb):
        z = alpha * x + (1.0 + g) * y
        xhat, rstd = _ln_stats(z)
        out = xhat * lg + lb
        err = out - t
        loss = jnp.sum(jnp.sum(err * err, axis=-1, keepdims=True), axis=0, keepdims=True)
        dout = err * (1.0 / D)
        dz = _ln_bwd(dout, xhat, rstd, lg)
        return (alpha * dz, (1.0 + g) * dz, jnp.broadcast_to(loss, (1, LANE)),
                _colsum(dout * xhat), _colsum(dout), _colsum(dz * y))

    return _rowwise("final_ln_loss", fn, [x, y, tgt], [g, lg, lb], [(D, F32), (D, MXU_DTYPE)], [LANE, D, D, D])


def _bwd_boundary(dx_res, dh, z_p, y_p, sc, g_p, lg_p, lb_p, alpha):
    D = dh.shape[1]

    def fn(dxr, dh, z, y, sc, g, lg, lb):
        xhat, rstd = _ln_stats(z)
        x_in = xhat * lg + lb
        dx = dxr + dh * (1.0 + sc)
        dz = _ln_bwd(dx, xhat, rstd, lg)
        return (alpha * dz, (1.0 + g) * dz,
                _colsum(dh * x_in), _colsum(dh), _colsum(dx * xhat), _colsum(dx), _colsum(dz * y))

    return _rowwise("bwd_boundary", fn, [dx_res, dh, z_p, y_p], [sc, g_p, lg_p, lb_p],
                    [(D, F32), (D, MXU_DTYPE)], [D, D, D, D, D])


def _first_bwd(dx_res, dh, x, sc):
    D = dh.shape[1]

    def fn(dxr, dh, x, sc):
        return dxr + dh * (1.0 + sc), _colsum(dh * x), _colsum(dh)

    return _rowwise("first_bwd", fn, [dx_res, dh, x], [sc], [(D, F32)], [D, D])


def _ride(ride, phase, bufs=None):
    if ride is None:
        return None
    bufs = ride["bufs"] if bufs is None else bufs
    return _gather_carry(bufs, ride["lays"], [0] * len(bufs), phase)


def _ffn_fwd(h, w1, w3, w2, ride=None):
    S, F = h.shape[0], _wdim(w1, 1)

    def epi(accs):
        a, b = accs
        return a, b, a * _sigmoid(a) * b

    res = _mm("ffn_up", [(h, w1, False, False, 0, 0), (h, w3, False, False, 1, 0)], S, F,
              [MXU_DTYPE, MXU_DTYPE, MXU_DTYPE], epilogue=epi, carry=_ride(ride, "ici"))
    (a, b, u), bufs = res if ride else (res, [])
    res = _mm1("ffn_down", u, w2, carry=_ride(ride, "d2d", bufs))
    y, bufs = res if ride else (res, [])
    return y, (a, b, u), bufs


def _ffn_bwd(dy, h, saved, w1, w3, w2):
    a, b, u = saved
    S, F = a.shape
    D = h.shape[1]

    def epi(accs, a, b):
        du = accs[0]
        a = a.astype(F32)
        b = b.astype(F32)
        sg = _sigmoid(a)
        return du * b * (sg * (1.0 + a * (1.0 - sg))), du * (a * sg)

    da, db = _mm("ffn_du", [(dy, w2, False, True, 0, 0)], S, F, [MXU_DTYPE, MXU_DTYPE], epilogue=epi, extras=(a, b))
    dw2 = _mm1("ffn_dw2", u, dy, ta=True, out_dtype=WIRE_DTYPE)
    dw1, dw3 = _mm("ffn_dw13", [(h, da, True, False, 0, 0), (h, db, True, False, 1, 0)], D, F,
                   [WIRE_DTYPE, WIRE_DTYPE], epilogue=lambda accs: (accs[0], accs[1]))
    dh = _mm("ffn_dh", [(da, w1, False, True, 0, 0), (db, w3, False, True, 0, 0)], S, D, [F32], tn=256)[0]
    return dh, dw1, dw3, dw2


def _rope_tables(pos):
    j = np.arange(LANE)
    invf = ROPE_THETA ** (-jnp.arange(0, ROPE_DIM, 2, dtype=F32) / ROPE_DIM)
    invf = invf[(j % ROPE_DIM) // 2].reshape(1, LANE)
    sgn = jnp.asarray(np.where(j % 2 == 0, -1.0, 1.0).reshape(1, LANE), F32)

    def fn(pos, invf, sgn):
        ang = pos.astype(F32) * invf
        return jnp.cos(ang), jnp.sin(ang) * sgn

    return _rowwise("rope_tables", fn, [pos], [invf, sgn], [(LANE, F32), (LANE, F32)], tr=256)


def _pair_swap(x):
    w = x.shape[1]
    even = (lax.broadcasted_iota(jnp.int32, x.shape, 1) % 2) == 0
    return jnp.where(even, pltpu.roll(x, w - 1, 1), pltpu.roll(x, 1, 1))


def _rope_fwd(x, c, s):
    return x * c + _pair_swap(x) * s


def _rope_bwd(d, c, s):
    return d * c + _pair_swap(d * s)


ATT_T = 256


def _dot_nt(a, b):
    return lax.dot_general(a, b, (((1,), (1,)), ((), ())), preferred_element_type=F32)


def _dot_tn(a, b):
    return lax.dot_general(a, b, (((0,), (0,)), ((), ())), preferred_element_type=F32)


def _dot_nn(a, b):
    return lax.dot_general(a, b, (((1,), (0,)), ((), ())), preferred_element_type=F32)


def _diag_mask(T, gran):
    r = lax.broadcasted_iota(jnp.int32, (T, T), 0)
    c = lax.broadcasted_iota(jnp.int32, (T, T), 1)
    if gran > 1:
        sh = int(np.log2(gran))
        r, c = lax.shift_right_logical(r, sh), lax.shift_right_logical(c, sh)
    return r >= c


def _attn_specs(S, H, T, mla, col_q, col_k, col_v):
    W = 2 * HEAD_DIM
    specs = [pl.BlockSpec((T, W), lambda p, i: (i, col_q + p))]
    if mla:
        specs.append(pl.BlockSpec((T, 2 * ROPE_DIM), lambda p, i: (i, p)))
    specs.append(pl.BlockSpec((S, W), lambda p, i: (0, col_k + p)))
    if mla:
        specs.append(pl.BlockSpec((S, ROPE_DIM), lambda p, i: (0, 0)))
    specs.append(pl.BlockSpec((S, W), lambda p, i: (0, col_v + p)))
    if not mla:
        specs.append(pl.BlockSpec((2, T, 1), lambda p, i: (p, i, 0)))
        specs.append(pl.BlockSpec((2, 1, S), lambda p, i: (p, 0, 0)))
    return specs


def _attn_fwd(name, S, H, mla, q, k, v, q_pe=None, k_pe=None, cum_col=None, cum_row=None, cols=(0, 0, 0),
              carry=None):
    T = _pick(S, (ATT_T, 128))
    nq = S // T
    scale = (HEAD_DIM + ROPE_DIM) ** -0.5 if mla else HEAD_DIM ** -0.5
    gran = CHUNK if mla else 1

    def body(*refs):
        refs, c_in, c_out, c_sems = _carry_split(refs, 5, 2, 3, carry)
        if mla:
            q_ref, qpe_ref, k_ref, kpe_ref, v_ref, o_ref, lse_ref, m_s, l_s, acc_s = refs
        else:
            q_ref, k_ref, v_ref, cc_ref, cr_ref, o_ref, lse_ref, m_s, l_s, acc_s = refs
        hp, qi = pl.program_id(0), pl.program_id(1)
        _carry_run(carry, c_in, c_out, c_sems, jnp.logical_and(hp == 0, qi == 0), True)
        for hh in range(2):
            hl = slice(hh * HEAD_DIM, (hh + 1) * HEAD_DIM)
            qf = q_ref[:, hl]
            if mla:
                qf = jnp.concatenate([qf, qpe_ref[:, hh * ROPE_DIM:(hh + 1) * ROPE_DIM]], axis=-1)
            cq = None if mla else cc_ref[hh]
            m_s[...] = jnp.full(m_s.shape, NEG, F32)
            l_s[...] = jnp.zeros(l_s.shape, F32)
            acc_s[...] = jnp.zeros(acc_s.shape, F32)

            def step(j, masked):
                rows = pl.ds(pl.multiple_of(j * T, T), T)
                kf = k_ref[rows, hl]
                if mla:
                    kf = jnp.concatenate([kf, kpe_ref[rows, :]], axis=-1)
                s = _dot_nt(qf, kf) * scale
                if not mla:
                    s = s + (cq - cr_ref[hh, :, rows])
                if masked:
                    s = jnp.where(_diag_mask(T, gran), s, NEG)
                m_old = m_s[:, 0:1]
                m_new = jnp.maximum(m_old, jnp.max(s, axis=-1, keepdims=True))
                p = jnp.exp(s - m_new)
                corr = jnp.exp(m_old - m_new)
                l_s[...] = jnp.broadcast_to(corr * l_s[:, 0:1] + jnp.sum(p, axis=-1, keepdims=True), l_s.shape)
                acc_s[...] = corr * acc_s[...] + _dot_nn(p.astype(MXU_DTYPE), v_ref[rows, hl])
                m_s[...] = jnp.broadcast_to(m_new, m_s.shape)

            lax.fori_loop(0, qi, lambda j, c: (step(j, False), c)[1], 0)
            step(qi, True)
            l = l_s[:, 0:1]
            o_ref[:, hl] = (acc_s[...] / l).astype(o_ref.dtype)
            lse_ref[hh] = jnp.broadcast_to(m_s[:, 0:1] + jnp.log(l), (T, LANE))
        _carry_run(carry, c_in, c_out, c_sems, jnp.logical_and(hp == H // 2 - 1, qi == nq - 1), False)

    args = [q] + ([q_pe] if mla else []) + [k] + ([k_pe] if mla else []) + [v]
    if not mla:
        args += [cum_col, cum_row]
    c_is, c_os, c_shape, c_sems, c_alias = _carry_call_args(carry, 5, 2)
    res = pl.pallas_call(
        body, name=name, grid=(H // 2, nq),
        in_specs=_attn_specs(S, H, T, mla, *cols) + c_is,
        out_specs=[pl.BlockSpec((T, 2 * HEAD_DIM), lambda p, i: (i, p)),
                   pl.BlockSpec((2, T, LANE), lambda p, i: (p, i, 0))] + c_os,
        out_shape=[jax.ShapeDtypeStruct((S, H * HEAD_DIM), MXU_DTYPE),
                   jax.ShapeDtypeStruct((H, S, LANE), F32)] + c_shape,
        input_output_aliases=c_alias,
        scratch_shapes=[pltpu.VMEM((T, LANE), F32), pltpu.VMEM((T, LANE), F32), pltpu.VMEM((T, HEAD_DIM), F32)]
        + c_sems,
        compiler_params=_params(),
    )(*args, *(carry["ins"] if carry else ()))
    return (res[0], res[1], list(res[2:])) if carry else (res[0], res[1], [])


def _carry_split(refs, n_in, n_out, n_scr, carry):
    if carry is None:
        return refs, (), (), ()
    ci, co = len(carry["ins"]), len(carry["out_shape"])
    own = refs[:n_in] + refs[n_in + ci:n_in + ci + n_out] + refs[n_in + ci + n_out + co:n_in + ci + n_out + co + n_scr]
    return (own, refs[n_in:n_in + ci], refs[n_in + ci + n_out:n_in + ci + n_out + co],
            refs[n_in + ci + n_out + co + n_scr:])


def _carry_run(carry, c_in, c_out, c_sems, when, start):
    if carry is None:
        return

    @pl.when(when)
    def _():
        for cp in carry["copies"](c_in, c_out, *c_sems):
            if start:
                cp.start()
            else:
                cp.wait()


def _carry_call_args(carry, n_in, n_out):
    if carry is None:
        return [], [], [], [], {}
    sems = [pltpu.SemaphoreType.DMA(carry["sems"]), pltpu.SemaphoreType.DMA(carry["sems"])]
    alias = {n_in + i: n_out + o for i, o in carry.get("alias", {}).items()}
    return [ANY] * len(carry["ins"]), [ANY] * len(carry["out_shape"]), list(carry["out_shape"]), sems, alias


def _attn_bwd(name, S, H, mla, q, k, v, do, lse, q_pe=None, k_pe=None, cum_col=None, cum_row=None,
              cols=(0, 0, 0), carry=None):
    T = _pick(S, (ATT_T, 128))
    nq = S // T
    scale = (HEAD_DIM + ROPE_DIM) ** -0.5 if mla else HEAD_DIM ** -0.5
    gran = CHUNK if mla else 1
    dqk = HEAD_DIM + (ROPE_DIM if mla else 0)

    def body(*refs):
        refs, c_in, c_out, c_sems = _carry_split(refs, 7, 5, 3, carry)
        if mla:
            (q_ref, qpe_ref, k_ref, kpe_ref, v_ref, do_ref, lse_ref,
             dq_ref, dk_ref, dv_ref, dqpe_ref, dkpe_ref, dq_s, dl_s, r_s) = refs
        else:
            (q_ref, k_ref, v_ref, cc_ref, cr_ref, do_ref, lse_ref,
             dq_ref, dk_ref, dv_ref, dck_ref, dcq_ref, dq_s, dl_s, r_s) = refs
        hp, qi = pl.program_id(0), pl.program_id(1)
        _carry_run(carry, c_in, c_out, c_sems, jnp.logical_and(hp == 0, qi == 0), True)

        @pl.when(qi == 0)
        def _():
            dk_ref[...] = jnp.zeros(dk_ref.shape, F32)
            dv_ref[...] = jnp.zeros(dv_ref.shape, F32)
            if not mla:
                dck_ref[...] = jnp.zeros(dck_ref.shape, F32)

        if mla:
            @pl.when(jnp.logical_and(qi == 0, hp == 0))
            def _():
                dkpe_ref[...] = jnp.zeros(dkpe_ref.shape, F32)

        for hh in range(2):
            hl = slice(hh * HEAD_DIM, (hh + 1) * HEAD_DIM)
            qf = q_ref[:, hl]
            if mla:
                qf = jnp.concatenate([qf, qpe_ref[:, hh * ROPE_DIM:(hh + 1) * ROPE_DIM]], axis=-1)
            dof = do_ref[:, hl]
            lse = lse_ref[hh][:, 0:1]
            cq = None if mla else cc_ref[hh]
            dq_s[...] = jnp.zeros(dq_s.shape, F32)
            dl_s[...] = jnp.zeros(dl_s.shape, F32)
            r_s[...] = jnp.zeros(r_s.shape, F32)

            def p_dp(j, masked):
                rows = pl.ds(pl.multiple_of(j * T, T), T)
                kf = k_ref[rows, hl]
                if mla:
                    kf = jnp.concatenate([kf, kpe_ref[rows, :]], axis=-1)
                s = _dot_nt(qf, kf) * scale
                if not mla:
                    s = s + (cq - cr_ref[hh, :, rows])
                if masked:
                    s = jnp.where(_diag_mask(T, gran), s, NEG)
                return rows, kf, jnp.exp(s - lse), _dot_nt(dof, v_ref[rows, hl])

            def sweep1(j, masked):
                rows, _, p, dp = p_dp(j, masked)
                dl_s[...] += jnp.broadcast_to(jnp.sum(p * dp, axis=-1, keepdims=True), dl_s.shape)
                dv_ref[rows, hl] += _dot_tn(p.astype(MXU_DTYPE), dof)

            def sweep2(j, masked):
                rows, kf, p, dp = p_dp(j, masked)
                ds = p * (dp - dl_s[:, 0:1])
                dsb = (ds * scale).astype(MXU_DTYPE)
                dkf = _dot_tn(dsb, qf)
                dk_ref[rows, hl] += dkf[:, :HEAD_DIM]
                if mla:
                    dkpe_ref[rows, :] += dkf[:, HEAD_DIM:]
                else:
                    dck_ref[hh, :, rows] -= jnp.sum(ds, axis=0, keepdims=True)
                    r_s[...] += jnp.broadcast_to(jnp.sum(ds, axis=-1, keepdims=True), r_s.shape)
                dq_s[...] += _dot_nn(dsb, kf)

            for sweep in (sweep1, sweep2):
                lax.fori_loop(0, qi, lambda j, c, sweep=sweep: (sweep(j, False), c)[1], 0)
                sweep(qi, True)
            dq_ref[:, hl] = dq_s[:, :HEAD_DIM].astype(dq_ref.dtype)
            if mla:
                dqpe_ref[:, hh * ROPE_DIM:(hh + 1) * ROPE_DIM] = dq_s[:, HEAD_DIM:]
            else:
                dcq_ref[hh] = r_s[:, 0:1]
        _carry_run(carry, c_in, c_out, c_sems, jnp.logical_and(hp == H // 2 - 1, qi == nq - 1), False)

    W = 2 * HEAD_DIM
    args = [q] + ([q_pe] if mla else []) + [k] + ([k_pe] if mla else []) + [v]
    if not mla:
        args += [cum_col, cum_row]
    args += [do, lse]
    in_specs = _attn_specs(S, H, T, mla, *cols)
    in_specs += [pl.BlockSpec((T, W), lambda p, i: (i, p)), pl.BlockSpec((2, T, LANE), lambda p, i: (p, i, 0))]
    out_specs = [pl.BlockSpec((T, W), lambda p, i: (i, p)), pl.BlockSpec((S, W), lambda p, i: (0, p)),
                 pl.BlockSpec((S, W), lambda p, i: (0, p))]
    out_shape = [jax.ShapeDtypeStruct((S, H * HEAD_DIM), MXU_DTYPE), jax.ShapeDtypeStruct((S, H * HEAD_DIM), F32),
                 jax.ShapeDtypeStruct((S, H * HEAD_DIM), F32)]
    if mla:
        out_specs += [pl.BlockSpec((T, 2 * ROPE_DIM), lambda p, i: (i, p)),
                      pl.BlockSpec((S, ROPE_DIM), lambda p, i: (0, 0))]
        out_shape += [jax.ShapeDtypeStruct((S, H * ROPE_DIM), F32), jax.ShapeDtypeStruct((S, ROPE_DIM), F32)]
    else:
        out_specs += [pl.BlockSpec((2, 1, S), lambda p, i: (p, 0, 0)), pl.BlockSpec((2, T, 1), lambda p, i: (p, i, 0))]
        out_shape += [jax.ShapeDtypeStruct((H, 1, S), F32), jax.ShapeDtypeStruct((H, S, 1), F32)]
    assert len(args) == 7 and len(out_shape) == 5
    c_is, c_os, c_shape, c_sems, c_alias = _carry_call_args(carry, 7, 5)
    res = pl.pallas_call(
        body, name=name, grid=(H // 2, nq), in_specs=in_specs + c_is, out_specs=out_specs + c_os,
        out_shape=out_shape + c_shape, input_output_aliases=c_alias,
        scratch_shapes=[pltpu.VMEM((T, dqk), F32), pltpu.VMEM((T, LANE), F32), pltpu.VMEM((T, LANE), F32)] + c_sems,
        compiler_params=_params(),
    )(*args, *(carry["ins"] if carry else ()))
    return (res[:5], res[5:]) if carry else res


def _mla_prep(lat, cos, sin, qn, kvn, ql, kvl):
    def fn(lat, c, s, qn, kvn):
        ql_ = lat[:, :ql]
        kv_ = lat[:, ql:ql + kvl]
        kp = lat[:, ql + kvl:]
        cq = ql_ * lax.rsqrt(jnp.mean(ql_ * ql_, axis=-1, keepdims=True) + RMS_EPS) * qn
        ckv = kv_ * lax.rsqrt(jnp.mean(kv_ * kv_, axis=-1, keepdims=True) + RMS_EPS) * kvn
        kp2 = jnp.concatenate([kp, jnp.zeros_like(kp)], axis=-1)
        kr = _rope_fwd(kp2, c, s)[:, :ROPE_DIM]
        return cq, ckv, kr

    return _rowwise("mla_prep", fn, [lat, cos, sin], [qn, kvn],
                    [(ql, MXU_DTYPE), (kvl, MXU_DTYPE), (ROPE_DIM, MXU_DTYPE)])


def _mla_prep_bwd(lat, cos, sin, qn, kvn, dcq, dckv, dkr, ql, kvl):
    def fn(lat, c, s, dcq, dckv, dkr, qn, kvn):
        outs, reds = [], []
        for (x, g, d) in ((lat[:, :ql], qn, dcq), (lat[:, ql:ql + kvl], kvn, dckv)):
            r = lax.rsqrt(jnp.mean(x * x, axis=-1, keepdims=True) + RMS_EPS)
            n = x * r
            dn = d * g
            outs.append(r * (dn - n * jnp.mean(dn * n, axis=-1, keepdims=True)))
            reds.append(_colsum(d * n))
        d2 = jnp.concatenate([dkr, jnp.zeros_like(dkr)], axis=-1)
        outs.append(_rope_bwd(d2, c, s)[:, :ROPE_DIM])
        return (jnp.concatenate(outs, axis=-1), *reds)

    return _rowwise("mla_prep_bwd", fn, [lat, cos, sin, dcq, dckv, dkr], [qn, kvn],
                    [(ql + kvl + ROPE_DIM, MXU_DTYPE)], [ql, kvl])


def _mla_fwd(h, w, rc, ride=None):
    S = h.shape[0]
    ql, kvl = w["q_norm"].shape[1], w["kv_norm"].shape[1]
    H = _wdim(w["w_uk"], 1) // HEAD_DIM
    n_nope, n_pe = H * HEAD_DIM, H * ROPE_DIM
    lat = _mm1("mla_down", h, w["w_down"])
    cq, ckv, kr = _mla_prep(lat, rc[0], rc[1], w["q_norm"], w["kv_norm"], ql, kvl)
    q_nope = _mm1("mla_uq_nope", cq, w["w_uq"], out_dtype=MXU_DTYPE, N=n_nope)
    q_pe = _mm("mla_uq_pe", [(cq, w["w_uq"], False, False, 0, n_nope // LANE)], S, n_pe, [MXU_DTYPE],
               epilogue=lambda accs, c, s: (_rope_fwd(accs[0], c, s),), row_extras=rc, tn=LANE)[0]
    k_nope, v = _mm("mla_ukv", [(ckv, w["w_uk"], False, False, 0, 0), (ckv, w["w_uv"], False, False, 1, 0)],
                    S, n_nope, [MXU_DTYPE, MXU_DTYPE], epilogue=lambda accs: (accs[0], accs[1]))
    o, lse, bufs = _attn_fwd("mla_attn_fwd", S, H, True, q_nope, k_nope, v, q_pe=q_pe, k_pe=kr,
                             carry=_ride(ride, "ici"))
    res = _mm1("mla_wo", o, w["w_o"], carry=_ride(ride, "d2d", bufs))
    y, bufs = res if ride else (res, [])
    return y, (lat, cq, ckv, kr, q_nope, q_pe, k_nope, v, o, lse), bufs


def _mla_bwd(dy, h, saved, w, rc, carry):
    lat, cq, ckv, kr, q_nope, q_pe, k_nope, v, o, lse = saved
    S = h.shape[0]
    ql, kvl = w["q_norm"].shape[1], w["kv_norm"].shape[1]
    H = _wdim(w["w_uk"], 1) // HEAD_DIM
    n_nope, n_pe = H * HEAD_DIM, H * ROPE_DIM
    do = _mm1("mla_do", dy, w["w_o"], tb=True, out_dtype=MXU_DTYPE)
    dw_o = _mm1("mla_dwo", o, dy, ta=True, out_dtype=WIRE_DTYPE)
    (dq_nope, dk_nope, dv, dq_pe_r, dk_pe_r), carried = _attn_bwd(
        "mla_attn_bwd", S, H, True, q_nope, k_nope, v, do, lse, q_pe=q_pe, k_pe=kr, carry=carry)

    def unrope(d, c, s):
        reps = (1, n_pe // LANE)
        return (_rope_bwd(d, jnp.tile(c, reps), jnp.tile(s, reps)),)

    dq_pe = _rowwise("mla_unrope_q", unrope, [dq_pe_r, rc[0], rc[1]], [], [(n_pe, MXU_DTYPE)])[0]
    dq = jnp.concatenate([dq_nope, dq_pe], axis=1)
    dw_uq = _mm1("mla_dwuq", cq, dq, ta=True, out_dtype=WIRE_DTYPE)
    dcq = _mm1("mla_dcq", dq, w["w_uq"], tb=True)
    dw_uk, dw_uv = _mm("mla_dwukv", [(ckv, dk_nope, True, False, 0, 0), (ckv, dv, True, False, 1, 0)], kvl, n_nope,
                       [WIRE_DTYPE, WIRE_DTYPE], epilogue=lambda accs: (accs[0], accs[1]))
    dckv = _mm("mla_dckv", [(dk_nope, w["w_uk"], False, True, 0, 0), (dv, w["w_uv"], False, True, 0, 0)],
               S, kvl, [F32])[0]
    dlat, dqn, dkvn = _mla_prep_bwd(lat, rc[0], rc[1], w["q_norm"], w["kv_norm"], dcq, dckv, dk_pe_r, ql, kvl)
    dw_down = _mm1("mla_dwdown", h, dlat, ta=True, out_dtype=WIRE_DTYPE)
    dh = _mm1("mla_dh", dlat, w["w_down"], tb=True)
    return dh, dict(w_down=dw_down, q_norm=dqn, w_uq=dw_uq, kv_norm=dkvn, w_uk=dw_uk, w_uv=dw_uv, w_o=dw_o), carried


def _log_sigmoid(z):
    return jnp.minimum(z, 0.0) - jnp.log(1.0 + jnp.exp(-jnp.abs(z)))


def _fox_gate_fwd(f, bf):
    S = f.shape[0]
    B = LANE

    def body(f_ref, b_ref, cum_ref):
        r = lax.broadcasted_iota(jnp.int32, (B, B), 0)
        c = lax.broadcasted_iota(jnp.int32, (B, B), 1)
        tri = (r >= c).astype(F32)
        carry = jnp.zeros((1, LANE), F32)
        for blk in range(S // B):
            rows = slice(blk * B, (blk + 1) * B)
            lf = _log_sigmoid(f_ref[rows, :] + b_ref[...])
            cs = jnp.dot(tri, lf, precision=lax.Precision.HIGHEST, preferred_element_type=F32) + carry
            cum_ref[rows, :] = cs
            carry = cs[B - 1:B, :]

    return pl.pallas_call(body, name="fox_gate_fwd", out_shape=jax.ShapeDtypeStruct((S, LANE), F32),
                          compiler_params=_params())(f, bf)


def _fox_gate_bwd(dcum, f, bf):
    S = f.shape[0]
    B = LANE

    def body(d_ref, f_ref, b_ref, df_ref, db_ref):
        r = lax.broadcasted_iota(jnp.int32, (B, B), 0)
        c = lax.broadcasted_iota(jnp.int32, (B, B), 1)
        tri = (r <= c).astype(F32)
        carry = jnp.zeros((1, LANE), F32)
        db = jnp.zeros((1, LANE), F32)
        for blk in reversed(range(S // B)):
            rows = slice(blk * B, (blk + 1) * B)
            dlf = jnp.dot(tri, d_ref[rows, :], precision=lax.Precision.HIGHEST, preferred_element_type=F32) + carry
            carry = dlf[0:1, :]
            z = f_ref[rows, :] + b_ref[...]
            dz = dlf * _sigmoid(-z)
            df_ref[rows, :] = dz.astype(df_ref.dtype)
            db = db + jnp.sum(dz, axis=0, keepdims=True)
        db_ref[...] = db

    return pl.pallas_call(body, name="fox_gate_bwd",
                          out_shape=[jax.ShapeDtypeStruct((S, LANE), MXU_DTYPE), jax.ShapeDtypeStruct((1, LANE), F32)],
                          compiler_params=_params())(dcum, f, bf)


def _fox_fwd(h, w, ride=None):
    S, D = h.shape
    H = D // HEAD_DIM
    qkv = _mm1("fox_qkv", h, w["w_in"], out_dtype=MXU_DTYPE, N=3 * D)
    f = _mm1("fox_f", h, w["w_in"], bcol=3 * D // LANE, N=LANE, tn=LANE)
    cum = _fox_gate_fwd(f, w["b_f"])
    cumT = cum[:, :H].T
    cum_col, cum_row = cumT.reshape(H, S, 1), cumT.reshape(H, 1, S)
    nb = D // (2 * HEAD_DIM)
    o, lse, bufs = _attn_fwd("fox_attn_fwd", S, H, False, qkv, qkv, qkv, cum_col=cum_col, cum_row=cum_row,
                             cols=(0, nb, 2 * nb), carry=_ride(ride, "ici"))
    res = _mm1("fox_wo", o, w["w_o"], carry=_ride(ride, "d2d", bufs))
    y, bufs = res if ride else (res, [])
    return y, (qkv, f, cum_col, cum_row, o, lse), bufs


def _fox_bwd(dy, h, saved, w, carry):
    qkv, f, cum_col, cum_row, o, lse = saved
    S, D = h.shape
    H = D // HEAD_DIM
    nb = D // (2 * HEAD_DIM)
    do = _mm1("fox_do", dy, w["w_o"], tb=True, out_dtype=MXU_DTYPE)
    dw_o = _mm1("fox_dwo", o, dy, ta=True, out_dtype=WIRE_DTYPE)
    (dq, dk, dv, dck, dcq), carried = _attn_bwd("fox_attn_bwd", S, H, False, qkv, qkv, qkv, do, lse, cum_col=cum_col,
                                                cum_row=cum_row, cols=(0, nb, 2 * nb), carry=carry)
    dcum = jnp.pad((dck.reshape(H, S) + dcq.reshape(H, S)).T, ((0, 0), (0, LANE - H)))
    df, dbf = _fox_gate_bwd(dcum, f, w["b_f"])
    dproj = jnp.concatenate([dq, dk.astype(MXU_DTYPE), dv.astype(MXU_DTYPE), df], axis=1)
    dw_in = _mm1("fox_dwin", h, dproj, ta=True, out_dtype=WIRE_DTYPE)
    dh = _mm1("fox_dh", dproj, w["w_in"], tb=True, tn=256)
    return dh, dict(w_in=dw_in, b_f=dbf[:, :H], w_o=dw_o), carried


def _place():
    x, y, c = lax.axis_index("x"), lax.axis_index("y"), lax.axis_index("c")
    return x, y, c, [(1 - x, y), (x, 1 - y), (1 - x, 1 - y)]


def _ag_small(name, blk):
    m, n = blk.shape

    def body(x_ref, out_ref, send_sems, recv_sems, local_sem):
        x, y, c, chips = _place()
        me, sibling = (x, y, c), (x, y, 1 - c)

        def rows(px, py, pc):
            return out_ref.at[pl.ds((4 * px + 2 * py + pc) * m, m), :]

        def copy(k, block, to, src=None):
            return pltpu.make_async_remote_copy(
                src_ref=rows(*block) if src is None else src, dst_ref=rows(*block),
                send_sem=send_sems.at[k], recv_sem=recv_sems.at[k], device_id=to, device_id_type=MESH)

        mine = pltpu.make_async_copy(x_ref, rows(*me), local_sem)
        mine.start()
        first = [copy(0, me, sibling, src=x_ref)]
        first += [copy(1 + j, me, (*chip, c), src=x_ref) for j, chip in enumerate(chips)]
        for cp in first:
            cp.start()
        passed = [copy(4 + j, (*chip, c), sibling) for j, chip in enumerate(chips)]
        for j, chip in enumerate(chips):
            copy(1 + j, (*chip, c), me).wait_recv()
            passed[j].start()
        copy(0, sibling, me).wait_recv()
        for j, chip in enumerate(chips):
            copy(4 + j, (*chip, 1 - c), me).wait_recv()
        for cp in first + passed:
            cp.wait_send()
        mine.wait()

    return pl.pallas_call(
        body, name=name, out_shape=jax.ShapeDtypeStruct((N_DEV * m, n), blk.dtype),
        in_specs=[pl.BlockSpec(memory_space=pltpu.VMEM)], out_specs=pl.BlockSpec(memory_space=pltpu.VMEM),
        scratch_shapes=[pltpu.SemaphoreType.DMA((7,)), pltpu.SemaphoreType.DMA((7,)), pltpu.SemaphoreType.DMA],
        compiler_params=_params(),
    )(blk)


def _half(ref, row_axis, c, rows):
    idx = [slice(None)] * len(ref.shape)
    idx[row_axis] = pl.ds(pl.multiple_of(c * rows, 16), rows)
    return ref.at[tuple(idx)]


def _shard(ref, layout, k):
    if layout == "row":
        return ref.at[:, k]
    w = ref.shape[2] // N_CHIPS
    return ref.at[:, :, pl.ds(pl.multiple_of(k * w, LANE), w)]


def _full_shape(shape, layout):
    L, r, w = shape
    return (L, N_CHIPS, r, w) if layout == "row" else (L, r, N_CHIPS * w)


def _cast_full(name, a, layout, k_idx, lyr):
    _, r, C = a.shape
    tr = _pick(r, (256, 128, 64, 32, 16))

    def body(k_ref, a_ref, o_ref):
        o_ref[...] = a_ref[...].astype(o_ref.dtype)

    if layout == "row":
        o_spec = pl.BlockSpec((None, None, tr, C), lambda l, i, k: (0, k[0], i, 0))
    else:
        o_spec = pl.BlockSpec((None, tr, C), lambda l, i, k: (0, i, k[0]))
    return pl.pallas_call(
        body, name=name,
        grid_spec=pltpu.PrefetchScalarGridSpec(
            num_scalar_prefetch=1, grid=(1, r // tr),
            in_specs=[pl.BlockSpec((None, tr, C), lambda l, i, k: (lyr, i, 0))], out_specs=o_spec),
        out_shape=jax.ShapeDtypeStruct(_full_shape((1, r, C), layout), WIRE_DTYPE),
        compiler_params=_params(),
    )(k_idx, a)


def _gather_carry(fulls, layouts, lyrs, phase):
    n = len(fulls)
    half_rows = [f.shape[2 if lay == "row" else 1] // 2 for f, lay in zip(fulls, layouts)]

    def copies(ins, outs, send_sems, recv_sems):
        x, y, c, chips = _place()
        cps = []
        for i in range(n):
            for j, chip in enumerate(chips):
                who, to = ((x, y), (*chip, c)) if phase == "ici" else (chip, (x, y, 1 - c))
                w = _half(_shard(outs[i].at[pl.ds(lyrs[i], 1)], layouts[i], 2 * who[0] + who[1]), 1, c, half_rows[i])
                cps.append(pltpu.make_async_remote_copy(
                    src_ref=w, dst_ref=w, send_sem=send_sems.at[i, j], recv_sem=recv_sems.at[i, j], device_id=to,
                    device_id_type=MESH))
        return cps

    return dict(ins=list(fulls), sems=(n, 3), copies=copies, alias={i: i for i in range(n)},
                out_shape=[jax.ShapeDtypeStruct(f.shape, f.dtype) for f in fulls])


def _run_carry(name, carry):
    n = len(carry["ins"])

    def body(*refs):
        cps = carry["copies"](refs[:n], refs[n:2 * n], *refs[2 * n:])
        for cp in cps:
            cp.start()
        for cp in cps:
            cp.wait()

    return pl.pallas_call(
        body, name=name, in_specs=[ANY] * n, out_specs=[ANY] * len(carry["out_shape"]), out_shape=carry["out_shape"],
        input_output_aliases=dict(carry.get("alias", {})),
        scratch_shapes=[pltpu.SemaphoreType.DMA(carry["sems"]), pltpu.SemaphoreType.DMA(carry["sems"])],
        compiler_params=_params(),
    )(*carry["ins"])


def _gather_weights(fulls, layouts, lyrs):
    n = len(fulls)
    half_rows = [f.shape[2 if lay == "row" else 1] // 2 for f, lay in zip(fulls, layouts)]

    def body(*refs):
        outs = refs[n:2 * n]
        send_sems, recv_sems = refs[2 * n:]
        x, y, c, chips = _place()
        sibling = (x, y, 1 - c)

        def window(i, kx, ky, half):
            return _half(_shard(outs[i].at[pl.ds(lyrs[i], 1)], layouts[i], 2 * kx + ky), 1, half, half_rows[i])

        first, passed = [], []
        for i in range(n):
            mine = window(i, x, y, c)
            for j, chip in enumerate(chips):
                cp = pltpu.make_async_remote_copy(
                    src_ref=mine, dst_ref=mine, send_sem=send_sems.at[i, j], recv_sem=recv_sems.at[i, j],
                    device_id=(*chip, c), device_id_type=MESH)
                cp.start()
                first.append(cp)
        for i in range(n):
            for j, chip in enumerate(chips):
                got = window(i, *chip, c)
                pltpu.make_async_remote_copy(
                    src_ref=got, dst_ref=got, send_sem=send_sems.at[i, j], recv_sem=recv_sems.at[i, j],
                    device_id=(*chip, c), device_id_type=MESH).wait_recv()
                cp = pltpu.make_async_remote_copy(
                    src_ref=got, dst_ref=got, send_sem=send_sems.at[i, 3 + j], recv_sem=recv_sems.at[i, 3 + j],
                    device_id=sibling, device_id_type=MESH)
                cp.start()
                passed.append(cp)
        for i in range(n):
            for j, chip in enumerate(chips):
                got = window(i, *chip, 1 - c)
                pltpu.make_async_remote_copy(
                    src_ref=got, dst_ref=got, send_sem=send_sems.at[i, 3 + j], recv_sem=recv_sems.at[i, 3 + j],
                    device_id=sibling, device_id_type=MESH).wait_recv()
        for cp in first + passed:
            cp.wait_send()

    return pl.pallas_call(
        body, name="gather_weights", in_specs=[ANY] * n, out_specs=[ANY] * n,
        out_shape=[jax.ShapeDtypeStruct(f.shape, f.dtype) for f in fulls],
        input_output_aliases={i: i for i in range(n)},
        scratch_shapes=[pltpu.SemaphoreType.DMA((n, 6)), pltpu.SemaphoreType.DMA((n, 6))],
        compiler_params=_params(),
    )(*fulls)


def _half_shape(shape, layout):
    s = list(shape)
    s[2 if layout == "row" else 1] //= 2
    return tuple(s)


def _swap_halves(name, grads, layouts):
    n = len(grads)
    row_axis = [2 if lay == "row" else 1 for lay in layouts]
    half_rows = [g.shape[ra] // 2 for g, ra in zip(grads, row_axis)]

    def body(*refs):
        ins, outs = refs[:n], refs[n:2 * n]
        send_sems, recv_sems = refs[2 * n:]
        x, y, c, _ = _place()
        cps = []
        for i in range(n):
            cp = pltpu.make_async_remote_copy(
                src_ref=_half(ins[i], row_axis[i], 1 - c, half_rows[i]), dst_ref=outs[i],
                send_sem=send_sems.at[i], recv_sem=recv_sems.at[i], device_id=(x, y, 1 - c), device_id_type=MESH)
            cp.start()
            cps.append(cp)
        for cp in cps:
            cp.wait()

    return pl.pallas_call(
        body, name=name, in_specs=[ANY] * n, out_specs=[ANY] * n,
        out_shape=[jax.ShapeDtypeStruct(_half_shape(g.shape, lay), g.dtype) for g, lay in zip(grads, layouts)],
        scratch_shapes=[pltpu.SemaphoreType.DMA((n,)), pltpu.SemaphoreType.DMA((n,))],
        compiler_params=_params(),
    )(*grads)


def _add_half(name, g, r, layout, c_idx):
    L = g.shape[0]
    if layout == "row":
        A, rows, W = L * N_CHIPS, g.shape[2] // 2, g.shape[3]
    else:
        A, rows, W = L, g.shape[1] // 2, g.shape[2]
    g3 = g.reshape(A, 2 * rows, W)
    r3 = r.reshape(A, rows, W)
    tr = _pick(rows, (256, 128, 64, 32, 16))
    nb = rows // tr

    def body(c_ref, g_ref, r_ref, o_ref):
        o_ref[...] = (g_ref[...].astype(F32) + r_ref[...].astype(F32)).astype(o_ref.dtype)

    out = pl.pallas_call(
        body, name=name,
        grid_spec=pltpu.PrefetchScalarGridSpec(
            num_scalar_prefetch=1, grid=(A, nb),
            in_specs=[pl.BlockSpec((None, tr, W), lambda a, i, c: (a, c[0] * nb + i, 0)),
                      pl.BlockSpec((None, tr, W), lambda a, i, c: (a, i, 0))],
            out_specs=pl.BlockSpec((None, tr, W), lambda a, i, c: (a, i, 0))),
        out_shape=jax.ShapeDtypeStruct((A, rows, W), WIRE_DTYPE),
        compiler_params=_params(),
    )(c_idx, g3, r3)
    return out.reshape(r.shape)


def _exchange_carry(parts, layouts):
    n = len(parts)

    def shard_half_shape(p, lay):
        if lay == "row":
            return (p.shape[0],) + p.shape[2:]
        return (p.shape[0], p.shape[1], p.shape[2] // N_CHIPS)

    def copies(ins, outs, send_sems, recv_sems):
        x, y, c, chips = _place()
        return [pltpu.make_async_remote_copy(
            src_ref=_shard(ins[i], layouts[i], 2 * kx + ky), dst_ref=outs[i].at[j], send_sem=send_sems.at[i, j],
            recv_sem=recv_sems.at[i, j], device_id=(kx, ky, c), device_id_type=MESH)
            for i in range(n) for j, (kx, ky) in enumerate(chips)]

    return dict(ins=list(parts), sems=(n, 3), copies=copies,
                out_shape=[jax.ShapeDtypeStruct((3,) + shard_half_shape(p, lay), p.dtype)
                           for p, lay in zip(parts, layouts)])


def _exchange_shards(parts, layouts):
    n = len(parts)
    carry = _exchange_carry(parts, layouts)

    def body(*refs):
        cps = carry["copies"](refs[:n], refs[n:2 * n], *refs[2 * n:])
        for cp in cps:
            cp.start()
        for cp in cps:
            cp.wait()

    return pl.pallas_call(
        body, name="exchange_shards", in_specs=[ANY] * n, out_specs=[ANY] * n, out_shape=carry["out_shape"],
        scratch_shapes=[pltpu.SemaphoreType.DMA((n, 3)), pltpu.SemaphoreType.DMA((n, 3))],
        compiler_params=_params(),
    )(*parts)


def _sum_shards(name, p, r, layout, kc_idx, dst, lyr, n_lyr):
    rows, W = r.shape[2], r.shape[3]
    tr = _pick(rows, (256, 128, 64, 32, 16))
    nb = rows // tr

    def body(kc_ref, p_ref, r_ref, *rest):
        acc = p_ref[...].astype(F32)
        for j in range(3):
            acc = acc + r_ref[j].astype(F32)
        rest[-1][...] = acc

    if layout == "row":
        p_spec = pl.BlockSpec((None, None, tr, W), lambda a, i, kc: (0, kc[0], i, 0))
    else:
        p_spec = pl.BlockSpec((None, tr, W), lambda a, i, kc: (0, i, kc[0]))
    in_specs = [p_spec, pl.BlockSpec((3, None, tr, W), lambda a, i, kc: (0, 0, i, 0))]
    args = [kc_idx, p, r]
    if dst is not None:
        in_specs.append(ANY)
        args.append(dst)
    return pl.pallas_call(
        body, name=name,
        grid_spec=pltpu.PrefetchScalarGridSpec(
            num_scalar_prefetch=1, grid=(1, nb), in_specs=in_specs,
            out_specs=pl.BlockSpec((None, tr, W), lambda a, i, kc: (lyr, kc[1] * nb + i, 0))),
        out_shape=jax.ShapeDtypeStruct((n_lyr, 2 * rows, W), F32),
        input_output_aliases={3: 0} if dst is not None else {},
        compiler_params=_params(),
    )(*args)


def _join_halves(shards):
    n = len(shards)

    def body(*refs):
        outs = refs[n:2 * n]
        send_sems, recv_sems = refs[2 * n:]
        x, y, c, _ = _place()
        cps = []
        for i in range(n):
            mine = _half(outs[i], 1, c, outs[i].shape[1] // 2)
            cp = pltpu.make_async_remote_copy(
                src_ref=mine, dst_ref=mine, send_sem=send_sems.at[i], recv_sem=recv_sems.at[i],
                device_id=(x, y, 1 - c), device_id_type=MESH)
            cp.start()
            cps.append(cp)
        for cp in cps:
            cp.wait()

    return pl.pallas_call(
        body, name="join_halves", in_specs=[ANY] * n, out_specs=[ANY] * n,
        out_shape=[jax.ShapeDtypeStruct(s.shape, s.dtype) for s in shards],
        input_output_aliases={i: i for i in range(n)},
        scratch_shapes=[pltpu.SemaphoreType.DMA((n,)), pltpu.SemaphoreType.DMA((n,))],
        compiler_params=_params(),
    )(*shards)


def _ada_fwd(c_all, ada_w, ada_b):
    L, D, w = ada_w.shape
    tn = _pick(w, (512, 256, 128))

    def body(c_ref, w_ref, b_ref, o_ref, a_ref):
        c = c_ref[...]
        act = (c * _sigmoid(c)).astype(MXU_DTYPE)
        a_ref[...] = act
        o_ref[...] = jnp.dot(act, w_ref[...].astype(MXU_DTYPE), preferred_element_type=F32) + b_ref[...]

    return pl.pallas_call(
        body, name="ada_fwd", grid=(L, w // tn),
        in_specs=[pl.BlockSpec((16, D), lambda l, j: (0, 0)), pl.BlockSpec((None, D, tn), lambda l, j: (l, 0, j)),
                  pl.BlockSpec((None, 1, tn), lambda l, j: (l, 0, j))],
        out_specs=[pl.BlockSpec((None, 16, tn), lambda l, j: (l, 0, j)), pl.BlockSpec((16, D), lambda l, j: (0, 0))],
        out_shape=[jax.ShapeDtypeStruct((L, 16, w), F32), jax.ShapeDtypeStruct((16, D), MXU_DTYPE)],
        compiler_params=_params(),
    )(c_all, ada_w, ada_b)


def _sum_devices(name, parts):
    n, R, W = parts.shape
    tw = _pick(W, (2048, 1024, 512, 256, 128))

    def body(p_ref, o_ref):
        acc = p_ref[0]
        for d in range(1, n):
            acc = acc + p_ref[d]
        o_ref[...] = acc

    return pl.pallas_call(
        body, name=name, grid=(W // tw,), in_specs=[pl.BlockSpec((n, R, tw), lambda j: (0, 0, j))],
        out_specs=pl.BlockSpec((R, tw), lambda j: (0, j)), out_shape=jax.ShapeDtypeStruct((R, W), F32),
        compiler_params=_params(),
    )(parts)


def _adamw(name, w, g, m, v):
    W = w.shape[1]
    bc1 = 1.0 - ADAM_B1 ** ADAM_STEP
    bc2 = 1.0 - ADAM_B2 ** ADAM_STEP

    def fn(w, g, m, v):
        m2 = ADAM_B1 * m + (1.0 - ADAM_B1) * g
        v2 = ADAM_B2 * v + (1.0 - ADAM_B2) * (g * g)
        delta = -ADAM_LR * ((m2 / bc1) / (jnp.sqrt(v2 / bc2) + ADAM_EPS) + ADAM_WD * w)
        return delta, m2, v2

    tr = 256 if W <= 1024 else (128 if W <= 2048 else 64)
    return _rowwise(name, fn, [w, g, m, v], [], [(W, F32)] * 3, tr=tr)


def _flat(a):
    return a.reshape(-1, a.shape[-1])


BIG = ("ffn_w1", "ffn_w3", "ffn_w2", "mla_w_down", "mla_w_uq", "mla_w_uk", "mla_w_uv", "mla_w_o", "fox_w_in",
       "fox_w_o")
LAYOUT = dict(ffn_w1="col", ffn_w3="col", ffn_w2="row", mla_w_down="row", mla_w_uq="col", mla_w_uk="col",
              mla_w_uv="col", mla_w_o="row", fox_w_in="row", fox_w_o="row")
FFN = ("ffn_w1", "ffn_w3", "ffn_w2")
MLA = ("mla_w_down", "mla_w_uq", "mla_w_uk", "mla_w_uv", "mla_w_o")
FOX = ("fox_w_in", "fox_w_o")
SMALL = ("ln1_g", "ln1_b", "ln2_g", "ln2_b", "mla_q_norm", "mla_kv_norm", "fox_b_f")
WEIGHTS = ("ada_w", "ada_b", "ln1_g", "ln1_b", "ln2_g", "ln2_b", "ffn_w1", "ffn_w3", "ffn_w2", "mla_w_down",
           "mla_q_norm", "mla_w_uq", "mla_kv_norm", "mla_w_uk", "mla_w_uv", "mla_w_o", "fox_w_in", "fox_b_f",
           "fox_w_o")


def _uq_perm(H):
    d = HEAD_DIM + ROPE_DIM
    nope = (np.arange(H)[:, None] * d + np.arange(HEAD_DIM)[None, :]).reshape(-1)
    pe = (np.arange(H)[:, None] * d + HEAD_DIM + np.arange(ROPE_DIM)[None, :]).reshape(-1)
    return np.concatenate([nope, pe])


def kernel(x, c, positions, ada_w, ada_b, ln1_g, ln1_b, ln2_g, ln2_b, ffn_w1, ffn_w3, ffn_w2, mla_w_down, mla_q_norm, mla_w_uq, mla_kv_norm, mla_w_uk, mla_w_uv, mla_w_o, fox_w_in, fox_b_f, fox_w_o, loss_target, m_ada_w, m_ada_b, m_ln1_g, m_ln1_b, m_ln2_g, m_ln2_b, m_ffn_w1, m_ffn_w3, m_ffn_w2, m_mla_w_down, m_mla_q_norm, m_mla_w_uq, m_mla_kv_norm, m_mla_w_uk, m_mla_w_uv, m_mla_w_o, m_fox_w_in, m_fox_b_f, m_fox_w_o, v_ada_w, v_ada_b, v_ln1_g, v_ln1_b, v_ln2_g, v_ln2_b, v_ffn_w1, v_ffn_w3, v_ffn_w2, v_mla_w_down, v_mla_q_norm, v_mla_w_uq, v_mla_kv_norm, v_mla_w_uk, v_mla_w_uv, v_mla_w_o, v_fox_w_in, v_fox_b_f, v_fox_w_o):
    W = dict(ada_w=ada_w, ada_b=ada_b, ln1_g=ln1_g, ln1_b=ln1_b, ln2_g=ln2_g, ln2_b=ln2_b, ffn_w1=ffn_w1,
             ffn_w3=ffn_w3, ffn_w2=ffn_w2, mla_w_down=mla_w_down, mla_q_norm=mla_q_norm, mla_w_uq=mla_w_uq,
             mla_kv_norm=mla_kv_norm, mla_w_uk=mla_w_uk, mla_w_uv=mla_w_uv, mla_w_o=mla_w_o, fox_w_in=fox_w_in,
             fox_b_f=fox_b_f, fox_w_o=fox_w_o)
    Mo = dict(ada_w=m_ada_w, ada_b=m_ada_b, ln1_g=m_ln1_g, ln1_b=m_ln1_b, ln2_g=m_ln2_g, ln2_b=m_ln2_b,
              ffn_w1=m_ffn_w1, ffn_w3=m_ffn_w3, ffn_w2=m_ffn_w2, mla_w_down=m_mla_w_down, mla_q_norm=m_mla_q_norm,
              mla_w_uq=m_mla_w_uq, mla_kv_norm=m_mla_kv_norm, mla_w_uk=m_mla_w_uk, mla_w_uv=m_mla_w_uv,
              mla_w_o=m_mla_w_o, fox_w_in=m_fox_w_in, fox_b_f=m_fox_b_f, fox_w_o=m_fox_w_o)
    Vo = dict(ada_w=v_ada_w, ada_b=v_ada_b, ln1_g=v_ln1_g, ln1_b=v_ln1_b, ln2_g=v_ln2_g, ln2_b=v_ln2_b,
              ffn_w1=v_ffn_w1, ffn_w3=v_ffn_w3, ffn_w2=v_ffn_w2, mla_w_down=v_mla_w_down, mla_q_norm=v_mla_q_norm,
              mla_w_uq=v_mla_w_uq, mla_kv_norm=v_mla_kv_norm, mla_w_uk=v_mla_w_uk, mla_w_uv=v_mla_w_uv,
              mla_w_o=v_mla_w_o, fox_w_in=v_fox_w_in, fox_b_f=v_fox_b_f, fox_w_o=v_fox_w_o)

    S, D = x.shape[1], x.shape[2]
    L = ada_w.shape[0]
    alpha = float((2 * L) ** 0.25)
    H_mla = mla_w_uk.shape[2] * N_CHIPS // HEAD_DIM
    H_fox = D // HEAD_DIM
    xi, yi, ci = lax.axis_index("x"), lax.axis_index("y"), lax.axis_index("c")
    chip = 2 * xi + yi
    dev = 2 * chip + ci
    c_idx = jnp.reshape(ci, (1,)).astype(jnp.int32)
    x0, tgt = x[0], loss_target[0]
    pos = positions.reshape(S, 1)

    c_all = _ag_small("gather_c", jnp.pad(c, ((0, 7), (0, 0)))).reshape(N_DEV, 8, D)[:, 0]
    w_ada = ada_w.shape[2]
    ada_b_sh = lax.dynamic_slice_in_dim(ada_b, chip * w_ada, w_ada, axis=1).reshape(L, 1, w_ada)
    mod_sh, c_act = _ada_fwd(jnp.pad(c_all, ((0, 8), (0, 0))), ada_w, ada_b_sh)
    mod_all = _ag_small("gather_mod", mod_sh.transpose(1, 0, 2).reshape(16, L * w_ada))
    mod_all = mod_all.reshape(N_CHIPS, 2, 16, L, w_ada)[:, 0]
    mod = lax.dynamic_index_in_dim(mod_all, dev, axis=1, keepdims=False)
    mod = mod.transpose(1, 0, 2).reshape(L, 6, 1, D)

    kc_idx = jnp.stack([chip, ci]).astype(jnp.int32)
    raw = {n: [_cast_full("cast_" + n, W[n], LAYOUT[n], kc_idx, l) for l in range(W[n].shape[0])] for n in BIG}
    full = {n: [None] * W[n].shape[0] for n in BIG}
    perm = _uq_perm(H_mla)
    n_in = fox_w_in.shape[2] * N_CHIPS

    def group(i):
        return FFN + (MLA if i % 2 == 0 else FOX)

    def ride_of(names, i):
        return dict(bufs=[raw[n][i if n in FFN else i // 2] for n in names], lays=[LAYOUT[n] for n in names])

    def land(names, i, bufs):
        for n, f in zip(names, bufs):
            if n == "fox_w_in":
                fw = f.reshape(N_CHIPS, D, -1).transpose(1, 0, 2).reshape(D, n_in)
                f = jnp.pad(fw, ((0, 0), (0, 3 * D + LANE - n_in)))[None]
            elif n == "mla_w_uq":
                f = f[:, :, perm]
            elif LAYOUT[n] == "row":
                f = f.reshape(1, f.shape[1] * f.shape[2], f.shape[3])
            full[n][i if n in FFN else i // 2] = f

    r0 = ride_of(group(0), 0)
    land(group(0), 0, _gather_weights(r0["bufs"], r0["lays"], [0] * len(r0["bufs"])))

    rc = tuple(_rope_tables(pos))

    def mixer_w(i):
        j = i // 2
        if i % 2 == 0:
            return dict(w_down=(full["mla_w_down"][j], 0), q_norm=mla_q_norm[j:j + 1], w_uq=(full["mla_w_uq"][j], 0),
                        kv_norm=mla_kv_norm[j:j + 1], w_uk=(full["mla_w_uk"][j], 0), w_uv=(full["mla_w_uv"][j], 0),
                        w_o=(full["mla_w_o"][j], 0))
        return dict(w_in=(full["fox_w_in"][j], 0), b_f=jnp.pad(fox_b_f[j:j + 1], ((0, 0), (0, LANE - H_fox))),
                    w_o=(full["fox_w_o"][j], 0))

    saved = []
    xc = x0
    h = _modulate(x0, mod[0, 1], mod[0, 0])
    for i in range(L):
        mw = mixer_w(i)
        nxt = i + 1 < L
        mix_n = group(i + 1)[3:]
        if i % 2 == 0:
            y1, ms, got = _mla_fwd(h, mw, rc, ride_of(FFN, i + 1) if nxt else None)
        else:
            y1, ms, got = _fox_fwd(h, mw, ride_of(FFN, i + 1) if nxt else None)
        land(FFN, i + 1, got)
        z1, x1, h2 = _resid_ln_mod(xc, y1, mod[i, 2], ln1_g[i:i + 1], ln1_b[i:i + 1], mod[i, 4], mod[i, 3], alpha)
        y2, fs, got = _ffn_fwd(h2, (full["ffn_w1"][i], 0), (full["ffn_w3"][i], 0), (full["ffn_w2"][i], 0),
                               ride_of(mix_n, i + 1) if nxt else None)
        land(mix_n, i + 1, got)
        rec = dict(h1=h, ms=ms, y1=y1, z1=z1, h2=h2, fs=fs, y2=y2)
        if i + 1 < L:
            z2, xc, h = _resid_ln_mod(x1, y2, mod[i, 5], ln2_g[i:i + 1], ln2_b[i:i + 1], mod[i + 1, 1],
                                      mod[i + 1, 0], alpha)
            rec["z2"] = z2
        else:
            dx_res, dy, loss_v, dlg, dlb, dgate = _final_ln_loss(x1, y2, tgt, mod[i, 5], ln2_g[i:i + 1],
                                                                 ln2_b[i:i + 1], alpha)
        saved.append(rec)
    loss = lax.psum(loss_v[0, 0] * (0.5 / D), ("x", "y", "c"))

    G = {n: [None] * W[n].shape[0] for n in SMALL}
    dmod = [[None] * 6 for _ in range(L)]
    red = {n: None for n in BIG}
    inv_perm = np.argsort(perm)

    def rs_view(n, g):
        if n == "mla_w_uq":
            g = g[:, inv_perm]
        if n == "fox_w_in":
            g = g[:, :n_in].reshape(D, N_CHIPS, n_in // N_CHIPS).transpose(1, 0, 2)
        elif LAYOUT[n] == "row":
            g = g.reshape(N_CHIPS, g.shape[0] // N_CHIPS, g.shape[1])
        return g[None]

    def half_sums(tag, names, grads):
        lays = [LAYOUT[n] for n in names]
        gs = [rs_view(n, g) for n, g in zip(names, grads)]
        recv = _swap_halves("swap_" + tag, gs, lays)
        return [_add_half("add_half_" + n, g, r, lay, c_idx) for n, g, r, lay in zip(names, gs, recv, lays)]

    def finish(names, parts, recv2, lyr):
        for n, p, r in zip(names, parts, recv2):
            red[n] = _sum_shards("sum_shards_" + n, p, r, LAYOUT[n], kc_idx, red[n], lyr, W[n].shape[0])

    pending = None
    for i in reversed(range(L)):
        rec = saved[i]
        mw = mixer_w(i)
        j = i // 2
        G["ln2_g"][i], G["ln2_b"][i], dmod[i][5] = dlg, dlb, dgate
        dh2, dw1, dw3, dw2 = _ffn_bwd(dy, rec["h2"], rec["fs"], (full["ffn_w1"][i], 0), (full["ffn_w3"][i], 0),
                                      (full["ffn_w2"][i], 0))
        ffn_parts = half_sums("ffn", FFN, (dw1, dw3, dw2))
        dx_res, dy, dmod[i][4], dmod[i][3], G["ln1_g"][i], G["ln1_b"][i], dmod[i][2] = _bwd_boundary(
            dx_res, dh2, rec["z1"], rec["y1"], mod[i, 4], mod[i, 2], ln1_g[i:i + 1], ln1_b[i:i + 1], alpha)
        ride_names = FFN + (pending[0] if pending else ())
        ride_parts = ffn_parts + (pending[1] if pending else [])
        carry = _exchange_carry(ride_parts, [LAYOUT[n] for n in ride_names])
        if i % 2 == 0:
            dh1, gm, recv2 = _mla_bwd(dy, rec["h1"], rec["ms"], mw, rc, carry)
            names, pre = MLA, "mla_"
            G["mla_q_norm"][j], G["mla_kv_norm"][j] = gm["q_norm"], gm["kv_norm"]
        else:
            dh1, gm, recv2 = _fox_bwd(dy, rec["h1"], rec["ms"], mw, carry)
            names, pre = FOX, "fox_"
            G["fox_b_f"][j] = gm["b_f"]
        finish(FFN, ffn_parts, recv2[:3], i)
        if pending:
            finish(pending[0], pending[1], recv2[3:], pending[2])
        pending = (names, half_sums(pre[:-1], names, [gm[n[len(pre):]] for n in names]), j)
        if i > 0:
            p = saved[i - 1]
            dx_res, dy, dmod[i][1], dmod[i][0], dlg, dlb, dgate = _bwd_boundary(
                dx_res, dh1, p["z2"], p["y2"], mod[i, 1], mod[i - 1, 5], ln2_g[i - 1:i], ln2_b[i - 1:i], alpha)
        else:
            grad_x, dmod[i][1], dmod[i][0] = _first_bwd(dx_res, dh1, x0, mod[i, 1])
    finish(pending[0], pending[1], _exchange_shards(pending[1], [LAYOUT[n] for n in pending[0]]), pending[2])
    Gb = dict(zip(BIG, _join_halves([red[n] for n in BIG])))

    small = jnp.concatenate([jnp.concatenate([g.reshape(-1) for g in G[n]]) for n in SMALL])
    dmod_v = jnp.concatenate([jnp.concatenate([d.reshape(-1) for d in row]) for row in dmod])
    n_small, n_dmod = small.shape[0], dmod_v.shape[0]
    wblk = -(-(n_small + n_dmod) // (8 * LANE)) * LANE
    blk = jnp.pad(jnp.concatenate([dmod_v, small]), (0, 8 * wblk - n_small - n_dmod)).reshape(8, wblk)
    parts = _ag_small("gather_small", blk).reshape(N_DEV, 8, wblk)
    tot = _sum_devices("sum_small", parts).reshape(-1)
    g_ada_b = tot[:n_dmod].reshape(L, 6 * D)
    off = n_dmod
    Gs = {}
    for n in SMALL:
        Gs[n] = tot[off:off + W[n].size].reshape(W[n].shape)
        off += W[n].size
    dmod_all = parts.reshape(N_DEV, 8 * wblk)[:, :n_dmod].reshape(N_DEV, L, N_CHIPS, w_ada)
    dmod_sh = lax.dynamic_index_in_dim(dmod_all, chip, axis=2, keepdims=False)
    dmod_sh = jnp.pad(dmod_sh, ((0, 8), (0, 0), (0, 0)))
    g_ada_w = jnp.stack([_mm1("ada_dw", c_act, dmod_sh[:, l], ta=True) for l in range(L)])

    grads = dict(Gb)
    grads.update(Gs)
    grads["ada_w"] = g_ada_w
    grads["ada_b"] = g_ada_b
    delta, new_m, new_v = {}, {}, {}
    for n in WEIGHTS:
        if n in SMALL or n == "ada_b":
            continue
        shp = W[n].shape
        delta[n], new_m[n], new_v[n] = [r.reshape(shp) for r in _adamw(
            "adamw_" + n, _flat(W[n]), _flat(grads[n]), _flat(Mo[n]), _flat(Vo[n]))]
    names_s = SMALL + ("ada_b",)
    cat = lambda d: jnp.concatenate([d[n].reshape(-1) for n in names_s])
    n_s = sum(W[n].size for n in names_s)
    ws = -(-n_s // (8 * LANE)) * LANE
    pk = lambda d: jnp.pad(cat(d), (0, 8 * ws - n_s)).reshape(8, ws)
    ds, ms_, vs = _adamw("adamw_small", pk(W), pk(grads), pk(Mo), pk(Vo))
    off = 0
    for n in names_s:
        sz, shp = W[n].size, W[n].shape
        delta[n] = ds.reshape(-1)[off:off + sz].reshape(shp)
        new_m[n] = ms_.reshape(-1)[off:off + sz].reshape(shp)
        new_v[n] = vs.reshape(-1)[off:off + sz].reshape(shp)
        off += sz

    return (loss, grad_x[None], *[grads[n].reshape(W[n].shape) for n in WEIGHTS], *[delta[n] for n in WEIGHTS],
            *[new_m[n] for n in WEIGHTS], *[new_v[n] for n in WEIGHTS])
```

```python
import functools

import numpy as np
import jax
import jax.numpy as jnp
from jax import lax
from jax.experimental import pallas as pl
from jax.experimental.pallas import tpu as pltpu

F32 = jnp.float32
BF16 = jnp.bfloat16
MXU_DTYPE = jnp.bfloat16
WIRE_DTYPE = jnp.bfloat16

HEAD_DIM = 128
ROPE_DIM = 64
CHUNK = 64
ROPE_THETA = 10000.0
LN_EPS = 1e-5
RMS_EPS = 1e-6
ADAM_LR, ADAM_B1, ADAM_B2, ADAM_EPS, ADAM_WD, ADAM_STEP = 0.001, 0.9, 0.999, 1e-08, 0.01, 10

N_CHIPS = 4
N_DEV = 8
LANE = 128
VMEM_LIMIT = 56 * 1024 * 1024
MESH = pl.DeviceIdType.MESH
ANY = pl.BlockSpec(memory_space=pl.ANY)
NEG = -1e30


def _params(**kw):
    return pltpu.CompilerParams(vmem_limit_bytes=VMEM_LIMIT, **kw)


def _pick(n, cands):
    for c in cands:
        if n % c == 0:
            return c
    return n


def _sigmoid(x):
    return 1.0 / (1.0 + jnp.exp(-x))


def _mm(name, terms, M, N, out_dtypes, epilogue=None, extras=(), row_extras=(), tm=512, tn=512, carry=None):
    tm = _pick(M, (tm, 256, 128))
    tn = _pick(N, (tn, 896, 768, 640, 384, 256, 128))
    extras = tuple(extras) + tuple(row_extras)
    n_row = len(row_extras)
    n_terms, n_ex, n_out = len(terms), len(extras), len(out_dtypes)
    n_acc = 1 + max(t[4] for t in terms)
    flags = [(t[2], t[3], t[4]) for t in terms]

    gi, gj = M // tm, N // tn

    def body(*refs):
        refs, c_in, c_out, c_sems = _carry_split(refs, 2 * n_terms + n_ex, n_out, 0, carry)
        pi, pj = pl.program_id(0), pl.program_id(1)
        _carry_run(carry, c_in, c_out, c_sems, jnp.logical_and(pi == 0, pj == 0), True)
        accs = [None] * n_acc
        for k, (ta, tb, ai) in enumerate(flags):
            a = refs[2 * k][...].astype(MXU_DTYPE)
            b = refs[2 * k + 1][...].astype(MXU_DTYPE)
            dn = (((0 if ta else 1,), (1 if tb else 0,)), ((), ()))
            r = lax.dot_general(a, b, dn, preferred_element_type=F32)
            accs[ai] = r if accs[ai] is None else accs[ai] + r
        ex = [refs[2 * n_terms + k][...] for k in range(n_ex)]
        outs = epilogue(accs, *ex) if epilogue is not None else (accs[0],)
        for k in range(n_out):
            o_ref = refs[2 * n_terms + n_ex + k]
            o_ref[...] = outs[k].astype(o_ref.dtype)
        _carry_run(carry, c_in, c_out, c_sems, jnp.logical_and(pi == gi - 1, pj == gj - 1), False)

    in_specs, args = [], []
    for (a, b, ta, tb, _, bcol) in terms:
        K = a.shape[0] if ta else a.shape[1]
        in_specs.append(pl.BlockSpec((K, tm), lambda i, j: (0, i)) if ta
                        else pl.BlockSpec((tm, K), lambda i, j: (i, 0)))
        if isinstance(b, tuple):
            b, lyr = b
            in_specs.append(pl.BlockSpec((None, tn, K), lambda i, j, o=bcol, l=lyr: (l, j + o, 0)) if tb
                            else pl.BlockSpec((None, K, tn), lambda i, j, o=bcol, l=lyr: (l, 0, j + o)))
        else:
            in_specs.append(pl.BlockSpec((tn, K), lambda i, j, o=bcol: (j + o, 0)) if tb
                            else pl.BlockSpec((K, tn), lambda i, j, o=bcol: (0, j + o)))
        args += [a, b]
    for k, e in enumerate(extras):
        in_specs.append(pl.BlockSpec((tm, tn), (lambda i, j: (i, 0)) if k >= n_ex - n_row else (lambda i, j: (i, j))))
        args.append(e)
    c_is, c_os, c_shape, c_sems, c_alias = _carry_call_args(carry, len(args), n_out)
    outs = pl.pallas_call(
        body, name=name, grid=(gi, gj), in_specs=in_specs + c_is,
        out_specs=[pl.BlockSpec((tm, tn), lambda i, j: (i, j)) for _ in out_dtypes] + c_os,
        out_shape=[jax.ShapeDtypeStruct((M, N), d) for d in out_dtypes] + c_shape,
        input_output_aliases=c_alias, scratch_shapes=c_sems,
        compiler_params=_params(),
    )(*args, *(carry["ins"] if carry else ()))
    return (outs[:n_out], list(outs[n_out:])) if carry else outs


def _wdim(b, axis):
    return b[0].shape[1 + axis] if isinstance(b, tuple) else b.shape[axis]


def _mm1(name, a, b, ta=False, tb=False, out_dtype=F32, bcol=0, N=None, **kw):
    M = a.shape[1] if ta else a.shape[0]
    if N is None:
        N = _wdim(b, 0 if tb else 1)
    res = _mm(name, [(a, b, ta, tb, 0, bcol)], M, N, [out_dtype], **kw)
    return (res[0][0], res[1]) if kw.get("carry") else res[0]


def _rowwise(name, fn, tiled, vecs, outs, reds=(), tr=128):
    R = tiled[0].shape[0]
    tr = _pick(R, (tr, 64, 32, 16, 8))
    nt, nv, no, nr = len(tiled), len(vecs), len(outs), len(reds)

    def body(*refs):
        vals = [r[...] for r in refs[:nt + nv]]
        res = fn(*vals)
        for k in range(no):
            o_ref = refs[nt + nv + k]
            o_ref[...] = res[k].astype(o_ref.dtype)
        if nr:
            first = pl.program_id(0) == 0
            for k in range(nr):
                r_ref = refs[nt + nv + no + k]

                @pl.when(first)
                def _(r_ref=r_ref, v=res[no + k]):
                    r_ref[...] = v

                @pl.when(jnp.logical_not(first))
                def _(r_ref=r_ref, v=res[no + k]):
                    r_ref[...] += v

    in_specs = [pl.BlockSpec((tr, t.shape[1]), lambda i: (i, 0)) for t in tiled]
    in_specs += [pl.BlockSpec(v.shape, lambda i, n=v.ndim: (0,) * n) for v in vecs]
    out_specs = [pl.BlockSpec((tr, w), lambda i: (i, 0)) for (w, _) in outs]
    out_specs += [pl.BlockSpec((1, w), lambda i: (0, 0)) for w in reds]
    out_shape = [jax.ShapeDtypeStruct((R, w), d) for (w, d) in outs]
    out_shape += [jax.ShapeDtypeStruct((1, w), F32) for w in reds]
    return pl.pallas_call(
        body, name=name, grid=(R // tr,), in_specs=in_specs, out_specs=out_specs, out_shape=out_shape,
        compiler_params=_params(),
    )(*tiled, *vecs)


def _colsum(v):
    return jnp.sum(v, axis=0, keepdims=True)


def _ln_stats(z):
    mu = jnp.mean(z, axis=-1, keepdims=True)
    zc = z - mu
    var = jnp.mean(zc * zc, axis=-1, keepdims=True)
    rstd = lax.rsqrt(var + LN_EPS)
    return zc * rstd, rstd


def _ln_bwd(dout, xhat, rstd, lg):
    dxh = dout * lg
    m1 = jnp.mean(dxh, axis=-1, keepdims=True)
    m2 = jnp.mean(dxh * xhat, axis=-1, keepdims=True)
    return rstd * (dxh - m1 - xhat * m2)


def _modulate(x, sc, sh):
    D = x.shape[1]
    return _rowwise("modulate", lambda x, sc, sh: ((x * (1.0 + sc) + sh),), [x], [sc, sh], [(D, MXU_DTYPE)])[0]


def _resid_ln_mod(x, y, g, lg, lb, sc_n, sh_n, alpha):
    D = x.shape[1]

    def fn(x, y, g, lg, lb, sc, sh):
        z = alpha * x + (1.0 + g) * y
        xhat, _ = _ln_stats(z)
        xo = xhat * lg + lb
        return z, xo, xo * (1.0 + sc) + sh

    return _rowwise("resid_ln_mod", fn, [x, y], [g, lg, lb, sc_n, sh_n], [(D, F32), (D, F32), (D, MXU_DTYPE)])


def _final_ln_loss(x, y, tgt, g, lg, lb, alpha):
    D = x.shape[1]

    def fn(x, y, t, g, lg, lb):
        z = alpha * x + (1.0 + g) * y
        xhat, rstd = _ln_stats(z)
        out = xhat * lg + lb
        err = out - t
        loss = jnp.sum(jnp.sum(err * err, axis=-1, keepdims=True), axis=0, keepdims=True)
        dout = err * (1.0 / D)
        dz = _ln_bwd(dout, xhat, rstd, lg)
        return (alpha * dz, (1.0 + g) * dz, jnp.broadcast_to(loss, (1, LANE)),
                _colsum(dout * xhat), _colsum(dout), _colsum(dz * y))

    return _rowwise("final_ln_loss", fn, [x, y, tgt], [g, lg, lb], [(D, F32), (D, MXU_DTYPE)], [LANE, D, D, D])


def _bwd_boundary(dx_res, dh, z_p, y_p, sc, g_p, lg_p, lb_p, alpha):
    D = dh.shape[1]

    def fn(dxr, dh, z, y, sc, g, lg, lb):
        xhat, rstd = _ln_stats(z)
        x_in = xhat * lg + lb
        dx = dxr + dh * (1.0 + sc)
        dz = _ln_bwd(dx, xhat, rstd, lg)
        return (alpha * dz, (1.0 + g) * dz,
                _colsum(dh * x_in), _colsum(dh), _colsum(dx * xhat), _colsum(dx), _colsum(dz * y))

    return _rowwise("bwd_boundary", fn, [dx_res, dh, z_p, y_p], [sc, g_p, lg_p, lb_p],
                    [(D, F32), (D, MXU_DTYPE)], [D, D, D, D, D])


def _first_bwd(dx_res, dh, x, sc):
    D = dh.shape[1]

    def fn(dxr, dh, x, sc):
        return dxr + dh * (1.0 + sc), _colsum(dh * x), _colsum(dh)

    return _rowwise("first_bwd", fn, [dx_res, dh, x], [sc], [(D, F32)], [D, D])


def _ride(ride, phase, bufs=None):
    if ride is None:
        return None
    bufs = ride["bufs"] if bufs is None else bufs
    return _gather_carry(bufs, ride["lays"], [0] * len(bufs), phase)


def _ffn_fwd(h, w1, w3, w2, ride=None):
    S, F = h.shape[0], _wdim(w1, 1)

    def epi(accs):
        a, b = accs
        return a, b, a * _sigmoid(a) * b

    res = _mm("ffn_up", [(h, w1, False, False, 0, 0), (h, w3, False, False, 1, 0)], S, F,
              [MXU_DTYPE, MXU_DTYPE, MXU_DTYPE], epilogue=epi, carry=_ride(ride, "ici"))
    (a, b, u), bufs = res if ride else (res, [])
    res = _mm1("ffn_down", u, w2, carry=_ride(ride, "d2d", bufs))
    y, bufs = res if ride else (res, [])
    return y, (a, b, u), bufs


def _ffn_bwd(dy, h, saved, w1, w3, w2):
    a, b, u = saved
    S, F = a.shape
    D = h.shape[1]

    def epi(accs, a, b):
        du = accs[0]
        a = a.astype(F32)
        b = b.astype(F32)
        sg = _sigmoid(a)
        return du * b * (sg * (1.0 + a * (1.0 - sg))), du * (a * sg)

    da, db = _mm("ffn_du", [(dy, w2, False, True, 0, 0)], S, F, [MXU_DTYPE, MXU_DTYPE], epilogue=epi, extras=(a, b))
    dw2 = _mm1("ffn_dw2", u, dy, ta=True, out_dtype=WIRE_DTYPE)
    dw1, dw3 = _mm("ffn_dw13", [(h, da, True, False, 0, 0), (h, db, True, False, 1, 0)], D, F,
                   [WIRE_DTYPE, WIRE_DTYPE], epilogue=lambda accs: (accs[0], accs[1]))
    dh = _mm("ffn_dh", [(da, w1, False, True, 0, 0), (db, w3, False, True, 0, 0)], S, D, [F32], tn=256)[0]
    return dh, dw1, dw3, dw2


def _rope_tables(pos):
    j = np.arange(LANE)
    invf = ROPE_THETA ** (-jnp.arange(0, ROPE_DIM, 2, dtype=F32) / ROPE_DIM)
    invf = invf[(j % ROPE_DIM) // 2].reshape(1, LANE)
    sgn = jnp.asarray(np.where(j % 2 == 0, -1.0, 1.0).reshape(1, LANE), F32)

    def fn(pos, invf, sgn):
        ang = pos.astype(F32) * invf
        return jnp.cos(ang), jnp.sin(ang) * sgn

    return _rowwise("rope_tables", fn, [pos], [invf, sgn], [(LANE, F32), (LANE, F32)], tr=256)


def _pair_swap(x):
    w = x.shape[1]
    even = (lax.broadcasted_iota(jnp.int32, x.shape, 1) % 2) == 0
    return jnp.where(even, pltpu.roll(x, w - 1, 1), pltpu.roll(x, 1, 1))


def _rope_fwd(x, c, s):
    return x * c + _pair_swap(x) * s


def _rope_bwd(d, c, s):
    return d * c + _pair_swap(d * s)


ATT_T = 256


def _dot_nt(a, b):
    return lax.dot_general(a, b, (((1,), (1,)), ((), ())), preferred_element_type=F32)


def _dot_tn(a, b):
    return lax.dot_general(a, b, (((0,), (0,)), ((), ())), preferred_element_type=F32)


def _dot_nn(a, b):
    return lax.dot_general(a, b, (((1,), (0,)), ((), ())), preferred_element_type=F32)


def _diag_mask(T, gran):
    r = lax.broadcasted_iota(jnp.int32, (T, T), 0)
    c = lax.broadcasted_iota(jnp.int32, (T, T), 1)
    if gran > 1:
        sh = int(np.log2(gran))
        r, c = lax.shift_right_logical(r, sh), lax.shift_right_logical(c, sh)
    return r >= c


def _attn_specs(S, H, T, mla, col_q, col_k, col_v):
    W = 2 * HEAD_DIM
    specs = [pl.BlockSpec((T, W), lambda p, i: (i, col_q + p))]
    if mla:
        specs.append(pl.BlockSpec((T, 2 * ROPE_DIM), lambda p, i: (i, p)))
    specs.append(pl.BlockSpec((S, W), lambda p, i: (0, col_k + p)))
    if mla:
        specs.append(pl.BlockSpec((S, ROPE_DIM), lambda p, i: (0, 0)))
    specs.append(pl.BlockSpec((S, W), lambda p, i: (0, col_v + p)))
    if not mla:
        specs.append(pl.BlockSpec((2, T, 1), lambda p, i: (p, i, 0)))
        specs.append(pl.BlockSpec((2, 1, S), lambda p, i: (p, 0, 0)))
    return specs


def _attn_fwd(name, S, H, mla, q, k, v, q_pe=None, k_pe=None, cum_col=None, cum_row=None, cols=(0, 0, 0),
              carry=None):
    T = _pick(S, (ATT_T, 128))
    nq = S // T
    scale = (HEAD_DIM + ROPE_DIM) ** -0.5 if mla else HEAD_DIM ** -0.5
    gran = CHUNK if mla else 1

    def body(*refs):
        refs, c_in, c_out, c_sems = _carry_split(refs, 5, 3, 3, carry)
        if mla:
            q_ref, qpe_ref, k_ref, kpe_ref, v_ref, o_ref, of_ref, lse_ref, m_s, l_s, acc_s = refs
        else:
            q_ref, k_ref, v_ref, cc_ref, cr_ref, o_ref, of_ref, lse_ref, m_s, l_s, acc_s = refs
        hp, qi = pl.program_id(0), pl.program_id(1)
        _carry_run(carry, c_in, c_out, c_sems, jnp.logical_and(hp == 0, qi == 0), True)
        hls = [slice(hh * HEAD_DIM, (hh + 1) * HEAD_DIM) for hh in range(2)]
        qn = [q_ref[:, hl] for hl in hls]
        qp = [qpe_ref[:, hh * ROPE_DIM:(hh + 1) * ROPE_DIM] for hh in range(2)] if mla else None
        m_s[...] = jnp.full(m_s.shape, NEG, F32)
        l_s[...] = jnp.zeros(l_s.shape, F32)
        acc_s[...] = jnp.zeros(acc_s.shape, F32)

        def step(j, masked):
            rows = pl.ds(pl.multiple_of(j * T, T), T)
            for hh, hl in enumerate(hls):
                s = _dot_nt(qn[hh], k_ref[rows, hl])
                if mla:
                    s = s + _dot_nt(qp[hh], kpe_ref[rows, :])
                s = s * scale
                if not mla:
                    s = s + (cc_ref[hh] - cr_ref[hh, :, rows])
                if masked:
                    s = jnp.where(_diag_mask(T, gran), s, NEG)
                m_old = m_s[hh, :, 0:1]
                m_new = jnp.maximum(m_old, jnp.max(s, axis=-1, keepdims=True))
                p = jnp.exp(s - m_new)
                corr = jnp.exp(m_old - m_new)
                l_s[hh] = jnp.broadcast_to(corr * l_s[hh, :, 0:1] + jnp.sum(p, axis=-1, keepdims=True), (T, LANE))
                p_hi = p.astype(MXU_DTYPE)
                p_lo = (p - p_hi.astype(F32)).astype(MXU_DTYPE)
                vf = v_ref[rows, hl]
                acc_s[hh] = corr * acc_s[hh] + (_dot_nn(p_hi, vf) + _dot_nn(p_lo, vf))
                m_s[hh] = jnp.broadcast_to(m_new, (T, LANE))

        lax.fori_loop(0, qi, lambda j, c: (step(j, False), c)[1], 0)
        step(qi, True)
        for hh, hl in enumerate(hls):
            l = l_s[hh, :, 0:1]
            of = acc_s[hh] / l
            of_ref[:, hl] = of
            o_ref[:, hl] = of.astype(o_ref.dtype)
            lse_ref[hh] = jnp.broadcast_to(m_s[hh, :, 0:1] + jnp.log(l), (T, LANE))
        _carry_run(carry, c_in, c_out, c_sems, jnp.logical_and(hp == H // 2 - 1, qi == nq - 1), False)

    args = [q] + ([q_pe] if mla else []) + [k] + ([k_pe] if mla else []) + [v]
    if not mla:
        args += [cum_col, cum_row]
    c_is, c_os, c_shape, c_sems, c_alias = _carry_call_args(carry, 5, 3)
    res = pl.pallas_call(
        body, name=name, grid=(H // 2, nq),
        in_specs=_attn_specs(S, H, T, mla, *cols) + c_is,
        out_specs=[pl.BlockSpec((T, 2 * HEAD_DIM), lambda p, i: (i, p)),
                   pl.BlockSpec((T, 2 * HEAD_DIM), lambda p, i: (i, p)),
                   pl.BlockSpec((2, T, LANE), lambda p, i: (p, i, 0))] + c_os,
        out_shape=[jax.ShapeDtypeStruct((S, H * HEAD_DIM), MXU_DTYPE), jax.ShapeDtypeStruct((S, H * HEAD_DIM), F32),
                   jax.ShapeDtypeStruct((H, S, LANE), F32)] + c_shape,
        input_output_aliases=c_alias,
        scratch_shapes=[pltpu.VMEM((2, T, LANE), F32), pltpu.VMEM((2, T, LANE), F32),
                        pltpu.VMEM((2, T, HEAD_DIM), F32)] + c_sems,
        compiler_params=_params(),
    )(*args, *(carry["ins"] if carry else ()))
    return res[0], res[1], res[2], list(res[3:])


def _carry_split(refs, n_in, n_out, n_scr, carry):
    if carry is None:
        return refs, (), (), ()
    ci, co = len(carry["ins"]), len(carry["out_shape"])
    own = refs[:n_in] + refs[n_in + ci:n_in + ci + n_out] + refs[n_in + ci + n_out + co:n_in + ci + n_out + co + n_scr]
    return (own, refs[n_in:n_in + ci], refs[n_in + ci + n_out:n_in + ci + n_out + co],
            refs[n_in + ci + n_out + co + n_scr:])


def _carry_run(carry, c_in, c_out, c_sems, when, start):
    if carry is None:
        return

    @pl.when(when)
    def _():
        for cp in carry["copies"](c_in, c_out, *c_sems):
            if start:
                cp.start()
            else:
                cp.wait()


def _carry_call_args(carry, n_in, n_out):
    if carry is None:
        return [], [], [], [], {}
    sems = [pltpu.SemaphoreType.DMA(carry["sems"]), pltpu.SemaphoreType.DMA(carry["sems"])]
    alias = {n_in + i: n_out + o for i, o in carry.get("alias", {}).items()}
    return [ANY] * len(carry["ins"]), [ANY] * len(carry["out_shape"]), list(carry["out_shape"]), sems, alias


def _attn_bwd(name, S, H, mla, q, k, v, do, of, lse, q_pe=None, k_pe=None, cum_col=None, cum_row=None,
              cols=(0, 0, 0), carry=None):
    T = _pick(S, (ATT_T, 128))
    nq = S // T
    scale = (HEAD_DIM + ROPE_DIM) ** -0.5 if mla else HEAD_DIM ** -0.5
    gran = CHUNK if mla else 1
    dqk = HEAD_DIM + (ROPE_DIM if mla else 0)

    def body(*refs):
        refs, c_in, c_out, c_sems = _carry_split(refs, 8, 5, 2, carry)
        if mla:
            (q_ref, qpe_ref, k_ref, kpe_ref, v_ref, do_ref, of_ref, lse_ref,
             dq_ref, dk_ref, dv_ref, dqpe_ref, dkpe_ref, dq_s, r_s) = refs
        else:
            (q_ref, k_ref, v_ref, cc_ref, cr_ref, do_ref, of_ref, lse_ref,
             dq_ref, dk_ref, dv_ref, dck_ref, dcq_ref, dq_s, r_s) = refs
        hp, qi = pl.program_id(0), pl.program_id(1)
        _carry_run(carry, c_in, c_out, c_sems, jnp.logical_and(hp == 0, qi == 0), True)

        @pl.when(qi == 0)
        def _():
            dk_ref[...] = jnp.zeros(dk_ref.shape, F32)
            dv_ref[...] = jnp.zeros(dv_ref.shape, F32)
            if not mla:
                dck_ref[...] = jnp.zeros(dck_ref.shape, F32)

        if mla:
            @pl.when(jnp.logical_and(qi == 0, hp == 0))
            def _():
                dkpe_ref[...] = jnp.zeros(dkpe_ref.shape, F32)

        hls = [slice(hh * HEAD_DIM, (hh + 1) * HEAD_DIM) for hh in range(2)]
        qn = [q_ref[:, hl] for hl in hls]
        qp = [qpe_ref[:, hh * ROPE_DIM:(hh + 1) * ROPE_DIM] for hh in range(2)] if mla else None
        dof = [do_ref[:, hl] for hl in hls]
        delta = [jnp.sum(dof[hh].astype(F32) * of_ref[:, hl], axis=-1, keepdims=True) for hh, hl in enumerate(hls)]
        lse = [lse_ref[hh][:, 0:1] for hh in range(2)]
        dq_s[...] = jnp.zeros(dq_s.shape, F32)
        r_s[...] = jnp.zeros(r_s.shape, F32)

        def step(j, masked):
            rows = pl.ds(pl.multiple_of(j * T, T), T)
            for hh, hl in enumerate(hls):
                kn = k_ref[rows, hl]
                s = _dot_nt(qn[hh], kn)
                if mla:
                    kp = kpe_ref[rows, :]
                    s = s + _dot_nt(qp[hh], kp)
                s = s * scale
                if not mla:
                    s = s + (cc_ref[hh] - cr_ref[hh, :, rows])
                if masked:
                    s = jnp.where(_diag_mask(T, gran), s, NEG)
                p = jnp.exp(s - lse[hh])
                dp = _dot_nt(dof[hh], v_ref[rows, hl])
                ds = p * (dp - delta[hh])
                dv_ref[rows, hl] += _dot_tn(p.astype(MXU_DTYPE), dof[hh])
                dsb = (ds * scale).astype(MXU_DTYPE)
                dk_ref[rows, hl] += _dot_tn(dsb, qn[hh])
                dq_s[hh, :, :HEAD_DIM] += _dot_nn(dsb, kn)
                if mla:
                    dkpe_ref[rows, :] += _dot_tn(dsb, qp[hh])
                    dq_s[hh, :, HEAD_DIM:] += _dot_nn(dsb, kp)
                else:
                    dck_ref[hh, :, rows] -= jnp.sum(ds, axis=0, keepdims=True)
                    r_s[hh] += jnp.broadcast_to(jnp.sum(ds, axis=-1, keepdims=True), (T, LANE))

        lax.fori_loop(0, qi, lambda j, c: (step(j, False), c)[1], 0)
        step(qi, True)
        for hh, hl in enumerate(hls):
            dq_ref[:, hl] = dq_s[hh, :, :HEAD_DIM].astype(dq_ref.dtype)
            if mla:
                dqpe_ref[:, hh * ROPE_DIM:(hh + 1) * ROPE_DIM] = dq_s[hh, :, HEAD_DIM:]
            else:
                dcq_ref[hh] = r_s[hh, :, 0:1]
        _carry_run(carry, c_in, c_out, c_sems, jnp.logical_and(hp == H // 2 - 1, qi == nq - 1), False)

    W = 2 * HEAD_DIM
    args = [q] + ([q_pe] if mla else []) + [k] + ([k_pe] if mla else []) + [v]
    if not mla:
        args += [cum_col, cum_row]
    args += [do, of, lse]
    in_specs = _attn_specs(S, H, T, mla, *cols)
    in_specs += [pl.BlockSpec((T, W), lambda p, i: (i, p)), pl.BlockSpec((T, W), lambda p, i: (i, p)),
                 pl.BlockSpec((2, T, LANE), lambda p, i: (p, i, 0))]
    out_specs = [pl.BlockSpec((T, W), lambda p, i: (i, p)), pl.BlockSpec((S, W), lambda p, i: (0, p)),
                 pl.BlockSpec((S, W), lambda p, i: (0, p))]
    out_shape = [jax.ShapeDtypeStruct((S, H * HEAD_DIM), MXU_DTYPE), jax.ShapeDtypeStruct((S, H * HEAD_DIM), F32),
                 jax.ShapeDtypeStruct((S, H * HEAD_DIM), F32)]
    if mla:
        out_specs += [pl.BlockSpec((T, 2 * ROPE_DIM), lambda p, i: (i, p)),
                      pl.BlockSpec((S, ROPE_DIM), lambda p, i: (0, 0))]
        out_shape += [jax.ShapeDtypeStruct((S, H * ROPE_DIM), F32), jax.ShapeDtypeStruct((S, ROPE_DIM), F32)]
    else:
        out_specs += [pl.BlockSpec((2, 1, S), lambda p, i: (p, 0, 0)), pl.BlockSpec((2, T, 1), lambda p, i: (p, i, 0))]
        out_shape += [jax.ShapeDtypeStruct((H, 1, S), F32), jax.ShapeDtypeStruct((H, S, 1), F32)]
    assert len(args) == 8 and len(out_shape) == 5
    c_is, c_os, c_shape, c_sems, c_alias = _carry_call_args(carry, 8, 5)
    res = pl.pallas_call(
        body, name=name, grid=(H // 2, nq), in_specs=in_specs + c_is, out_specs=out_specs + c_os,
        out_shape=out_shape + c_shape, input_output_aliases=c_alias,
        scratch_shapes=[pltpu.VMEM((2, T, dqk), F32), pltpu.VMEM((2, T, LANE), F32)] + c_sems,
        compiler_params=_params(),
    )(*args, *(carry["ins"] if carry else ()))
    return (res[:5], res[5:]) if carry else (res, [])


def _mla_prep(lat, cos, sin, qn, kvn, ql, kvl):
    def fn(lat, c, s, qn, kvn):
        ql_ = lat[:, :ql]
        kv_ = lat[:, ql:ql + kvl]
        kp = lat[:, ql + kvl:]
        cq = ql_ * lax.rsqrt(jnp.mean(ql_ * ql_, axis=-1, keepdims=True) + RMS_EPS) * qn
        ckv = kv_ * lax.rsqrt(jnp.mean(kv_ * kv_, axis=-1, keepdims=True) + RMS_EPS) * kvn
        kp2 = jnp.concatenate([kp, jnp.zeros_like(kp)], axis=-1)
        kr = _rope_fwd(kp2, c, s)[:, :ROPE_DIM]
        return cq, ckv, kr

    return _rowwise("mla_prep", fn, [lat, cos, sin], [qn, kvn],
                    [(ql, MXU_DTYPE), (kvl, MXU_DTYPE), (ROPE_DIM, MXU_DTYPE)])


def _mla_prep_bwd(lat, cos, sin, qn, kvn, dcq, dckv, dkr, ql, kvl):
    def fn(lat, c, s, dcq, dckv, dkr, qn, kvn):
        outs, reds = [], []
        for (x, g, d) in ((lat[:, :ql], qn, dcq), (lat[:, ql:ql + kvl], kvn, dckv)):
            r = lax.rsqrt(jnp.mean(x * x, axis=-1, keepdims=True) + RMS_EPS)
            n = x * r
            dn = d * g
            outs.append(r * (dn - n * jnp.mean(dn * n, axis=-1, keepdims=True)))
            reds.append(_colsum(d * n))
        d2 = jnp.concatenate([dkr, jnp.zeros_like(dkr)], axis=-1)
        outs.append(_rope_bwd(d2, c, s)[:, :ROPE_DIM])
        return (jnp.concatenate(outs, axis=-1), *reds)

    return _rowwise("mla_prep_bwd", fn, [lat, cos, sin, dcq, dckv, dkr], [qn, kvn],
                    [(ql + kvl + ROPE_DIM, MXU_DTYPE)], [ql, kvl])


def _mla_fwd(h, w, rc, ride=None):
    S = h.shape[0]
    ql, kvl = w["q_norm"].shape[1], w["kv_norm"].shape[1]
    H = _wdim(w["w_uk"], 1) // HEAD_DIM
    n_nope, n_pe = H * HEAD_DIM, H * ROPE_DIM
    lat = _mm1("mla_down", h, w["w_down"])
    cq, ckv, kr = _mla_prep(lat, rc[0], rc[1], w["q_norm"], w["kv_norm"], ql, kvl)
    q_nope = _mm1("mla_uq_nope", cq, w["w_uq"], out_dtype=MXU_DTYPE, N=n_nope)
    q_pe = _mm("mla_uq_pe", [(cq, w["w_uq"], False, False, 0, n_nope // LANE)], S, n_pe, [MXU_DTYPE],
               epilogue=lambda accs, c, s: (_rope_fwd(accs[0], c, s),), row_extras=rc, tn=LANE)[0]
    k_nope, v = _mm("mla_ukv", [(ckv, w["w_uk"], False, False, 0, 0), (ckv, w["w_uv"], False, False, 1, 0)],
                    S, n_nope, [MXU_DTYPE, MXU_DTYPE], epilogue=lambda accs: (accs[0], accs[1]))
    o, of, lse, bufs = _attn_fwd("mla_attn_fwd", S, H, True, q_nope, k_nope, v, q_pe=q_pe, k_pe=kr,
                                 carry=_ride(ride, "ici"))
    res = _mm1("mla_wo", o, w["w_o"], carry=_ride(ride, "d2d", bufs))
    y, bufs = res if ride else (res, [])
    return y, (lat, cq, ckv, kr, q_nope, q_pe, k_nope, v, o, of, lse), bufs


def _mla_bwd(dy, h, saved, w, rc, carry):
    lat, cq, ckv, kr, q_nope, q_pe, k_nope, v, o, of, lse = saved
    S = h.shape[0]
    ql, kvl = w["q_norm"].shape[1], w["kv_norm"].shape[1]
    H = _wdim(w["w_uk"], 1) // HEAD_DIM
    n_nope, n_pe = H * HEAD_DIM, H * ROPE_DIM
    do = _mm1("mla_do", dy, w["w_o"], tb=True, out_dtype=MXU_DTYPE)
    dw_o = _mm1("mla_dwo", o, dy, ta=True, out_dtype=WIRE_DTYPE)
    (dq_nope, dk_nope, dv, dq_pe_r, dk_pe_r), carried = _attn_bwd(
        "mla_attn_bwd", S, H, True, q_nope, k_nope, v, do, of, lse, q_pe=q_pe, k_pe=kr, carry=carry)

    def unrope(d, c, s):
        reps = (1, n_pe // LANE)
        return (_rope_bwd(d, jnp.tile(c, reps), jnp.tile(s, reps)),)

    dq_pe = _rowwise("mla_unrope_q", unrope, [dq_pe_r, rc[0], rc[1]], [], [(n_pe, MXU_DTYPE)])[0]
    dq = jnp.concatenate([dq_nope, dq_pe], axis=1)
    dw_uq = _mm1("mla_dwuq", cq, dq, ta=True, out_dtype=WIRE_DTYPE)
    dcq = _mm1("mla_dcq", dq, w["w_uq"], tb=True)
    dw_uk, dw_uv = _mm("mla_dwukv", [(ckv, dk_nope, True, False, 0, 0), (ckv, dv, True, False, 1, 0)], kvl, n_nope,
                       [WIRE_DTYPE, WIRE_DTYPE], epilogue=lambda accs: (accs[0], accs[1]))
    dckv = _mm("mla_dckv", [(dk_nope, w["w_uk"], False, True, 0, 0), (dv, w["w_uv"], False, True, 0, 0)],
               S, kvl, [F32])[0]
    dlat, dqn, dkvn = _mla_prep_bwd(lat, rc[0], rc[1], w["q_norm"], w["kv_norm"], dcq, dckv, dk_pe_r, ql, kvl)
    dw_down = _mm1("mla_dwdown", h, dlat, ta=True, out_dtype=WIRE_DTYPE)
    dh = _mm1("mla_dh", dlat, w["w_down"], tb=True)
    return dh, dict(w_down=dw_down, q_norm=dqn, w_uq=dw_uq, kv_norm=dkvn, w_uk=dw_uk, w_uv=dw_uv, w_o=dw_o), carried


def _log_sigmoid(z):
    return jnp.minimum(z, 0.0) - jnp.log(1.0 + jnp.exp(-jnp.abs(z)))


def _fox_gate_fwd(f, bf):
    S = f.shape[0]
    B = LANE

    def body(f_ref, b_ref, cum_ref):
        r = lax.broadcasted_iota(jnp.int32, (B, B), 0)
        c = lax.broadcasted_iota(jnp.int32, (B, B), 1)
        tri = (r >= c).astype(F32)
        carry = jnp.zeros((1, LANE), F32)
        for blk in range(S // B):
            rows = slice(blk * B, (blk + 1) * B)
            lf = _log_sigmoid(f_ref[rows, :] + b_ref[...])
            cs = jnp.dot(tri, lf, precision=lax.Precision.HIGHEST, preferred_element_type=F32) + carry
            cum_ref[rows, :] = cs
            carry = cs[B - 1:B, :]

    return pl.pallas_call(body, name="fox_gate_fwd", out_shape=jax.ShapeDtypeStruct((S, LANE), F32),
                          compiler_params=_params())(f, bf)


def _fox_gate_bwd(dcum, f, bf):
    S = f.shape[0]
    B = LANE

    def body(d_ref, f_ref, b_ref, df_ref, db_ref):
        r = lax.broadcasted_iota(jnp.int32, (B, B), 0)
        c = lax.broadcasted_iota(jnp.int32, (B, B), 1)
        tri = (r <= c).astype(F32)
        carry = jnp.zeros((1, LANE), F32)
        db = jnp.zeros((1, LANE), F32)
        for blk in reversed(range(S // B)):
            rows = slice(blk * B, (blk + 1) * B)
            dlf = jnp.dot(tri, d_ref[rows, :], precision=lax.Precision.HIGHEST, preferred_element_type=F32) + carry
            carry = dlf[0:1, :]
            z = f_ref[rows, :] + b_ref[...]
            dz = dlf * _sigmoid(-z)
            df_ref[rows, :] = dz.astype(df_ref.dtype)
            db = db + jnp.sum(dz, axis=0, keepdims=True)
        db_ref[...] = db

    return pl.pallas_call(body, name="fox_gate_bwd",
                          out_shape=[jax.ShapeDtypeStruct((S, LANE), MXU_DTYPE), jax.ShapeDtypeStruct((1, LANE), F32)],
                          compiler_params=_params())(dcum, f, bf)


def _fox_fwd(h, w, ride=None):
    S, D = h.shape
    H = D // HEAD_DIM
    qkv = _mm1("fox_qkv", h, w["w_in"], out_dtype=MXU_DTYPE, N=3 * D)
    f = _mm1("fox_f", h, w["w_in"], bcol=3 * D // LANE, N=LANE, tn=LANE)
    cum = _fox_gate_fwd(f, w["b_f"])
    cumT = cum[:, :H].T
    cum_col, cum_row = cumT.reshape(H, S, 1), cumT.reshape(H, 1, S)
    nb = D // (2 * HEAD_DIM)
    o, of, lse, bufs = _attn_fwd("fox_attn_fwd", S, H, False, qkv, qkv, qkv, cum_col=cum_col, cum_row=cum_row,
                                 cols=(0, nb, 2 * nb), carry=_ride(ride, "ici"))
    res = _mm1("fox_wo", o, w["w_o"], carry=_ride(ride, "d2d", bufs))
    y, bufs = res if ride else (res, [])
    return y, (qkv, f, cum_col, cum_row, o, of, lse), bufs


def _fox_bwd(dy, h, saved, w, carry):
    qkv, f, cum_col, cum_row, o, of, lse = saved
    S, D = h.shape
    H = D // HEAD_DIM
    nb = D // (2 * HEAD_DIM)
    do = _mm1("fox_do", dy, w["w_o"], tb=True, out_dtype=MXU_DTYPE)
    dw_o = _mm1("fox_dwo", o, dy, ta=True, out_dtype=WIRE_DTYPE)
    (dq, dk, dv, dck, dcq), carried = _attn_bwd("fox_attn_bwd", S, H, False, qkv, qkv, qkv, do, of, lse,
                                                cum_col=cum_col, cum_row=cum_row, cols=(0, nb, 2 * nb), carry=carry)
    dcum = jnp.pad((dck.reshape(H, S) + dcq.reshape(H, S)).T, ((0, 0), (0, LANE - H)))
    df, dbf = _fox_gate_bwd(dcum, f, w["b_f"])
    dproj = jnp.concatenate([dq, dk.astype(MXU_DTYPE), dv.astype(MXU_DTYPE), df], axis=1)
    dw_in = _mm1("fox_dwin", h, dproj, ta=True, out_dtype=WIRE_DTYPE)
    dh = _mm1("fox_dh", dproj, w["w_in"], tb=True, tn=256)
    return dh, dict(w_in=dw_in, b_f=dbf[:, :H], w_o=dw_o), carried


def _place():
    x, y, c = lax.axis_index("x"), lax.axis_index("y"), lax.axis_index("c")
    return x, y, c, [(1 - x, y), (x, 1 - y), (1 - x, 1 - y)]


def _ag_small(name, blk):
    m, n = blk.shape

    def body(x_ref, out_ref, send_sems, recv_sems, local_sem):
        x, y, c, chips = _place()
        me, sibling = (x, y, c), (x, y, 1 - c)

        def rows(px, py, pc):
            return out_ref.at[pl.ds((4 * px + 2 * py + pc) * m, m), :]

        def copy(k, block, to, src=None):
            return pltpu.make_async_remote_copy(
                src_ref=rows(*block) if src is None else src, dst_ref=rows(*block),
                send_sem=send_sems.at[k], recv_sem=recv_sems.at[k], device_id=to, device_id_type=MESH)

        mine = pltpu.make_async_copy(x_ref, rows(*me), local_sem)
        mine.start()
        first = [copy(0, me, sibling, src=x_ref)]
        first += [copy(1 + j, me, (*chip, c), src=x_ref) for j, chip in enumerate(chips)]
        for cp in first:
            cp.start()
        passed = [copy(4 + j, (*chip, c), sibling) for j, chip in enumerate(chips)]
        for j, chip in enumerate(chips):
            copy(1 + j, (*chip, c), me).wait_recv()
            passed[j].start()
        copy(0, sibling, me).wait_recv()
        for j, chip in enumerate(chips):
            copy(4 + j, (*chip, 1 - c), me).wait_recv()
        for cp in first + passed:
            cp.wait_send()
        mine.wait()

    return pl.pallas_call(
        body, name=name, out_shape=jax.ShapeDtypeStruct((N_DEV * m, n), blk.dtype),
        in_specs=[pl.BlockSpec(memory_space=pltpu.VMEM)], out_specs=pl.BlockSpec(memory_space=pltpu.VMEM),
        scratch_shapes=[pltpu.SemaphoreType.DMA((7,)), pltpu.SemaphoreType.DMA((7,)), pltpu.SemaphoreType.DMA],
        compiler_params=_params(),
    )(blk)


def _half(ref, row_axis, c, rows):
    idx = [slice(None)] * len(ref.shape)
    idx[row_axis] = pl.ds(pl.multiple_of(c * rows, 16), rows)
    return ref.at[tuple(idx)]


def _shard(ref, layout, k):
    if layout == "row":
        return ref.at[:, k]
    w = ref.shape[2] // N_CHIPS
    return ref.at[:, :, pl.ds(pl.multiple_of(k * w, LANE), w)]


def _full_shape(shape, layout):
    L, r, w = shape
    return (L, N_CHIPS, r, w) if layout == "row" else (L, r, N_CHIPS * w)


def _cast_full(name, a, layout, k_idx, lyr):
    _, r, C = a.shape
    tr = _pick(r, (256, 128, 64, 32, 16))

    def body(k_ref, a_ref, o_ref):
        o_ref[...] = a_ref[...].astype(o_ref.dtype)

    if layout == "row":
        o_spec = pl.BlockSpec((None, None, tr, C), lambda l, i, k: (0, k[0], i, 0))
    else:
        o_spec = pl.BlockSpec((None, tr, C), lambda l, i, k: (0, i, k[0]))
    return pl.pallas_call(
        body, name=name,
        grid_spec=pltpu.PrefetchScalarGridSpec(
            num_scalar_prefetch=1, grid=(1, r // tr),
            in_specs=[pl.BlockSpec((None, tr, C), lambda l, i, k: (lyr, i, 0))], out_specs=o_spec),
        out_shape=jax.ShapeDtypeStruct(_full_shape((1, r, C), layout), WIRE_DTYPE),
        compiler_params=_params(),
    )(k_idx, a)


def _gather_carry(fulls, layouts, lyrs, phase):
    n = len(fulls)
    half_rows = [f.shape[2 if lay == "row" else 1] // 2 for f, lay in zip(fulls, layouts)]

    def copies(ins, outs, send_sems, recv_sems):
        x, y, c, chips = _place()
        cps = []
        for i in range(n):
            for j, chip in enumerate(chips):
                who, to = ((x, y), (*chip, c)) if phase == "ici" else (chip, (x, y, 1 - c))
                w = _half(_shard(outs[i].at[pl.ds(lyrs[i], 1)], layouts[i], 2 * who[0] + who[1]), 1, c, half_rows[i])
                cps.append(pltpu.make_async_remote_copy(
                    src_ref=w, dst_ref=w, send_sem=send_sems.at[i, j], recv_sem=recv_sems.at[i, j], device_id=to,
                    device_id_type=MESH))
        return cps

    return dict(ins=list(fulls), sems=(n, 3), copies=copies, alias={i: i for i in range(n)},
                out_shape=[jax.ShapeDtypeStruct(f.shape, f.dtype) for f in fulls])


def _run_carry(name, carry):
    n = len(carry["ins"])

    def body(*refs):
        cps = carry["copies"](refs[:n], refs[n:2 * n], *refs[2 * n:])
        for cp in cps:
            cp.start()
        for cp in cps:
            cp.wait()

    return pl.pallas_call(
        body, name=name, in_specs=[ANY] * n, out_specs=[ANY] * len(carry["out_shape"]), out_shape=carry["out_shape"],
        input_output_aliases=dict(carry.get("alias", {})),
        scratch_shapes=[pltpu.SemaphoreType.DMA(carry["sems"]), pltpu.SemaphoreType.DMA(carry["sems"])],
        compiler_params=_params(),
    )(*carry["ins"])


def _gather_weights(fulls, layouts, lyrs):
    n = len(fulls)
    half_rows = [f.shape[2 if lay == "row" else 1] // 2 for f, lay in zip(fulls, layouts)]

    def body(*refs):
        outs = refs[n:2 * n]
        send_sems, recv_sems = refs[2 * n:]
        x, y, c, chips = _place()
        sibling = (x, y, 1 - c)

        def window(i, kx, ky, half):
            return _half(_shard(outs[i].at[pl.ds(lyrs[i], 1)], layouts[i], 2 * kx + ky), 1, half, half_rows[i])

        first, passed = [], []
        for i in range(n):
            mine = window(i, x, y, c)
            for j, chip in enumerate(chips):
                cp = pltpu.make_async_remote_copy(
                    src_ref=mine, dst_ref=mine, send_sem=send_sems.at[i, j], recv_sem=recv_sems.at[i, j],
                    device_id=(*chip, c), device_id_type=MESH)
                cp.start()
                first.append(cp)
        for i in range(n):
            for j, chip in enumerate(chips):
                got = window(i, *chip, c)
                pltpu.make_async_remote_copy(
                    src_ref=got, dst_ref=got, send_sem=send_sems.at[i, j], recv_sem=recv_sems.at[i, j],
                    device_id=(*chip, c), device_id_type=MESH).wait_recv()
                cp = pltpu.make_async_remote_copy(
                    src_ref=got, dst_ref=got, send_sem=send_sems.at[i, 3 + j], recv_sem=recv_sems.at[i, 3 + j],
                    device_id=sibling, device_id_type=MESH)
                cp.start()
                passed.append(cp)
        for i in range(n):
            for j, chip in enumerate(chips):
                got = window(i, *chip, 1 - c)
                pltpu.make_async_remote_copy(
                    src_ref=got, dst_ref=got, send_sem=send_sems.at[i, 3 + j], recv_sem=recv_sems.at[i, 3 + j],
                    device_id=sibling, device_id_type=MESH).wait_recv()
        for cp in first + passed:
            cp.wait_send()

    return pl.pallas_call(
        body, name="gather_weights", in_specs=[ANY] * n, out_specs=[ANY] * n,
        out_shape=[jax.ShapeDtypeStruct(f.shape, f.dtype) for f in fulls],
        input_output_aliases={i: i for i in range(n)},
        scratch_shapes=[pltpu.SemaphoreType.DMA((n, 6)), pltpu.SemaphoreType.DMA((n, 6))],
        compiler_params=_params(),
    )(*fulls)


def _half_shape(shape, layout):
    s = list(shape)
    s[2 if layout == "row" else 1] //= 2
    return tuple(s)


def _swap_halves(name, grads, layouts):
    n = len(grads)
    row_axis = [2 if lay == "row" else 1 for lay in layouts]
    half_rows = [g.shape[ra] // 2 for g, ra in zip(grads, row_axis)]

    def body(*refs):
        ins, outs = refs[:n], refs[n:2 * n]
        send_sems, recv_sems = refs[2 * n:]
        x, y, c, _ = _place()
        cps = []
        for i in range(n):
            cp = pltpu.make_async_remote_copy(
                src_ref=_half(ins[i], row_axis[i], 1 - c, half_rows[i]), dst_ref=outs[i],
                send_sem=send_sems.at[i], recv_sem=recv_sems.at[i], device_id=(x, y, 1 - c), device_id_type=MESH)
            cp.start()
            cps.append(cp)
        for cp in cps:
            cp.wait()

    return pl.pallas_call(
        body, name=name, in_specs=[ANY] * n, out_specs=[ANY] * n,
        out_shape=[jax.ShapeDtypeStruct(_half_shape(g.shape, lay), g.dtype) for g, lay in zip(grads, layouts)],
        scratch_shapes=[pltpu.SemaphoreType.DMA((n,)), pltpu.SemaphoreType.DMA((n,))],
        compiler_params=_params(),
    )(*grads)


def _add_half(name, g, r, layout, c_idx):
    L = g.shape[0]
    if layout == "row":
        A, rows, W = L * N_CHIPS, g.shape[2] // 2, g.shape[3]
    else:
        A, rows, W = L, g.shape[1] // 2, g.shape[2]
    g3 = g.reshape(A, 2 * rows, W)
    r3 = r.reshape(A, rows, W)
    tr = _pick(rows, (256, 128, 64, 32, 16))
    nb = rows // tr

    def body(c_ref, g_ref, r_ref, o_ref):
        o_ref[...] = (g_ref[...].astype(F32) + r_ref[...].astype(F32)).astype(o_ref.dtype)

    out = pl.pallas_call(
        body, name=name,
        grid_spec=pltpu.PrefetchScalarGridSpec(
            num_scalar_prefetch=1, grid=(A, nb),
            in_specs=[pl.BlockSpec((None, tr, W), lambda a, i, c: (a, c[0] * nb + i, 0)),
                      pl.BlockSpec((None, tr, W), lambda a, i, c: (a, i, 0))],
            out_specs=pl.BlockSpec((None, tr, W), lambda a, i, c: (a, i, 0))),
        out_shape=jax.ShapeDtypeStruct((A, rows, W), WIRE_DTYPE),
        compiler_params=_params(),
    )(c_idx, g3, r3)
    return out.reshape(r.shape)


def _exchange_carry(parts, layouts):
    n = len(parts)

    def shard_half_shape(p, lay):
        if lay == "row":
            return (p.shape[0],) + p.shape[2:]
        return (p.shape[0], p.shape[1], p.shape[2] // N_CHIPS)

    def copies(ins, outs, send_sems, recv_sems):
        x, y, c, chips = _place()
        return [pltpu.make_async_remote_copy(
            src_ref=_shard(ins[i], layouts[i], 2 * kx + ky), dst_ref=outs[i].at[j], send_sem=send_sems.at[i, j],
            recv_sem=recv_sems.at[i, j], device_id=(kx, ky, c), device_id_type=MESH)
            for i in range(n) for j, (kx, ky) in enumerate(chips)]

    return dict(ins=list(parts), sems=(n, 3), copies=copies,
                out_shape=[jax.ShapeDtypeStruct((3,) + shard_half_shape(p, lay), p.dtype)
                           for p, lay in zip(parts, layouts)])


def _exchange_shards(parts, layouts):
    n = len(parts)
    carry = _exchange_carry(parts, layouts)

    def body(*refs):
        cps = carry["copies"](refs[:n], refs[n:2 * n], *refs[2 * n:])
        for cp in cps:
            cp.start()
        for cp in cps:
            cp.wait()

    return pl.pallas_call(
        body, name="exchange_shards", in_specs=[ANY] * n, out_specs=[ANY] * n, out_shape=carry["out_shape"],
        scratch_shapes=[pltpu.SemaphoreType.DMA((n, 3)), pltpu.SemaphoreType.DMA((n, 3))],
        compiler_params=_params(),
    )(*parts)


def _sum_shards(name, p, r, layout, kc_idx, dst, lyr, n_lyr):
    rows, W = r.shape[2], r.shape[3]
    tr = _pick(rows, (256, 128, 64, 32, 16))
    nb = rows // tr

    def body(kc_ref, p_ref, r_ref, *rest):
        acc = p_ref[...].astype(F32)
        for j in range(3):
            acc = acc + r_ref[j].astype(F32)
        rest[-1][...] = acc

    if layout == "row":
        p_spec = pl.BlockSpec((None, None, tr, W), lambda a, i, kc: (0, kc[0], i, 0))
    else:
        p_spec = pl.BlockSpec((None, tr, W), lambda a, i, kc: (0, i, kc[0]))
    in_specs = [p_spec, pl.BlockSpec((3, None, tr, W), lambda a, i, kc: (0, 0, i, 0))]
    args = [kc_idx, p, r]
    if dst is not None:
        in_specs.append(ANY)
        args.append(dst)
    return pl.pallas_call(
        body, name=name,
        grid_spec=pltpu.PrefetchScalarGridSpec(
            num_scalar_prefetch=1, grid=(1, nb), in_specs=in_specs,
            out_specs=pl.BlockSpec((None, tr, W), lambda a, i, kc: (lyr, kc[1] * nb + i, 0))),
        out_shape=jax.ShapeDtypeStruct((n_lyr, 2 * rows, W), F32),
        input_output_aliases={3: 0} if dst is not None else {},
        compiler_params=_params(),
    )(*args)


def _join_halves(shards):
    n = len(shards)

    def body(*refs):
        outs = refs[n:2 * n]
        send_sems, recv_sems = refs[2 * n:]
        x, y, c, _ = _place()
        cps = []
        for i in range(n):
            mine = _half(outs[i], 1, c, outs[i].shape[1] // 2)
            cp = pltpu.make_async_remote_copy(
                src_ref=mine, dst_ref=mine, send_sem=send_sems.at[i], recv_sem=recv_sems.at[i],
                device_id=(x, y, 1 - c), device_id_type=MESH)
            cp.start()
            cps.append(cp)
        for cp in cps:
            cp.wait()

    return pl.pallas_call(
        body, name="join_halves", in_specs=[ANY] * n, out_specs=[ANY] * n,
        out_shape=[jax.ShapeDtypeStruct(s.shape, s.dtype) for s in shards],
        input_output_aliases={i: i for i in range(n)},
        scratch_shapes=[pltpu.SemaphoreType.DMA((n,)), pltpu.SemaphoreType.DMA((n,))],
        compiler_params=_params(),
    )(*shards)


def _ada_fwd(c_all, ada_w, ada_b):
    L, D, w = ada_w.shape
    tn = _pick(w, (512, 256, 128))

    def body(c_ref, w_ref, b_ref, o_ref, a_ref):
        c = c_ref[...]
        act = (c * _sigmoid(c)).astype(MXU_DTYPE)
        a_ref[...] = act
        o_ref[...] = jnp.dot(act, w_ref[...].astype(MXU_DTYPE), preferred_element_type=F32) + b_ref[...]

    return pl.pallas_call(
        body, name="ada_fwd", grid=(L, w // tn),
        in_specs=[pl.BlockSpec((16, D), lambda l, j: (0, 0)), pl.BlockSpec((None, D, tn), lambda l, j: (l, 0, j)),
                  pl.BlockSpec((None, 1, tn), lambda l, j: (l, 0, j))],
        out_specs=[pl.BlockSpec((None, 16, tn), lambda l, j: (l, 0, j)), pl.BlockSpec((16, D), lambda l, j: (0, 0))],
        out_shape=[jax.ShapeDtypeStruct((L, 16, w), F32), jax.ShapeDtypeStruct((16, D), MXU_DTYPE)],
        compiler_params=_params(),
    )(c_all, ada_w, ada_b)


def _sum_devices(name, parts):
    n, R, W = parts.shape
    tw = _pick(W, (2048, 1024, 512, 256, 128))

    def body(p_ref, o_ref):
        acc = p_ref[0]
        for d in range(1, n):
            acc = acc + p_ref[d]
        o_ref[...] = acc

    return pl.pallas_call(
        body, name=name, grid=(W // tw,), in_specs=[pl.BlockSpec((n, R, tw), lambda j: (0, 0, j))],
        out_specs=pl.BlockSpec((R, tw), lambda j: (0, j)), out_shape=jax.ShapeDtypeStruct((R, W), F32),
        compiler_params=_params(),
    )(parts)


def _adamw(name, w, g, m, v):
    W = w.shape[1]
    bc1 = 1.0 - ADAM_B1 ** ADAM_STEP
    bc2 = 1.0 - ADAM_B2 ** ADAM_STEP

    def fn(w, g, m, v):
        m2 = ADAM_B1 * m + (1.0 - ADAM_B1) * g
        v2 = ADAM_B2 * v + (1.0 - ADAM_B2) * (g * g)
        delta = -ADAM_LR * ((m2 / bc1) / (jnp.sqrt(v2 / bc2) + ADAM_EPS) + ADAM_WD * w)
        return delta, m2, v2

    tr = 256 if W <= 1024 else (128 if W <= 2048 else 64)
    return _rowwise(name, fn, [w, g, m, v], [], [(W, F32)] * 3, tr=tr)


def _flat(a):
    return a.reshape(-1, a.shape[-1])


BIG = ("ffn_w1", "ffn_w3", "ffn_w2", "mla_w_down", "mla_w_uq", "mla_w_uk", "mla_w_uv", "mla_w_o", "fox_w_in",
       "fox_w_o")
LAYOUT = dict(ffn_w1="col", ffn_w3="col", ffn_w2="row", mla_w_down="row", mla_w_uq="col", mla_w_uk="col",
              mla_w_uv="col", mla_w_o="row", fox_w_in="row", fox_w_o="row")
FFN = ("ffn_w1", "ffn_w3", "ffn_w2")
MLA = ("mla_w_down", "mla_w_uq", "mla_w_uk", "mla_w_uv", "mla_w_o")
FOX = ("fox_w_in", "fox_w_o")
SMALL = ("ln1_g", "ln1_b", "ln2_g", "ln2_b", "mla_q_norm", "mla_kv_norm", "fox_b_f")
WEIGHTS = ("ada_w", "ada_b", "ln1_g", "ln1_b", "ln2_g", "ln2_b", "ffn_w1", "ffn_w3", "ffn_w2", "mla_w_down",
           "mla_q_norm", "mla_w_uq", "mla_kv_norm", "mla_w_uk", "mla_w_uv", "mla_w_o", "fox_w_in", "fox_b_f",
           "fox_w_o")


def _uq_perm(H):
    d = HEAD_DIM + ROPE_DIM
    nope = (np.arange(H)[:, None] * d + np.arange(HEAD_DIM)[None, :]).reshape(-1)
    pe = (np.arange(H)[:, None] * d + HEAD_DIM + np.arange(ROPE_DIM)[None, :]).reshape(-1)
    return np.concatenate([nope, pe])


def kernel(x, c, positions, ada_w, ada_b, ln1_g, ln1_b, ln2_g, ln2_b, ffn_w1, ffn_w3, ffn_w2, mla_w_down, mla_q_norm, mla_w_uq, mla_kv_norm, mla_w_uk, mla_w_uv, mla_w_o, fox_w_in, fox_b_f, fox_w_o, loss_target, m_ada_w, m_ada_b, m_ln1_g, m_ln1_b, m_ln2_g, m_ln2_b, m_ffn_w1, m_ffn_w3, m_ffn_w2, m_mla_w_down, m_mla_q_norm, m_mla_w_uq, m_mla_kv_norm, m_mla_w_uk, m_mla_w_uv, m_mla_w_o, m_fox_w_in, m_fox_b_f, m_fox_w_o, v_ada_w, v_ada_b, v_ln1_g, v_ln1_b, v_ln2_g, v_ln2_b, v_ffn_w1, v_ffn_w3, v_ffn_w2, v_mla_w_down, v_mla_q_norm, v_mla_w_uq, v_mla_kv_norm, v_mla_w_uk, v_mla_w_uv, v_mla_w_o, v_fox_w_in, v_fox_b_f, v_fox_w_o):
    W = dict(ada_w=ada_w, ada_b=ada_b, ln1_g=ln1_g, ln1_b=ln1_b, ln2_g=ln2_g, ln2_b=ln2_b, ffn_w1=ffn_w1,
             ffn_w3=ffn_w3, ffn_w2=ffn_w2, mla_w_down=mla_w_down, mla_q_norm=mla_q_norm, mla_w_uq=mla_w_uq,
             mla_kv_norm=mla_kv_norm, mla_w_uk=mla_w_uk, mla_w_uv=mla_w_uv, mla_w_o=mla_w_o, fox_w_in=fox_w_in,
             fox_b_f=fox_b_f, fox_w_o=fox_w_o)
    Mo = dict(ada_w=m_ada_w, ada_b=m_ada_b, ln1_g=m_ln1_g, ln1_b=m_ln1_b, ln2_g=m_ln2_g, ln2_b=m_ln2_b,
              ffn_w1=m_ffn_w1, ffn_w3=m_ffn_w3, ffn_w2=m_ffn_w2, mla_w_down=m_mla_w_down, mla_q_norm=m_mla_q_norm,
              mla_w_uq=m_mla_w_uq, mla_kv_norm=m_mla_kv_norm, mla_w_uk=m_mla_w_uk, mla_w_uv=m_mla_w_uv,
              mla_w_o=m_mla_w_o, fox_w_in=m_fox_w_in, fox_b_f=m_fox_b_f, fox_w_o=m_fox_w_o)
    Vo = dict(ada_w=v_ada_w, ada_b=v_ada_b, ln1_g=v_ln1_g, ln1_b=v_ln1_b, ln2_g=v_ln2_g, ln2_b=v_ln2_b,
              ffn_w1=v_ffn_w1, ffn_w3=v_ffn_w3, ffn_w2=v_ffn_w2, mla_w_down=v_mla_w_down, mla_q_norm=v_mla_q_norm,
              mla_w_uq=v_mla_w_uq, mla_kv_norm=v_mla_kv_norm, mla_w_uk=v_mla_w_uk, mla_w_uv=v_mla_w_uv,
              mla_w_o=v_mla_w_o, fox_w_in=v_fox_w_in, fox_b_f=v_fox_b_f, fox_w_o=v_fox_w_o)

    S, D = x.shape[1], x.shape[2]
    L = ada_w.shape[0]
    alpha = float((2 * L) ** 0.25)
    H_mla = mla_w_uk.shape[2] * N_CHIPS // HEAD_DIM
    H_fox = D // HEAD_DIM
    xi, yi, ci = lax.axis_index("x"), lax.axis_index("y"), lax.axis_index("c")
    chip = 2 * xi + yi
    dev = 2 * chip + ci
    c_idx = jnp.reshape(ci, (1,)).astype(jnp.int32)
    x0, tgt = x[0], loss_target[0]
    pos = positions.reshape(S, 1)

    c_all = _ag_small("gather_c", jnp.pad(c, ((0, 7), (0, 0)))).reshape(N_DEV, 8, D)[:, 0]
    w_ada = ada_w.shape[2]
    ada_b_sh = lax.dynamic_slice_in_dim(ada_b, chip * w_ada, w_ada, axis=1).reshape(L, 1, w_ada)
    mod_sh, c_act = _ada_fwd(jnp.pad(c_all, ((0, 8), (0, 0))), ada_w, ada_b_sh)
    mod_all = _ag_small("gather_mod", mod_sh.transpose(1, 0, 2).reshape(16, L * w_ada))
    mod_all = mod_all.reshape(N_CHIPS, 2, 16, L, w_ada)[:, 0]
    mod = lax.dynamic_index_in_dim(mod_all, dev, axis=1, keepdims=False)
    mod = mod.transpose(1, 0, 2).reshape(L, 6, 1, D)

    kc_idx = jnp.stack([chip, ci]).astype(jnp.int32)
    raw = {n: [_cast_full("cast_" + n, W[n], LAYOUT[n], kc_idx, l) for l in range(W[n].shape[0])] for n in BIG}
    full = {n: [None] * W[n].shape[0] for n in BIG}
    perm = _uq_perm(H_mla)
    n_in = fox_w_in.shape[2] * N_CHIPS

    def group(i):
        return FFN + (MLA if i % 2 == 0 else FOX)

    def ride_of(names, i):
        return dict(bufs=[raw[n][i if n in FFN else i // 2] for n in names], lays=[LAYOUT[n] for n in names])

    def land(names, i, bufs):
        for n, f in zip(names, bufs):
            if n == "fox_w_in":
                fw = f.reshape(N_CHIPS, D, -1).transpose(1, 0, 2).reshape(D, n_in)
                f = jnp.pad(fw, ((0, 0), (0, 3 * D + LANE - n_in)))[None]
            elif n == "mla_w_uq":
                f = f[:, :, perm]
            elif LAYOUT[n] == "row":
                f = f.reshape(1, f.shape[1] * f.shape[2], f.shape[3])
            full[n][i if n in FFN else i // 2] = f

    r0 = ride_of(group(0), 0)
    land(group(0), 0, _gather_weights(r0["bufs"], r0["lays"], [0] * len(r0["bufs"])))

    rc = tuple(_rope_tables(pos))

    def mixer_w(i):
        j = i // 2
        if i % 2 == 0:
            return dict(w_down=(full["mla_w_down"][j], 0), q_norm=mla_q_norm[j:j + 1], w_uq=(full["mla_w_uq"][j], 0),
                        kv_norm=mla_kv_norm[j:j + 1], w_uk=(full["mla_w_uk"][j], 0), w_uv=(full["mla_w_uv"][j], 0),
                        w_o=(full["mla_w_o"][j], 0))
        return dict(w_in=(full["fox_w_in"][j], 0), b_f=jnp.pad(fox_b_f[j:j + 1], ((0, 0), (0, LANE - H_fox))),
                    w_o=(full["fox_w_o"][j], 0))

    saved = []
    xc = x0
    h = _modulate(x0, mod[0, 1], mod[0, 0])
    for i in range(L):
        mw = mixer_w(i)
        nxt = i + 1 < L
        mix_n = group(i + 1)[3:]
        if i % 2 == 0:
            y1, ms, got = _mla_fwd(h, mw, rc, ride_of(FFN, i + 1) if nxt else None)
        else:
            y1, ms, got = _fox_fwd(h, mw, ride_of(FFN, i + 1) if nxt else None)
        land(FFN, i + 1, got)
        z1, x1, h2 = _resid_ln_mod(xc, y1, mod[i, 2], ln1_g[i:i + 1], ln1_b[i:i + 1], mod[i, 4], mod[i, 3], alpha)
        y2, fs, got = _ffn_fwd(h2, (full["ffn_w1"][i], 0), (full["ffn_w3"][i], 0), (full["ffn_w2"][i], 0),
                               ride_of(mix_n, i + 1) if nxt else None)
        land(mix_n, i + 1, got)
        rec = dict(h1=h, ms=ms, y1=y1, z1=z1, h2=h2, fs=fs, y2=y2)
        if i + 1 < L:
            z2, xc, h = _resid_ln_mod(x1, y2, mod[i, 5], ln2_g[i:i + 1], ln2_b[i:i + 1], mod[i + 1, 1],
                                      mod[i + 1, 0], alpha)
            rec["z2"] = z2
        else:
            dx_res, dy, loss_v, dlg, dlb, dgate = _final_ln_loss(x1, y2, tgt, mod[i, 5], ln2_g[i:i + 1],
                                                                 ln2_b[i:i + 1], alpha)
        saved.append(rec)
    loss = lax.psum(loss_v[0, 0] * (0.5 / D), ("x", "y", "c"))

    G = {n: [None] * W[n].shape[0] for n in SMALL}
    dmod = [[None] * 6 for _ in range(L)]
    red = {n: None for n in BIG}
    inv_perm = np.argsort(perm)

    def rs_view(n, g):
        if n == "mla_w_uq":
            g = g[:, inv_perm]
        if n == "fox_w_in":
            g = g[:, :n_in].reshape(D, N_CHIPS, n_in // N_CHIPS).transpose(1, 0, 2)
        elif LAYOUT[n] == "row":
            g = g.reshape(N_CHIPS, g.shape[0] // N_CHIPS, g.shape[1])
        return g[None]

    def half_sums(tag, names, grads):
        lays = [LAYOUT[n] for n in names]
        gs = [rs_view(n, g) for n, g in zip(names, grads)]
        recv = _swap_halves("swap_" + tag, gs, lays)
        return [_add_half("add_half_" + n, g, r, lay, c_idx) for n, g, r, lay in zip(names, gs, recv, lays)]

    def finish(names, parts, recv2, lyr):
        for n, p, r in zip(names, parts, recv2):
            red[n] = _sum_shards("sum_shards_" + n, p, r, LAYOUT[n], kc_idx, red[n], lyr, W[n].shape[0])

    pending = None
    for i in reversed(range(L)):
        rec = saved[i]
        mw = mixer_w(i)
        j = i // 2
        G["ln2_g"][i], G["ln2_b"][i], dmod[i][5] = dlg, dlb, dgate
        dh2, dw1, dw3, dw2 = _ffn_bwd(dy, rec["h2"], rec["fs"], (full["ffn_w1"][i], 0), (full["ffn_w3"][i], 0),
                                      (full["ffn_w2"][i], 0))
        ffn_parts = half_sums("ffn", FFN, (dw1, dw3, dw2))
        dx_res, dy, dmod[i][4], dmod[i][3], G["ln1_g"][i], G["ln1_b"][i], dmod[i][2] = _bwd_boundary(
            dx_res, dh2, rec["z1"], rec["y1"], mod[i, 4], mod[i, 2], ln1_g[i:i + 1], ln1_b[i:i + 1], alpha)
        ride_names = FFN + (pending[0] if pending else ())
        ride_parts = ffn_parts + (pending[1] if pending else [])
        carry = _exchange_carry(ride_parts, [LAYOUT[n] for n in ride_names])
        if i % 2 == 0:
            dh1, gm, recv2 = _mla_bwd(dy, rec["h1"], rec["ms"], mw, rc, carry)
            names, pre = MLA, "mla_"
            G["mla_q_norm"][j], G["mla_kv_norm"][j] = gm["q_norm"], gm["kv_norm"]
        else:
            dh1, gm, recv2 = _fox_bwd(dy, rec["h1"], rec["ms"], mw, carry)
            names, pre = FOX, "fox_"
            G["fox_b_f"][j] = gm["b_f"]
        finish(FFN, ffn_parts, recv2[:3], i)
        if pending:
            finish(pending[0], pending[1], recv2[3:], pending[2])
        pending = (names, half_sums(pre[:-1], names, [gm[n[len(pre):]] for n in names]), j)
        if i > 0:
            p = saved[i - 1]
            dx_res, dy, dmod[i][1], dmod[i][0], dlg, dlb, dgate = _bwd_boundary(
                dx_res, dh1, p["z2"], p["y2"], mod[i, 1], mod[i - 1, 5], ln2_g[i - 1:i], ln2_b[i - 1:i], alpha)
        else:
            grad_x, dmod[i][1], dmod[i][0] = _first_bwd(dx_res, dh1, x0, mod[i, 1])
    finish(pending[0], pending[1], _exchange_shards(pending[1], [LAYOUT[n] for n in pending[0]]), pending[2])
    Gb = dict(zip(BIG, _join_halves([red[n] for n in BIG])))

    small = jnp.concatenate([jnp.concatenate([g.reshape(-1) for g in G[n]]) for n in SMALL])
    dmod_v = jnp.concatenate([jnp.concatenate([d.reshape(-1) for d in row]) for row in dmod])
    n_small, n_dmod = small.shape[0], dmod_v.shape[0]
    wblk = -(-(n_small + n_dmod) // (8 * LANE)) * LANE
    blk = jnp.pad(jnp.concatenate([dmod_v, small]), (0, 8 * wblk - n_small - n_dmod)).reshape(8, wblk)
    parts = _ag_small("gather_small", blk).reshape(N_DEV, 8, wblk)
    tot = _sum_devices("sum_small", parts).reshape(-1)
    g_ada_b = tot[:n_dmod].reshape(L, 6 * D)
    off = n_dmod
    Gs = {}
    for n in SMALL:
        Gs[n] = tot[off:off + W[n].size].reshape(W[n].shape)
        off += W[n].size
    dmod_all = parts.reshape(N_DEV, 8 * wblk)[:, :n_dmod].reshape(N_DEV, L, N_CHIPS, w_ada)
    dmod_sh = lax.dynamic_index_in_dim(dmod_all, chip, axis=2, keepdims=False)
    dmod_sh = jnp.pad(dmod_sh, ((0, 8), (0, 0), (0, 0)))
    g_ada_w = jnp.stack([_mm1("ada_dw", c_act, dmod_sh[:, l], ta=True) for l in range(L)])

    grads = dict(Gb)
    grads.update(Gs)
    grads["ada_w"] = g_ada_w
    grads["ada_b"] = g_ada_b
    delta, new_m, new_v = {}, {}, {}
    for n in WEIGHTS:
        if n in SMALL or n == "ada_b":
            continue
        shp = W[n].shape
        delta[n], new_m[n], new_v[n] = [r.reshape(shp) for r in _adamw(
            "adamw_" + n, _flat(W[n]), _flat(grads[n]), _flat(Mo[n]), _flat(Vo[n]))]
    names_s = SMALL + ("ada_b",)
    cat = lambda d: jnp.concatenate([d[n].reshape(-1) for n in names_s])
    n_s = sum(W[n].size for n in names_s)
    ws = -(-n_s // (8 * LANE)) * LANE
    pk = lambda d: jnp.pad(cat(d), (0, 8 * ws - n_s)).reshape(8, ws)
    ds, ms_, vs = _adamw("adamw_small", pk(W), pk(grads), pk(Mo), pk(Vo))
    off = 0
    for n in names_s:
        sz, shp = W[n].size, W[n].shape
        delta[n] = ds.reshape(-1)[off:off + sz].reshape(shp)
        new_m[n] = ms_.reshape(-1)[off:off + sz].reshape(shp)
        new_v[n] = vs.reshape(-1)[off:off + sz].reshape(shp)
        off += sz

    return (loss, grad_x[None], *[grads[n].reshape(W[n].shape) for n in WEIGHTS], *[delta[n] for n in WEIGHTS],
            *[new_m[n] for n in WEIGHTS], *[new_v[n] for n in WEIGHTS])
```

```python
import functools

import numpy as np
import jax
import jax.numpy as jnp
from jax import lax
from jax.experimental import pallas as pl
from jax.experimental.pallas import tpu as pltpu

F32 = jnp.float32
BF16 = jnp.bfloat16
MXU_DTYPE = jnp.bfloat16
WIRE_DTYPE = jnp.bfloat16

HEAD_DIM = 128
ROPE_DIM = 64
CHUNK = 64
ROPE_THETA = 10000.0
LN_EPS = 1e-5
RMS_EPS = 1e-6
ADAM_LR, ADAM_B1, ADAM_B2, ADAM_EPS, ADAM_WD, ADAM_STEP = 0.001, 0.9, 0.999, 1e-08, 0.01, 10

N_CHIPS = 4
N_DEV = 8
LANE = 128
VMEM_LIMIT = 56 * 1024 * 1024
MESH = pl.DeviceIdType.MESH
ANY = pl.BlockSpec(memory_space=pl.ANY)
NEG = -1e30


def _params(**kw):
    return pltpu.CompilerParams(vmem_limit_bytes=VMEM_LIMIT, **kw)


def _pick(n, cands):
    for c in cands:
        if n % c == 0:
            return c
    return n


def _sigmoid(x):
    return 1.0 / (1.0 + jnp.exp(-x))


def _mm(name, terms, M, N, out_dtypes, epilogue=None, extras=(), row_extras=(), tm=512, tn=512, carry=None):
    tm = _pick(M, (tm, 256, 128))
    tn = _pick(N, (tn, 896, 768, 640, 384, 256, 128))
    extras = tuple(extras) + tuple(row_extras)
    n_row = len(row_extras)
    n_terms, n_ex, n_out = len(terms), len(extras), len(out_dtypes)
    n_acc = 1 + max(t[4] for t in terms)
    flags = [(t[2], t[3], t[4]) for t in terms]

    gi, gj = M // tm, N // tn

    def body(*refs):
        refs, c_in, c_out, c_sems = _carry_split(refs, 2 * n_terms + n_ex, n_out, 0, carry)
        pi, pj = pl.program_id(0), pl.program_id(1)
        _carry_run(carry, c_in, c_out, c_sems, jnp.logical_and(pi == 0, pj == 0), True)
        accs = [None] * n_acc
        for k, (ta, tb, ai) in enumerate(flags):
            a = refs[2 * k][...].astype(MXU_DTYPE)
            b = refs[2 * k + 1][...].astype(MXU_DTYPE)
            dn = (((0 if ta else 1,), (1 if tb else 0,)), ((), ()))
            r = lax.dot_general(a, b, dn, preferred_element_type=F32)
            accs[ai] = r if accs[ai] is None else accs[ai] + r
        ex = [refs[2 * n_terms + k][...] for k in range(n_ex)]
        outs = epilogue(accs, *ex) if epilogue is not None else (accs[0],)
        for k in range(n_out):
            o_ref = refs[2 * n_terms + n_ex + k]
            o_ref[...] = outs[k].astype(o_ref.dtype)
        _carry_run(carry, c_in, c_out, c_sems, jnp.logical_and(pi == gi - 1, pj == gj - 1), False)

    in_specs, args = [], []
    for (a, b, ta, tb, _, bcol) in terms:
        K = a.shape[0] if ta else a.shape[1]
        in_specs.append(pl.BlockSpec((K, tm), lambda i, j: (0, i)) if ta
                        else pl.BlockSpec((tm, K), lambda i, j: (i, 0)))
        if isinstance(b, tuple):
            b, lyr = b
            in_specs.append(pl.BlockSpec((None, tn, K), lambda i, j, o=bcol, l=lyr: (l, j + o, 0)) if tb
                            else pl.BlockSpec((None, K, tn), lambda i, j, o=bcol, l=lyr: (l, 0, j + o)))
        else:
            in_specs.append(pl.BlockSpec((tn, K), lambda i, j, o=bcol: (j + o, 0)) if tb
                            else pl.BlockSpec((K, tn), lambda i, j, o=bcol: (0, j + o)))
        args += [a, b]
    for k, e in enumerate(extras):
        in_specs.append(pl.BlockSpec((tm, tn), (lambda i, j: (i, 0)) if k >= n_ex - n_row else (lambda i, j: (i, j))))
        args.append(e)
    c_is, c_os, c_shape, c_sems, c_alias = _carry_call_args(carry, len(args), n_out)
    outs = pl.pallas_call(
        body, name=name, grid=(gi, gj), in_specs=in_specs + c_is,
        out_specs=[pl.BlockSpec((tm, tn), lambda i, j: (i, j)) for _ in out_dtypes] + c_os,
        out_shape=[jax.ShapeDtypeStruct((M, N), d) for d in out_dtypes] + c_shape,
        input_output_aliases=c_alias, scratch_shapes=c_sems,
        compiler_params=_params(),
    )(*args, *(carry["ins"] if carry else ()))
    return (outs[:n_out], list(outs[n_out:])) if carry else outs


def _wdim(b, axis):
    return b[0].shape[1 + axis] if isinstance(b, tuple) else b.shape[axis]


def _mm1(name, a, b, ta=False, tb=False, out_dtype=F32, bcol=0, N=None, **kw):
    M = a.shape[1] if ta else a.shape[0]
    if N is None:
        N = _wdim(b, 0 if tb else 1)
    res = _mm(name, [(a, b, ta, tb, 0, bcol)], M, N, [out_dtype], **kw)
    return (res[0][0], res[1]) if kw.get("carry") else res[0]


def _rowwise(name, fn, tiled, vecs, outs, reds=(), tr=128):
    R = tiled[0].shape[0]
    tr = _pick(R, (tr, 64, 32, 16, 8))
    nt, nv, no, nr = len(tiled), len(vecs), len(outs), len(reds)

    def body(*refs):
        vals = [r[...] for r in refs[:nt + nv]]
        res = fn(*vals)
        for k in range(no):
            o_ref = refs[nt + nv + k]
            o_ref[...] = res[k].astype(o_ref.dtype)
        if nr:
            first = pl.program_id(0) == 0
            for k in range(nr):
                r_ref = refs[nt + nv + no + k]

                @pl.when(first)
                def _(r_ref=r_ref, v=res[no + k]):
                    r_ref[...] = v

                @pl.when(jnp.logical_not(first))
                def _(r_ref=r_ref, v=res[no + k]):
                    r_ref[...] += v

    in_specs = [pl.BlockSpec((tr, t.shape[1]), lambda i: (i, 0)) for t in tiled]
    in_specs += [pl.BlockSpec(v.shape, lambda i, n=v.ndim: (0,) * n) for v in vecs]
    out_specs = [pl.BlockSpec((tr, w), lambda i: (i, 0)) for (w, _) in outs]
    out_specs += [pl.BlockSpec((1, w), lambda i: (0, 0)) for w in reds]
    out_shape = [jax.ShapeDtypeStruct((R, w), d) for (w, d) in outs]
    out_shape += [jax.ShapeDtypeStruct((1, w), F32) for w in reds]
    return pl.pallas_call(
        body, name=name, grid=(R // tr,), in_specs=in_specs, out_specs=out_specs, out_shape=out_shape,
        compiler_params=_params(),
    )(*tiled, *vecs)


def _colsum(v):
    return jnp.sum(v, axis=0, keepdims=True)


def _ln_stats(z):
    mu = jnp.mean(z, axis=-1, keepdims=True)
    zc = z - mu
    var = jnp.mean(zc * zc, axis=-1, keepdims=True)
    rstd = lax.rsqrt(var + LN_EPS)
    return zc * rstd, rstd


def _ln_bwd(dout, xhat, rstd, lg):
    dxh = dout * lg
    m1 = jnp.mean(dxh, axis=-1, keepdims=True)
    m2 = jnp.mean(dxh * xhat, axis=-1, keepdims=True)
    return rstd * (dxh - m1 - xhat * m2)


def _modulate(x, sc, sh):
    D = x.shape[1]
    return _rowwise("modulate", lambda x, sc, sh: ((x * (1.0 + sc) + sh),), [x], [sc, sh], [(D, MXU_DTYPE)])[0]


def _resid_ln_mod(x, y, g, lg, lb, sc_n, sh_n, alpha):
    D = x.shape[1]

    def fn(x, y, g, lg, lb, sc, sh):
        z = alpha * x + (1.0 + g) * y
        xhat, _ = _ln_stats(z)
        xo = xhat * lg + lb
        return z, xo, xo * (1.0 + sc) + sh

    return _rowwise("resid_ln_mod", fn, [x, y], [g, lg, lb, sc_n, sh_n], [(D, F32), (D, F32), (D, MXU_DTYPE)])


def _final_ln_loss(x, y, tgt, g, lg, lb, alpha):
    D = x.shape[1]

    def fn(x, y, t, g, lg, lb):
        z = alpha * x + (1.0 + g) * y
        xhat, rstd = _ln_stats(z)
        out = xhat * lg + lb
        err = out - t
        loss = jnp.sum(jnp.sum(err * err, axis=-1, keepdims=True), axis=0, keepdims=True)
        dout = err * (1.0 / D)
        dz = _ln_bwd(dout, xhat, rstd, lg)
        return (alpha * dz, (1.0 + g) * dz, jnp.broadcast_to(loss, (1, LANE)),
                _colsum(dout * xhat), _colsum(dout), _colsum(dz * y))

    return _rowwise("final_ln_loss", fn, [x, y, tgt], [g, lg, lb], [(D, F32), (D, MXU_DTYPE)], [LANE, D, D, D])


def _bwd_boundary(dx_res, dh, z_p, y_p, sc, g_p, lg_p, lb_p, alpha):
    D = dh.shape[1]

    def fn(dxr, dh, z, y, sc, g, lg, lb):
        xhat, rstd = _ln_stats(z)
        x_in = xhat * lg + lb
        dx = dxr + dh * (1.0 + sc)
        dz = _ln_bwd(dx, xhat, rstd, lg)
        return (alpha * dz, (1.0 + g) * dz,
                _colsum(dh * x_in), _colsum(dh), _colsum(dx * xhat), _colsum(dx), _colsum(dz * y))

    return _rowwise("bwd_boundary", fn, [dx_res, dh, z_p, y_p], [sc, g_p, lg_p, lb_p],
                    [(D, F32), (D, MXU_DTYPE)], [D, D, D, D, D])


def _first_bwd(dx_res, dh, x, sc):
    D = dh.shape[1]

    def fn(dxr, dh, x, sc):
        return dxr + dh * (1.0 + sc), _colsum(dh * x), _colsum(dh)

    return _rowwise("first_bwd", fn, [dx_res, dh, x], [sc], [(D, F32)], [D, D])


def _ride(ride, phase, bufs=None):
    if ride is None:
        return None
    bufs = ride["bufs"] if bufs is None else bufs
    return _gather_carry(bufs, ride["lays"], [0] * len(bufs), phase)


def _ffn_fwd(h, w1, w3, w2, ride=None):
    S, F = h.shape[0], _wdim(w1, 1)

    def epi(accs):
        a, b = accs
        return a, b, a * _sigmoid(a) * b

    res = _mm("ffn_up", [(h, w1, False, False, 0, 0), (h, w3, False, False, 1, 0)], S, F,
              [MXU_DTYPE, MXU_DTYPE, MXU_DTYPE], epilogue=epi, carry=_ride(ride, "ici"))
    (a, b, u), bufs = res if ride else (res, [])
    res = _mm1("ffn_down", u, w2, carry=_ride(ride, "d2d", bufs))
    y, bufs = res if ride else (res, [])
    return y, (a, b, u), bufs


def _ffn_bwd(dy, h, saved, w1, w3, w2, carry_du=None, carry_dh=None):
    a, b, u = saved
    S, F = a.shape
    D = h.shape[1]

    def epi(accs, a, b):
        du = accs[0]
        a = a.astype(F32)
        b = b.astype(F32)
        sg = _sigmoid(a)
        return du * b * (sg * (1.0 + a * (1.0 - sg))), du * (a * sg)

    res = _mm("ffn_du", [(dy, w2, False, True, 0, 0)], S, F, [MXU_DTYPE, MXU_DTYPE], epilogue=epi, extras=(a, b),
              carry=carry_du)
    (da, db), got_du = res if carry_du else (res, [])
    dw2 = _mm1("ffn_dw2", u, dy, ta=True, out_dtype=WIRE_DTYPE)
    dw1, dw3 = _mm("ffn_dw13", [(h, da, True, False, 0, 0), (h, db, True, False, 1, 0)], D, F,
                   [WIRE_DTYPE, WIRE_DTYPE], epilogue=lambda accs: (accs[0], accs[1]))
    res = _mm("ffn_dh", [(da, w1, False, True, 0, 0), (db, w3, False, True, 0, 0)], S, D, [F32], tn=256,
              carry=carry_dh)
    (dh,), got_dh = res if carry_dh else (res, [])
    return dh, dw1, dw3, dw2, got_du, got_dh


def _rope_tables(pos):
    j = np.arange(LANE)
    invf = ROPE_THETA ** (-jnp.arange(0, ROPE_DIM, 2, dtype=F32) / ROPE_DIM)
    invf = invf[(j % ROPE_DIM) // 2].reshape(1, LANE)
    sgn = jnp.asarray(np.where(j % 2 == 0, -1.0, 1.0).reshape(1, LANE), F32)

    def fn(pos, invf, sgn):
        ang = pos.astype(F32) * invf
        return jnp.cos(ang), jnp.sin(ang) * sgn

    return _rowwise("rope_tables", fn, [pos], [invf, sgn], [(LANE, F32), (LANE, F32)], tr=256)


def _pair_swap(x):
    w = x.shape[1]
    even = (lax.broadcasted_iota(jnp.int32, x.shape, 1) % 2) == 0
    return jnp.where(even, pltpu.roll(x, w - 1, 1), pltpu.roll(x, 1, 1))


def _rope_fwd(x, c, s):
    return x * c + _pair_swap(x) * s


def _rope_bwd(d, c, s):
    return d * c + _pair_swap(d * s)


ATT_T = 256


def _dot_nt(a, b):
    return lax.dot_general(a, b, (((1,), (1,)), ((), ())), preferred_element_type=F32)


def _dot_tn(a, b):
    return lax.dot_general(a, b, (((0,), (0,)), ((), ())), preferred_element_type=F32)


def _dot_nn(a, b):
    return lax.dot_general(a, b, (((1,), (0,)), ((), ())), preferred_element_type=F32)


def _diag_mask(T, gran):
    r = lax.broadcasted_iota(jnp.int32, (T, T), 0)
    c = lax.broadcasted_iota(jnp.int32, (T, T), 1)
    if gran > 1:
        sh = int(np.log2(gran))
        r, c = lax.shift_right_logical(r, sh), lax.shift_right_logical(c, sh)
    return r >= c


def _attn_specs(S, H, T, mla, col_q, col_k, col_v):
    W = 2 * HEAD_DIM
    specs = [pl.BlockSpec((T, W), lambda p, i: (i, col_q + p))]
    if mla:
        specs.append(pl.BlockSpec((T, 2 * ROPE_DIM), lambda p, i: (i, p)))
    specs.append(pl.BlockSpec((S, W), lambda p, i: (0, col_k + p)))
    if mla:
        specs.append(pl.BlockSpec((S, ROPE_DIM), lambda p, i: (0, 0)))
    specs.append(pl.BlockSpec((S, W), lambda p, i: (0, col_v + p)))
    if not mla:
        specs.append(pl.BlockSpec((2, T, 1), lambda p, i: (p, i, 0)))
        specs.append(pl.BlockSpec((2, 1, S), lambda p, i: (p, 0, 0)))
    return specs


def _attn_fwd(name, S, H, mla, q, k, v, q_pe=None, k_pe=None, cum_col=None, cum_row=None, cols=(0, 0, 0),
              carry=None):
    T = _pick(S, (ATT_T, 128))
    nq = S // T
    scale = (HEAD_DIM + ROPE_DIM) ** -0.5 if mla else HEAD_DIM ** -0.5
    gran = CHUNK if mla else 1

    def body(*refs):
        refs, c_in, c_out, c_sems = _carry_split(refs, 5, 3, 3, carry)
        if mla:
            q_ref, qpe_ref, k_ref, kpe_ref, v_ref, o_ref, of_ref, lse_ref, m_s, l_s, acc_s = refs
        else:
            q_ref, k_ref, v_ref, cc_ref, cr_ref, o_ref, of_ref, lse_ref, m_s, l_s, acc_s = refs
        hp, qi = pl.program_id(0), pl.program_id(1)
        _carry_run(carry, c_in, c_out, c_sems, jnp.logical_and(hp == 0, qi == 0), True)
        hls = [slice(hh * HEAD_DIM, (hh + 1) * HEAD_DIM) for hh in range(2)]
        qn = [q_ref[:, hl] for hl in hls]
        qp = [qpe_ref[:, hh * ROPE_DIM:(hh + 1) * ROPE_DIM] for hh in range(2)] if mla else None
        m_s[...] = jnp.full(m_s.shape, NEG, F32)
        l_s[...] = jnp.zeros(l_s.shape, F32)
        acc_s[...] = jnp.zeros(acc_s.shape, F32)

        def step(j, masked):
            rows = pl.ds(pl.multiple_of(j * T, T), T)
            for hh, hl in enumerate(hls):
                s = _dot_nt(qn[hh], k_ref[rows, hl])
                if mla:
                    s = s + _dot_nt(qp[hh], kpe_ref[rows, :])
                s = s * scale
                if not mla:
                    s = s + (cc_ref[hh] - cr_ref[hh, :, rows])
                if masked:
                    s = jnp.where(_diag_mask(T, gran), s, NEG)
                m_old = m_s[hh, :, 0:1]
                m_new = jnp.maximum(m_old, jnp.max(s, axis=-1, keepdims=True))
                p = jnp.exp(s - m_new)
                corr = jnp.exp(m_old - m_new)
                l_s[hh] = jnp.broadcast_to(corr * l_s[hh, :, 0:1] + jnp.sum(p, axis=-1, keepdims=True), (T, LANE))
                p_hi = p.astype(MXU_DTYPE)
                p_lo = (p - p_hi.astype(F32)).astype(MXU_DTYPE)
                vf = v_ref[rows, hl]
                acc_s[hh] = corr * acc_s[hh] + (_dot_nn(p_hi, vf) + _dot_nn(p_lo, vf))
                m_s[hh] = jnp.broadcast_to(m_new, (T, LANE))

        lax.fori_loop(0, qi, lambda j, c: (step(j, False), c)[1], 0)
        step(qi, True)
        for hh, hl in enumerate(hls):
            l = l_s[hh, :, 0:1]
            of = acc_s[hh] / l
            of_ref[:, hl] = of
            o_ref[:, hl] = of.astype(o_ref.dtype)
            lse_ref[hh] = jnp.broadcast_to(m_s[hh, :, 0:1] + jnp.log(l), (T, LANE))
        _carry_run(carry, c_in, c_out, c_sems, jnp.logical_and(hp == H // 2 - 1, qi == nq - 1), False)

    args = [q] + ([q_pe] if mla else []) + [k] + ([k_pe] if mla else []) + [v]
    if not mla:
        args += [cum_col, cum_row]
    c_is, c_os, c_shape, c_sems, c_alias = _carry_call_args(carry, 5, 3)
    res = pl.pallas_call(
        body, name=name, grid=(H // 2, nq),
        in_specs=_attn_specs(S, H, T, mla, *cols) + c_is,
        out_specs=[pl.BlockSpec((T, 2 * HEAD_DIM), lambda p, i: (i, p)),
                   pl.BlockSpec((T, 2 * HEAD_DIM), lambda p, i: (i, p)),
                   pl.BlockSpec((2, T, LANE), lambda p, i: (p, i, 0))] + c_os,
        out_shape=[jax.ShapeDtypeStruct((S, H * HEAD_DIM), MXU_DTYPE), jax.ShapeDtypeStruct((S, H * HEAD_DIM), F32),
                   jax.ShapeDtypeStruct((H, S, LANE), F32)] + c_shape,
        input_output_aliases=c_alias,
        scratch_shapes=[pltpu.VMEM((2, T, LANE), F32), pltpu.VMEM((2, T, LANE), F32),
                        pltpu.VMEM((2, T, HEAD_DIM), F32)] + c_sems,
        compiler_params=_params(),
    )(*args, *(carry["ins"] if carry else ()))
    return res[0], res[1], res[2], list(res[3:])


def _carry_split(refs, n_in, n_out, n_scr, carry):
    if carry is None:
        return refs, (), (), ()
    ci, co = len(carry["ins"]), len(carry["out_shape"])
    own = refs[:n_in] + refs[n_in + ci:n_in + ci + n_out] + refs[n_in + ci + n_out + co:n_in + ci + n_out + co + n_scr]
    return (own, refs[n_in:n_in + ci], refs[n_in + ci + n_out:n_in + ci + n_out + co],
            refs[n_in + ci + n_out + co + n_scr:])


def _carry_run(carry, c_in, c_out, c_sems, when, start):
    if carry is None:
        return

    @pl.when(when)
    def _():
        for cp in carry["copies"](c_in, c_out, *c_sems):
            if start:
                cp.start()
            else:
                cp.wait()


def _carry_call_args(carry, n_in, n_out):
    if carry is None:
        return [], [], [], [], {}
    sems = [pltpu.SemaphoreType.DMA(carry["sems"]), pltpu.SemaphoreType.DMA(carry["sems"])]
    alias = {n_in + i: n_out + o for i, o in carry.get("alias", {}).items()}
    return [ANY] * len(carry["ins"]), [ANY] * len(carry["out_shape"]), list(carry["out_shape"]), sems, alias


def _attn_bwd(name, S, H, mla, q, k, v, do, of, lse, q_pe=None, k_pe=None, cum_col=None, cum_row=None,
              cols=(0, 0, 0), carry=None):
    T = _pick(S, (ATT_T, 128))
    nq = S // T
    scale = (HEAD_DIM + ROPE_DIM) ** -0.5 if mla else HEAD_DIM ** -0.5
    gran = CHUNK if mla else 1
    dqk = HEAD_DIM + (ROPE_DIM if mla else 0)

    def body(*refs):
        refs, c_in, c_out, c_sems = _carry_split(refs, 8, 5, 2, carry)
        if mla:
            (q_ref, qpe_ref, k_ref, kpe_ref, v_ref, do_ref, of_ref, lse_ref,
             dq_ref, dk_ref, dv_ref, dqpe_ref, dkpe_ref, dq_s, r_s) = refs
        else:
            (q_ref, k_ref, v_ref, cc_ref, cr_ref, do_ref, of_ref, lse_ref,
             dq_ref, dk_ref, dv_ref, dck_ref, dcq_ref, dq_s, r_s) = refs
        hp, qi = pl.program_id(0), pl.program_id(1)
        _carry_run(carry, c_in, c_out, c_sems, jnp.logical_and(hp == 0, qi == 0), True)

        @pl.when(qi == 0)
        def _():
            dk_ref[...] = jnp.zeros(dk_ref.shape, F32)
            dv_ref[...] = jnp.zeros(dv_ref.shape, F32)
            if not mla:
                dck_ref[...] = jnp.zeros(dck_ref.shape, F32)

        if mla:
            @pl.when(jnp.logical_and(qi == 0, hp == 0))
            def _():
                dkpe_ref[...] = jnp.zeros(dkpe_ref.shape, F32)

        hls = [slice(hh * HEAD_DIM, (hh + 1) * HEAD_DIM) for hh in range(2)]
        qn = [q_ref[:, hl] for hl in hls]
        qp = [qpe_ref[:, hh * ROPE_DIM:(hh + 1) * ROPE_DIM] for hh in range(2)] if mla else None
        dof = [do_ref[:, hl] for hl in hls]
        delta = [jnp.sum(dof[hh].astype(F32) * of_ref[:, hl], axis=-1, keepdims=True) for hh, hl in enumerate(hls)]
        lse = [lse_ref[hh][:, 0:1] for hh in range(2)]
        dq_s[...] = jnp.zeros(dq_s.shape, F32)
        r_s[...] = jnp.zeros(r_s.shape, F32)

        def step(j, masked):
            rows = pl.ds(pl.multiple_of(j * T, T), T)
            for hh, hl in enumerate(hls):
                kn = k_ref[rows, hl]
                s = _dot_nt(qn[hh], kn)
                if mla:
                    kp = kpe_ref[rows, :]
                    s = s + _dot_nt(qp[hh], kp)
                s = s * scale
                if not mla:
                    s = s + (cc_ref[hh] - cr_ref[hh, :, rows])
                if masked:
                    s = jnp.where(_diag_mask(T, gran), s, NEG)
                p = jnp.exp(s - lse[hh])
                dp = _dot_nt(dof[hh], v_ref[rows, hl])
                ds = p * (dp - delta[hh])
                dv_ref[rows, hl] += _dot_tn(p.astype(MXU_DTYPE), dof[hh])
                dsb = (ds * scale).astype(MXU_DTYPE)
                dk_ref[rows, hl] += _dot_tn(dsb, qn[hh])
                dq_s[hh, :, :HEAD_DIM] += _dot_nn(dsb, kn)
                if mla:
                    dkpe_ref[rows, :] += _dot_tn(dsb, qp[hh])
                    dq_s[hh, :, HEAD_DIM:] += _dot_nn(dsb, kp)
                else:
                    dck_ref[hh, :, rows] -= jnp.sum(ds, axis=0, keepdims=True)
                    r_s[hh] += jnp.broadcast_to(jnp.sum(ds, axis=-1, keepdims=True), (T, LANE))

        lax.fori_loop(0, qi, lambda j, c: (step(j, False), c)[1], 0)
        step(qi, True)
        for hh, hl in enumerate(hls):
            dq_ref[:, hl] = dq_s[hh, :, :HEAD_DIM].astype(dq_ref.dtype)
            if mla:
                dqpe_ref[:, hh * ROPE_DIM:(hh + 1) * ROPE_DIM] = dq_s[hh, :, HEAD_DIM:]
            else:
                dcq_ref[hh] = r_s[hh, :, 0:1]
        _carry_run(carry, c_in, c_out, c_sems, jnp.logical_and(hp == H // 2 - 1, qi == nq - 1), False)

    W = 2 * HEAD_DIM
    args = [q] + ([q_pe] if mla else []) + [k] + ([k_pe] if mla else []) + [v]
    if not mla:
        args += [cum_col, cum_row]
    args += [do, of, lse]
    in_specs = _attn_specs(S, H, T, mla, *cols)
    in_specs += [pl.BlockSpec((T, W), lambda p, i: (i, p)), pl.BlockSpec((T, W), lambda p, i: (i, p)),
                 pl.BlockSpec((2, T, LANE), lambda p, i: (p, i, 0))]
    out_specs = [pl.BlockSpec((T, W), lambda p, i: (i, p)), pl.BlockSpec((S, W), lambda p, i: (0, p)),
                 pl.BlockSpec((S, W), lambda p, i: (0, p))]
    out_shape = [jax.ShapeDtypeStruct((S, H * HEAD_DIM), MXU_DTYPE), jax.ShapeDtypeStruct((S, H * HEAD_DIM), F32),
                 jax.ShapeDtypeStruct((S, H * HEAD_DIM), F32)]
    if mla:
        out_specs += [pl.BlockSpec((T, 2 * ROPE_DIM), lambda p, i: (i, p)),
                      pl.BlockSpec((S, ROPE_DIM), lambda p, i: (0, 0))]
        out_shape += [jax.ShapeDtypeStruct((S, H * ROPE_DIM), F32), jax.ShapeDtypeStruct((S, ROPE_DIM), F32)]
    else:
        out_specs += [pl.BlockSpec((2, 1, S), lambda p, i: (p, 0, 0)), pl.BlockSpec((2, T, 1), lambda p, i: (p, i, 0))]
        out_shape += [jax.ShapeDtypeStruct((H, 1, S), F32), jax.ShapeDtypeStruct((H, S, 1), F32)]
    assert len(args) == 8 and len(out_shape) == 5
    c_is, c_os, c_shape, c_sems, c_alias = _carry_call_args(carry, 8, 5)
    res = pl.pallas_call(
        body, name=name, grid=(H // 2, nq), in_specs=in_specs + c_is, out_specs=out_specs + c_os,
        out_shape=out_shape + c_shape, input_output_aliases=c_alias,
        scratch_shapes=[pltpu.VMEM((2, T, dqk), F32), pltpu.VMEM((2, T, LANE), F32)] + c_sems,
        compiler_params=_params(),
    )(*args, *(carry["ins"] if carry else ()))
    return (res[:5], res[5:]) if carry else (res, [])


def _mla_prep(lat, cos, sin, qn, kvn, ql, kvl):
    def fn(lat, c, s, qn, kvn):
        ql_ = lat[:, :ql]
        kv_ = lat[:, ql:ql + kvl]
        kp = lat[:, ql + kvl:]
        cq = ql_ * lax.rsqrt(jnp.mean(ql_ * ql_, axis=-1, keepdims=True) + RMS_EPS) * qn
        ckv = kv_ * lax.rsqrt(jnp.mean(kv_ * kv_, axis=-1, keepdims=True) + RMS_EPS) * kvn
        kp2 = jnp.concatenate([kp, jnp.zeros_like(kp)], axis=-1)
        kr = _rope_fwd(kp2, c, s)[:, :ROPE_DIM]
        return cq, ckv, kr

    return _rowwise("mla_prep", fn, [lat, cos, sin], [qn, kvn],
                    [(ql, MXU_DTYPE), (kvl, MXU_DTYPE), (ROPE_DIM, MXU_DTYPE)])


def _mla_prep_bwd(lat, cos, sin, qn, kvn, dcq, dckv, dkr, ql, kvl):
    def fn(lat, c, s, dcq, dckv, dkr, qn, kvn):
        outs, reds = [], []
        for (x, g, d) in ((lat[:, :ql], qn, dcq), (lat[:, ql:ql + kvl], kvn, dckv)):
            r = lax.rsqrt(jnp.mean(x * x, axis=-1, keepdims=True) + RMS_EPS)
            n = x * r
            dn = d * g
            outs.append(r * (dn - n * jnp.mean(dn * n, axis=-1, keepdims=True)))
            reds.append(_colsum(d * n))
        d2 = jnp.concatenate([dkr, jnp.zeros_like(dkr)], axis=-1)
        outs.append(_rope_bwd(d2, c, s)[:, :ROPE_DIM])
        return (jnp.concatenate(outs, axis=-1), *reds)

    return _rowwise("mla_prep_bwd", fn, [lat, cos, sin, dcq, dckv, dkr], [qn, kvn],
                    [(ql + kvl + ROPE_DIM, MXU_DTYPE)], [ql, kvl])


def _mla_fwd(h, w, rc, ride=None):
    S = h.shape[0]
    ql, kvl = w["q_norm"].shape[1], w["kv_norm"].shape[1]
    H = _wdim(w["w_uk"], 1) // HEAD_DIM
    n_nope, n_pe = H * HEAD_DIM, H * ROPE_DIM
    lat = _mm1("mla_down", h, w["w_down"])
    cq, ckv, kr = _mla_prep(lat, rc[0], rc[1], w["q_norm"], w["kv_norm"], ql, kvl)
    q_nope = _mm1("mla_uq_nope", cq, w["w_uq"], out_dtype=MXU_DTYPE, N=n_nope)
    q_pe = _mm("mla_uq_pe", [(cq, w["w_uq"], False, False, 0, n_nope // LANE)], S, n_pe, [MXU_DTYPE],
               epilogue=lambda accs, c, s: (_rope_fwd(accs[0], c, s),), row_extras=rc, tn=LANE)[0]
    k_nope, v = _mm("mla_ukv", [(ckv, w["w_uk"], False, False, 0, 0), (ckv, w["w_uv"], False, False, 1, 0)],
                    S, n_nope, [MXU_DTYPE, MXU_DTYPE], epilogue=lambda accs: (accs[0], accs[1]))
    o, of, lse, bufs = _attn_fwd("mla_attn_fwd", S, H, True, q_nope, k_nope, v, q_pe=q_pe, k_pe=kr,
                                 carry=_ride(ride, "ici"))
    res = _mm1("mla_wo", o, w["w_o"], carry=_ride(ride, "d2d", bufs))
    y, bufs = res if ride else (res, [])
    return y, (lat, cq, ckv, kr, q_nope, q_pe, k_nope, v, o, of, lse), bufs


def _mla_bwd(dy, h, saved, w, rc, carry):
    lat, cq, ckv, kr, q_nope, q_pe, k_nope, v, o, of, lse = saved
    S = h.shape[0]
    ql, kvl = w["q_norm"].shape[1], w["kv_norm"].shape[1]
    H = _wdim(w["w_uk"], 1) // HEAD_DIM
    n_nope, n_pe = H * HEAD_DIM, H * ROPE_DIM
    do = _mm1("mla_do", dy, w["w_o"], tb=True, out_dtype=MXU_DTYPE)
    dw_o = _mm1("mla_dwo", o, dy, ta=True, out_dtype=WIRE_DTYPE)
    (dq_nope, dk_nope, dv, dq_pe_r, dk_pe_r), carried = _attn_bwd(
        "mla_attn_bwd", S, H, True, q_nope, k_nope, v, do, of, lse, q_pe=q_pe, k_pe=kr, carry=carry)

    def unrope(d, c, s):
        reps = (1, n_pe // LANE)
        return (_rope_bwd(d, jnp.tile(c, reps), jnp.tile(s, reps)),)

    dq_pe = _rowwise("mla_unrope_q", unrope, [dq_pe_r, rc[0], rc[1]], [], [(n_pe, MXU_DTYPE)])[0]
    dq = jnp.concatenate([dq_nope, dq_pe], axis=1)
    dw_uq = _mm1("mla_dwuq", cq, dq, ta=True, out_dtype=WIRE_DTYPE)
    dcq = _mm1("mla_dcq", dq, w["w_uq"], tb=True)
    dw_uk, dw_uv = _mm("mla_dwukv", [(ckv, dk_nope, True, False, 0, 0), (ckv, dv, True, False, 1, 0)], kvl, n_nope,
                       [WIRE_DTYPE, WIRE_DTYPE], epilogue=lambda accs: (accs[0], accs[1]))
    dckv = _mm("mla_dckv", [(dk_nope, w["w_uk"], False, True, 0, 0), (dv, w["w_uv"], False, True, 0, 0)],
               S, kvl, [F32])[0]
    dlat, dqn, dkvn = _mla_prep_bwd(lat, rc[0], rc[1], w["q_norm"], w["kv_norm"], dcq, dckv, dk_pe_r, ql, kvl)
    dw_down = _mm1("mla_dwdown", h, dlat, ta=True, out_dtype=WIRE_DTYPE)
    dh = _mm1("mla_dh", dlat, w["w_down"], tb=True)
    return dh, dict(w_down=dw_down, q_norm=dqn, w_uq=dw_uq, kv_norm=dkvn, w_uk=dw_uk, w_uv=dw_uv, w_o=dw_o), carried


def _log_sigmoid(z):
    return jnp.minimum(z, 0.0) - jnp.log(1.0 + jnp.exp(-jnp.abs(z)))


def _fox_gate_fwd(f, bf):
    S = f.shape[0]
    B = LANE

    def body(f_ref, b_ref, cum_ref):
        r = lax.broadcasted_iota(jnp.int32, (B, B), 0)
        c = lax.broadcasted_iota(jnp.int32, (B, B), 1)
        tri = (r >= c).astype(F32)
        carry = jnp.zeros((1, LANE), F32)
        for blk in range(S // B):
            rows = slice(blk * B, (blk + 1) * B)
            lf = _log_sigmoid(f_ref[rows, :] + b_ref[...])
            cs = jnp.dot(tri, lf, precision=lax.Precision.HIGHEST, preferred_element_type=F32) + carry
            cum_ref[rows, :] = cs
            carry = cs[B - 1:B, :]

    return pl.pallas_call(body, name="fox_gate_fwd", out_shape=jax.ShapeDtypeStruct((S, LANE), F32),
                          compiler_params=_params())(f, bf)


def _fox_gate_bwd(dcum, f, bf):
    S = f.shape[0]
    B = LANE

    def body(d_ref, f_ref, b_ref, df_ref, db_ref):
        r = lax.broadcasted_iota(jnp.int32, (B, B), 0)
        c = lax.broadcasted_iota(jnp.int32, (B, B), 1)
        tri = (r <= c).astype(F32)
        carry = jnp.zeros((1, LANE), F32)
        db = jnp.zeros((1, LANE), F32)
        for blk in reversed(range(S // B)):
            rows = slice(blk * B, (blk + 1) * B)
            dlf = jnp.dot(tri, d_ref[rows, :], precision=lax.Precision.HIGHEST, preferred_element_type=F32) + carry
            carry = dlf[0:1, :]
            z = f_ref[rows, :] + b_ref[...]
            dz = dlf * _sigmoid(-z)
            df_ref[rows, :] = dz.astype(df_ref.dtype)
            db = db + jnp.sum(dz, axis=0, keepdims=True)
        db_ref[...] = db

    return pl.pallas_call(body, name="fox_gate_bwd",
                          out_shape=[jax.ShapeDtypeStruct((S, LANE), MXU_DTYPE), jax.ShapeDtypeStruct((1, LANE), F32)],
                          compiler_params=_params())(dcum, f, bf)


def _fox_fwd(h, w, ride=None):
    S, D = h.shape
    H = D // HEAD_DIM
    qkv = _mm1("fox_qkv", h, w["w_in"], out_dtype=MXU_DTYPE, N=3 * D)
    f = _mm1("fox_f", h, w["w_in"], bcol=3 * D // LANE, N=LANE, tn=LANE)
    cum = _fox_gate_fwd(f, w["b_f"])
    cumT = cum[:, :H].T
    cum_col, cum_row = cumT.reshape(H, S, 1), cumT.reshape(H, 1, S)
    nb = D // (2 * HEAD_DIM)
    o, of, lse, bufs = _attn_fwd("fox_attn_fwd", S, H, False, qkv, qkv, qkv, cum_col=cum_col, cum_row=cum_row,
                                 cols=(0, nb, 2 * nb), carry=_ride(ride, "ici"))
    res = _mm1("fox_wo", o, w["w_o"], carry=_ride(ride, "d2d", bufs))
    y, bufs = res if ride else (res, [])
    return y, (qkv, f, cum_col, cum_row, o, of, lse), bufs


def _fox_bwd(dy, h, saved, w, carry):
    qkv, f, cum_col, cum_row, o, of, lse = saved
    S, D = h.shape
    H = D // HEAD_DIM
    nb = D // (2 * HEAD_DIM)
    do = _mm1("fox_do", dy, w["w_o"], tb=True, out_dtype=MXU_DTYPE)
    dw_o = _mm1("fox_dwo", o, dy, ta=True, out_dtype=WIRE_DTYPE)
    (dq, dk, dv, dck, dcq), carried = _attn_bwd("fox_attn_bwd", S, H, False, qkv, qkv, qkv, do, of, lse,
                                                cum_col=cum_col, cum_row=cum_row, cols=(0, nb, 2 * nb), carry=carry)
    dcum = jnp.pad((dck.reshape(H, S) + dcq.reshape(H, S)).T, ((0, 0), (0, LANE - H)))
    df, dbf = _fox_gate_bwd(dcum, f, w["b_f"])
    dproj = jnp.concatenate([dq, dk.astype(MXU_DTYPE), dv.astype(MXU_DTYPE), df], axis=1)
    dw_in = _mm1("fox_dwin", h, dproj, ta=True, out_dtype=WIRE_DTYPE)
    dh = _mm1("fox_dh", dproj, w["w_in"], tb=True, tn=256)
    return dh, dict(w_in=dw_in, b_f=dbf[:, :H], w_o=dw_o), carried


def _place():
    x, y, c = lax.axis_index("x"), lax.axis_index("y"), lax.axis_index("c")
    return x, y, c, [(1 - x, y), (x, 1 - y), (1 - x, 1 - y)]


def _ag_small(name, blk):
    m, n = blk.shape

    def body(x_ref, out_ref, send_sems, recv_sems, local_sem):
        x, y, c, chips = _place()
        me, sibling = (x, y, c), (x, y, 1 - c)

        def rows(px, py, pc):
            return out_ref.at[pl.ds((4 * px + 2 * py + pc) * m, m), :]

        def copy(k, block, to, src=None):
            return pltpu.make_async_remote_copy(
                src_ref=rows(*block) if src is None else src, dst_ref=rows(*block),
                send_sem=send_sems.at[k], recv_sem=recv_sems.at[k], device_id=to, device_id_type=MESH)

        mine = pltpu.make_async_copy(x_ref, rows(*me), local_sem)
        mine.start()
        first = [copy(0, me, sibling, src=x_ref)]
        first += [copy(1 + j, me, (*chip, c), src=x_ref) for j, chip in enumerate(chips)]
        for cp in first:
            cp.start()
        passed = [copy(4 + j, (*chip, c), sibling) for j, chip in enumerate(chips)]
        for j, chip in enumerate(chips):
            copy(1 + j, (*chip, c), me).wait_recv()
            passed[j].start()
        copy(0, sibling, me).wait_recv()
        for j, chip in enumerate(chips):
            copy(4 + j, (*chip, 1 - c), me).wait_recv()
        for cp in first + passed:
            cp.wait_send()
        mine.wait()

    return pl.pallas_call(
        body, name=name, out_shape=jax.ShapeDtypeStruct((N_DEV * m, n), blk.dtype),
        in_specs=[pl.BlockSpec(memory_space=pltpu.VMEM)], out_specs=pl.BlockSpec(memory_space=pltpu.VMEM),
        scratch_shapes=[pltpu.SemaphoreType.DMA((7,)), pltpu.SemaphoreType.DMA((7,)), pltpu.SemaphoreType.DMA],
        compiler_params=_params(),
    )(blk)


def _half(ref, row_axis, c, rows):
    idx = [slice(None)] * len(ref.shape)
    idx[row_axis] = pl.ds(pl.multiple_of(c * rows, 16), rows)
    return ref.at[tuple(idx)]


def _shard(ref, layout, k):
    if layout == "row":
        return ref.at[:, k]
    w = ref.shape[2] // N_CHIPS
    return ref.at[:, :, pl.ds(pl.multiple_of(k * w, LANE), w)]


def _full_shape(shape, layout):
    L, r, w = shape
    return (L, N_CHIPS, r, w) if layout == "row" else (L, r, N_CHIPS * w)


def _cast_full(name, a, layout, k_idx, lyr):
    _, r, C = a.shape
    tr = _pick(r, (256, 128, 64, 32, 16))

    def body(k_ref, a_ref, o_ref):
        o_ref[...] = a_ref[...].astype(o_ref.dtype)

    if layout == "row":
        o_spec = pl.BlockSpec((None, None, tr, C), lambda l, i, k: (0, k[0], i, 0))
    else:
        o_spec = pl.BlockSpec((None, tr, C), lambda l, i, k: (0, i, k[0]))
    return pl.pallas_call(
        body, name=name,
        grid_spec=pltpu.PrefetchScalarGridSpec(
            num_scalar_prefetch=1, grid=(1, r // tr),
            in_specs=[pl.BlockSpec((None, tr, C), lambda l, i, k: (lyr, i, 0))], out_specs=o_spec),
        out_shape=jax.ShapeDtypeStruct(_full_shape((1, r, C), layout), WIRE_DTYPE),
        compiler_params=_params(),
    )(k_idx, a)


def _gather_carry(fulls, layouts, lyrs, phase):
    n = len(fulls)
    half_rows = [f.shape[2 if lay == "row" else 1] // 2 for f, lay in zip(fulls, layouts)]

    def copies(ins, outs, send_sems, recv_sems):
        x, y, c, chips = _place()
        cps = []
        for i in range(n):
            for j, chip in enumerate(chips):
                who, to = ((x, y), (*chip, c)) if phase == "ici" else (chip, (x, y, 1 - c))
                w = _half(_shard(outs[i].at[pl.ds(lyrs[i], 1)], layouts[i], 2 * who[0] + who[1]), 1, c, half_rows[i])
                cps.append(pltpu.make_async_remote_copy(
                    src_ref=w, dst_ref=w, send_sem=send_sems.at[i, j], recv_sem=recv_sems.at[i, j], device_id=to,
                    device_id_type=MESH))
        return cps

    return dict(ins=list(fulls), sems=(n, 3), copies=copies, alias={i: i for i in range(n)},
                out_shape=[jax.ShapeDtypeStruct(f.shape, f.dtype) for f in fulls])


def _run_carry(name, carry):
    n = len(carry["ins"])

    def body(*refs):
        cps = carry["copies"](refs[:n], refs[n:2 * n], *refs[2 * n:])
        for cp in cps:
            cp.start()
        for cp in cps:
            cp.wait()

    return pl.pallas_call(
        body, name=name, in_specs=[ANY] * n, out_specs=[ANY] * len(carry["out_shape"]), out_shape=carry["out_shape"],
        input_output_aliases=dict(carry.get("alias", {})),
        scratch_shapes=[pltpu.SemaphoreType.DMA(carry["sems"]), pltpu.SemaphoreType.DMA(carry["sems"])],
        compiler_params=_params(),
    )(*carry["ins"])


def _gather_weights(fulls, layouts, lyrs):
    n = len(fulls)
    half_rows = [f.shape[2 if lay == "row" else 1] // 2 for f, lay in zip(fulls, layouts)]

    def body(*refs):
        outs = refs[n:2 * n]
        send_sems, recv_sems = refs[2 * n:]
        x, y, c, chips = _place()
        sibling = (x, y, 1 - c)

        def window(i, kx, ky, half):
            return _half(_shard(outs[i].at[pl.ds(lyrs[i], 1)], layouts[i], 2 * kx + ky), 1, half, half_rows[i])

        first, passed = [], []
        for i in range(n):
            mine = window(i, x, y, c)
            for j, chip in enumerate(chips):
                cp = pltpu.make_async_remote_copy(
                    src_ref=mine, dst_ref=mine, send_sem=send_sems.at[i, j], recv_sem=recv_sems.at[i, j],
                    device_id=(*chip, c), device_id_type=MESH)
                cp.start()
                first.append(cp)
        for i in range(n):
            for j, chip in enumerate(chips):
                got = window(i, *chip, c)
                pltpu.make_async_remote_copy(
                    src_ref=got, dst_ref=got, send_sem=send_sems.at[i, j], recv_sem=recv_sems.at[i, j],
                    device_id=(*chip, c), device_id_type=MESH).wait_recv()
                cp = pltpu.make_async_remote_copy(
                    src_ref=got, dst_ref=got, send_sem=send_sems.at[i, 3 + j], recv_sem=recv_sems.at[i, 3 + j],
                    device_id=sibling, device_id_type=MESH)
                cp.start()
                passed.append(cp)
        for i in range(n):
            for j, chip in enumerate(chips):
                got = window(i, *chip, 1 - c)
                pltpu.make_async_remote_copy(
                    src_ref=got, dst_ref=got, send_sem=send_sems.at[i, 3 + j], recv_sem=recv_sems.at[i, 3 + j],
                    device_id=sibling, device_id_type=MESH).wait_recv()
        for cp in first + passed:
            cp.wait_send()

    return pl.pallas_call(
        body, name="gather_weights", in_specs=[ANY] * n, out_specs=[ANY] * n,
        out_shape=[jax.ShapeDtypeStruct(f.shape, f.dtype) for f in fulls],
        input_output_aliases={i: i for i in range(n)},
        scratch_shapes=[pltpu.SemaphoreType.DMA((n, 6)), pltpu.SemaphoreType.DMA((n, 6))],
        compiler_params=_params(),
    )(*fulls)


def _half_shape(shape, layout):
    s = list(shape)
    s[2 if layout == "row" else 1] //= 2
    return tuple(s)


def _swap_halves(name, grads, layouts):
    n = len(grads)
    row_axis = [2 if lay == "row" else 1 for lay in layouts]
    half_rows = [g.shape[ra] // 2 for g, ra in zip(grads, row_axis)]

    def body(*refs):
        ins, outs = refs[:n], refs[n:2 * n]
        send_sems, recv_sems = refs[2 * n:]
        x, y, c, _ = _place()
        cps = []
        for i in range(n):
            cp = pltpu.make_async_remote_copy(
                src_ref=_half(ins[i], row_axis[i], 1 - c, half_rows[i]), dst_ref=outs[i],
                send_sem=send_sems.at[i], recv_sem=recv_sems.at[i], device_id=(x, y, 1 - c), device_id_type=MESH)
            cp.start()
            cps.append(cp)
        for cp in cps:
            cp.wait()

    return pl.pallas_call(
        body, name=name, in_specs=[ANY] * n, out_specs=[ANY] * n,
        out_shape=[jax.ShapeDtypeStruct(_half_shape(g.shape, lay), g.dtype) for g, lay in zip(grads, layouts)],
        scratch_shapes=[pltpu.SemaphoreType.DMA((n,)), pltpu.SemaphoreType.DMA((n,))],
        compiler_params=_params(),
    )(*grads)


def _add_half(name, g, r, layout, c_idx):
    L = g.shape[0]
    if layout == "row":
        A, rows, W = L * N_CHIPS, g.shape[2] // 2, g.shape[3]
    else:
        A, rows, W = L, g.shape[1] // 2, g.shape[2]
    g3 = g.reshape(A, 2 * rows, W)
    r3 = r.reshape(A, rows, W)
    tr = _pick(rows, (256, 128, 64, 32, 16))
    nb = rows // tr

    def body(c_ref, g_ref, r_ref, o_ref):
        o_ref[...] = (g_ref[...].astype(F32) + r_ref[...].astype(F32)).astype(o_ref.dtype)

    out = pl.pallas_call(
        body, name=name,
        grid_spec=pltpu.PrefetchScalarGridSpec(
            num_scalar_prefetch=1, grid=(A, nb),
            in_specs=[pl.BlockSpec((None, tr, W), lambda a, i, c: (a, c[0] * nb + i, 0)),
                      pl.BlockSpec((None, tr, W), lambda a, i, c: (a, i, 0))],
            out_specs=pl.BlockSpec((None, tr, W), lambda a, i, c: (a, i, 0))),
        out_shape=jax.ShapeDtypeStruct((A, rows, W), WIRE_DTYPE),
        compiler_params=_params(),
    )(c_idx, g3, r3)
    return out.reshape(r.shape)


def _exchange_carry(parts, layouts):
    n = len(parts)

    def shard_half_shape(p, lay):
        if lay == "row":
            return (p.shape[0],) + p.shape[2:]
        return (p.shape[0], p.shape[1], p.shape[2] // N_CHIPS)

    def copies(ins, outs, send_sems, recv_sems):
        x, y, c, chips = _place()
        return [pltpu.make_async_remote_copy(
            src_ref=_shard(ins[i], layouts[i], 2 * kx + ky), dst_ref=outs[i].at[j], send_sem=send_sems.at[i, j],
            recv_sem=recv_sems.at[i, j], device_id=(kx, ky, c), device_id_type=MESH)
            for i in range(n) for j, (kx, ky) in enumerate(chips)]

    return dict(ins=list(parts), sems=(n, 3), copies=copies,
                out_shape=[jax.ShapeDtypeStruct((3,) + shard_half_shape(p, lay), p.dtype)
                           for p, lay in zip(parts, layouts)])


def _exchange_shards(parts, layouts):
    n = len(parts)
    carry = _exchange_carry(parts, layouts)

    def body(*refs):
        cps = carry["copies"](refs[:n], refs[n:2 * n], *refs[2 * n:])
        for cp in cps:
            cp.start()
        for cp in cps:
            cp.wait()

    return pl.pallas_call(
        body, name="exchange_shards", in_specs=[ANY] * n, out_specs=[ANY] * n, out_shape=carry["out_shape"],
        scratch_shapes=[pltpu.SemaphoreType.DMA((n, 3)), pltpu.SemaphoreType.DMA((n, 3))],
        compiler_params=_params(),
    )(*parts)


def _sum_shards(name, p, r, layout, kc_idx, dst, lyr, n_lyr):
    rows, W = r.shape[2], r.shape[3]
    tr = _pick(rows, (256, 128, 64, 32, 16))
    nb = rows // tr

    def body(kc_ref, p_ref, r_ref, *rest):
        acc = p_ref[...].astype(F32)
        for j in range(3):
            acc = acc + r_ref[j].astype(F32)
        rest[-1][...] = acc

    if layout == "row":
        p_spec = pl.BlockSpec((None, None, tr, W), lambda a, i, kc: (0, kc[0], i, 0))
    else:
        p_spec = pl.BlockSpec((None, tr, W), lambda a, i, kc: (0, i, kc[0]))
    in_specs = [p_spec, pl.BlockSpec((3, None, tr, W), lambda a, i, kc: (0, 0, i, 0))]
    args = [kc_idx, p, r]
    if dst is not None:
        in_specs.append(ANY)
        args.append(dst)
    return pl.pallas_call(
        body, name=name,
        grid_spec=pltpu.PrefetchScalarGridSpec(
            num_scalar_prefetch=1, grid=(1, nb), in_specs=in_specs,
            out_specs=pl.BlockSpec((None, tr, W), lambda a, i, kc: (lyr, kc[1] * nb + i, 0))),
        out_shape=jax.ShapeDtypeStruct((n_lyr, 2 * rows, W), F32),
        input_output_aliases={3: 0} if dst is not None else {},
        compiler_params=_params(),
    )(*args)


def _join_halves(shards):
    n = len(shards)

    def body(*refs):
        outs = refs[n:2 * n]
        send_sems, recv_sems = refs[2 * n:]
        x, y, c, _ = _place()
        cps = []
        for i in range(n):
            mine = _half(outs[i], 1, c, outs[i].shape[1] // 2)
            cp = pltpu.make_async_remote_copy(
                src_ref=mine, dst_ref=mine, send_sem=send_sems.at[i], recv_sem=recv_sems.at[i],
                device_id=(x, y, 1 - c), device_id_type=MESH)
            cp.start()
            cps.append(cp)
        for cp in cps:
            cp.wait()

    return pl.pallas_call(
        body, name="join_halves", in_specs=[ANY] * n, out_specs=[ANY] * n,
        out_shape=[jax.ShapeDtypeStruct(s.shape, s.dtype) for s in shards],
        input_output_aliases={i: i for i in range(n)},
        scratch_shapes=[pltpu.SemaphoreType.DMA((n,)), pltpu.SemaphoreType.DMA((n,))],
        compiler_params=_params(),
    )(*shards)


def _ada_fwd(c_all, ada_w, ada_b):
    L, D, w = ada_w.shape
    tn = _pick(w, (512, 256, 128))

    def body(c_ref, w_ref, b_ref, o_ref, a_ref):
        c = c_ref[...]
        act = (c * _sigmoid(c)).astype(MXU_DTYPE)
        a_ref[...] = act
        o_ref[...] = jnp.dot(act, w_ref[...].astype(MXU_DTYPE), preferred_element_type=F32) + b_ref[...]

    return pl.pallas_call(
        body, name="ada_fwd", grid=(L, w // tn),
        in_specs=[pl.BlockSpec((16, D), lambda l, j: (0, 0)), pl.BlockSpec((None, D, tn), lambda l, j: (l, 0, j)),
                  pl.BlockSpec((None, 1, tn), lambda l, j: (l, 0, j))],
        out_specs=[pl.BlockSpec((None, 16, tn), lambda l, j: (l, 0, j)), pl.BlockSpec((16, D), lambda l, j: (0, 0))],
        out_shape=[jax.ShapeDtypeStruct((L, 16, w), F32), jax.ShapeDtypeStruct((16, D), MXU_DTYPE)],
        compiler_params=_params(),
    )(c_all, ada_w, ada_b)


def _sum_devices(name, parts):
    n, R, W = parts.shape
    tw = _pick(W, (2048, 1024, 512, 256, 128))

    def body(p_ref, o_ref):
        acc = p_ref[0]
        for d in range(1, n):
            acc = acc + p_ref[d]
        o_ref[...] = acc

    return pl.pallas_call(
        body, name=name, grid=(W // tw,), in_specs=[pl.BlockSpec((n, R, tw), lambda j: (0, 0, j))],
        out_specs=pl.BlockSpec((R, tw), lambda j: (0, j)), out_shape=jax.ShapeDtypeStruct((R, W), F32),
        compiler_params=_params(),
    )(parts)


def _adamw_math(w, g, m, v):
    m2 = ADAM_B1 * m + (1.0 - ADAM_B1) * g
    v2 = ADAM_B2 * v + (1.0 - ADAM_B2) * (g * g)
    m_hat = m2 / (1.0 - ADAM_B1 ** ADAM_STEP)
    v_hat = v2 / (1.0 - ADAM_B2 ** ADAM_STEP)
    return -ADAM_LR * (m_hat / (jnp.sqrt(v_hat) + ADAM_EPS) + ADAM_WD * w), m2, v2


def _adamw(name, w, g, m, v):
    W = w.shape[1]
    tr = 256 if W <= 1024 else (128 if W <= 2048 else 64)
    return _rowwise(name, _adamw_math, [w, g, m, v], [], [(W, F32)] * 3, tr=tr)


def _adamw_ada(c_act, dmod, w, m, v):
    L, D, wd = w.shape
    tr = LANE

    def body(c_ref, d_ref, w_ref, m_ref, v_ref, g_ref, dl_ref, m2_ref, v2_ref):
        g = _dot_tn(c_ref[...], d_ref[...].astype(MXU_DTYPE))
        g_ref[...] = g
        dl_ref[...], m2_ref[...], v2_ref[...] = _adamw_math(w_ref[...], g, m_ref[...], v_ref[...])

    big = pl.BlockSpec((None, tr, wd), lambda l, i: (l, i, 0))
    return pl.pallas_call(
        body, name="adamw_ada_w", grid=(L, D // tr),
        in_specs=[pl.BlockSpec((16, tr), lambda l, i: (0, i)), pl.BlockSpec((None, 16, wd), lambda l, i: (l, 0, 0)),
                  big, big, big],
        out_specs=[big] * 4, out_shape=[jax.ShapeDtypeStruct(w.shape, F32)] * 4,
        compiler_params=_params(),
    )(c_act, dmod, w, m, v)


def _flat(a):
    return a.reshape(-1, a.shape[-1])


BIG = ("ffn_w1", "ffn_w3", "ffn_w2", "mla_w_down", "mla_w_uq", "mla_w_uk", "mla_w_uv", "mla_w_o", "fox_w_in",
       "fox_w_o")
LAYOUT = dict(ffn_w1="col", ffn_w3="col", ffn_w2="row", mla_w_down="row", mla_w_uq="col", mla_w_uk="col",
              mla_w_uv="col", mla_w_o="row", fox_w_in="row", fox_w_o="row")
FFN = ("ffn_w1", "ffn_w3", "ffn_w2")
MLA = ("mla_w_down", "mla_w_uq", "mla_w_uk", "mla_w_uv", "mla_w_o")
FOX = ("fox_w_in", "fox_w_o")
SMALL = ("ln1_g", "ln1_b", "ln2_g", "ln2_b", "mla_q_norm", "mla_kv_norm", "fox_b_f")
WEIGHTS = ("ada_w", "ada_b", "ln1_g", "ln1_b", "ln2_g", "ln2_b", "ffn_w1", "ffn_w3", "ffn_w2", "mla_w_down",
           "mla_q_norm", "mla_w_uq", "mla_kv_norm", "mla_w_uk", "mla_w_uv", "mla_w_o", "fox_w_in", "fox_b_f",
           "fox_w_o")


def _uq_perm(H):
    d = HEAD_DIM + ROPE_DIM
    nope = (np.arange(H)[:, None] * d + np.arange(HEAD_DIM)[None, :]).reshape(-1)
    pe = (np.arange(H)[:, None] * d + HEAD_DIM + np.arange(ROPE_DIM)[None, :]).reshape(-1)
    return np.concatenate([nope, pe])


def kernel(x, c, positions, ada_w, ada_b, ln1_g, ln1_b, ln2_g, ln2_b, ffn_w1, ffn_w3, ffn_w2, mla_w_down, mla_q_norm, mla_w_uq, mla_kv_norm, mla_w_uk, mla_w_uv, mla_w_o, fox_w_in, fox_b_f, fox_w_o, loss_target, m_ada_w, m_ada_b, m_ln1_g, m_ln1_b, m_ln2_g, m_ln2_b, m_ffn_w1, m_ffn_w3, m_ffn_w2, m_mla_w_down, m_mla_q_norm, m_mla_w_uq, m_mla_kv_norm, m_mla_w_uk, m_mla_w_uv, m_mla_w_o, m_fox_w_in, m_fox_b_f, m_fox_w_o, v_ada_w, v_ada_b, v_ln1_g, v_ln1_b, v_ln2_g, v_ln2_b, v_ffn_w1, v_ffn_w3, v_ffn_w2, v_mla_w_down, v_mla_q_norm, v_mla_w_uq, v_mla_kv_norm, v_mla_w_uk, v_mla_w_uv, v_mla_w_o, v_fox_w_in, v_fox_b_f, v_fox_w_o):
    W = dict(ada_w=ada_w, ada_b=ada_b, ln1_g=ln1_g, ln1_b=ln1_b, ln2_g=ln2_g, ln2_b=ln2_b, ffn_w1=ffn_w1,
             ffn_w3=ffn_w3, ffn_w2=ffn_w2, mla_w_down=mla_w_down, mla_q_norm=mla_q_norm, mla_w_uq=mla_w_uq,
             mla_kv_norm=mla_kv_norm, mla_w_uk=mla_w_uk, mla_w_uv=mla_w_uv, mla_w_o=mla_w_o, fox_w_in=fox_w_in,
             fox_b_f=fox_b_f, fox_w_o=fox_w_o)
    Mo = dict(ada_w=m_ada_w, ada_b=m_ada_b, ln1_g=m_ln1_g, ln1_b=m_ln1_b, ln2_g=m_ln2_g, ln2_b=m_ln2_b,
              ffn_w1=m_ffn_w1, ffn_w3=m_ffn_w3, ffn_w2=m_ffn_w2, mla_w_down=m_mla_w_down, mla_q_norm=m_mla_q_norm,
              mla_w_uq=m_mla_w_uq, mla_kv_norm=m_mla_kv_norm, mla_w_uk=m_mla_w_uk, mla_w_uv=m_mla_w_uv,
              mla_w_o=m_mla_w_o, fox_w_in=m_fox_w_in, fox_b_f=m_fox_b_f, fox_w_o=m_fox_w_o)
    Vo = dict(ada_w=v_ada_w, ada_b=v_ada_b, ln1_g=v_ln1_g, ln1_b=v_ln1_b, ln2_g=v_ln2_g, ln2_b=v_ln2_b,
              ffn_w1=v_ffn_w1, ffn_w3=v_ffn_w3, ffn_w2=v_ffn_w2, mla_w_down=v_mla_w_down, mla_q_norm=v_mla_q_norm,
              mla_w_uq=v_mla_w_uq, mla_kv_norm=v_mla_kv_norm, mla_w_uk=v_mla_w_uk, mla_w_uv=v_mla_w_uv,
              mla_w_o=v_mla_w_o, fox_w_in=v_fox_w_in, fox_b_f=v_fox_b_f, fox_w_o=v_fox_w_o)

    S, D = x.shape[1], x.shape[2]
    L = ada_w.shape[0]
    alpha = float((2 * L) ** 0.25)
    H_mla = mla_w_uk.shape[2] * N_CHIPS // HEAD_DIM
    H_fox = D // HEAD_DIM
    xi, yi, ci = lax.axis_index("x"), lax.axis_index("y"), lax.axis_index("c")
    chip = 2 * xi + yi
    dev = 2 * chip + ci
    c_idx = jnp.reshape(ci, (1,)).astype(jnp.int32)
    x0, tgt = x[0], loss_target[0]
    pos = positions.reshape(S, 1)

    c_all = _ag_small("gather_c", jnp.pad(c, ((0, 7), (0, 0)))).reshape(N_DEV, 8, D)[:, 0]
    w_ada = ada_w.shape[2]
    ada_b_sh = lax.dynamic_slice_in_dim(ada_b, chip * w_ada, w_ada, axis=1).reshape(L, 1, w_ada)
    mod_sh, c_act = _ada_fwd(jnp.pad(c_all, ((0, 8), (0, 0))), ada_w, ada_b_sh)
    mod_all = _ag_small("gather_mod", mod_sh.transpose(1, 0, 2).reshape(16, L * w_ada))
    mod_all = mod_all.reshape(N_CHIPS, 2, 16, L, w_ada)[:, 0]
    mod = lax.dynamic_index_in_dim(mod_all, dev, axis=1, keepdims=False)
    mod = mod.transpose(1, 0, 2).reshape(L, 6, 1, D)

    kc_idx = jnp.stack([chip, ci]).astype(jnp.int32)
    raw = {n: [_cast_full("cast_" + n, W[n], LAYOUT[n], kc_idx, l) for l in range(W[n].shape[0])] for n in BIG}
    full = {n: [None] * W[n].shape[0] for n in BIG}
    perm = _uq_perm(H_mla)
    n_in = fox_w_in.shape[2] * N_CHIPS

    def group(i):
        return FFN + (MLA if i % 2 == 0 else FOX)

    def ride_of(names, i):
        return dict(bufs=[raw[n][i if n in FFN else i // 2] for n in names], lays=[LAYOUT[n] for n in names])

    def land(names, i, bufs):
        for n, f in zip(names, bufs):
            if n == "fox_w_in":
                fw = f.reshape(N_CHIPS, D, -1).transpose(1, 0, 2).reshape(D, n_in)
                f = jnp.pad(fw, ((0, 0), (0, 3 * D + LANE - n_in)))[None]
            elif n == "mla_w_uq":
                f = f[:, :, perm]
            elif LAYOUT[n] == "row":
                f = f.reshape(1, f.shape[1] * f.shape[2], f.shape[3])
            full[n][i if n in FFN else i // 2] = f

    r0 = ride_of(group(0), 0)
    land(group(0), 0, _gather_weights(r0["bufs"], r0["lays"], [0] * len(r0["bufs"])))

    rc = tuple(_rope_tables(pos))

    def mixer_w(i):
        j = i // 2
        if i % 2 == 0:
            return dict(w_down=(full["mla_w_down"][j], 0), q_norm=mla_q_norm[j:j + 1], w_uq=(full["mla_w_uq"][j], 0),
                        kv_norm=mla_kv_norm[j:j + 1], w_uk=(full["mla_w_uk"][j], 0), w_uv=(full["mla_w_uv"][j], 0),
                        w_o=(full["mla_w_o"][j], 0))
        return dict(w_in=(full["fox_w_in"][j], 0), b_f=jnp.pad(fox_b_f[j:j + 1], ((0, 0), (0, LANE - H_fox))),
                    w_o=(full["fox_w_o"][j], 0))

    saved = []
    xc = x0
    h = _modulate(x0, mod[0, 1], mod[0, 0])
    for i in range(L):
        mw = mixer_w(i)
        nxt = i + 1 < L
        mix_n = group(i + 1)[3:]
        if i % 2 == 0:
            y1, ms, got = _mla_fwd(h, mw, rc, ride_of(FFN, i + 1) if nxt else None)
        else:
            y1, ms, got = _fox_fwd(h, mw, ride_of(FFN, i + 1) if nxt else None)
        land(FFN, i + 1, got)
        z1, x1, h2 = _resid_ln_mod(xc, y1, mod[i, 2], ln1_g[i:i + 1], ln1_b[i:i + 1], mod[i, 4], mod[i, 3], alpha)
        y2, fs, got = _ffn_fwd(h2, (full["ffn_w1"][i], 0), (full["ffn_w3"][i], 0), (full["ffn_w2"][i], 0),
                               ride_of(mix_n, i + 1) if nxt else None)
        land(mix_n, i + 1, got)
        rec = dict(h1=h, ms=ms, y1=y1, z1=z1, h2=h2, fs=fs, y2=y2)
        if i + 1 < L:
            z2, xc, h = _resid_ln_mod(x1, y2, mod[i, 5], ln2_g[i:i + 1], ln2_b[i:i + 1], mod[i + 1, 1],
                                      mod[i + 1, 0], alpha)
            rec["z2"] = z2
        else:
            dx_res, dy, loss_v, dlg, dlb, dgate = _final_ln_loss(x1, y2, tgt, mod[i, 5], ln2_g[i:i + 1],
                                                                 ln2_b[i:i + 1], alpha)
        saved.append(rec)
    loss = lax.psum(loss_v[0, 0] * (0.5 / D), ("x", "y", "c"))

    G = {n: [None] * W[n].shape[0] for n in SMALL}
    dmod = [[None] * 6 for _ in range(L)]
    red = {n: None for n in BIG}
    inv_perm = np.argsort(perm)

    def rs_view(n, g):
        if n == "mla_w_uq":
            g = g[:, inv_perm]
        if n == "fox_w_in":
            g = g[:, :n_in].reshape(D, N_CHIPS, n_in // N_CHIPS).transpose(1, 0, 2)
        elif LAYOUT[n] == "row":
            g = g.reshape(N_CHIPS, g.shape[0] // N_CHIPS, g.shape[1])
        return g[None]

    def half_sums(tag, names, grads):
        lays = [LAYOUT[n] for n in names]
        gs = [rs_view(n, g) for n, g in zip(names, grads)]
        recv = _swap_halves("swap_" + tag, gs, lays)
        return [_add_half("add_half_" + n, g, r, lay, c_idx) for n, g, r, lay in zip(names, gs, recv, lays)]

    def finish(names, parts, recv2, lyr):
        for n, p, r in zip(names, parts, recv2):
            red[n] = _sum_shards("sum_shards_" + n, p, r, LAYOUT[n], kc_idx, red[n], lyr, W[n].shape[0])

    queue = []

    def carry_of(item):
        return _exchange_carry(item[1], [LAYOUT[n] for n in item[0]]) if item else None

    for i in reversed(range(L)):
        rec = saved[i]
        mw = mixer_w(i)
        j = i // 2
        G["ln2_g"][i], G["ln2_b"][i], dmod[i][5] = dlg, dlb, dgate
        ride_du = queue.pop(0) if queue else None
        ride_dh = queue.pop(0) if queue else None
        dh2, dw1, dw3, dw2, got_du, got_dh = _ffn_bwd(
            dy, rec["h2"], rec["fs"], (full["ffn_w1"][i], 0), (full["ffn_w3"][i], 0), (full["ffn_w2"][i], 0),
            carry_of(ride_du), carry_of(ride_dh))
        for item, got in ((ride_du, got_du), (ride_dh, got_dh)):
            if item:
                finish(item[0], item[1], got, item[2])
        ffn_parts = half_sums("ffn", FFN, (dw1, dw3, dw2))
        dx_res, dy, dmod[i][4], dmod[i][3], G["ln1_g"][i], G["ln1_b"][i], dmod[i][2] = _bwd_boundary(
            dx_res, dh2, rec["z1"], rec["y1"], mod[i, 4], mod[i, 2], ln1_g[i:i + 1], ln1_b[i:i + 1], alpha)
        ride_at = (FFN[:2], ffn_parts[:2], i)
        if i % 2 == 0:
            dh1, gm, recv2 = _mla_bwd(dy, rec["h1"], rec["ms"], mw, rc, carry_of(ride_at))
            names, pre = MLA, "mla_"
            G["mla_q_norm"][j], G["mla_kv_norm"][j] = gm["q_norm"], gm["kv_norm"]
        else:
            dh1, gm, recv2 = _fox_bwd(dy, rec["h1"], rec["ms"], mw, carry_of(ride_at))
            names, pre = FOX, "fox_"
            G["fox_b_f"][j] = gm["b_f"]
        finish(ride_at[0], ride_at[1], recv2, i)
        queue.append((FFN[2:], ffn_parts[2:], i))
        queue.append((names, half_sums(pre[:-1], names, [gm[n[len(pre):]] for n in names]), j))
        if i > 0:
            p = saved[i - 1]
            dx_res, dy, dmod[i][1], dmod[i][0], dlg, dlb, dgate = _bwd_boundary(
                dx_res, dh1, p["z2"], p["y2"], mod[i, 1], mod[i - 1, 5], ln2_g[i - 1:i], ln2_b[i - 1:i], alpha)
        else:
            grad_x, dmod[i][1], dmod[i][0] = _first_bwd(dx_res, dh1, x0, mod[i, 1])
    last = (sum((q[0] for q in queue), ()), sum((q[1] for q in queue), []))
    got = _exchange_shards(last[1], [LAYOUT[n] for n in last[0]])
    for q in queue:
        finish(q[0], q[1], got[:len(q[0])], q[2])
        got = got[len(q[0]):]
    Gb = dict(zip(BIG, _join_halves([red[n] for n in BIG])))

    small = jnp.concatenate([jnp.concatenate([g.reshape(-1) for g in G[n]]) for n in SMALL])
    dmod_v = jnp.concatenate([jnp.concatenate([d.reshape(-1) for d in row]) for row in dmod])
    n_small, n_dmod = small.shape[0], dmod_v.shape[0]
    wblk = -(-(n_small + n_dmod) // (8 * LANE)) * LANE
    blk = jnp.pad(jnp.concatenate([dmod_v, small]), (0, 8 * wblk - n_small - n_dmod)).reshape(8, wblk)
    parts = _ag_small("gather_small", blk).reshape(N_DEV, 8, wblk)
    tot = _sum_devices("sum_small", parts).reshape(-1)
    g_ada_b = tot[:n_dmod].reshape(L, 6 * D)
    off = n_dmod
    Gs = {}
    for n in SMALL:
        Gs[n] = tot[off:off + W[n].size].reshape(W[n].shape)
        off += W[n].size
    dmod_all = parts.reshape(N_DEV, 8 * wblk)[:, :n_dmod].reshape(N_DEV, L, N_CHIPS, w_ada)
    dmod_sh = lax.dynamic_index_in_dim(dmod_all, chip, axis=2, keepdims=False)
    dmod_sh = jnp.pad(dmod_sh, ((0, 8), (0, 0), (0, 0)))
    dmod_sh = dmod_sh.transpose(1, 0, 2)

    grads = dict(Gb)
    grads.update(Gs)
    grads["ada_b"] = g_ada_b
    delta, new_m, new_v = {}, {}, {}
    grads["ada_w"], delta["ada_w"], new_m["ada_w"], new_v["ada_w"] = _adamw_ada(c_act, dmod_sh, ada_w, m_ada_w, v_ada_w)
    for n in WEIGHTS:
        if n in SMALL or n in ("ada_b", "ada_w"):
            continue
        shp = W[n].shape
        delta[n], new_m[n], new_v[n] = [r.reshape(shp) for r in _adamw(
            "adamw_" + n, _flat(W[n]), _flat(grads[n]), _flat(Mo[n]), _flat(Vo[n]))]
    names_s = SMALL + ("ada_b",)
    cat = lambda d: jnp.concatenate([d[n].reshape(-1) for n in names_s])
    n_s = sum(W[n].size for n in names_s)
    ws = -(-n_s // (8 * LANE)) * LANE
    pk = lambda d: jnp.pad(cat(d), (0, 8 * ws - n_s)).reshape(8, ws)
    ds, ms_, vs = _adamw("adamw_small", pk(W), pk(grads), pk(Mo), pk(Vo))
    off = 0
    for n in names_s:
        sz, shp = W[n].size, W[n].shape
        delta[n] = ds.reshape(-1)[off:off + sz].reshape(shp)
        new_m[n] = ms_.reshape(-1)[off:off + sz].reshape(shp)
        new_v[n] = vs.reshape(-1)[off:off + sz].reshape(shp)
        off += sz

    return (loss, grad_x[None], *[grads[n].reshape(W[n].shape) for n in WEIGHTS], *[delta[n] for n in WEIGHTS],
            *[new_m[n] for n in WEIGHTS], *[new_v[n] for n in WEIGHTS])
```

```python
import functools

import numpy as np
import jax
import jax.numpy as jnp
from jax import lax
from jax.experimental import pallas as pl
from jax.experimental.pallas import tpu as pltpu

F32 = jnp.float32
BF16 = jnp.bfloat16
MXU_DTYPE = jnp.bfloat16
WIRE_DTYPE = jnp.bfloat16

HEAD_DIM = 128
ROPE_DIM = 64
CHUNK = 64
ROPE_THETA = 10000.0
LN_EPS = 1e-5
RMS_EPS = 1e-6
ADAM_LR, ADAM_B1, ADAM_B2, ADAM_EPS, ADAM_WD, ADAM_STEP = 0.001, 0.9, 0.999, 1e-08, 0.01, 10

N_CHIPS = 4
N_DEV = 8
LANE = 128
VMEM_LIMIT = 56 * 1024 * 1024
MESH = pl.DeviceIdType.MESH
ANY = pl.BlockSpec(memory_space=pl.ANY)
NEG = -1e30


def _params(**kw):
    return pltpu.CompilerParams(vmem_limit_bytes=VMEM_LIMIT, **kw)


def _pick(n, cands):
    for c in cands:
        if n % c == 0:
            return c
    return n


def _sigmoid(x):
    return 1.0 / (1.0 + jnp.exp(-x))


def _mm(name, terms, M, N, out_dtypes, epilogue=None, extras=(), row_extras=(), tm=512, tn=512, carry=None):
    tm = _pick(M, (tm, 256, 128))
    tn = _pick(N, (tn, 896, 768, 640, 384, 256, 128))
    extras = tuple(extras) + tuple(row_extras)
    n_row = len(row_extras)
    n_terms, n_ex, n_out = len(terms), len(extras), len(out_dtypes)
    n_acc = 1 + max(t[4] for t in terms)
    flags = [(t[2], t[3], t[4]) for t in terms]

    gi, gj = M // tm, N // tn

    def body(*refs):
        refs, c_in, c_out, c_sems = _carry_split(refs, 2 * n_terms + n_ex, n_out, 0, carry)
        pi, pj = pl.program_id(0), pl.program_id(1)
        _carry_run(carry, c_in, c_out, c_sems, jnp.logical_and(pi == 0, pj == 0), True)
        accs = [None] * n_acc
        for k, (ta, tb, ai) in enumerate(flags):
            a = refs[2 * k][...].astype(MXU_DTYPE)
            b = refs[2 * k + 1][...].astype(MXU_DTYPE)
            dn = (((0 if ta else 1,), (1 if tb else 0,)), ((), ()))
            r = lax.dot_general(a, b, dn, preferred_element_type=F32)
            accs[ai] = r if accs[ai] is None else accs[ai] + r
        ex = [refs[2 * n_terms + k][...] for k in range(n_ex)]
        outs = epilogue(accs, *ex) if epilogue is not None else (accs[0],)
        for k in range(n_out):
            o_ref = refs[2 * n_terms + n_ex + k]
            o_ref[...] = outs[k].astype(o_ref.dtype)
        _carry_run(carry, c_in, c_out, c_sems, jnp.logical_and(pi == gi - 1, pj == gj - 1), False)

    in_specs, args = [], []
    for (a, b, ta, tb, _, bcol) in terms:
        K = a.shape[0] if ta else a.shape[1]
        in_specs.append(pl.BlockSpec((K, tm), lambda i, j: (0, i)) if ta
                        else pl.BlockSpec((tm, K), lambda i, j: (i, 0)))
        if isinstance(b, tuple):
            b, lyr = b
            in_specs.append(pl.BlockSpec((None, tn, K), lambda i, j, o=bcol, l=lyr: (l, j + o, 0)) if tb
                            else pl.BlockSpec((None, K, tn), lambda i, j, o=bcol, l=lyr: (l, 0, j + o)))
        else:
            in_specs.append(pl.BlockSpec((tn, K), lambda i, j, o=bcol: (j + o, 0)) if tb
                            else pl.BlockSpec((K, tn), lambda i, j, o=bcol: (0, j + o)))
        args += [a, b]
    for k, e in enumerate(extras):
        in_specs.append(pl.BlockSpec((tm, tn), (lambda i, j: (i, 0)) if k >= n_ex - n_row else (lambda i, j: (i, j))))
        args.append(e)
    c_is, c_os, c_shape, c_sems, c_alias = _carry_call_args(carry, len(args), n_out)
    outs = pl.pallas_call(
        body, name=name, grid=(gi, gj), in_specs=in_specs + c_is,
        out_specs=[pl.BlockSpec((tm, tn), lambda i, j: (i, j)) for _ in out_dtypes] + c_os,
        out_shape=[jax.ShapeDtypeStruct((M, N), d) for d in out_dtypes] + c_shape,
        input_output_aliases=c_alias, scratch_shapes=c_sems,
        compiler_params=_params(),
    )(*args, *(carry["ins"] if carry else ()))
    return (outs[:n_out], list(outs[n_out:])) if carry else outs


def _wdim(b, axis):
    return b[0].shape[1 + axis] if isinstance(b, tuple) else b.shape[axis]


def _mm1(name, a, b, ta=False, tb=False, out_dtype=F32, bcol=0, N=None, **kw):
    M = a.shape[1] if ta else a.shape[0]
    if N is None:
        N = _wdim(b, 0 if tb else 1)
    res = _mm(name, [(a, b, ta, tb, 0, bcol)], M, N, [out_dtype], **kw)
    return (res[0][0], res[1]) if kw.get("carry") else res[0]


def _rowwise(name, fn, tiled, vecs, outs, reds=(), tr=128):
    R = tiled[0].shape[0]
    tr = _pick(R, (tr, 64, 32, 16, 8))
    nt, nv, no, nr = len(tiled), len(vecs), len(outs), len(reds)

    def body(*refs):
        vals = [r[...] for r in refs[:nt + nv]]
        res = fn(*vals)
        for k in range(no):
            o_ref = refs[nt + nv + k]
            o_ref[...] = res[k].astype(o_ref.dtype)
        if nr:
            first = pl.program_id(0) == 0
            for k in range(nr):
                r_ref = refs[nt + nv + no + k]

                @pl.when(first)
                def _(r_ref=r_ref, v=res[no + k]):
                    r_ref[...] = v

                @pl.when(jnp.logical_not(first))
                def _(r_ref=r_ref, v=res[no + k]):
                    r_ref[...] += v

    in_specs = [pl.BlockSpec((tr, t.shape[1]), lambda i: (i, 0)) for t in tiled]
    in_specs += [pl.BlockSpec(v.shape, lambda i, n=v.ndim: (0,) * n) for v in vecs]
    out_specs = [pl.BlockSpec((tr, w), lambda i: (i, 0)) for (w, _) in outs]
    out_specs += [pl.BlockSpec((1, w), lambda i: (0, 0)) for w in reds]
    out_shape = [jax.ShapeDtypeStruct((R, w), d) for (w, d) in outs]
    out_shape += [jax.ShapeDtypeStruct((1, w), F32) for w in reds]
    return pl.pallas_call(
        body, name=name, grid=(R // tr,), in_specs=in_specs, out_specs=out_specs, out_shape=out_shape,
        compiler_params=_params(),
    )(*tiled, *vecs)


def _colsum(v):
    return jnp.sum(v, axis=0, keepdims=True)


def _ln_stats(z):
    mu = jnp.mean(z, axis=-1, keepdims=True)
    zc = z - mu
    var = jnp.mean(zc * zc, axis=-1, keepdims=True)
    rstd = lax.rsqrt(var + LN_EPS)
    return zc * rstd, rstd


def _ln_bwd(dout, xhat, rstd, lg):
    dxh = dout * lg
    m1 = jnp.mean(dxh, axis=-1, keepdims=True)
    m2 = jnp.mean(dxh * xhat, axis=-1, keepdims=True)
    return rstd * (dxh - m1 - xhat * m2)


def _modulate(x, sc, sh):
    D = x.shape[1]
    return _rowwise("modulate", lambda x, sc, sh: ((x * (1.0 + sc) + sh),), [x], [sc, sh], [(D, MXU_DTYPE)])[0]


def _resid_ln_mod(x, y, g, lg, lb, sc_n, sh_n, alpha):
    D = x.shape[1]

    def fn(x, y, g, lg, lb, sc, sh):
        z = alpha * x + (1.0 + g) * y
        xhat, _ = _ln_stats(z)
        xo = xhat * lg + lb
        return z, xo, xo * (1.0 + sc) + sh

    return _rowwise("resid_ln_mod", fn, [x, y], [g, lg, lb, sc_n, sh_n], [(D, F32), (D, F32), (D, MXU_DTYPE)])


def _final_ln_loss(x, y, tgt, g, lg, lb, alpha):
    D = x.shape[1]

    def fn(x, y, t, g, lg, lb):
        z = alpha * x + (1.0 + g) * y
        xhat, rstd = _ln_stats(z)
        out = xhat * lg + lb
        err = out - t
        loss = jnp.sum(jnp.sum(err * err, axis=-1, keepdims=True), axis=0, keepdims=True)
        dout = err * (1.0 / D)
        dz = _ln_bwd(dout, xhat, rstd, lg)
        return (alpha * dz, (1.0 + g) * dz, jnp.broadcast_to(loss, (1, LANE)),
                _colsum(dout * xhat), _colsum(dout), _colsum(dz * y))

    return _rowwise("final_ln_loss", fn, [x, y, tgt], [g, lg, lb], [(D, F32), (D, MXU_DTYPE)], [LANE, D, D, D])


def _bwd_boundary(dx_res, dh, z_p, y_p, sc, g_p, lg_p, lb_p, alpha):
    D = dh.shape[1]

    def fn(dxr, dh, z, y, sc, g, lg, lb):
        xhat, rstd = _ln_stats(z)
        x_in = xhat * lg + lb
        dx = dxr + dh * (1.0 + sc)
        dz = _ln_bwd(dx, xhat, rstd, lg)
        return (alpha * dz, (1.0 + g) * dz,
                _colsum(dh * x_in), _colsum(dh), _colsum(dx * xhat), _colsum(dx), _colsum(dz * y))

    return _rowwise("bwd_boundary", fn, [dx_res, dh, z_p, y_p], [sc, g_p, lg_p, lb_p],
                    [(D, F32), (D, MXU_DTYPE)], [D, D, D, D, D])


def _first_bwd(dx_res, dh, x, sc):
    D = dh.shape[1]

    def fn(dxr, dh, x, sc):
        return dxr + dh * (1.0 + sc), _colsum(dh * x), _colsum(dh)

    return _rowwise("first_bwd", fn, [dx_res, dh, x], [sc], [(D, F32)], [D, D])


def _ride(ride, phase, bufs=None):
    if ride is None:
        return None
    bufs = ride["bufs"] if bufs is None else bufs
    return _gather_carry(bufs, ride["lays"], [0] * len(bufs), phase)


def _ffn_fwd(h, w1, w3, w2, ride=None):
    S, F = h.shape[0], _wdim(w1, 1)

    def epi(accs):
        a, b = accs
        return a, b, a * _sigmoid(a) * b

    res = _mm("ffn_up", [(h, w1, False, False, 0, 0), (h, w3, False, False, 1, 0)], S, F,
              [MXU_DTYPE, MXU_DTYPE, MXU_DTYPE], epilogue=epi, carry=_ride(ride, "ici"))
    (a, b, u), bufs = res if ride else (res, [])
    res = _mm1("ffn_down", u, w2, carry=_ride(ride, "d2d", bufs))
    y, bufs = res if ride else (res, [])
    return y, (a, b, u), bufs


def _ffn_bwd(dy, h, saved, w1, w3, w2, carry_du=None, carry_dh=None, swap_of=None):
    a, b, u = saved
    S, F = a.shape
    D = h.shape[1]

    def epi(accs, a, b):
        du = accs[0]
        a = a.astype(F32)
        b = b.astype(F32)
        sg = _sigmoid(a)
        return du * b * (sg * (1.0 + a * (1.0 - sg))), du * (a * sg)

    res = _mm("ffn_du", [(dy, w2, False, True, 0, 0)], S, F, [MXU_DTYPE, MXU_DTYPE], epilogue=epi, extras=(a, b),
              carry=carry_du)
    (da, db), got_du = res if carry_du else (res, [])
    dw2 = _mm1("ffn_dw2", u, dy, ta=True, out_dtype=WIRE_DTYPE)
    dw1, dw3 = _mm("ffn_dw13", [(h, da, True, False, 0, 0), (h, db, True, False, 1, 0)], D, F,
                   [WIRE_DTYPE, WIRE_DTYPE], epilogue=lambda accs: (accs[0], accs[1]))
    carry = _merge_carries(carry_dh, swap_of((dw1, dw3, dw2)) if swap_of else None)
    res = _mm("ffn_dh", [(da, w1, False, True, 0, 0), (db, w3, False, True, 0, 0)], S, D, [F32], tn=256, carry=carry)
    (dh,), got = res if carry else (res, [])
    n_dh = len(carry_dh["out_shape"]) if carry_dh else 0
    return dh, dw1, dw3, dw2, got_du, got[:n_dh], got[n_dh:]


def _rope_tables(pos):
    j = np.arange(LANE)
    invf = ROPE_THETA ** (-jnp.arange(0, ROPE_DIM, 2, dtype=F32) / ROPE_DIM)
    invf = invf[(j % ROPE_DIM) // 2].reshape(1, LANE)
    sgn = jnp.asarray(np.where(j % 2 == 0, -1.0, 1.0).reshape(1, LANE), F32)

    def fn(pos, invf, sgn):
        ang = pos.astype(F32) * invf
        return jnp.cos(ang), jnp.sin(ang) * sgn

    return _rowwise("rope_tables", fn, [pos], [invf, sgn], [(LANE, F32), (LANE, F32)], tr=256)


def _pair_swap(x):
    w = x.shape[1]
    even = (lax.broadcasted_iota(jnp.int32, x.shape, 1) % 2) == 0
    return jnp.where(even, pltpu.roll(x, w - 1, 1), pltpu.roll(x, 1, 1))


def _rope_fwd(x, c, s):
    return x * c + _pair_swap(x) * s


def _rope_bwd(d, c, s):
    return d * c + _pair_swap(d * s)


ATT_T = 256


def _dot_nt(a, b):
    return lax.dot_general(a, b, (((1,), (1,)), ((), ())), preferred_element_type=F32)


def _dot_tn(a, b):
    return lax.dot_general(a, b, (((0,), (0,)), ((), ())), preferred_element_type=F32)


def _dot_nn(a, b):
    return lax.dot_general(a, b, (((1,), (0,)), ((), ())), preferred_element_type=F32)


def _diag_mask(T, gran):
    r = lax.broadcasted_iota(jnp.int32, (T, T), 0)
    c = lax.broadcasted_iota(jnp.int32, (T, T), 1)
    if gran > 1:
        sh = int(np.log2(gran))
        r, c = lax.shift_right_logical(r, sh), lax.shift_right_logical(c, sh)
    return r >= c


def _attn_specs(S, H, T, mla, col_q, col_k, col_v):
    W = 2 * HEAD_DIM
    specs = [pl.BlockSpec((T, W), lambda p, i: (i, col_q + p))]
    if mla:
        specs.append(pl.BlockSpec((T, 2 * ROPE_DIM), lambda p, i: (i, p)))
    specs.append(pl.BlockSpec((S, W), lambda p, i: (0, col_k + p)))
    if mla:
        specs.append(pl.BlockSpec((S, ROPE_DIM), lambda p, i: (0, 0)))
    specs.append(pl.BlockSpec((S, W), lambda p, i: (0, col_v + p)))
    if not mla:
        specs.append(pl.BlockSpec((2, T, 1), lambda p, i: (p, i, 0)))
        specs.append(pl.BlockSpec((2, 1, S), lambda p, i: (p, 0, 0)))
    return specs


def _attn_fwd(name, S, H, mla, q, k, v, q_pe=None, k_pe=None, cum_col=None, cum_row=None, cols=(0, 0, 0),
              carry=None):
    T = _pick(S, (ATT_T, 128))
    nq = S // T
    scale = (HEAD_DIM + ROPE_DIM) ** -0.5 if mla else HEAD_DIM ** -0.5
    gran = CHUNK if mla else 1

    def body(*refs):
        refs, c_in, c_out, c_sems = _carry_split(refs, 5, 3, 3, carry)
        if mla:
            q_ref, qpe_ref, k_ref, kpe_ref, v_ref, o_ref, of_ref, lse_ref, m_s, l_s, acc_s = refs
        else:
            q_ref, k_ref, v_ref, cc_ref, cr_ref, o_ref, of_ref, lse_ref, m_s, l_s, acc_s = refs
        hp, qi = pl.program_id(0), pl.program_id(1)
        _carry_run(carry, c_in, c_out, c_sems, jnp.logical_and(hp == 0, qi == 0), True)
        hls = [slice(hh * HEAD_DIM, (hh + 1) * HEAD_DIM) for hh in range(2)]
        qn = [q_ref[:, hl] for hl in hls]
        qp = [qpe_ref[:, hh * ROPE_DIM:(hh + 1) * ROPE_DIM] for hh in range(2)] if mla else None
        m_s[...] = jnp.full(m_s.shape, NEG, F32)
        l_s[...] = jnp.zeros(l_s.shape, F32)
        acc_s[...] = jnp.zeros(acc_s.shape, F32)

        def step(j, masked):
            rows = pl.ds(pl.multiple_of(j * T, T), T)
            for hh, hl in enumerate(hls):
                s = _dot_nt(qn[hh], k_ref[rows, hl])
                if mla:
                    s = s + _dot_nt(qp[hh], kpe_ref[rows, :])
                s = s * scale
                if not mla:
                    s = s + (cc_ref[hh] - cr_ref[hh, :, rows])
                if masked:
                    s = jnp.where(_diag_mask(T, gran), s, NEG)
                m_old = m_s[hh, :, 0:1]
                m_new = jnp.maximum(m_old, jnp.max(s, axis=-1, keepdims=True))
                p = jnp.exp(s - m_new)
                corr = jnp.exp(m_old - m_new)
                l_s[hh] = jnp.broadcast_to(corr * l_s[hh, :, 0:1] + jnp.sum(p, axis=-1, keepdims=True), (T, LANE))
                p_hi = p.astype(MXU_DTYPE)
                p_lo = (p - p_hi.astype(F32)).astype(MXU_DTYPE)
                vf = v_ref[rows, hl]
                acc_s[hh] = corr * acc_s[hh] + (_dot_nn(p_hi, vf) + _dot_nn(p_lo, vf))
                m_s[hh] = jnp.broadcast_to(m_new, (T, LANE))

        lax.fori_loop(0, qi, lambda j, c: (step(j, False), c)[1], 0)
        step(qi, True)
        for hh, hl in enumerate(hls):
            l = l_s[hh, :, 0:1]
            of = acc_s[hh] / l
            of_ref[:, hl] = of
            o_ref[:, hl] = of.astype(o_ref.dtype)
            lse_ref[hh] = jnp.broadcast_to(m_s[hh, :, 0:1] + jnp.log(l), (T, LANE))
        _carry_run(carry, c_in, c_out, c_sems, jnp.logical_and(hp == H // 2 - 1, qi == nq - 1), False)

    args = [q] + ([q_pe] if mla else []) + [k] + ([k_pe] if mla else []) + [v]
    if not mla:
        args += [cum_col, cum_row]
    c_is, c_os, c_shape, c_sems, c_alias = _carry_call_args(carry, 5, 3)
    res = pl.pallas_call(
        body, name=name, grid=(H // 2, nq),
        in_specs=_attn_specs(S, H, T, mla, *cols) + c_is,
        out_specs=[pl.BlockSpec((T, 2 * HEAD_DIM), lambda p, i: (i, p)),
                   pl.BlockSpec((T, 2 * HEAD_DIM), lambda p, i: (i, p)),
                   pl.BlockSpec((2, T, LANE), lambda p, i: (p, i, 0))] + c_os,
        out_shape=[jax.ShapeDtypeStruct((S, H * HEAD_DIM), MXU_DTYPE), jax.ShapeDtypeStruct((S, H * HEAD_DIM), F32),
                   jax.ShapeDtypeStruct((H, S, LANE), F32)] + c_shape,
        input_output_aliases=c_alias,
        scratch_shapes=[pltpu.VMEM((2, T, LANE), F32), pltpu.VMEM((2, T, LANE), F32),
                        pltpu.VMEM((2, T, HEAD_DIM), F32)] + c_sems,
        compiler_params=_params(),
    )(*args, *(carry["ins"] if carry else ()))
    return res[0], res[1], res[2], list(res[3:])


def _carry_split(refs, n_in, n_out, n_scr, carry):
    if carry is None:
        return refs, (), (), ()
    ci, co = len(carry["ins"]), len(carry["out_shape"])
    own = refs[:n_in] + refs[n_in + ci:n_in + ci + n_out] + refs[n_in + ci + n_out + co:n_in + ci + n_out + co + n_scr]
    return (own, refs[n_in:n_in + ci], refs[n_in + ci + n_out:n_in + ci + n_out + co],
            refs[n_in + ci + n_out + co + n_scr:])


def _carry_run(carry, c_in, c_out, c_sems, when, start):
    if carry is None:
        return

    @pl.when(when)
    def _():
        for cp in carry["copies"](c_in, c_out, *c_sems):
            if start:
                cp.start()
            else:
                cp.wait()


def _carry_call_args(carry, n_in, n_out):
    if carry is None:
        return [], [], [], [], {}
    sems = [pltpu.SemaphoreType.DMA(carry["sems"]), pltpu.SemaphoreType.DMA(carry["sems"])]
    alias = {n_in + i: n_out + o for i, o in carry.get("alias", {}).items()}
    return [ANY] * len(carry["ins"]), [ANY] * len(carry["out_shape"]), list(carry["out_shape"]), sems, alias


def _attn_bwd(name, S, H, mla, q, k, v, do, of, lse, q_pe=None, k_pe=None, cum_col=None, cum_row=None,
              cols=(0, 0, 0), carry=None):
    T = _pick(S, (ATT_T, 128))
    nq = S // T
    scale = (HEAD_DIM + ROPE_DIM) ** -0.5 if mla else HEAD_DIM ** -0.5
    gran = CHUNK if mla else 1
    dqk = HEAD_DIM + (ROPE_DIM if mla else 0)

    def body(*refs):
        refs, c_in, c_out, c_sems = _carry_split(refs, 8, 5, 2, carry)
        if mla:
            (q_ref, qpe_ref, k_ref, kpe_ref, v_ref, do_ref, of_ref, lse_ref,
             dq_ref, dk_ref, dv_ref, dqpe_ref, dkpe_ref, dq_s, r_s) = refs
        else:
            (q_ref, k_ref, v_ref, cc_ref, cr_ref, do_ref, of_ref, lse_ref,
             dq_ref, dk_ref, dv_ref, dck_ref, dcq_ref, dq_s, r_s) = refs
        hp, qi = pl.program_id(0), pl.program_id(1)
        _carry_run(carry, c_in, c_out, c_sems, jnp.logical_and(hp == 0, qi == 0), True)

        @pl.when(qi == 0)
        def _():
            dk_ref[...] = jnp.zeros(dk_ref.shape, F32)
            dv_ref[...] = jnp.zeros(dv_ref.shape, F32)
            if not mla:
                dck_ref[...] = jnp.zeros(dck_ref.shape, F32)

        if mla:
            @pl.when(jnp.logical_and(qi == 0, hp == 0))
            def _():
                dkpe_ref[...] = jnp.zeros(dkpe_ref.shape, F32)

        hls = [slice(hh * HEAD_DIM, (hh + 1) * HEAD_DIM) for hh in range(2)]
        qn = [q_ref[:, hl] for hl in hls]
        qp = [qpe_ref[:, hh * ROPE_DIM:(hh + 1) * ROPE_DIM] for hh in range(2)] if mla else None
        dof = [do_ref[:, hl] for hl in hls]
        delta = [jnp.sum(dof[hh].astype(F32) * of_ref[:, hl], axis=-1, keepdims=True) for hh, hl in enumerate(hls)]
        lse = [lse_ref[hh][:, 0:1] for hh in range(2)]
        dq_s[...] = jnp.zeros(dq_s.shape, F32)
        r_s[...] = jnp.zeros(r_s.shape, F32)

        def step(j, masked):
            rows = pl.ds(pl.multiple_of(j * T, T), T)
            for hh, hl in enumerate(hls):
                kn = k_ref[rows, hl]
                s = _dot_nt(qn[hh], kn)
                if mla:
                    kp = kpe_ref[rows, :]
                    s = s + _dot_nt(qp[hh], kp)
                s = s * scale
                if not mla:
                    s = s + (cc_ref[hh] - cr_ref[hh, :, rows])
                if masked:
                    s = jnp.where(_diag_mask(T, gran), s, NEG)
                p = jnp.exp(s - lse[hh])
                dp = _dot_nt(dof[hh], v_ref[rows, hl])
                ds = p * (dp - delta[hh])
                dv_ref[rows, hl] += _dot_tn(p.astype(MXU_DTYPE), dof[hh])
                dsb = (ds * scale).astype(MXU_DTYPE)
                dk_ref[rows, hl] += _dot_tn(dsb, qn[hh])
                dq_s[hh, :, :HEAD_DIM] += _dot_nn(dsb, kn)
                if mla:
                    dkpe_ref[rows, :] += _dot_tn(dsb, qp[hh])
                    dq_s[hh, :, HEAD_DIM:] += _dot_nn(dsb, kp)
                else:
                    dck_ref[hh, :, rows] -= jnp.sum(ds, axis=0, keepdims=True)
                    r_s[hh] += jnp.broadcast_to(jnp.sum(ds, axis=-1, keepdims=True), (T, LANE))

        lax.fori_loop(0, qi, lambda j, c: (step(j, False), c)[1], 0)
        step(qi, True)
        for hh, hl in enumerate(hls):
            dq_ref[:, hl] = dq_s[hh, :, :HEAD_DIM].astype(dq_ref.dtype)
            if mla:
                dqpe_ref[:, hh * ROPE_DIM:(hh + 1) * ROPE_DIM] = dq_s[hh, :, HEAD_DIM:]
            else:
                dcq_ref[hh] = r_s[hh, :, 0:1]
        _carry_run(carry, c_in, c_out, c_sems, jnp.logical_and(hp == H // 2 - 1, qi == nq - 1), False)

    W = 2 * HEAD_DIM
    args = [q] + ([q_pe] if mla else []) + [k] + ([k_pe] if mla else []) + [v]
    if not mla:
        args += [cum_col, cum_row]
    args += [do, of, lse]
    in_specs = _attn_specs(S, H, T, mla, *cols)
    in_specs += [pl.BlockSpec((T, W), lambda p, i: (i, p)), pl.BlockSpec((T, W), lambda p, i: (i, p)),
                 pl.BlockSpec((2, T, LANE), lambda p, i: (p, i, 0))]
    out_specs = [pl.BlockSpec((T, W), lambda p, i: (i, p)), pl.BlockSpec((S, W), lambda p, i: (0, p)),
                 pl.BlockSpec((S, W), lambda p, i: (0, p))]
    out_shape = [jax.ShapeDtypeStruct((S, H * HEAD_DIM), MXU_DTYPE), jax.ShapeDtypeStruct((S, H * HEAD_DIM), F32),
                 jax.ShapeDtypeStruct((S, H * HEAD_DIM), F32)]
    if mla:
        out_specs += [pl.BlockSpec((T, 2 * ROPE_DIM), lambda p, i: (i, p)),
                      pl.BlockSpec((S, ROPE_DIM), lambda p, i: (0, 0))]
        out_shape += [jax.ShapeDtypeStruct((S, H * ROPE_DIM), F32), jax.ShapeDtypeStruct((S, ROPE_DIM), F32)]
    else:
        out_specs += [pl.BlockSpec((2, 1, S), lambda p, i: (p, 0, 0)), pl.BlockSpec((2, T, 1), lambda p, i: (p, i, 0))]
        out_shape += [jax.ShapeDtypeStruct((H, 1, S), F32), jax.ShapeDtypeStruct((H, S, 1), F32)]
    assert len(args) == 8 and len(out_shape) == 5
    c_is, c_os, c_shape, c_sems, c_alias = _carry_call_args(carry, 8, 5)
    res = pl.pallas_call(
        body, name=name, grid=(H // 2, nq), in_specs=in_specs + c_is, out_specs=out_specs + c_os,
        out_shape=out_shape + c_shape, input_output_aliases=c_alias,
        scratch_shapes=[pltpu.VMEM((2, T, dqk), F32), pltpu.VMEM((2, T, LANE), F32)] + c_sems,
        compiler_params=_params(),
    )(*args, *(carry["ins"] if carry else ()))
    return (res[:5], res[5:]) if carry else (res, [])


def _mla_prep(lat, cos, sin, qn, kvn, ql, kvl):
    def fn(lat, c, s, qn, kvn):
        ql_ = lat[:, :ql]
        kv_ = lat[:, ql:ql + kvl]
        kp = lat[:, ql + kvl:]
        cq = ql_ * lax.rsqrt(jnp.mean(ql_ * ql_, axis=-1, keepdims=True) + RMS_EPS) * qn
        ckv = kv_ * lax.rsqrt(jnp.mean(kv_ * kv_, axis=-1, keepdims=True) + RMS_EPS) * kvn
        kp2 = jnp.concatenate([kp, jnp.zeros_like(kp)], axis=-1)
        kr = _rope_fwd(kp2, c, s)[:, :ROPE_DIM]
        return cq, ckv, kr

    return _rowwise("mla_prep", fn, [lat, cos, sin], [qn, kvn],
                    [(ql, MXU_DTYPE), (kvl, MXU_DTYPE), (ROPE_DIM, MXU_DTYPE)])


def _mla_prep_bwd(lat, cos, sin, qn, kvn, dcq, dckv, dkr, ql, kvl):
    def fn(lat, c, s, dcq, dckv, dkr, qn, kvn):
        outs, reds = [], []
        for (x, g, d) in ((lat[:, :ql], qn, dcq), (lat[:, ql:ql + kvl], kvn, dckv)):
            r = lax.rsqrt(jnp.mean(x * x, axis=-1, keepdims=True) + RMS_EPS)
            n = x * r
            dn = d * g
            outs.append(r * (dn - n * jnp.mean(dn * n, axis=-1, keepdims=True)))
            reds.append(_colsum(d * n))
        d2 = jnp.concatenate([dkr, jnp.zeros_like(dkr)], axis=-1)
        outs.append(_rope_bwd(d2, c, s)[:, :ROPE_DIM])
        return (jnp.concatenate(outs, axis=-1), *reds)

    return _rowwise("mla_prep_bwd", fn, [lat, cos, sin, dcq, dckv, dkr], [qn, kvn],
                    [(ql + kvl + ROPE_DIM, MXU_DTYPE)], [ql, kvl])


def _mla_fwd(h, w, rc, ride=None):
    S = h.shape[0]
    ql, kvl = w["q_norm"].shape[1], w["kv_norm"].shape[1]
    H = _wdim(w["w_uk"], 1) // HEAD_DIM
    n_nope, n_pe = H * HEAD_DIM, H * ROPE_DIM
    lat = _mm1("mla_down", h, w["w_down"])
    cq, ckv, kr = _mla_prep(lat, rc[0], rc[1], w["q_norm"], w["kv_norm"], ql, kvl)
    q_nope = _mm1("mla_uq_nope", cq, w["w_uq"], out_dtype=MXU_DTYPE, N=n_nope)
    q_pe = _mm("mla_uq_pe", [(cq, w["w_uq"], False, False, 0, n_nope // LANE)], S, n_pe, [MXU_DTYPE],
               epilogue=lambda accs, c, s: (_rope_fwd(accs[0], c, s),), row_extras=rc, tn=LANE)[0]
    k_nope, v = _mm("mla_ukv", [(ckv, w["w_uk"], False, False, 0, 0), (ckv, w["w_uv"], False, False, 1, 0)],
                    S, n_nope, [MXU_DTYPE, MXU_DTYPE], epilogue=lambda accs: (accs[0], accs[1]))
    o, of, lse, bufs = _attn_fwd("mla_attn_fwd", S, H, True, q_nope, k_nope, v, q_pe=q_pe, k_pe=kr,
                                 carry=_ride(ride, "ici"))
    res = _mm1("mla_wo", o, w["w_o"], carry=_ride(ride, "d2d", bufs))
    y, bufs = res if ride else (res, [])
    return y, (lat, cq, ckv, kr, q_nope, q_pe, k_nope, v, o, of, lse), bufs


def _mla_bwd(dy, h, saved, w, rc, carry):
    lat, cq, ckv, kr, q_nope, q_pe, k_nope, v, o, of, lse = saved
    S = h.shape[0]
    ql, kvl = w["q_norm"].shape[1], w["kv_norm"].shape[1]
    H = _wdim(w["w_uk"], 1) // HEAD_DIM
    n_nope, n_pe = H * HEAD_DIM, H * ROPE_DIM
    do = _mm1("mla_do", dy, w["w_o"], tb=True, out_dtype=MXU_DTYPE)
    dw_o = _mm1("mla_dwo", o, dy, ta=True, out_dtype=WIRE_DTYPE)
    (dq_nope, dk_nope, dv, dq_pe_r, dk_pe_r), carried = _attn_bwd(
        "mla_attn_bwd", S, H, True, q_nope, k_nope, v, do, of, lse, q_pe=q_pe, k_pe=kr, carry=carry)

    def unrope(d, c, s):
        reps = (1, n_pe // LANE)
        return (_rope_bwd(d, jnp.tile(c, reps), jnp.tile(s, reps)),)

    dq_pe = _rowwise("mla_unrope_q", unrope, [dq_pe_r, rc[0], rc[1]], [], [(n_pe, MXU_DTYPE)])[0]
    dq = jnp.concatenate([dq_nope, dq_pe], axis=1)
    dw_uq = _mm1("mla_dwuq", cq, dq, ta=True, out_dtype=WIRE_DTYPE)
    dcq = _mm1("mla_dcq", dq, w["w_uq"], tb=True)
    dw_uk, dw_uv = _mm("mla_dwukv", [(ckv, dk_nope, True, False, 0, 0), (ckv, dv, True, False, 1, 0)], kvl, n_nope,
                       [WIRE_DTYPE, WIRE_DTYPE], epilogue=lambda accs: (accs[0], accs[1]))
    dckv = _mm("mla_dckv", [(dk_nope, w["w_uk"], False, True, 0, 0), (dv, w["w_uv"], False, True, 0, 0)],
               S, kvl, [F32])[0]
    dlat, dqn, dkvn = _mla_prep_bwd(lat, rc[0], rc[1], w["q_norm"], w["kv_norm"], dcq, dckv, dk_pe_r, ql, kvl)
    dw_down = _mm1("mla_dwdown", h, dlat, ta=True, out_dtype=WIRE_DTYPE)
    dh = _mm1("mla_dh", dlat, w["w_down"], tb=True)
    return dh, dict(w_down=dw_down, q_norm=dqn, w_uq=dw_uq, kv_norm=dkvn, w_uk=dw_uk, w_uv=dw_uv, w_o=dw_o), carried


def _log_sigmoid(z):
    return jnp.minimum(z, 0.0) - jnp.log(1.0 + jnp.exp(-jnp.abs(z)))


def _fox_gate_fwd(f, bf):
    S = f.shape[0]
    B = LANE

    def body(f_ref, b_ref, cum_ref):
        r = lax.broadcasted_iota(jnp.int32, (B, B), 0)
        c = lax.broadcasted_iota(jnp.int32, (B, B), 1)
        tri = (r >= c).astype(F32)
        carry = jnp.zeros((1, LANE), F32)
        for blk in range(S // B):
            rows = slice(blk * B, (blk + 1) * B)
            lf = _log_sigmoid(f_ref[rows, :] + b_ref[...])
            cs = jnp.dot(tri, lf, precision=lax.Precision.HIGHEST, preferred_element_type=F32) + carry
            cum_ref[rows, :] = cs
            carry = cs[B - 1:B, :]

    return pl.pallas_call(body, name="fox_gate_fwd", out_shape=jax.ShapeDtypeStruct((S, LANE), F32),
                          compiler_params=_params())(f, bf)


def _fox_gate_bwd(dcum, f, bf):
    S = f.shape[0]
    B = LANE

    def body(d_ref, f_ref, b_ref, df_ref, db_ref):
        r = lax.broadcasted_iota(jnp.int32, (B, B), 0)
        c = lax.broadcasted_iota(jnp.int32, (B, B), 1)
        tri = (r <= c).astype(F32)
        carry = jnp.zeros((1, LANE), F32)
        db = jnp.zeros((1, LANE), F32)
        for blk in reversed(range(S // B)):
            rows = slice(blk * B, (blk + 1) * B)
            dlf = jnp.dot(tri, d_ref[rows, :], precision=lax.Precision.HIGHEST, preferred_element_type=F32) + carry
            carry = dlf[0:1, :]
            z = f_ref[rows, :] + b_ref[...]
            dz = dlf * _sigmoid(-z)
            df_ref[rows, :] = dz.astype(df_ref.dtype)
            db = db + jnp.sum(dz, axis=0, keepdims=True)
        db_ref[...] = db

    return pl.pallas_call(body, name="fox_gate_bwd",
                          out_shape=[jax.ShapeDtypeStruct((S, LANE), MXU_DTYPE), jax.ShapeDtypeStruct((1, LANE), F32)],
                          compiler_params=_params())(dcum, f, bf)


def _fox_fwd(h, w, ride=None):
    S, D = h.shape
    H = D // HEAD_DIM
    qkv = _mm1("fox_qkv", h, w["w_in"], out_dtype=MXU_DTYPE, N=3 * D)
    f = _mm1("fox_f", h, w["w_in"], bcol=3 * D // LANE, N=LANE, tn=LANE)
    cum = _fox_gate_fwd(f, w["b_f"])
    cumT = cum[:, :H].T
    cum_col, cum_row = cumT.reshape(H, S, 1), cumT.reshape(H, 1, S)
    nb = D // (2 * HEAD_DIM)
    o, of, lse, bufs = _attn_fwd("fox_attn_fwd", S, H, False, qkv, qkv, qkv, cum_col=cum_col, cum_row=cum_row,
                                 cols=(0, nb, 2 * nb), carry=_ride(ride, "ici"))
    res = _mm1("fox_wo", o, w["w_o"], carry=_ride(ride, "d2d", bufs))
    y, bufs = res if ride else (res, [])
    return y, (qkv, f, cum_col, cum_row, o, of, lse), bufs


def _fox_bwd(dy, h, saved, w, carry):
    qkv, f, cum_col, cum_row, o, of, lse = saved
    S, D = h.shape
    H = D // HEAD_DIM
    nb = D // (2 * HEAD_DIM)
    do = _mm1("fox_do", dy, w["w_o"], tb=True, out_dtype=MXU_DTYPE)
    dw_o = _mm1("fox_dwo", o, dy, ta=True, out_dtype=WIRE_DTYPE)
    (dq, dk, dv, dck, dcq), carried = _attn_bwd("fox_attn_bwd", S, H, False, qkv, qkv, qkv, do, of, lse,
                                                cum_col=cum_col, cum_row=cum_row, cols=(0, nb, 2 * nb), carry=carry)
    dcum = jnp.pad((dck.reshape(H, S) + dcq.reshape(H, S)).T, ((0, 0), (0, LANE - H)))
    df, dbf = _fox_gate_bwd(dcum, f, w["b_f"])
    dproj = jnp.concatenate([dq, dk.astype(MXU_DTYPE), dv.astype(MXU_DTYPE), df], axis=1)
    dw_in = _mm1("fox_dwin", h, dproj, ta=True, out_dtype=WIRE_DTYPE)
    dh = _mm1("fox_dh", dproj, w["w_in"], tb=True, tn=256)
    return dh, dict(w_in=dw_in, b_f=dbf[:, :H], w_o=dw_o), carried


def _place():
    x, y, c = lax.axis_index("x"), lax.axis_index("y"), lax.axis_index("c")
    return x, y, c, [(1 - x, y), (x, 1 - y), (1 - x, 1 - y)]


def _ag_small(name, blk):
    m, n = blk.shape

    def body(x_ref, out_ref, send_sems, recv_sems, local_sem):
        x, y, c, chips = _place()
        me, sibling = (x, y, c), (x, y, 1 - c)

        def rows(px, py, pc):
            return out_ref.at[pl.ds((4 * px + 2 * py + pc) * m, m), :]

        def copy(k, block, to, src=None):
            return pltpu.make_async_remote_copy(
                src_ref=rows(*block) if src is None else src, dst_ref=rows(*block),
                send_sem=send_sems.at[k], recv_sem=recv_sems.at[k], device_id=to, device_id_type=MESH)

        mine = pltpu.make_async_copy(x_ref, rows(*me), local_sem)
        mine.start()
        first = [copy(0, me, sibling, src=x_ref)]
        first += [copy(1 + j, me, (*chip, c), src=x_ref) for j, chip in enumerate(chips)]
        for cp in first:
            cp.start()
        passed = [copy(4 + j, (*chip, c), sibling) for j, chip in enumerate(chips)]
        for j, chip in enumerate(chips):
            copy(1 + j, (*chip, c), me).wait_recv()
            passed[j].start()
        copy(0, sibling, me).wait_recv()
        for j, chip in enumerate(chips):
            copy(4 + j, (*chip, 1 - c), me).wait_recv()
        for cp in first + passed:
            cp.wait_send()
        mine.wait()

    return pl.pallas_call(
        body, name=name, out_shape=jax.ShapeDtypeStruct((N_DEV * m, n), blk.dtype),
        in_specs=[pl.BlockSpec(memory_space=pltpu.VMEM)], out_specs=pl.BlockSpec(memory_space=pltpu.VMEM),
        scratch_shapes=[pltpu.SemaphoreType.DMA((7,)), pltpu.SemaphoreType.DMA((7,)), pltpu.SemaphoreType.DMA],
        compiler_params=_params(),
    )(blk)


def _half(ref, row_axis, c, rows):
    idx = [slice(None)] * len(ref.shape)
    idx[row_axis] = pl.ds(pl.multiple_of(c * rows, 16), rows)
    return ref.at[tuple(idx)]


def _shard(ref, layout, k):
    if layout == "row":
        return ref.at[:, k]
    w = ref.shape[2] // N_CHIPS
    return ref.at[:, :, pl.ds(pl.multiple_of(k * w, LANE), w)]


def _full_shape(shape, layout):
    L, r, w = shape
    return (L, N_CHIPS, r, w) if layout == "row" else (L, r, N_CHIPS * w)


def _cast_full(name, a, layout, k_idx, lyr):
    _, r, C = a.shape
    tr = _pick(r, (256, 128, 64, 32, 16))

    def body(k_ref, a_ref, o_ref):
        o_ref[...] = a_ref[...].astype(o_ref.dtype)

    if layout == "row":
        o_spec = pl.BlockSpec((None, None, tr, C), lambda l, i, k: (0, k[0], i, 0))
    else:
        o_spec = pl.BlockSpec((None, tr, C), lambda l, i, k: (0, i, k[0]))
    return pl.pallas_call(
        body, name=name,
        grid_spec=pltpu.PrefetchScalarGridSpec(
            num_scalar_prefetch=1, grid=(1, r // tr),
            in_specs=[pl.BlockSpec((None, tr, C), lambda l, i, k: (lyr, i, 0))], out_specs=o_spec),
        out_shape=jax.ShapeDtypeStruct(_full_shape((1, r, C), layout), WIRE_DTYPE),
        compiler_params=_params(),
    )(k_idx, a)


def _gather_carry(fulls, layouts, lyrs, phase):
    n = len(fulls)
    half_rows = [f.shape[2 if lay == "row" else 1] // 2 for f, lay in zip(fulls, layouts)]

    def copies(ins, outs, send_sems, recv_sems, off=0):
        x, y, c, chips = _place()
        cps = []
        for i in range(n):
            for j, chip in enumerate(chips):
                who, to = ((x, y), (*chip, c)) if phase == "ici" else (chip, (x, y, 1 - c))
                w = _half(_shard(outs[i].at[pl.ds(lyrs[i], 1)], layouts[i], 2 * who[0] + who[1]), 1, c, half_rows[i])
                cps.append(pltpu.make_async_remote_copy(
                    src_ref=w, dst_ref=w, send_sem=send_sems.at[off + i, j], recv_sem=recv_sems.at[off + i, j],
                    device_id=to, device_id_type=MESH))
        return cps

    return dict(ins=list(fulls), sems=(n, 3), copies=copies, alias={i: i for i in range(n)},
                out_shape=[jax.ShapeDtypeStruct(f.shape, f.dtype) for f in fulls])


def _run_carry(name, carry):
    n = len(carry["ins"])

    def body(*refs):
        cps = carry["copies"](refs[:n], refs[n:2 * n], *refs[2 * n:])
        for cp in cps:
            cp.start()
        for cp in cps:
            cp.wait()

    return pl.pallas_call(
        body, name=name, in_specs=[ANY] * n, out_specs=[ANY] * len(carry["out_shape"]), out_shape=carry["out_shape"],
        input_output_aliases=dict(carry.get("alias", {})),
        scratch_shapes=[pltpu.SemaphoreType.DMA(carry["sems"]), pltpu.SemaphoreType.DMA(carry["sems"])],
        compiler_params=_params(),
    )(*carry["ins"])


def _gather_weights(fulls, layouts, lyrs):
    n = len(fulls)
    half_rows = [f.shape[2 if lay == "row" else 1] // 2 for f, lay in zip(fulls, layouts)]

    def body(*refs):
        outs = refs[n:2 * n]
        send_sems, recv_sems = refs[2 * n:]
        x, y, c, chips = _place()
        sibling = (x, y, 1 - c)

        def window(i, kx, ky, half):
            return _half(_shard(outs[i].at[pl.ds(lyrs[i], 1)], layouts[i], 2 * kx + ky), 1, half, half_rows[i])

        first, passed = [], []
        for i in range(n):
            mine = window(i, x, y, c)
            for j, chip in enumerate(chips):
                cp = pltpu.make_async_remote_copy(
                    src_ref=mine, dst_ref=mine, send_sem=send_sems.at[i, j], recv_sem=recv_sems.at[i, j],
                    device_id=(*chip, c), device_id_type=MESH)
                cp.start()
                first.append(cp)
        for i in range(n):
            for j, chip in enumerate(chips):
                got = window(i, *chip, c)
                pltpu.make_async_remote_copy(
                    src_ref=got, dst_ref=got, send_sem=send_sems.at[i, j], recv_sem=recv_sems.at[i, j],
                    device_id=(*chip, c), device_id_type=MESH).wait_recv()
                cp = pltpu.make_async_remote_copy(
                    src_ref=got, dst_ref=got, send_sem=send_sems.at[i, 3 + j], recv_sem=recv_sems.at[i, 3 + j],
                    device_id=sibling, device_id_type=MESH)
                cp.start()
                passed.append(cp)
        for i in range(n):
            for j, chip in enumerate(chips):
                got = window(i, *chip, 1 - c)
                pltpu.make_async_remote_copy(
                    src_ref=got, dst_ref=got, send_sem=send_sems.at[i, 3 + j], recv_sem=recv_sems.at[i, 3 + j],
                    device_id=sibling, device_id_type=MESH).wait_recv()
        for cp in first + passed:
            cp.wait_send()

    return pl.pallas_call(
        body, name="gather_weights", in_specs=[ANY] * n, out_specs=[ANY] * n,
        out_shape=[jax.ShapeDtypeStruct(f.shape, f.dtype) for f in fulls],
        input_output_aliases={i: i for i in range(n)},
        scratch_shapes=[pltpu.SemaphoreType.DMA((n, 6)), pltpu.SemaphoreType.DMA((n, 6))],
        compiler_params=_params(),
    )(*fulls)


def _half_shape(shape, layout):
    s = list(shape)
    s[2 if layout == "row" else 1] //= 2
    return tuple(s)


def _swap_carry(grads, layouts):
    n = len(grads)
    row_axis = [2 if lay == "row" else 1 for lay in layouts]
    half_rows = [g.shape[ra] // 2 for g, ra in zip(grads, row_axis)]

    def copies(ins, outs, send_sems, recv_sems, off=0):
        x, y, c, _ = _place()
        return [pltpu.make_async_remote_copy(
            src_ref=_half(ins[i], row_axis[i], 1 - c, half_rows[i]), dst_ref=outs[i], send_sem=send_sems.at[off + i, 0],
            recv_sem=recv_sems.at[off + i, 0], device_id=(x, y, 1 - c), device_id_type=MESH) for i in range(n)]

    return dict(ins=list(grads), sems=(n, 3), copies=copies,
                out_shape=[jax.ShapeDtypeStruct(_half_shape(g.shape, lay), g.dtype) for g, lay in zip(grads, layouts)])


def _merge_carries(a, b):
    if a is None or b is None:
        return a if b is None else b
    na, ia, oa = a["sems"][0], len(a["ins"]), len(a["out_shape"])

    def copies(ins, outs, send_sems, recv_sems, off=0):
        return (a["copies"](ins[:ia], outs[:oa], send_sems, recv_sems, off)
                + b["copies"](ins[ia:], outs[oa:], send_sems, recv_sems, off + na))

    alias = dict(a.get("alias", {}))
    alias.update({ia + i: oa + o for i, o in b.get("alias", {}).items()})
    return dict(ins=a["ins"] + b["ins"], out_shape=a["out_shape"] + b["out_shape"], sems=(na + b["sems"][0], 3),
                copies=copies, alias=alias)


def _add_half(name, g, r, layout, c_idx):
    L = g.shape[0]
    if layout == "row":
        A, rows, W = L * N_CHIPS, g.shape[2] // 2, g.shape[3]
    else:
        A, rows, W = L, g.shape[1] // 2, g.shape[2]
    g3 = g.reshape(A, 2 * rows, W)
    r3 = r.reshape(A, rows, W)
    tr = _pick(rows, (256, 128, 64, 32, 16))
    nb = rows // tr

    def body(c_ref, g_ref, r_ref, o_ref):
        o_ref[...] = (g_ref[...].astype(F32) + r_ref[...].astype(F32)).astype(o_ref.dtype)

    out = pl.pallas_call(
        body, name=name,
        grid_spec=pltpu.PrefetchScalarGridSpec(
            num_scalar_prefetch=1, grid=(A, nb),
            in_specs=[pl.BlockSpec((None, tr, W), lambda a, i, c: (a, c[0] * nb + i, 0)),
                      pl.BlockSpec((None, tr, W), lambda a, i, c: (a, i, 0))],
            out_specs=pl.BlockSpec((None, tr, W), lambda a, i, c: (a, i, 0))),
        out_shape=jax.ShapeDtypeStruct((A, rows, W), WIRE_DTYPE),
        compiler_params=_params(),
    )(c_idx, g3, r3)
    return out.reshape(r.shape)


def _exchange_carry(parts, layouts):
    n = len(parts)

    def shard_half_shape(p, lay):
        if lay == "row":
            return (p.shape[0],) + p.shape[2:]
        return (p.shape[0], p.shape[1], p.shape[2] // N_CHIPS)

    def copies(ins, outs, send_sems, recv_sems, off=0):
        x, y, c, chips = _place()
        return [pltpu.make_async_remote_copy(
            src_ref=_shard(ins[i], layouts[i], 2 * kx + ky), dst_ref=outs[i].at[j], send_sem=send_sems.at[off + i, j],
            recv_sem=recv_sems.at[off + i, j], device_id=(kx, ky, c), device_id_type=MESH)
            for i in range(n) for j, (kx, ky) in enumerate(chips)]

    return dict(ins=list(parts), sems=(n, 3), copies=copies,
                out_shape=[jax.ShapeDtypeStruct((3,) + shard_half_shape(p, lay), p.dtype)
                           for p, lay in zip(parts, layouts)])


def _sum_shards(name, p, r, layout, kc_idx, dst, lyr, n_lyr):
    rows, W = r.shape[2], r.shape[3]
    tr = _pick(rows, (256, 128, 64, 32, 16))
    nb = rows // tr

    def body(kc_ref, p_ref, r_ref, *rest):
        acc = p_ref[...].astype(F32)
        for j in range(3):
            acc = acc + r_ref[j].astype(F32)
        rest[-1][...] = acc

    if layout == "row":
        p_spec = pl.BlockSpec((None, None, tr, W), lambda a, i, kc: (0, kc[0], i, 0))
    else:
        p_spec = pl.BlockSpec((None, tr, W), lambda a, i, kc: (0, i, kc[0]))
    in_specs = [p_spec, pl.BlockSpec((3, None, tr, W), lambda a, i, kc: (0, 0, i, 0))]
    args = [kc_idx, p, r]
    if dst is not None:
        in_specs.append(ANY)
        args.append(dst)
    return pl.pallas_call(
        body, name=name,
        grid_spec=pltpu.PrefetchScalarGridSpec(
            num_scalar_prefetch=1, grid=(1, nb), in_specs=in_specs,
            out_specs=pl.BlockSpec((None, tr, W), lambda a, i, kc: (lyr, kc[1] * nb + i, 0))),
        out_shape=jax.ShapeDtypeStruct((n_lyr, 2 * rows, W), F32),
        input_output_aliases={3: 0} if dst is not None else {},
        compiler_params=_params(),
    )(*args)


def _join_halves(shards):
    n = len(shards)

    def body(*refs):
        outs = refs[n:2 * n]
        send_sems, recv_sems = refs[2 * n:]
        x, y, c, _ = _place()
        cps = []
        for i in range(n):
            mine = _half(outs[i], 1, c, outs[i].shape[1] // 2)
            cp = pltpu.make_async_remote_copy(
                src_ref=mine, dst_ref=mine, send_sem=send_sems.at[i], recv_sem=recv_sems.at[i],
                device_id=(x, y, 1 - c), device_id_type=MESH)
            cp.start()
            cps.append(cp)
        for cp in cps:
            cp.wait()

    return pl.pallas_call(
        body, name="join_halves", in_specs=[ANY] * n, out_specs=[ANY] * n,
        out_shape=[jax.ShapeDtypeStruct(s.shape, s.dtype) for s in shards],
        input_output_aliases={i: i for i in range(n)},
        scratch_shapes=[pltpu.SemaphoreType.DMA((n,)), pltpu.SemaphoreType.DMA((n,))],
        compiler_params=_params(),
    )(*shards)


def _ada_fwd(c_all, ada_w, ada_b):
    L, D, w = ada_w.shape
    tn = _pick(w, (512, 256, 128))

    def body(c_ref, w_ref, b_ref, o_ref, a_ref):
        c = c_ref[...]
        act = (c * _sigmoid(c)).astype(MXU_DTYPE)
        a_ref[...] = act
        o_ref[...] = jnp.dot(act, w_ref[...].astype(MXU_DTYPE), preferred_element_type=F32) + b_ref[...]

    return pl.pallas_call(
        body, name="ada_fwd", grid=(L, w // tn),
        in_specs=[pl.BlockSpec((16, D), lambda l, j: (0, 0)), pl.BlockSpec((None, D, tn), lambda l, j: (l, 0, j)),
                  pl.BlockSpec((None, 1, tn), lambda l, j: (l, 0, j))],
        out_specs=[pl.BlockSpec((None, 16, tn), lambda l, j: (l, 0, j)), pl.BlockSpec((16, D), lambda l, j: (0, 0))],
        out_shape=[jax.ShapeDtypeStruct((L, 16, w), F32), jax.ShapeDtypeStruct((16, D), MXU_DTYPE)],
        compiler_params=_params(),
    )(c_all, ada_w, ada_b)


def _sum_devices(name, parts):
    n, R, W = parts.shape
    tw = _pick(W, (2048, 1024, 512, 256, 128))

    def body(p_ref, o_ref):
        acc = p_ref[0]
        for d in range(1, n):
            acc = acc + p_ref[d]
        o_ref[...] = acc

    return pl.pallas_call(
        body, name=name, grid=(W // tw,), in_specs=[pl.BlockSpec((n, R, tw), lambda j: (0, 0, j))],
        out_specs=pl.BlockSpec((R, tw), lambda j: (0, j)), out_shape=jax.ShapeDtypeStruct((R, W), F32),
        compiler_params=_params(),
    )(parts)


def _adamw_math(w, g, m, v):
    m2 = ADAM_B1 * m + (1.0 - ADAM_B1) * g
    v2 = ADAM_B2 * v + (1.0 - ADAM_B2) * (g * g)
    m_hat = m2 / (1.0 - ADAM_B1 ** ADAM_STEP)
    v_hat = v2 / (1.0 - ADAM_B2 ** ADAM_STEP)
    return -ADAM_LR * (m_hat / (jnp.sqrt(v_hat) + ADAM_EPS) + ADAM_WD * w), m2, v2


def _adamw(name, w, g, m, v):
    W = w.shape[1]
    tr = 256 if W <= 1024 else (128 if W <= 2048 else 64)
    return _rowwise(name, _adamw_math, [w, g, m, v], [], [(W, F32)] * 3, tr=tr)


def _adamw_ada(c_act, dmod, w, m, v, carry=None):
    L, D, wd = w.shape
    tr = LANE

    def body(*refs):
        refs, c_in, c_out, c_sems = _carry_split(refs, 5, 4, 0, carry)
        c_ref, d_ref, w_ref, m_ref, v_ref, g_ref, dl_ref, m2_ref, v2_ref = refs
        pl_, pi = pl.program_id(0), pl.program_id(1)
        _carry_run(carry, c_in, c_out, c_sems, jnp.logical_and(pl_ == 0, pi == 0), True)
        g = _dot_tn(c_ref[...], d_ref[...].astype(MXU_DTYPE))
        g_ref[...] = g
        dl_ref[...], m2_ref[...], v2_ref[...] = _adamw_math(w_ref[...], g, m_ref[...], v_ref[...])
        _carry_run(carry, c_in, c_out, c_sems, jnp.logical_and(pl_ == L - 1, pi == D // tr - 1), False)

    big = pl.BlockSpec((None, tr, wd), lambda l, i: (l, i, 0))
    c_is, c_os, c_shape, c_sems, c_alias = _carry_call_args(carry, 5, 4)
    res = pl.pallas_call(
        body, name="adamw_ada_w", grid=(L, D // tr),
        in_specs=[pl.BlockSpec((16, tr), lambda l, i: (0, i)), pl.BlockSpec((None, 16, wd), lambda l, i: (l, 0, 0)),
                  big, big, big] + c_is,
        out_specs=[big] * 4 + c_os, out_shape=[jax.ShapeDtypeStruct(w.shape, F32)] * 4 + c_shape,
        input_output_aliases=c_alias, scratch_shapes=c_sems,
        compiler_params=_params(),
    )(c_act, dmod, w, m, v, *(carry["ins"] if carry else ()))
    return res[:4], list(res[4:])


def _flat(a):
    return a.reshape(-1, a.shape[-1])


BIG = ("ffn_w1", "ffn_w3", "ffn_w2", "mla_w_down", "mla_w_uq", "mla_w_uk", "mla_w_uv", "mla_w_o", "fox_w_in",
       "fox_w_o")
LAYOUT = dict(ffn_w1="col", ffn_w3="col", ffn_w2="row", mla_w_down="row", mla_w_uq="col", mla_w_uk="col",
              mla_w_uv="col", mla_w_o="row", fox_w_in="row", fox_w_o="row")
FFN = ("ffn_w1", "ffn_w3", "ffn_w2")
MLA = ("mla_w_down", "mla_w_uq", "mla_w_uk", "mla_w_uv", "mla_w_o")
FOX = ("fox_w_in", "fox_w_o")
SMALL = ("ln1_g", "ln1_b", "ln2_g", "ln2_b", "mla_q_norm", "mla_kv_norm", "fox_b_f")
WEIGHTS = ("ada_w", "ada_b", "ln1_g", "ln1_b", "ln2_g", "ln2_b", "ffn_w1", "ffn_w3", "ffn_w2", "mla_w_down",
           "mla_q_norm", "mla_w_uq", "mla_kv_norm", "mla_w_uk", "mla_w_uv", "mla_w_o", "fox_w_in", "fox_b_f",
           "fox_w_o")


def _uq_perm(H):
    d = HEAD_DIM + ROPE_DIM
    nope = (np.arange(H)[:, None] * d + np.arange(HEAD_DIM)[None, :]).reshape(-1)
    pe = (np.arange(H)[:, None] * d + HEAD_DIM + np.arange(ROPE_DIM)[None, :]).reshape(-1)
    return np.concatenate([nope, pe])


def kernel(x, c, positions, ada_w, ada_b, ln1_g, ln1_b, ln2_g, ln2_b, ffn_w1, ffn_w3, ffn_w2, mla_w_down, mla_q_norm, mla_w_uq, mla_kv_norm, mla_w_uk, mla_w_uv, mla_w_o, fox_w_in, fox_b_f, fox_w_o, loss_target, m_ada_w, m_ada_b, m_ln1_g, m_ln1_b, m_ln2_g, m_ln2_b, m_ffn_w1, m_ffn_w3, m_ffn_w2, m_mla_w_down, m_mla_q_norm, m_mla_w_uq, m_mla_kv_norm, m_mla_w_uk, m_mla_w_uv, m_mla_w_o, m_fox_w_in, m_fox_b_f, m_fox_w_o, v_ada_w, v_ada_b, v_ln1_g, v_ln1_b, v_ln2_g, v_ln2_b, v_ffn_w1, v_ffn_w3, v_ffn_w2, v_mla_w_down, v_mla_q_norm, v_mla_w_uq, v_mla_kv_norm, v_mla_w_uk, v_mla_w_uv, v_mla_w_o, v_fox_w_in, v_fox_b_f, v_fox_w_o):
    W = dict(ada_w=ada_w, ada_b=ada_b, ln1_g=ln1_g, ln1_b=ln1_b, ln2_g=ln2_g, ln2_b=ln2_b, ffn_w1=ffn_w1,
             ffn_w3=ffn_w3, ffn_w2=ffn_w2, mla_w_down=mla_w_down, mla_q_norm=mla_q_norm, mla_w_uq=mla_w_uq,
             mla_kv_norm=mla_kv_norm, mla_w_uk=mla_w_uk, mla_w_uv=mla_w_uv, mla_w_o=mla_w_o, fox_w_in=fox_w_in,
             fox_b_f=fox_b_f, fox_w_o=fox_w_o)
    Mo = dict(ada_w=m_ada_w, ada_b=m_ada_b, ln1_g=m_ln1_g, ln1_b=m_ln1_b, ln2_g=m_ln2_g, ln2_b=m_ln2_b,
              ffn_w1=m_ffn_w1, ffn_w3=m_ffn_w3, ffn_w2=m_ffn_w2, mla_w_down=m_mla_w_down, mla_q_norm=m_mla_q_norm,
              mla_w_uq=m_mla_w_uq, mla_kv_norm=m_mla_kv_norm, mla_w_uk=m_mla_w_uk, mla_w_uv=m_mla_w_uv,
              mla_w_o=m_mla_w_o, fox_w_in=m_fox_w_in, fox_b_f=m_fox_b_f, fox_w_o=m_fox_w_o)
    Vo = dict(ada_w=v_ada_w, ada_b=v_ada_b, ln1_g=v_ln1_g, ln1_b=v_ln1_b, ln2_g=v_ln2_g, ln2_b=v_ln2_b,
              ffn_w1=v_ffn_w1, ffn_w3=v_ffn_w3, ffn_w2=v_ffn_w2, mla_w_down=v_mla_w_down, mla_q_norm=v_mla_q_norm,
              mla_w_uq=v_mla_w_uq, mla_kv_norm=v_mla_kv_norm, mla_w_uk=v_mla_w_uk, mla_w_uv=v_mla_w_uv,
              mla_w_o=v_mla_w_o, fox_w_in=v_fox_w_in, fox_b_f=v_fox_b_f, fox_w_o=v_fox_w_o)

    S, D = x.shape[1], x.shape[2]
    L = ada_w.shape[0]
    alpha = float((2 * L) ** 0.25)
    H_mla = mla_w_uk.shape[2] * N_CHIPS // HEAD_DIM
    H_fox = D // HEAD_DIM
    xi, yi, ci = lax.axis_index("x"), lax.axis_index("y"), lax.axis_index("c")
    chip = 2 * xi + yi
    dev = 2 * chip + ci
    c_idx = jnp.reshape(ci, (1,)).astype(jnp.int32)
    x0, tgt = x[0], loss_target[0]
    pos = positions.reshape(S, 1)

    c_all = _ag_small("gather_c", jnp.pad(c, ((0, 7), (0, 0)))).reshape(N_DEV, 8, D)[:, 0]
    w_ada = ada_w.shape[2]
    ada_b_sh = lax.dynamic_slice_in_dim(ada_b, chip * w_ada, w_ada, axis=1).reshape(L, 1, w_ada)
    mod_sh, c_act = _ada_fwd(jnp.pad(c_all, ((0, 8), (0, 0))), ada_w, ada_b_sh)
    mod_all = _ag_small("gather_mod", mod_sh.transpose(1, 0, 2).reshape(16, L * w_ada))
    mod_all = mod_all.reshape(N_CHIPS, 2, 16, L, w_ada)[:, 0]
    mod = lax.dynamic_index_in_dim(mod_all, dev, axis=1, keepdims=False)
    mod = mod.transpose(1, 0, 2).reshape(L, 6, 1, D)

    kc_idx = jnp.stack([chip, ci]).astype(jnp.int32)
    raw = {n: [_cast_full("cast_" + n, W[n], LAYOUT[n], kc_idx, l) for l in range(W[n].shape[0])] for n in BIG}
    full = {n: [None] * W[n].shape[0] for n in BIG}
    perm = _uq_perm(H_mla)
    n_in = fox_w_in.shape[2] * N_CHIPS

    def group(i):
        return FFN + (MLA if i % 2 == 0 else FOX)

    def ride_of(names, i):
        return dict(bufs=[raw[n][i if n in FFN else i // 2] for n in names], lays=[LAYOUT[n] for n in names])

    def land(names, i, bufs):
        for n, f in zip(names, bufs):
            if n == "fox_w_in":
                fw = f.reshape(N_CHIPS, D, -1).transpose(1, 0, 2).reshape(D, n_in)
                f = jnp.pad(fw, ((0, 0), (0, 3 * D + LANE - n_in)))[None]
            elif n == "mla_w_uq":
                f = f[:, :, perm]
            elif LAYOUT[n] == "row":
                f = f.reshape(1, f.shape[1] * f.shape[2], f.shape[3])
            full[n][i if n in FFN else i // 2] = f

    r0 = ride_of(group(0)[3:], 0)
    land(group(0)[3:], 0, _gather_weights(r0["bufs"], r0["lays"], [0] * len(r0["bufs"])))

    rc = tuple(_rope_tables(pos))

    def mixer_w(i):
        j = i // 2
        if i % 2 == 0:
            return dict(w_down=(full["mla_w_down"][j], 0), q_norm=mla_q_norm[j:j + 1], w_uq=(full["mla_w_uq"][j], 0),
                        kv_norm=mla_kv_norm[j:j + 1], w_uk=(full["mla_w_uk"][j], 0), w_uv=(full["mla_w_uv"][j], 0),
                        w_o=(full["mla_w_o"][j], 0))
        return dict(w_in=(full["fox_w_in"][j], 0), b_f=jnp.pad(fox_b_f[j:j + 1], ((0, 0), (0, LANE - H_fox))),
                    w_o=(full["fox_w_o"][j], 0))

    saved = []
    xc = x0
    h = _modulate(x0, mod[0, 1], mod[0, 0])
    for i in range(L):
        mw = mixer_w(i)
        nxt = i + 1 < L
        mix_n = group(i + 1)[3:]
        if i % 2 == 0:
            y1, ms, got = _mla_fwd(h, mw, rc, ride_of(FFN, i))
        else:
            y1, ms, got = _fox_fwd(h, mw, ride_of(FFN, i))
        land(FFN, i, got)
        z1, x1, h2 = _resid_ln_mod(xc, y1, mod[i, 2], ln1_g[i:i + 1], ln1_b[i:i + 1], mod[i, 4], mod[i, 3], alpha)
        y2, fs, got = _ffn_fwd(h2, (full["ffn_w1"][i], 0), (full["ffn_w3"][i], 0), (full["ffn_w2"][i], 0),
                               ride_of(mix_n, i + 1) if nxt else None)
        land(mix_n, i + 1, got)
        rec = dict(h1=h, ms=ms, y1=y1, z1=z1, h2=h2, fs=fs, y2=y2)
        if i + 1 < L:
            z2, xc, h = _resid_ln_mod(x1, y2, mod[i, 5], ln2_g[i:i + 1], ln2_b[i:i + 1], mod[i + 1, 1],
                                      mod[i + 1, 0], alpha)
            rec["z2"] = z2
        else:
            dx_res, dy, loss_v, dlg, dlb, dgate = _final_ln_loss(x1, y2, tgt, mod[i, 5], ln2_g[i:i + 1],
                                                                 ln2_b[i:i + 1], alpha)
        saved.append(rec)
    loss = lax.psum(loss_v[0, 0] * (0.5 / D), ("x", "y", "c"))

    G = {n: [None] * W[n].shape[0] for n in SMALL}
    dmod = [[None] * 6 for _ in range(L)]
    red = {n: None for n in BIG}
    inv_perm = np.argsort(perm)

    def rs_view(n, g):
        if n == "mla_w_uq":
            g = g[:, inv_perm]
        if n == "fox_w_in":
            g = g[:, :n_in].reshape(D, N_CHIPS, n_in // N_CHIPS).transpose(1, 0, 2)
        elif LAYOUT[n] == "row":
            g = g.reshape(N_CHIPS, g.shape[0] // N_CHIPS, g.shape[1])
        return g[None]

    def add_halves(names, gs, recv):
        return [_add_half("add_half_" + n, g, r, LAYOUT[n], c_idx) for n, g, r in zip(names, gs, recv)]

    def half_sums(tag, names, grads):
        gs = [rs_view(n, g) for n, g in zip(names, grads)]
        return add_halves(names, gs, _run_carry("swap_" + tag, _swap_carry(gs, [LAYOUT[n] for n in names])))

    def finish(names, parts, recv2, lyr):
        for n, p, r in zip(names, parts, recv2):
            red[n] = _sum_shards("sum_shards_" + n, p, r, LAYOUT[n], kc_idx, red[n], lyr, W[n].shape[0])

    queue = []

    def carry_of(item):
        return _exchange_carry(item[1], [LAYOUT[n] for n in item[0]]) if item else None

    for i in reversed(range(L)):
        rec = saved[i]
        mw = mixer_w(i)
        j = i // 2
        G["ln2_g"][i], G["ln2_b"][i], dmod[i][5] = dlg, dlb, dgate
        ride_du = queue.pop(0) if queue else None
        ride_dh = queue.pop(0) if queue else None
        views = []

        def swap_of(grads):
            views.extend(rs_view(n, g) for n, g in zip(FFN, grads))
            return _swap_carry(views, [LAYOUT[n] for n in FFN])

        dh2, dw1, dw3, dw2, got_du, got_dh, got_swap = _ffn_bwd(
            dy, rec["h2"], rec["fs"], (full["ffn_w1"][i], 0), (full["ffn_w3"][i], 0), (full["ffn_w2"][i], 0),
            carry_of(ride_du), carry_of(ride_dh), swap_of)
        for item, got in ((ride_du, got_du), (ride_dh, got_dh)):
            if item:
                finish(item[0], item[1], got, item[2])
        ffn_parts = add_halves(FFN, views, got_swap)
        dx_res, dy, dmod[i][4], dmod[i][3], G["ln1_g"][i], G["ln1_b"][i], dmod[i][2] = _bwd_boundary(
            dx_res, dh2, rec["z1"], rec["y1"], mod[i, 4], mod[i, 2], ln1_g[i:i + 1], ln1_b[i:i + 1], alpha)
        ride_at = (FFN[:2], ffn_parts[:2], i)
        if i % 2 == 0:
            dh1, gm, recv2 = _mla_bwd(dy, rec["h1"], rec["ms"], mw, rc, carry_of(ride_at))
            names, pre = MLA, "mla_"
            G["mla_q_norm"][j], G["mla_kv_norm"][j] = gm["q_norm"], gm["kv_norm"]
        else:
            dh1, gm, recv2 = _fox_bwd(dy, rec["h1"], rec["ms"], mw, carry_of(ride_at))
            names, pre = FOX, "fox_"
            G["fox_b_f"][j] = gm["b_f"]
        finish(ride_at[0], ride_at[1], recv2, i)
        queue.append((FFN[2:], ffn_parts[2:], i))
        queue.append((names, half_sums(pre[:-1], names, [gm[n[len(pre):]] for n in names]), j))
        if i > 0:
            p = saved[i - 1]
            dx_res, dy, dmod[i][1], dmod[i][0], dlg, dlb, dgate = _bwd_boundary(
                dx_res, dh1, p["z2"], p["y2"], mod[i, 1], mod[i - 1, 5], ln2_g[i - 1:i], ln2_b[i - 1:i], alpha)
        else:
            grad_x, dmod[i][1], dmod[i][0] = _first_bwd(dx_res, dh1, x0, mod[i, 1])

    small = jnp.concatenate([jnp.concatenate([g.reshape(-1) for g in G[n]]) for n in SMALL])
    dmod_v = jnp.concatenate([jnp.concatenate([d.reshape(-1) for d in row]) for row in dmod])
    n_small, n_dmod = small.shape[0], dmod_v.shape[0]
    wblk = -(-(n_small + n_dmod) // (8 * LANE)) * LANE
    blk = jnp.pad(jnp.concatenate([dmod_v, small]), (0, 8 * wblk - n_small - n_dmod)).reshape(8, wblk)
    parts = _ag_small("gather_small", blk).reshape(N_DEV, 8, wblk)
    tot = _sum_devices("sum_small", parts).reshape(-1)
    g_ada_b = tot[:n_dmod].reshape(L, 6 * D)
    off = n_dmod
    Gs = {}
    for n in SMALL:
        Gs[n] = tot[off:off + W[n].size].reshape(W[n].shape)
        off += W[n].size
    dmod_all = parts.reshape(N_DEV, 8 * wblk)[:, :n_dmod].reshape(N_DEV, L, N_CHIPS, w_ada)
    dmod_sh = lax.dynamic_index_in_dim(dmod_all, chip, axis=2, keepdims=False)
    dmod_sh = jnp.pad(dmod_sh, ((0, 8), (0, 0), (0, 0)))
    dmod_sh = dmod_sh.transpose(1, 0, 2)

    last = (sum((q[0] for q in queue), ()), sum((q[1] for q in queue), []))
    (g_ada_w, d_ada_w, m_ada_w2, v_ada_w2), got = _adamw_ada(
        c_act, dmod_sh, ada_w, m_ada_w, v_ada_w, _exchange_carry(last[1], [LAYOUT[n] for n in last[0]]))
    for q in queue:
        finish(q[0], q[1], got[:len(q[0])], q[2])
        got = got[len(q[0]):]
    Gb = dict(zip(BIG, _join_halves([red[n] for n in BIG])))
    grads = dict(Gb)
    grads.update(Gs)
    grads["ada_b"] = g_ada_b
    delta, new_m, new_v = {}, {}, {}
    grads["ada_w"], delta["ada_w"], new_m["ada_w"], new_v["ada_w"] = g_ada_w, d_ada_w, m_ada_w2, v_ada_w2
    for n in WEIGHTS:
        if n in SMALL or n in ("ada_b", "ada_w"):
            continue
        shp = W[n].shape
        delta[n], new_m[n], new_v[n] = [r.reshape(shp) for r in _adamw(
            "adamw_" + n, _flat(W[n]), _flat(grads[n]), _flat(Mo[n]), _flat(Vo[n]))]
    names_s = SMALL + ("ada_b",)
    cat = lambda d: jnp.concatenate([d[n].reshape(-1) for n in names_s])
    n_s = sum(W[n].size for n in names_s)
    ws = -(-n_s // (8 * LANE)) * LANE
    pk = lambda d: jnp.pad(cat(d), (0, 8 * ws - n_s)).reshape(8, ws)
    ds, ms_, vs = _adamw("adamw_small", pk(W), pk(grads), pk(Mo), pk(Vo))
    off = 0
    for n in names_s:
        sz, shp = W[n].size, W[n].shape
        delta[n] = ds.reshape(-1)[off:off + sz].reshape(shp)
        new_m[n] = ms_.reshape(-1)[off:off + sz].reshape(shp)
        new_v[n] = vs.reshape(-1)[off:off + sz].reshape(shp)
        off += sz

    return (loss, grad_x[None], *[grads[n].reshape(W[n].shape) for n in WEIGHTS], *[delta[n] for n in WEIGHTS],
            *[new_m[n] for n in WEIGHTS], *[new_v[n] for n in WEIGHTS])
```

```python
import functools

import numpy as np
import jax
import jax.numpy as jnp
from jax import lax
from jax.experimental import pallas as pl
from jax.experimental.pallas import tpu as pltpu

F32 = jnp.float32
BF16 = jnp.bfloat16
MXU_DTYPE = jnp.bfloat16
WIRE_DTYPE = jnp.bfloat16

HEAD_DIM = 128
ROPE_DIM = 64
CHUNK = 64
ROPE_THETA = 10000.0
LN_EPS = 1e-5
RMS_EPS = 1e-6
ADAM_LR, ADAM_B1, ADAM_B2, ADAM_EPS, ADAM_WD, ADAM_STEP = 0.001, 0.9, 0.999, 1e-08, 0.01, 10

N_CHIPS = 4
N_DEV = 8
LANE = 128
VMEM_LIMIT = 56 * 1024 * 1024
MESH = pl.DeviceIdType.MESH
ANY = pl.BlockSpec(memory_space=pl.ANY)
NEG = -1e30


def _params(**kw):
    return pltpu.CompilerParams(vmem_limit_bytes=VMEM_LIMIT, **kw)


def _pick(n, cands):
    for c in cands:
        if n % c == 0:
            return c
    return n


def _sigmoid(x):
    return 1.0 / (1.0 + jnp.exp(-x))


def _mm(name, terms, M, N, out_dtypes, epilogue=None, extras=(), row_extras=(), tm=512, tn=512, carry=None):
    tm = _pick(M, (tm, 256, 128))
    tn = _pick(N, (tn, 896, 768, 640, 384, 256, 128))
    extras = tuple(extras) + tuple(row_extras)
    n_row = len(row_extras)
    n_terms, n_ex, n_out = len(terms), len(extras), len(out_dtypes)
    n_acc = 1 + max(t[4] for t in terms)
    flags = [(t[2], t[3], t[4]) for t in terms]

    gi, gj = M // tm, N // tn

    def body(*refs):
        refs, c_in, c_out, c_sems = _carry_split(refs, 2 * n_terms + n_ex, n_out, 0, carry)
        pi, pj = pl.program_id(0), pl.program_id(1)
        _carry_run(carry, c_in, c_out, c_sems, jnp.logical_and(pi == 0, pj == 0), True)
        accs = [None] * n_acc
        for k, (ta, tb, ai) in enumerate(flags):
            a = refs[2 * k][...].astype(MXU_DTYPE)
            b = refs[2 * k + 1][...].astype(MXU_DTYPE)
            dn = (((0 if ta else 1,), (1 if tb else 0,)), ((), ()))
            r = lax.dot_general(a, b, dn, preferred_element_type=F32)
            accs[ai] = r if accs[ai] is None else accs[ai] + r
        ex = [refs[2 * n_terms + k][...] for k in range(n_ex)]
        outs = epilogue(accs, *ex) if epilogue is not None else (accs[0],)
        for k in range(n_out):
            o_ref = refs[2 * n_terms + n_ex + k]
            o_ref[...] = outs[k].astype(o_ref.dtype)
        _carry_run(carry, c_in, c_out, c_sems, jnp.logical_and(pi == gi - 1, pj == gj - 1), False)

    in_specs, args = [], []
    for (a, b, ta, tb, _, bcol) in terms:
        K = a.shape[0] if ta else a.shape[1]
        in_specs.append(pl.BlockSpec((K, tm), lambda i, j: (0, i)) if ta
                        else pl.BlockSpec((tm, K), lambda i, j: (i, 0)))
        if isinstance(b, tuple):
            b, lyr = b
            in_specs.append(pl.BlockSpec((None, tn, K), lambda i, j, o=bcol, l=lyr: (l, j + o, 0)) if tb
                            else pl.BlockSpec((None, K, tn), lambda i, j, o=bcol, l=lyr: (l, 0, j + o)))
        else:
            in_specs.append(pl.BlockSpec((tn, K), lambda i, j, o=bcol: (j + o, 0)) if tb
                            else pl.BlockSpec((K, tn), lambda i, j, o=bcol: (0, j + o)))
        args += [a, b]
    for k, e in enumerate(extras):
        in_specs.append(pl.BlockSpec((tm, tn), (lambda i, j: (i, 0)) if k >= n_ex - n_row else (lambda i, j: (i, j))))
        args.append(e)
    c_is, c_os, c_shape, c_sems, c_alias = _carry_call_args(carry, len(args), n_out)
    outs = pl.pallas_call(
        body, name=name, grid=(gi, gj), in_specs=in_specs + c_is,
        out_specs=[pl.BlockSpec((tm, tn), lambda i, j: (i, j)) for _ in out_dtypes] + c_os,
        out_shape=[jax.ShapeDtypeStruct((M, N), d) for d in out_dtypes] + c_shape,
        input_output_aliases=c_alias, scratch_shapes=c_sems,
        compiler_params=_params(),
    )(*args, *(carry["ins"] if carry else ()))
    return (outs[:n_out], list(outs[n_out:])) if carry else outs


def _wdim(b, axis):
    return b[0].shape[1 + axis] if isinstance(b, tuple) else b.shape[axis]


def _mm1(name, a, b, ta=False, tb=False, out_dtype=F32, bcol=0, N=None, **kw):
    M = a.shape[1] if ta else a.shape[0]
    if N is None:
        N = _wdim(b, 0 if tb else 1)
    res = _mm(name, [(a, b, ta, tb, 0, bcol)], M, N, [out_dtype], **kw)
    return (res[0][0], res[1]) if kw.get("carry") else res[0]


def _rowwise(name, fn, tiled, vecs, outs, reds=(), tr=128):
    R = tiled[0].shape[0]
    tr = _pick(R, (tr, 64, 32, 16, 8))
    nt, nv, no, nr = len(tiled), len(vecs), len(outs), len(reds)

    def body(*refs):
        vals = [r[...] for r in refs[:nt + nv]]
        res = fn(*vals)
        for k in range(no):
            o_ref = refs[nt + nv + k]
            o_ref[...] = res[k].astype(o_ref.dtype)
        if nr:
            first = pl.program_id(0) == 0
            for k in range(nr):
                r_ref = refs[nt + nv + no + k]

                @pl.when(first)
                def _(r_ref=r_ref, v=res[no + k]):
                    r_ref[...] = v

                @pl.when(jnp.logical_not(first))
                def _(r_ref=r_ref, v=res[no + k]):
                    r_ref[...] += v

    in_specs = [pl.BlockSpec((tr, t.shape[1]), lambda i: (i, 0)) for t in tiled]
    in_specs += [pl.BlockSpec(v.shape, lambda i, n=v.ndim: (0,) * n) for v in vecs]
    out_specs = [pl.BlockSpec((tr, w), lambda i: (i, 0)) for (w, _) in outs]
    out_specs += [pl.BlockSpec((1, w), lambda i: (0, 0)) for w in reds]
    out_shape = [jax.ShapeDtypeStruct((R, w), d) for (w, d) in outs]
    out_shape += [jax.ShapeDtypeStruct((1, w), F32) for w in reds]
    return pl.pallas_call(
        body, name=name, grid=(R // tr,), in_specs=in_specs, out_specs=out_specs, out_shape=out_shape,
        compiler_params=_params(),
    )(*tiled, *vecs)


def _colsum(v):
    return jnp.sum(v, axis=0, keepdims=True)


def _ln_stats(z):
    mu = jnp.mean(z, axis=-1, keepdims=True)
    zc = z - mu
    var = jnp.mean(zc * zc, axis=-1, keepdims=True)
    rstd = lax.rsqrt(var + LN_EPS)
    return zc * rstd, rstd


def _ln_bwd(dout, xhat, rstd, lg):
    dxh = dout * lg
    m1 = jnp.mean(dxh, axis=-1, keepdims=True)
    m2 = jnp.mean(dxh * xhat, axis=-1, keepdims=True)
    return rstd * (dxh - m1 - xhat * m2)


def _modulate(x, sc, sh):
    D = x.shape[1]
    return _rowwise("modulate", lambda x, sc, sh: ((x * (1.0 + sc) + sh),), [x], [sc, sh], [(D, MXU_DTYPE)])[0]


def _resid_ln_mod(x, y, g, lg, lb, sc_n, sh_n, alpha):
    D = x.shape[1]

    def fn(x, y, g, lg, lb, sc, sh):
        z = alpha * x + (1.0 + g) * y
        xhat, _ = _ln_stats(z)
        xo = xhat * lg + lb
        return z, xo, xo * (1.0 + sc) + sh

    return _rowwise("resid_ln_mod", fn, [x, y], [g, lg, lb, sc_n, sh_n], [(D, F32), (D, F32), (D, MXU_DTYPE)])


def _final_ln_loss(x, y, tgt, g, lg, lb, alpha):
    D = x.shape[1]

    def fn(x, y, t, g, lg, lb):
        z = alpha * x + (1.0 + g) * y
        xhat, rstd = _ln_stats(z)
        out = xhat * lg + lb
        err = out - t
        loss = jnp.sum(jnp.sum(err * err, axis=-1, keepdims=True), axis=0, keepdims=True)
        dout = err * (1.0 / D)
        dz = _ln_bwd(dout, xhat, rstd, lg)
        return (alpha * dz, (1.0 + g) * dz, jnp.broadcast_to(loss, (1, LANE)),
                _colsum(dout * xhat), _colsum(dout), _colsum(dz * y))

    return _rowwise("final_ln_loss", fn, [x, y, tgt], [g, lg, lb], [(D, F32), (D, MXU_DTYPE)], [LANE, D, D, D])


def _bwd_boundary(dx_res, dh, z_p, y_p, sc, g_p, lg_p, lb_p, alpha):
    D = dh.shape[1]

    def fn(dxr, dh, z, y, sc, g, lg, lb):
        xhat, rstd = _ln_stats(z)
        x_in = xhat * lg + lb
        dx = dxr + dh * (1.0 + sc)
        dz = _ln_bwd(dx, xhat, rstd, lg)
        return (alpha * dz, (1.0 + g) * dz,
                _colsum(dh * x_in), _colsum(dh), _colsum(dx * xhat), _colsum(dx), _colsum(dz * y))

    return _rowwise("bwd_boundary", fn, [dx_res, dh, z_p, y_p], [sc, g_p, lg_p, lb_p],
                    [(D, F32), (D, MXU_DTYPE)], [D, D, D, D, D])


def _first_bwd(dx_res, dh, x, sc):
    D = dh.shape[1]

    def fn(dxr, dh, x, sc):
        return dxr + dh * (1.0 + sc), _colsum(dh * x), _colsum(dh)

    return _rowwise("first_bwd", fn, [dx_res, dh, x], [sc], [(D, F32)], [D, D])


def _ride(ride, phase, bufs=None):
    if ride is None:
        return None
    bufs = ride["bufs"] if bufs is None else bufs
    return _gather_carry(bufs, ride["lays"], [0] * len(bufs), phase)


def _ffn_fwd(h, w1, w3, w2, ride=None):
    S, F = h.shape[0], _wdim(w1, 1)

    def epi(accs):
        a, b = accs
        return a, b, a * _sigmoid(a) * b

    res = _mm("ffn_up", [(h, w1, False, False, 0, 0), (h, w3, False, False, 1, 0)], S, F,
              [MXU_DTYPE, MXU_DTYPE, MXU_DTYPE], epilogue=epi, carry=_ride(ride, "ici"))
    (a, b, u), bufs = res if ride else (res, [])
    res = _mm1("ffn_down", u, w2, carry=_ride(ride, "d2d", bufs))
    y, bufs = res if ride else (res, [])
    return y, (a, b, u), bufs


def _ffn_bwd(dy, h, saved, w1, w3, w2, carry_du=None, carry_dh=None, swap_of=None):
    a, b, u = saved
    S, F = a.shape
    D = h.shape[1]

    def epi(accs, a, b):
        du = accs[0]
        a = a.astype(F32)
        b = b.astype(F32)
        sg = _sigmoid(a)
        return du * b * (sg * (1.0 + a * (1.0 - sg))), du * (a * sg)

    res = _mm("ffn_du", [(dy, w2, False, True, 0, 0)], S, F, [MXU_DTYPE, MXU_DTYPE], epilogue=epi, extras=(a, b),
              carry=carry_du)
    (da, db), got_du = res if carry_du else (res, [])
    dw2 = _mm1("ffn_dw2", u, dy, ta=True, out_dtype=WIRE_DTYPE)
    dw1, dw3 = _mm("ffn_dw13", [(h, da, True, False, 0, 0), (h, db, True, False, 1, 0)], D, F,
                   [WIRE_DTYPE, WIRE_DTYPE], epilogue=lambda accs: (accs[0], accs[1]))
    carry = _merge_carries(carry_dh, swap_of((dw1, dw3, dw2)) if swap_of else None)
    res = _mm("ffn_dh", [(da, w1, False, True, 0, 0), (db, w3, False, True, 0, 0)], S, D, [F32], tn=256, carry=carry)
    (dh,), got = res if carry else (res, [])
    n_dh = len(carry_dh["out_shape"]) if carry_dh else 0
    return dh, dw1, dw3, dw2, got_du, got[:n_dh], got[n_dh:]


def _rope_tables(pos):
    j = np.arange(LANE)
    invf = ROPE_THETA ** (-jnp.arange(0, ROPE_DIM, 2, dtype=F32) / ROPE_DIM)
    invf = invf[(j % ROPE_DIM) // 2].reshape(1, LANE)
    sgn = jnp.asarray(np.where(j % 2 == 0, -1.0, 1.0).reshape(1, LANE), F32)

    def fn(pos, invf, sgn):
        ang = pos.astype(F32) * invf
        return jnp.cos(ang), jnp.sin(ang) * sgn

    return _rowwise("rope_tables", fn, [pos], [invf, sgn], [(LANE, F32), (LANE, F32)], tr=256)


def _pair_swap(x):
    w = x.shape[1]
    even = (lax.broadcasted_iota(jnp.int32, x.shape, 1) % 2) == 0
    return jnp.where(even, pltpu.roll(x, w - 1, 1), pltpu.roll(x, 1, 1))


def _rope_fwd(x, c, s):
    return x * c + _pair_swap(x) * s


def _rope_bwd(d, c, s):
    return d * c + _pair_swap(d * s)


ATT_T = 512


def _dot_nt(a, b):
    return lax.dot_general(a, b, (((1,), (1,)), ((), ())), preferred_element_type=F32)


def _dot_tn(a, b):
    return lax.dot_general(a, b, (((0,), (0,)), ((), ())), preferred_element_type=F32)


def _dot_nn(a, b):
    return lax.dot_general(a, b, (((1,), (0,)), ((), ())), preferred_element_type=F32)


def _diag_mask(T, gran):
    r = lax.broadcasted_iota(jnp.int32, (T, T), 0)
    c = lax.broadcasted_iota(jnp.int32, (T, T), 1)
    if gran > 1:
        sh = int(np.log2(gran))
        r, c = lax.shift_right_logical(r, sh), lax.shift_right_logical(c, sh)
    return r >= c


def _attn_specs(S, H, T, mla, col_q, col_k, col_v):
    W = 2 * HEAD_DIM
    specs = [pl.BlockSpec((T, W), lambda p, i: (i, col_q + p))]
    if mla:
        specs.append(pl.BlockSpec((T, 2 * ROPE_DIM), lambda p, i: (i, p)))
    specs.append(pl.BlockSpec((S, W), lambda p, i: (0, col_k + p)))
    if mla:
        specs.append(pl.BlockSpec((S, ROPE_DIM), lambda p, i: (0, 0)))
    specs.append(pl.BlockSpec((S, W), lambda p, i: (0, col_v + p)))
    if not mla:
        specs.append(pl.BlockSpec((2, T, 1), lambda p, i: (p, i, 0)))
        specs.append(pl.BlockSpec((2, 1, S), lambda p, i: (p, 0, 0)))
    return specs


def _attn_fwd(name, S, H, mla, q, k, v, q_pe=None, k_pe=None, cum_col=None, cum_row=None, cols=(0, 0, 0),
              carry=None):
    T = _pick(S, (ATT_T, 128))
    nq = S // T
    scale = (HEAD_DIM + ROPE_DIM) ** -0.5 if mla else HEAD_DIM ** -0.5
    gran = CHUNK if mla else 1

    def body(*refs):
        refs, c_in, c_out, c_sems = _carry_split(refs, 5, 3, 3, carry)
        if mla:
            q_ref, qpe_ref, k_ref, kpe_ref, v_ref, o_ref, of_ref, lse_ref, m_s, l_s, acc_s = refs
        else:
            q_ref, k_ref, v_ref, cc_ref, cr_ref, o_ref, of_ref, lse_ref, m_s, l_s, acc_s = refs
        hp, qi = pl.program_id(0), pl.program_id(1)
        _carry_run(carry, c_in, c_out, c_sems, jnp.logical_and(hp == 0, qi == 0), True)
        hls = [slice(hh * HEAD_DIM, (hh + 1) * HEAD_DIM) for hh in range(2)]
        qn = [q_ref[:, hl] for hl in hls]
        qp = [qpe_ref[:, hh * ROPE_DIM:(hh + 1) * ROPE_DIM] for hh in range(2)] if mla else None
        m_s[...] = jnp.full(m_s.shape, NEG, F32)
        l_s[...] = jnp.zeros(l_s.shape, F32)
        acc_s[...] = jnp.zeros(acc_s.shape, F32)

        def step(j, masked):
            rows = pl.ds(pl.multiple_of(j * T, T), T)
            for hh, hl in enumerate(hls):
                s = _dot_nt(qn[hh], k_ref[rows, hl])
                if mla:
                    s = s + _dot_nt(qp[hh], kpe_ref[rows, :])
                s = s * scale
                if not mla:
                    s = s + (cc_ref[hh] - cr_ref[hh, :, rows])
                if masked:
                    s = jnp.where(_diag_mask(T, gran), s, NEG)
                m_old = m_s[hh, :, 0:1]
                m_new = jnp.maximum(m_old, jnp.max(s, axis=-1, keepdims=True))
                p = jnp.exp(s - m_new)
                corr = jnp.exp(m_old - m_new)
                l_s[hh] = jnp.broadcast_to(corr * l_s[hh, :, 0:1] + jnp.sum(p, axis=-1, keepdims=True), (T, LANE))
                p_hi = p.astype(MXU_DTYPE)
                p_lo = (p - p_hi.astype(F32)).astype(MXU_DTYPE)
                vf = v_ref[rows, hl]
                acc_s[hh] = corr * acc_s[hh] + (_dot_nn(p_hi, vf) + _dot_nn(p_lo, vf))
                m_s[hh] = jnp.broadcast_to(m_new, (T, LANE))

        lax.fori_loop(0, qi, lambda j, c: (step(j, False), c)[1], 0)
        step(qi, True)
        for hh, hl in enumerate(hls):
            l = l_s[hh, :, 0:1]
            of = acc_s[hh] / l
            of_ref[:, hl] = of
            o_ref[:, hl] = of.astype(o_ref.dtype)
            lse_ref[hh] = jnp.broadcast_to(m_s[hh, :, 0:1] + jnp.log(l), (T, LANE))
        _carry_run(carry, c_in, c_out, c_sems, jnp.logical_and(hp == H // 2 - 1, qi == nq - 1), False)

    args = [q] + ([q_pe] if mla else []) + [k] + ([k_pe] if mla else []) + [v]
    if not mla:
        args += [cum_col, cum_row]
    c_is, c_os, c_shape, c_sems, c_alias = _carry_call_args(carry, 5, 3)
    res = pl.pallas_call(
        body, name=name, grid=(H // 2, nq),
        in_specs=_attn_specs(S, H, T, mla, *cols) + c_is,
        out_specs=[pl.BlockSpec((T, 2 * HEAD_DIM), lambda p, i: (i, p)),
                   pl.BlockSpec((T, 2 * HEAD_DIM), lambda p, i: (i, p)),
                   pl.BlockSpec((2, T, LANE), lambda p, i: (p, i, 0))] + c_os,
        out_shape=[jax.ShapeDtypeStruct((S, H * HEAD_DIM), MXU_DTYPE), jax.ShapeDtypeStruct((S, H * HEAD_DIM), F32),
                   jax.ShapeDtypeStruct((H, S, LANE), F32)] + c_shape,
        input_output_aliases=c_alias,
        scratch_shapes=[pltpu.VMEM((2, T, LANE), F32), pltpu.VMEM((2, T, LANE), F32),
                        pltpu.VMEM((2, T, HEAD_DIM), F32)] + c_sems,
        compiler_params=_params(),
    )(*args, *(carry["ins"] if carry else ()))
    return res[0], res[1], res[2], list(res[3:])


def _carry_split(refs, n_in, n_out, n_scr, carry):
    if carry is None:
        return refs, (), (), ()
    ci, co = len(carry["ins"]), len(carry["out_shape"])
    own = refs[:n_in] + refs[n_in + ci:n_in + ci + n_out] + refs[n_in + ci + n_out + co:n_in + ci + n_out + co + n_scr]
    return (own, refs[n_in:n_in + ci], refs[n_in + ci + n_out:n_in + ci + n_out + co],
            refs[n_in + ci + n_out + co + n_scr:])


def _carry_run(carry, c_in, c_out, c_sems, when, start):
    if carry is None:
        return

    @pl.when(when)
    def _():
        for cp in carry["copies"](c_in, c_out, *c_sems):
            if start:
                cp.start()
            else:
                cp.wait()


def _carry_call_args(carry, n_in, n_out):
    if carry is None:
        return [], [], [], [], {}
    sems = [pltpu.SemaphoreType.DMA(carry["sems"]), pltpu.SemaphoreType.DMA(carry["sems"])]
    alias = {n_in + i: n_out + o for i, o in carry.get("alias", {}).items()}
    return [ANY] * len(carry["ins"]), [ANY] * len(carry["out_shape"]), list(carry["out_shape"]), sems, alias


def _attn_bwd(name, S, H, mla, q, k, v, do, of, lse, q_pe=None, k_pe=None, cum_col=None, cum_row=None,
              cols=(0, 0, 0), carry=None):
    T = _pick(S, (ATT_T, 128))
    nq = S // T
    scale = (HEAD_DIM + ROPE_DIM) ** -0.5 if mla else HEAD_DIM ** -0.5
    gran = CHUNK if mla else 1
    dqk = HEAD_DIM + (ROPE_DIM if mla else 0)

    def body(*refs):
        refs, c_in, c_out, c_sems = _carry_split(refs, 8, 5, 2, carry)
        if mla:
            (q_ref, qpe_ref, k_ref, kpe_ref, v_ref, do_ref, of_ref, lse_ref,
             dq_ref, dk_ref, dv_ref, dqpe_ref, dkpe_ref, dq_s, r_s) = refs
        else:
            (q_ref, k_ref, v_ref, cc_ref, cr_ref, do_ref, of_ref, lse_ref,
             dq_ref, dk_ref, dv_ref, dck_ref, dcq_ref, dq_s, r_s) = refs
        hp, qi = pl.program_id(0), pl.program_id(1)
        _carry_run(carry, c_in, c_out, c_sems, jnp.logical_and(hp == 0, qi == 0), True)

        @pl.when(qi == 0)
        def _():
            dk_ref[...] = jnp.zeros(dk_ref.shape, F32)
            dv_ref[...] = jnp.zeros(dv_ref.shape, F32)
            if not mla:
                dck_ref[...] = jnp.zeros(dck_ref.shape, F32)

        if mla:
            @pl.when(jnp.logical_and(qi == 0, hp == 0))
            def _():
                dkpe_ref[...] = jnp.zeros(dkpe_ref.shape, F32)

        hls = [slice(hh * HEAD_DIM, (hh + 1) * HEAD_DIM) for hh in range(2)]
        qn = [q_ref[:, hl] for hl in hls]
        qp = [qpe_ref[:, hh * ROPE_DIM:(hh + 1) * ROPE_DIM] for hh in range(2)] if mla else None
        dof = [do_ref[:, hl] for hl in hls]
        delta = [jnp.sum(dof[hh].astype(F32) * of_ref[:, hl], axis=-1, keepdims=True) for hh, hl in enumerate(hls)]
        lse = [lse_ref[hh][:, 0:1] for hh in range(2)]
        dq_s[...] = jnp.zeros(dq_s.shape, F32)
        r_s[...] = jnp.zeros(r_s.shape, F32)

        def step(j, masked):
            rows = pl.ds(pl.multiple_of(j * T, T), T)
            for hh, hl in enumerate(hls):
                kn = k_ref[rows, hl]
                s = _dot_nt(qn[hh], kn)
                if mla:
                    kp = kpe_ref[rows, :]
                    s = s + _dot_nt(qp[hh], kp)
                s = s * scale
                if not mla:
                    s = s + (cc_ref[hh] - cr_ref[hh, :, rows])
                if masked:
                    s = jnp.where(_diag_mask(T, gran), s, NEG)
                p = jnp.exp(s - lse[hh])
                dp = _dot_nt(dof[hh], v_ref[rows, hl])
                ds = p * (dp - delta[hh])
                dv_ref[rows, hl] += _dot_tn(p.astype(MXU_DTYPE), dof[hh])
                dsb = (ds * scale).astype(MXU_DTYPE)
                dk_ref[rows, hl] += _dot_tn(dsb, qn[hh])
                dq_s[hh, :, :HEAD_DIM] += _dot_nn(dsb, kn)
                if mla:
                    dkpe_ref[rows, :] += _dot_tn(dsb, qp[hh])
                    dq_s[hh, :, HEAD_DIM:] += _dot_nn(dsb, kp)
                else:
                    dck_ref[hh, :, rows] -= jnp.sum(ds, axis=0, keepdims=True)
                    r_s[hh] += jnp.broadcast_to(jnp.sum(ds, axis=-1, keepdims=True), (T, LANE))

        lax.fori_loop(0, qi, lambda j, c: (step(j, False), c)[1], 0)
        step(qi, True)
        for hh, hl in enumerate(hls):
            dq_ref[:, hl] = dq_s[hh, :, :HEAD_DIM].astype(dq_ref.dtype)
            if mla:
                dqpe_ref[:, hh * ROPE_DIM:(hh + 1) * ROPE_DIM] = dq_s[hh, :, HEAD_DIM:]
            else:
                dcq_ref[hh] = r_s[hh, :, 0:1]
        _carry_run(carry, c_in, c_out, c_sems, jnp.logical_and(hp == H // 2 - 1, qi == nq - 1), False)

    W = 2 * HEAD_DIM
    args = [q] + ([q_pe] if mla else []) + [k] + ([k_pe] if mla else []) + [v]
    if not mla:
        args += [cum_col, cum_row]
    args += [do, of, lse]
    in_specs = _attn_specs(S, H, T, mla, *cols)
    in_specs += [pl.BlockSpec((T, W), lambda p, i: (i, p)), pl.BlockSpec((T, W), lambda p, i: (i, p)),
                 pl.BlockSpec((2, T, LANE), lambda p, i: (p, i, 0))]
    out_specs = [pl.BlockSpec((T, W), lambda p, i: (i, p)), pl.BlockSpec((S, W), lambda p, i: (0, p)),
                 pl.BlockSpec((S, W), lambda p, i: (0, p))]
    out_shape = [jax.ShapeDtypeStruct((S, H * HEAD_DIM), MXU_DTYPE), jax.ShapeDtypeStruct((S, H * HEAD_DIM), F32),
                 jax.ShapeDtypeStruct((S, H * HEAD_DIM), F32)]
    if mla:
        out_specs += [pl.BlockSpec((T, 2 * ROPE_DIM), lambda p, i: (i, p)),
                      pl.BlockSpec((S, ROPE_DIM), lambda p, i: (0, 0))]
        out_shape += [jax.ShapeDtypeStruct((S, H * ROPE_DIM), F32), jax.ShapeDtypeStruct((S, ROPE_DIM), F32)]
    else:
        out_specs += [pl.BlockSpec((2, 1, S), lambda p, i: (p, 0, 0)), pl.BlockSpec((2, T, 1), lambda p, i: (p, i, 0))]
        out_shape += [jax.ShapeDtypeStruct((H, 1, S), F32), jax.ShapeDtypeStruct((H, S, 1), F32)]
    assert len(args) == 8 and len(out_shape) == 5
    c_is, c_os, c_shape, c_sems, c_alias = _carry_call_args(carry, 8, 5)
    res = pl.pallas_call(
        body, name=name, grid=(H // 2, nq), in_specs=in_specs + c_is, out_specs=out_specs + c_os,
        out_shape=out_shape + c_shape, input_output_aliases=c_alias,
        scratch_shapes=[pltpu.VMEM((2, T, dqk), F32), pltpu.VMEM((2, T, LANE), F32)] + c_sems,
        compiler_params=_params(),
    )(*args, *(carry["ins"] if carry else ()))
    return (res[:5], res[5:]) if carry else (res, [])


def _mla_prep(lat, cos, sin, qn, kvn, ql, kvl):
    def fn(lat, c, s, qn, kvn):
        ql_ = lat[:, :ql]
        kv_ = lat[:, ql:ql + kvl]
        kp = lat[:, ql + kvl:]
        cq = ql_ * lax.rsqrt(jnp.mean(ql_ * ql_, axis=-1, keepdims=True) + RMS_EPS) * qn
        ckv = kv_ * lax.rsqrt(jnp.mean(kv_ * kv_, axis=-1, keepdims=True) + RMS_EPS) * kvn
        kp2 = jnp.concatenate([kp, jnp.zeros_like(kp)], axis=-1)
        kr = _rope_fwd(kp2, c, s)[:, :ROPE_DIM]
        return cq, ckv, kr

    return _rowwise("mla_prep", fn, [lat, cos, sin], [qn, kvn],
                    [(ql, MXU_DTYPE), (kvl, MXU_DTYPE), (ROPE_DIM, MXU_DTYPE)])


def _mla_prep_bwd(lat, cos, sin, qn, kvn, dcq, dckv, dkr, ql, kvl):
    def fn(lat, c, s, dcq, dckv, dkr, qn, kvn):
        outs, reds = [], []
        for (x, g, d) in ((lat[:, :ql], qn, dcq), (lat[:, ql:ql + kvl], kvn, dckv)):
            r = lax.rsqrt(jnp.mean(x * x, axis=-1, keepdims=True) + RMS_EPS)
            n = x * r
            dn = d * g
            outs.append(r * (dn - n * jnp.mean(dn * n, axis=-1, keepdims=True)))
            reds.append(_colsum(d * n))
        d2 = jnp.concatenate([dkr, jnp.zeros_like(dkr)], axis=-1)
        outs.append(_rope_bwd(d2, c, s)[:, :ROPE_DIM])
        return (jnp.concatenate(outs, axis=-1), *reds)

    return _rowwise("mla_prep_bwd", fn, [lat, cos, sin, dcq, dckv, dkr], [qn, kvn],
                    [(ql + kvl + ROPE_DIM, MXU_DTYPE)], [ql, kvl])


def _mla_fwd(h, w, rc, ride=None):
    S = h.shape[0]
    ql, kvl = w["q_norm"].shape[1], w["kv_norm"].shape[1]
    H = _wdim(w["w_uk"], 1) // HEAD_DIM
    n_nope, n_pe = H * HEAD_DIM, H * ROPE_DIM
    lat = _mm1("mla_down", h, w["w_down"])
    cq, ckv, kr = _mla_prep(lat, rc[0], rc[1], w["q_norm"], w["kv_norm"], ql, kvl)
    q_nope = _mm1("mla_uq_nope", cq, w["w_uq"], out_dtype=MXU_DTYPE, N=n_nope)
    q_pe = _mm("mla_uq_pe", [(cq, w["w_uq"], False, False, 0, n_nope // LANE)], S, n_pe, [MXU_DTYPE],
               epilogue=lambda accs, c, s: (_rope_fwd(accs[0], c, s),), row_extras=rc, tn=LANE)[0]
    k_nope, v = _mm("mla_ukv", [(ckv, w["w_uk"], False, False, 0, 0), (ckv, w["w_uv"], False, False, 1, 0)],
                    S, n_nope, [MXU_DTYPE, MXU_DTYPE], epilogue=lambda accs: (accs[0], accs[1]))
    o, of, lse, bufs = _attn_fwd("mla_attn_fwd", S, H, True, q_nope, k_nope, v, q_pe=q_pe, k_pe=kr,
                                 carry=_ride(ride, "ici"))
    res = _mm1("mla_wo", o, w["w_o"], carry=_ride(ride, "d2d", bufs))
    y, bufs = res if ride else (res, [])
    return y, (lat, cq, ckv, kr, q_nope, q_pe, k_nope, v, o, of, lse), bufs


def _mla_bwd(dy, h, saved, w, rc, carry):
    lat, cq, ckv, kr, q_nope, q_pe, k_nope, v, o, of, lse = saved
    S = h.shape[0]
    ql, kvl = w["q_norm"].shape[1], w["kv_norm"].shape[1]
    H = _wdim(w["w_uk"], 1) // HEAD_DIM
    n_nope, n_pe = H * HEAD_DIM, H * ROPE_DIM
    do = _mm1("mla_do", dy, w["w_o"], tb=True, out_dtype=MXU_DTYPE)
    dw_o = _mm1("mla_dwo", o, dy, ta=True, out_dtype=WIRE_DTYPE)
    (dq_nope, dk_nope, dv, dq_pe_r, dk_pe_r), carried = _attn_bwd(
        "mla_attn_bwd", S, H, True, q_nope, k_nope, v, do, of, lse, q_pe=q_pe, k_pe=kr, carry=carry)

    def unrope(d, c, s):
        reps = (1, n_pe // LANE)
        return (_rope_bwd(d, jnp.tile(c, reps), jnp.tile(s, reps)),)

    dq_pe = _rowwise("mla_unrope_q", unrope, [dq_pe_r, rc[0], rc[1]], [], [(n_pe, MXU_DTYPE)])[0]
    dq = jnp.concatenate([dq_nope, dq_pe], axis=1)
    dw_uq = _mm1("mla_dwuq", cq, dq, ta=True, out_dtype=WIRE_DTYPE)
    dcq = _mm1("mla_dcq", dq, w["w_uq"], tb=True)
    dw_uk, dw_uv = _mm("mla_dwukv", [(ckv, dk_nope, True, False, 0, 0), (ckv, dv, True, False, 1, 0)], kvl, n_nope,
                       [WIRE_DTYPE, WIRE_DTYPE], epilogue=lambda accs: (accs[0], accs[1]))
    dckv = _mm("mla_dckv", [(dk_nope, w["w_uk"], False, True, 0, 0), (dv, w["w_uv"], False, True, 0, 0)],
               S, kvl, [F32])[0]
    dlat, dqn, dkvn = _mla_prep_bwd(lat, rc[0], rc[1], w["q_norm"], w["kv_norm"], dcq, dckv, dk_pe_r, ql, kvl)
    dw_down = _mm1("mla_dwdown", h, dlat, ta=True, out_dtype=WIRE_DTYPE)
    dh = _mm1("mla_dh", dlat, w["w_down"], tb=True)
    return dh, dict(w_down=dw_down, q_norm=dqn, w_uq=dw_uq, kv_norm=dkvn, w_uk=dw_uk, w_uv=dw_uv, w_o=dw_o), carried


def _log_sigmoid(z):
    return jnp.minimum(z, 0.0) - jnp.log(1.0 + jnp.exp(-jnp.abs(z)))


def _fox_gate_fwd(f, bf):
    S = f.shape[0]
    B = LANE

    def body(f_ref, b_ref, cum_ref):
        r = lax.broadcasted_iota(jnp.int32, (B, B), 0)
        c = lax.broadcasted_iota(jnp.int32, (B, B), 1)
        tri = (r >= c).astype(F32)
        carry = jnp.zeros((1, LANE), F32)
        for blk in range(S // B):
            rows = slice(blk * B, (blk + 1) * B)
            lf = _log_sigmoid(f_ref[rows, :] + b_ref[...])
            cs = jnp.dot(tri, lf, precision=lax.Precision.HIGHEST, preferred_element_type=F32) + carry
            cum_ref[rows, :] = cs
            carry = cs[B - 1:B, :]

    return pl.pallas_call(body, name="fox_gate_fwd", out_shape=jax.ShapeDtypeStruct((S, LANE), F32),
                          compiler_params=_params())(f, bf)


def _fox_gate_bwd(dcum, f, bf):
    S = f.shape[0]
    B = LANE

    def body(d_ref, f_ref, b_ref, df_ref, db_ref):
        r = lax.broadcasted_iota(jnp.int32, (B, B), 0)
        c = lax.broadcasted_iota(jnp.int32, (B, B), 1)
        tri = (r <= c).astype(F32)
        carry = jnp.zeros((1, LANE), F32)
        db = jnp.zeros((1, LANE), F32)
        for blk in reversed(range(S // B)):
            rows = slice(blk * B, (blk + 1) * B)
            dlf = jnp.dot(tri, d_ref[rows, :], precision=lax.Precision.HIGHEST, preferred_element_type=F32) + carry
            carry = dlf[0:1, :]
            z = f_ref[rows, :] + b_ref[...]
            dz = dlf * _sigmoid(-z)
            df_ref[rows, :] = dz.astype(df_ref.dtype)
            db = db + jnp.sum(dz, axis=0, keepdims=True)
        db_ref[...] = db

    return pl.pallas_call(body, name="fox_gate_bwd",
                          out_shape=[jax.ShapeDtypeStruct((S, LANE), MXU_DTYPE), jax.ShapeDtypeStruct((1, LANE), F32)],
                          compiler_params=_params())(dcum, f, bf)


def _fox_fwd(h, w, ride=None):
    S, D = h.shape
    H = D // HEAD_DIM
    qkv = _mm1("fox_qkv", h, w["w_in"], out_dtype=MXU_DTYPE, N=3 * D)
    f = _mm1("fox_f", h, w["w_in"], bcol=3 * D // LANE, N=LANE, tn=LANE)
    cum = _fox_gate_fwd(f, w["b_f"])
    cumT = cum[:, :H].T
    cum_col, cum_row = cumT.reshape(H, S, 1), cumT.reshape(H, 1, S)
    nb = D // (2 * HEAD_DIM)
    o, of, lse, bufs = _attn_fwd("fox_attn_fwd", S, H, False, qkv, qkv, qkv, cum_col=cum_col, cum_row=cum_row,
                                 cols=(0, nb, 2 * nb), carry=_ride(ride, "ici"))
    res = _mm1("fox_wo", o, w["w_o"], carry=_ride(ride, "d2d", bufs))
    y, bufs = res if ride else (res, [])
    return y, (qkv, f, cum_col, cum_row, o, of, lse), bufs


def _fox_bwd(dy, h, saved, w, carry):
    qkv, f, cum_col, cum_row, o, of, lse = saved
    S, D = h.shape
    H = D // HEAD_DIM
    nb = D // (2 * HEAD_DIM)
    do = _mm1("fox_do", dy, w["w_o"], tb=True, out_dtype=MXU_DTYPE)
    dw_o = _mm1("fox_dwo", o, dy, ta=True, out_dtype=WIRE_DTYPE)
    (dq, dk, dv, dck, dcq), carried = _attn_bwd("fox_attn_bwd", S, H, False, qkv, qkv, qkv, do, of, lse,
                                                cum_col=cum_col, cum_row=cum_row, cols=(0, nb, 2 * nb), carry=carry)
    dcum = jnp.pad((dck.reshape(H, S) + dcq.reshape(H, S)).T, ((0, 0), (0, LANE - H)))
    df, dbf = _fox_gate_bwd(dcum, f, w["b_f"])
    dproj = jnp.concatenate([dq, dk.astype(MXU_DTYPE), dv.astype(MXU_DTYPE), df], axis=1)
    dw_in = _mm1("fox_dwin", h, dproj, ta=True, out_dtype=WIRE_DTYPE)
    dh = _mm1("fox_dh", dproj, w["w_in"], tb=True, tn=256)
    return dh, dict(w_in=dw_in, b_f=dbf[:, :H], w_o=dw_o), carried


def _place():
    x, y, c = lax.axis_index("x"), lax.axis_index("y"), lax.axis_index("c")
    return x, y, c, [(1 - x, y), (x, 1 - y), (1 - x, 1 - y)]


def _ag_small(name, blk):
    m, n = blk.shape

    def body(x_ref, out_ref, send_sems, recv_sems, local_sem):
        x, y, c, chips = _place()
        me, sibling = (x, y, c), (x, y, 1 - c)

        def rows(px, py, pc):
            return out_ref.at[pl.ds((4 * px + 2 * py + pc) * m, m), :]

        def copy(k, block, to, src=None):
            return pltpu.make_async_remote_copy(
                src_ref=rows(*block) if src is None else src, dst_ref=rows(*block),
                send_sem=send_sems.at[k], recv_sem=recv_sems.at[k], device_id=to, device_id_type=MESH)

        mine = pltpu.make_async_copy(x_ref, rows(*me), local_sem)
        mine.start()
        first = [copy(0, me, sibling, src=x_ref)]
        first += [copy(1 + j, me, (*chip, c), src=x_ref) for j, chip in enumerate(chips)]
        for cp in first:
            cp.start()
        passed = [copy(4 + j, (*chip, c), sibling) for j, chip in enumerate(chips)]
        for j, chip in enumerate(chips):
            copy(1 + j, (*chip, c), me).wait_recv()
            passed[j].start()
        copy(0, sibling, me).wait_recv()
        for j, chip in enumerate(chips):
            copy(4 + j, (*chip, 1 - c), me).wait_recv()
        for cp in first + passed:
            cp.wait_send()
        mine.wait()

    return pl.pallas_call(
        body, name=name, out_shape=jax.ShapeDtypeStruct((N_DEV * m, n), blk.dtype),
        in_specs=[pl.BlockSpec(memory_space=pltpu.VMEM)], out_specs=pl.BlockSpec(memory_space=pltpu.VMEM),
        scratch_shapes=[pltpu.SemaphoreType.DMA((7,)), pltpu.SemaphoreType.DMA((7,)), pltpu.SemaphoreType.DMA],
        compiler_params=_params(),
    )(blk)


def _half(ref, row_axis, c, rows):
    idx = [slice(None)] * len(ref.shape)
    idx[row_axis] = pl.ds(pl.multiple_of(c * rows, 16), rows)
    return ref.at[tuple(idx)]


def _shard(ref, layout, k):
    if layout == "row":
        return ref.at[:, k]
    w = ref.shape[2] // N_CHIPS
    return ref.at[:, :, pl.ds(pl.multiple_of(k * w, LANE), w)]


def _full_shape(shape, layout):
    L, r, w = shape
    return (L, N_CHIPS, r, w) if layout == "row" else (L, r, N_CHIPS * w)


def _cast_full(name, a, layout, k_idx, lyr):
    _, r, C = a.shape
    tr = _pick(r, (256, 128, 64, 32, 16))

    def body(k_ref, a_ref, o_ref):
        o_ref[...] = a_ref[...].astype(o_ref.dtype)

    if layout == "row":
        o_spec = pl.BlockSpec((None, None, tr, C), lambda l, i, k: (0, k[0], i, 0))
    else:
        o_spec = pl.BlockSpec((None, tr, C), lambda l, i, k: (0, i, k[0]))
    return pl.pallas_call(
        body, name=name,
        grid_spec=pltpu.PrefetchScalarGridSpec(
            num_scalar_prefetch=1, grid=(1, r // tr),
            in_specs=[pl.BlockSpec((None, tr, C), lambda l, i, k: (lyr, i, 0))], out_specs=o_spec),
        out_shape=jax.ShapeDtypeStruct(_full_shape((1, r, C), layout), WIRE_DTYPE),
        compiler_params=_params(),
    )(k_idx, a)


def _gather_carry(fulls, layouts, lyrs, phase):
    n = len(fulls)
    half_rows = [f.shape[2 if lay == "row" else 1] // 2 for f, lay in zip(fulls, layouts)]

    def copies(ins, outs, send_sems, recv_sems, off=0):
        x, y, c, chips = _place()
        cps = []
        for i in range(n):
            for j, chip in enumerate(chips):
                who, to = ((x, y), (*chip, c)) if phase == "ici" else (chip, (x, y, 1 - c))
                w = _half(_shard(outs[i].at[pl.ds(lyrs[i], 1)], layouts[i], 2 * who[0] + who[1]), 1, c, half_rows[i])
                cps.append(pltpu.make_async_remote_copy(
                    src_ref=w, dst_ref=w, send_sem=send_sems.at[off + i, j], recv_sem=recv_sems.at[off + i, j],
                    device_id=to, device_id_type=MESH))
        return cps

    return dict(ins=list(fulls), sems=(n, 3), copies=copies, alias={i: i for i in range(n)},
                out_shape=[jax.ShapeDtypeStruct(f.shape, f.dtype) for f in fulls])


def _run_carry(name, carry):
    n = len(carry["ins"])

    def body(*refs):
        cps = carry["copies"](refs[:n], refs[n:2 * n], *refs[2 * n:])
        for cp in cps:
            cp.start()
        for cp in cps:
            cp.wait()

    return pl.pallas_call(
        body, name=name, in_specs=[ANY] * n, out_specs=[ANY] * len(carry["out_shape"]), out_shape=carry["out_shape"],
        input_output_aliases=dict(carry.get("alias", {})),
        scratch_shapes=[pltpu.SemaphoreType.DMA(carry["sems"]), pltpu.SemaphoreType.DMA(carry["sems"])],
        compiler_params=_params(),
    )(*carry["ins"])


def _gather_weights(fulls, layouts, lyrs):
    n = len(fulls)
    half_rows = [f.shape[2 if lay == "row" else 1] // 2 for f, lay in zip(fulls, layouts)]

    def body(*refs):
        outs = refs[n:2 * n]
        send_sems, recv_sems = refs[2 * n:]
        x, y, c, chips = _place()
        sibling = (x, y, 1 - c)

        def window(i, kx, ky, half):
            return _half(_shard(outs[i].at[pl.ds(lyrs[i], 1)], layouts[i], 2 * kx + ky), 1, half, half_rows[i])

        first, passed = [], []
        for i in range(n):
            mine = window(i, x, y, c)
            for j, chip in enumerate(chips):
                cp = pltpu.make_async_remote_copy(
                    src_ref=mine, dst_ref=mine, send_sem=send_sems.at[i, j], recv_sem=recv_sems.at[i, j],
                    device_id=(*chip, c), device_id_type=MESH)
                cp.start()
                first.append(cp)
        for i in range(n):
            for j, chip in enumerate(chips):
                got = window(i, *chip, c)
                pltpu.make_async_remote_copy(
                    src_ref=got, dst_ref=got, send_sem=send_sems.at[i, j], recv_sem=recv_sems.at[i, j],
                    device_id=(*chip, c), device_id_type=MESH).wait_recv()
                cp = pltpu.make_async_remote_copy(
                    src_ref=got, dst_ref=got, send_sem=send_sems.at[i, 3 + j], recv_sem=recv_sems.at[i, 3 + j],
                    device_id=sibling, device_id_type=MESH)
                cp.start()
                passed.append(cp)
        for i in range(n):
            for j, chip in enumerate(chips):
                got = window(i, *chip, 1 - c)
                pltpu.make_async_remote_copy(
                    src_ref=got, dst_ref=got, send_sem=send_sems.at[i, 3 + j], recv_sem=recv_sems.at[i, 3 + j],
                    device_id=sibling, device_id_type=MESH).wait_recv()
        for cp in first + passed:
            cp.wait_send()

    return pl.pallas_call(
        body, name="gather_weights", in_specs=[ANY] * n, out_specs=[ANY] * n,
        out_shape=[jax.ShapeDtypeStruct(f.shape, f.dtype) for f in fulls],
        input_output_aliases={i: i for i in range(n)},
        scratch_shapes=[pltpu.SemaphoreType.DMA((n, 6)), pltpu.SemaphoreType.DMA((n, 6))],
        compiler_params=_params(),
    )(*fulls)


def _half_shape(shape, layout):
    s = list(shape)
    s[2 if layout == "row" else 1] //= 2
    return tuple(s)


def _swap_carry(grads, layouts):
    n = len(grads)
    row_axis = [2 if lay == "row" else 1 for lay in layouts]
    half_rows = [g.shape[ra] // 2 for g, ra in zip(grads, row_axis)]

    def copies(ins, outs, send_sems, recv_sems, off=0):
        x, y, c, _ = _place()
        return [pltpu.make_async_remote_copy(
            src_ref=_half(ins[i], row_axis[i], 1 - c, half_rows[i]), dst_ref=outs[i], send_sem=send_sems.at[off + i, 0],
            recv_sem=recv_sems.at[off + i, 0], device_id=(x, y, 1 - c), device_id_type=MESH) for i in range(n)]

    return dict(ins=list(grads), sems=(n, 3), copies=copies,
                out_shape=[jax.ShapeDtypeStruct(_half_shape(g.shape, lay), g.dtype) for g, lay in zip(grads, layouts)])


def _merge_carries(a, b):
    if a is None or b is None:
        return a if b is None else b
    na, ia, oa = a["sems"][0], len(a["ins"]), len(a["out_shape"])

    def copies(ins, outs, send_sems, recv_sems, off=0):
        return (a["copies"](ins[:ia], outs[:oa], send_sems, recv_sems, off)
                + b["copies"](ins[ia:], outs[oa:], send_sems, recv_sems, off + na))

    alias = dict(a.get("alias", {}))
    alias.update({ia + i: oa + o for i, o in b.get("alias", {}).items()})
    return dict(ins=a["ins"] + b["ins"], out_shape=a["out_shape"] + b["out_shape"], sems=(na + b["sems"][0], 3),
                copies=copies, alias=alias)


def _add_half(name, g, r, layout, c_idx):
    L = g.shape[0]
    if layout == "row":
        A, rows, W = L * N_CHIPS, g.shape[2] // 2, g.shape[3]
    else:
        A, rows, W = L, g.shape[1] // 2, g.shape[2]
    g3 = g.reshape(A, 2 * rows, W)
    r3 = r.reshape(A, rows, W)
    tr = _pick(rows, (256, 128, 64, 32, 16))
    nb = rows // tr

    def body(c_ref, g_ref, r_ref, o_ref):
        o_ref[...] = (g_ref[...].astype(F32) + r_ref[...].astype(F32)).astype(o_ref.dtype)

    out = pl.pallas_call(
        body, name=name,
        grid_spec=pltpu.PrefetchScalarGridSpec(
            num_scalar_prefetch=1, grid=(A, nb),
            in_specs=[pl.BlockSpec((None, tr, W), lambda a, i, c: (a, c[0] * nb + i, 0)),
                      pl.BlockSpec((None, tr, W), lambda a, i, c: (a, i, 0))],
            out_specs=pl.BlockSpec((None, tr, W), lambda a, i, c: (a, i, 0))),
        out_shape=jax.ShapeDtypeStruct((A, rows, W), WIRE_DTYPE),
        compiler_params=_params(),
    )(c_idx, g3, r3)
    return out.reshape(r.shape)


def _exchange_carry(parts, layouts):
    n = len(parts)

    def shard_half_shape(p, lay):
        if lay == "row":
            return (p.shape[0],) + p.shape[2:]
        return (p.shape[0], p.shape[1], p.shape[2] // N_CHIPS)

    def copies(ins, outs, send_sems, recv_sems, off=0):
        x, y, c, chips = _place()
        return [pltpu.make_async_remote_copy(
            src_ref=_shard(ins[i], layouts[i], 2 * kx + ky), dst_ref=outs[i].at[j], send_sem=send_sems.at[off + i, j],
            recv_sem=recv_sems.at[off + i, j], device_id=(kx, ky, c), device_id_type=MESH)
            for i in range(n) for j, (kx, ky) in enumerate(chips)]

    return dict(ins=list(parts), sems=(n, 3), copies=copies,
                out_shape=[jax.ShapeDtypeStruct((3,) + shard_half_shape(p, lay), p.dtype)
                           for p, lay in zip(parts, layouts)])


def _sum_shards(name, p, r, layout, kc_idx, dst, lyr, n_lyr):
    rows, W = r.shape[2], r.shape[3]
    tr = _pick(rows, (256, 128, 64, 32, 16))
    nb = rows // tr

    def body(kc_ref, p_ref, r_ref, *rest):
        acc = p_ref[...].astype(F32)
        for j in range(3):
            acc = acc + r_ref[j].astype(F32)
        rest[-1][...] = acc

    if layout == "row":
        p_spec = pl.BlockSpec((None, None, tr, W), lambda a, i, kc: (0, kc[0], i, 0))
    else:
        p_spec = pl.BlockSpec((None, tr, W), lambda a, i, kc: (0, i, kc[0]))
    in_specs = [p_spec, pl.BlockSpec((3, None, tr, W), lambda a, i, kc: (0, 0, i, 0))]
    args = [kc_idx, p, r]
    if dst is not None:
        in_specs.append(ANY)
        args.append(dst)
    return pl.pallas_call(
        body, name=name,
        grid_spec=pltpu.PrefetchScalarGridSpec(
            num_scalar_prefetch=1, grid=(1, nb), in_specs=in_specs,
            out_specs=pl.BlockSpec((None, tr, W), lambda a, i, kc: (lyr, kc[1] * nb + i, 0))),
        out_shape=jax.ShapeDtypeStruct((n_lyr, 2 * rows, W), F32),
        input_output_aliases={3: 0} if dst is not None else {},
        compiler_params=_params(),
    )(*args)


def _join_halves(shards):
    n = len(shards)

    def body(*refs):
        outs = refs[n:2 * n]
        send_sems, recv_sems = refs[2 * n:]
        x, y, c, _ = _place()
        cps = []
        for i in range(n):
            mine = _half(outs[i], 1, c, outs[i].shape[1] // 2)
            cp = pltpu.make_async_remote_copy(
                src_ref=mine, dst_ref=mine, send_sem=send_sems.at[i], recv_sem=recv_sems.at[i],
                device_id=(x, y, 1 - c), device_id_type=MESH)
            cp.start()
            cps.append(cp)
        for cp in cps:
            cp.wait()

    return pl.pallas_call(
        body, name="join_halves", in_specs=[ANY] * n, out_specs=[ANY] * n,
        out_shape=[jax.ShapeDtypeStruct(s.shape, s.dtype) for s in shards],
        input_output_aliases={i: i for i in range(n)},
        scratch_shapes=[pltpu.SemaphoreType.DMA((n,)), pltpu.SemaphoreType.DMA((n,))],
        compiler_params=_params(),
    )(*shards)


def _ada_fwd(c_all, ada_w, ada_b):
    L, D, w = ada_w.shape
    tn = _pick(w, (512, 256, 128))

    def body(c_ref, w_ref, b_ref, o_ref, a_ref):
        c = c_ref[...]
        act = (c * _sigmoid(c)).astype(MXU_DTYPE)
        a_ref[...] = act
        o_ref[...] = jnp.dot(act, w_ref[...].astype(MXU_DTYPE), preferred_element_type=F32) + b_ref[...]

    return pl.pallas_call(
        body, name="ada_fwd", grid=(L, w // tn),
        in_specs=[pl.BlockSpec((16, D), lambda l, j: (0, 0)), pl.BlockSpec((None, D, tn), lambda l, j: (l, 0, j)),
                  pl.BlockSpec((None, 1, tn), lambda l, j: (l, 0, j))],
        out_specs=[pl.BlockSpec((None, 16, tn), lambda l, j: (l, 0, j)), pl.BlockSpec((16, D), lambda l, j: (0, 0))],
        out_shape=[jax.ShapeDtypeStruct((L, 16, w), F32), jax.ShapeDtypeStruct((16, D), MXU_DTYPE)],
        compiler_params=_params(),
    )(c_all, ada_w, ada_b)


def _sum_devices(name, parts):
    n, R, W = parts.shape
    tw = _pick(W, (2048, 1024, 512, 256, 128))

    def body(p_ref, o_ref):
        acc = p_ref[0]
        for d in range(1, n):
            acc = acc + p_ref[d]
        o_ref[...] = acc

    return pl.pallas_call(
        body, name=name, grid=(W // tw,), in_specs=[pl.BlockSpec((n, R, tw), lambda j: (0, 0, j))],
        out_specs=pl.BlockSpec((R, tw), lambda j: (0, j)), out_shape=jax.ShapeDtypeStruct((R, W), F32),
        compiler_params=_params(),
    )(parts)


def _adamw_math(w, g, m, v):
    m2 = ADAM_B1 * m + (1.0 - ADAM_B1) * g
    v2 = ADAM_B2 * v + (1.0 - ADAM_B2) * (g * g)
    m_hat = m2 / (1.0 - ADAM_B1 ** ADAM_STEP)
    v_hat = v2 / (1.0 - ADAM_B2 ** ADAM_STEP)
    return -ADAM_LR * (m_hat / (jnp.sqrt(v_hat) + ADAM_EPS) + ADAM_WD * w), m2, v2


def _adamw(name, w, g, m, v):
    W = w.shape[1]
    tr = 256 if W <= 1024 else (128 if W <= 2048 else 64)
    return _rowwise(name, lambda w, g, m, v: (g,) + _adamw_math(w, g, m, v), [w, g, m, v], [], [(W, F32)] * 4, tr=tr)


def _adamw_ada(c_act, dmod, w, m, v, carry=None):
    L, D, wd = w.shape
    tr = LANE

    def body(*refs):
        refs, c_in, c_out, c_sems = _carry_split(refs, 5, 4, 0, carry)
        c_ref, d_ref, w_ref, m_ref, v_ref, g_ref, dl_ref, m2_ref, v2_ref = refs
        pl_, pi = pl.program_id(0), pl.program_id(1)
        _carry_run(carry, c_in, c_out, c_sems, jnp.logical_and(pl_ == 0, pi == 0), True)
        g = _dot_tn(c_ref[...], d_ref[...].astype(MXU_DTYPE))
        g_ref[...] = g
        dl_ref[...], m2_ref[...], v2_ref[...] = _adamw_math(w_ref[...], g, m_ref[...], v_ref[...])
        _carry_run(carry, c_in, c_out, c_sems, jnp.logical_and(pl_ == L - 1, pi == D // tr - 1), False)

    big = pl.BlockSpec((None, tr, wd), lambda l, i: (l, i, 0))
    c_is, c_os, c_shape, c_sems, c_alias = _carry_call_args(carry, 5, 4)
    res = pl.pallas_call(
        body, name="adamw_ada_w", grid=(L, D // tr),
        in_specs=[pl.BlockSpec((16, tr), lambda l, i: (0, i)), pl.BlockSpec((None, 16, wd), lambda l, i: (l, 0, 0)),
                  big, big, big] + c_is,
        out_specs=[big] * 4 + c_os, out_shape=[jax.ShapeDtypeStruct(w.shape, F32)] * 4 + c_shape,
        input_output_aliases=c_alias, scratch_shapes=c_sems,
        compiler_params=_params(),
    )(c_act, dmod, w, m, v, *(carry["ins"] if carry else ()))
    return res[:4], list(res[4:])


def _flat(a):
    return a.reshape(-1, a.shape[-1])


BIG = ("ffn_w1", "ffn_w3", "ffn_w2", "mla_w_down", "mla_w_uq", "mla_w_uk", "mla_w_uv", "mla_w_o", "fox_w_in",
       "fox_w_o")
LAYOUT = dict(ffn_w1="col", ffn_w3="col", ffn_w2="row", mla_w_down="row", mla_w_uq="col", mla_w_uk="col",
              mla_w_uv="col", mla_w_o="row", fox_w_in="row", fox_w_o="row")
FFN = ("ffn_w1", "ffn_w3", "ffn_w2")
MLA = ("mla_w_down", "mla_w_uq", "mla_w_uk", "mla_w_uv", "mla_w_o")
FOX = ("fox_w_in", "fox_w_o")
SMALL = ("ln1_g", "ln1_b", "ln2_g", "ln2_b", "mla_q_norm", "mla_kv_norm", "fox_b_f")
WEIGHTS = ("ada_w", "ada_b", "ln1_g", "ln1_b", "ln2_g", "ln2_b", "ffn_w1", "ffn_w3", "ffn_w2", "mla_w_down",
           "mla_q_norm", "mla_w_uq", "mla_kv_norm", "mla_w_uk", "mla_w_uv", "mla_w_o", "fox_w_in", "fox_b_f",
           "fox_w_o")


def _uq_perm(H):
    d = HEAD_DIM + ROPE_DIM
    nope = (np.arange(H)[:, None] * d + np.arange(HEAD_DIM)[None, :]).reshape(-1)
    pe = (np.arange(H)[:, None] * d + HEAD_DIM + np.arange(ROPE_DIM)[None, :]).reshape(-1)
    return np.concatenate([nope, pe])


def kernel(x, c, positions, ada_w, ada_b, ln1_g, ln1_b, ln2_g, ln2_b, ffn_w1, ffn_w3, ffn_w2, mla_w_down, mla_q_norm, mla_w_uq, mla_kv_norm, mla_w_uk, mla_w_uv, mla_w_o, fox_w_in, fox_b_f, fox_w_o, loss_target, m_ada_w, m_ada_b, m_ln1_g, m_ln1_b, m_ln2_g, m_ln2_b, m_ffn_w1, m_ffn_w3, m_ffn_w2, m_mla_w_down, m_mla_q_norm, m_mla_w_uq, m_mla_kv_norm, m_mla_w_uk, m_mla_w_uv, m_mla_w_o, m_fox_w_in, m_fox_b_f, m_fox_w_o, v_ada_w, v_ada_b, v_ln1_g, v_ln1_b, v_ln2_g, v_ln2_b, v_ffn_w1, v_ffn_w3, v_ffn_w2, v_mla_w_down, v_mla_q_norm, v_mla_w_uq, v_mla_kv_norm, v_mla_w_uk, v_mla_w_uv, v_mla_w_o, v_fox_w_in, v_fox_b_f, v_fox_w_o):
    W = dict(ada_w=ada_w, ada_b=ada_b, ln1_g=ln1_g, ln1_b=ln1_b, ln2_g=ln2_g, ln2_b=ln2_b, ffn_w1=ffn_w1,
             ffn_w3=ffn_w3, ffn_w2=ffn_w2, mla_w_down=mla_w_down, mla_q_norm=mla_q_norm, mla_w_uq=mla_w_uq,
             mla_kv_norm=mla_kv_norm, mla_w_uk=mla_w_uk, mla_w_uv=mla_w_uv, mla_w_o=mla_w_o, fox_w_in=fox_w_in,
             fox_b_f=fox_b_f, fox_w_o=fox_w_o)
    Mo = dict(ada_w=m_ada_w, ada_b=m_ada_b, ln1_g=m_ln1_g, ln1_b=m_ln1_b, ln2_g=m_ln2_g, ln2_b=m_ln2_b,
              ffn_w1=m_ffn_w1, ffn_w3=m_ffn_w3, ffn_w2=m_ffn_w2, mla_w_down=m_mla_w_down, mla_q_norm=m_mla_q_norm,
              mla_w_uq=m_mla_w_uq, mla_kv_norm=m_mla_kv_norm, mla_w_uk=m_mla_w_uk, mla_w_uv=m_mla_w_uv,
              mla_w_o=m_mla_w_o, fox_w_in=m_fox_w_in, fox_b_f=m_fox_b_f, fox_w_o=m_fox_w_o)
    Vo = dict(ada_w=v_ada_w, ada_b=v_ada_b, ln1_g=v_ln1_g, ln1_b=v_ln1_b, ln2_g=v_ln2_g, ln2_b=v_ln2_b,
              ffn_w1=v_ffn_w1, ffn_w3=v_ffn_w3, ffn_w2=v_ffn_w2, mla_w_down=v_mla_w_down, mla_q_norm=v_mla_q_norm,
              mla_w_uq=v_mla_w_uq, mla_kv_norm=v_mla_kv_norm, mla_w_uk=v_mla_w_uk, mla_w_uv=v_mla_w_uv,
              mla_w_o=v_mla_w_o, fox_w_in=v_fox_w_in, fox_b_f=v_fox_b_f, fox_w_o=v_fox_w_o)

    S, D = x.shape[1], x.shape[2]
    L = ada_w.shape[0]
    alpha = float((2 * L) ** 0.25)
    H_mla = mla_w_uk.shape[2] * N_CHIPS // HEAD_DIM
    H_fox = D // HEAD_DIM
    xi, yi, ci = lax.axis_index("x"), lax.axis_index("y"), lax.axis_index("c")
    chip = 2 * xi + yi
    dev = 2 * chip + ci
    c_idx = jnp.reshape(ci, (1,)).astype(jnp.int32)
    x0, tgt = x[0], loss_target[0]
    pos = positions.reshape(S, 1)

    c_all = _ag_small("gather_c", jnp.pad(c, ((0, 7), (0, 0)))).reshape(N_DEV, 8, D)[:, 0]
    w_ada = ada_w.shape[2]
    ada_b_sh = lax.dynamic_slice_in_dim(ada_b, chip * w_ada, w_ada, axis=1).reshape(L, 1, w_ada)
    mod_sh, c_act = _ada_fwd(jnp.pad(c_all, ((0, 8), (0, 0))), ada_w, ada_b_sh)
    mod_all = _ag_small("gather_mod", mod_sh.transpose(1, 0, 2).reshape(16, L * w_ada))
    mod_all = mod_all.reshape(N_CHIPS, 2, 16, L, w_ada)[:, 0]
    mod = lax.dynamic_index_in_dim(mod_all, dev, axis=1, keepdims=False)
    mod = mod.transpose(1, 0, 2).reshape(L, 6, 1, D)

    kc_idx = jnp.stack([chip, ci]).astype(jnp.int32)
    raw = {n: [_cast_full("cast_" + n, W[n], LAYOUT[n], kc_idx, l) for l in range(W[n].shape[0])] for n in BIG}
    full = {n: [None] * W[n].shape[0] for n in BIG}
    perm = _uq_perm(H_mla)
    n_in = fox_w_in.shape[2] * N_CHIPS

    def group(i):
        return FFN + (MLA if i % 2 == 0 else FOX)

    def ride_of(names, i):
        return dict(bufs=[raw[n][i if n in FFN else i // 2] for n in names], lays=[LAYOUT[n] for n in names])

    def land(names, i, bufs):
        for n, f in zip(names, bufs):
            if n == "fox_w_in":
                fw = f.reshape(N_CHIPS, D, -1).transpose(1, 0, 2).reshape(D, n_in)
                f = jnp.pad(fw, ((0, 0), (0, 3 * D + LANE - n_in)))[None]
            elif n == "mla_w_uq":
                f = f[:, :, perm]
            elif LAYOUT[n] == "row":
                f = f.reshape(1, f.shape[1] * f.shape[2], f.shape[3])
            full[n][i if n in FFN else i // 2] = f

    r0 = ride_of(group(0)[3:], 0)
    land(group(0)[3:], 0, _gather_weights(r0["bufs"], r0["lays"], [0] * len(r0["bufs"])))

    rc = tuple(_rope_tables(pos))

    def mixer_w(i):
        j = i // 2
        if i % 2 == 0:
            return dict(w_down=(full["mla_w_down"][j], 0), q_norm=mla_q_norm[j:j + 1], w_uq=(full["mla_w_uq"][j], 0),
                        kv_norm=mla_kv_norm[j:j + 1], w_uk=(full["mla_w_uk"][j], 0), w_uv=(full["mla_w_uv"][j], 0),
                        w_o=(full["mla_w_o"][j], 0))
        return dict(w_in=(full["fox_w_in"][j], 0), b_f=jnp.pad(fox_b_f[j:j + 1], ((0, 0), (0, LANE - H_fox))),
                    w_o=(full["fox_w_o"][j], 0))

    saved = []
    xc = x0
    h = _modulate(x0, mod[0, 1], mod[0, 0])
    for i in range(L):
        mw = mixer_w(i)
        nxt = i + 1 < L
        mix_n = group(i + 1)[3:]
        if i % 2 == 0:
            y1, ms, got = _mla_fwd(h, mw, rc, ride_of(FFN, i))
        else:
            y1, ms, got = _fox_fwd(h, mw, ride_of(FFN, i))
        land(FFN, i, got)
        z1, x1, h2 = _resid_ln_mod(xc, y1, mod[i, 2], ln1_g[i:i + 1], ln1_b[i:i + 1], mod[i, 4], mod[i, 3], alpha)
        y2, fs, got = _ffn_fwd(h2, (full["ffn_w1"][i], 0), (full["ffn_w3"][i], 0), (full["ffn_w2"][i], 0),
                               ride_of(mix_n, i + 1) if nxt else None)
        land(mix_n, i + 1, got)
        rec = dict(h1=h, ms=ms, y1=y1, z1=z1, h2=h2, fs=fs, y2=y2)
        if i + 1 < L:
            z2, xc, h = _resid_ln_mod(x1, y2, mod[i, 5], ln2_g[i:i + 1], ln2_b[i:i + 1], mod[i + 1, 1],
                                      mod[i + 1, 0], alpha)
            rec["z2"] = z2
        else:
            dx_res, dy, loss_v, dlg, dlb, dgate = _final_ln_loss(x1, y2, tgt, mod[i, 5], ln2_g[i:i + 1],
                                                                 ln2_b[i:i + 1], alpha)
        saved.append(rec)
    loss = lax.psum(loss_v[0, 0] * (0.5 / D), ("x", "y", "c"))

    G = {n: [None] * W[n].shape[0] for n in SMALL}
    dmod = [[None] * 6 for _ in range(L)]
    red = {n: None for n in BIG}
    inv_perm = np.argsort(perm)

    def rs_view(n, g):
        if n == "mla_w_uq":
            g = g[:, inv_perm]
        if n == "fox_w_in":
            g = g[:, :n_in].reshape(D, N_CHIPS, n_in // N_CHIPS).transpose(1, 0, 2)
        elif LAYOUT[n] == "row":
            g = g.reshape(N_CHIPS, g.shape[0] // N_CHIPS, g.shape[1])
        return g[None]

    def add_halves(names, gs, recv):
        return [_add_half("add_half_" + n, g, r, LAYOUT[n], c_idx) for n, g, r in zip(names, gs, recv)]

    def half_sums(tag, names, grads):
        gs = [rs_view(n, g) for n, g in zip(names, grads)]
        return add_halves(names, gs, _run_carry("swap_" + tag, _swap_carry(gs, [LAYOUT[n] for n in names])))

    def finish(names, parts, recv2, lyr):
        for n, p, r in zip(names, parts, recv2):
            red[n] = _sum_shards("sum_shards_" + n, p, r, LAYOUT[n], kc_idx, red[n], lyr, W[n].shape[0])

    queue = []

    def carry_of(item):
        return _exchange_carry(item[1], [LAYOUT[n] for n in item[0]]) if item else None

    for i in reversed(range(L)):
        rec = saved[i]
        mw = mixer_w(i)
        j = i // 2
        G["ln2_g"][i], G["ln2_b"][i], dmod[i][5] = dlg, dlb, dgate
        ride_du = queue.pop(0) if queue else None
        ride_dh = queue.pop(0) if queue else None
        views = []

        def swap_of(grads):
            views.extend(rs_view(n, g) for n, g in zip(FFN, grads))
            return _swap_carry(views, [LAYOUT[n] for n in FFN])

        dh2, dw1, dw3, dw2, got_du, got_dh, got_swap = _ffn_bwd(
            dy, rec["h2"], rec["fs"], (full["ffn_w1"][i], 0), (full["ffn_w3"][i], 0), (full["ffn_w2"][i], 0),
            carry_of(ride_du), carry_of(ride_dh), swap_of)
        for item, got in ((ride_du, got_du), (ride_dh, got_dh)):
            if item:
                finish(item[0], item[1], got, item[2])
        ffn_parts = add_halves(FFN, views, got_swap)
        dx_res, dy, dmod[i][4], dmod[i][3], G["ln1_g"][i], G["ln1_b"][i], dmod[i][2] = _bwd_boundary(
            dx_res, dh2, rec["z1"], rec["y1"], mod[i, 4], mod[i, 2], ln1_g[i:i + 1], ln1_b[i:i + 1], alpha)
        n_at = 2 if i > 0 else 3
        ride_at = (FFN[:n_at], ffn_parts[:n_at], i)
        if i % 2 == 0:
            dh1, gm, recv2 = _mla_bwd(dy, rec["h1"], rec["ms"], mw, rc, carry_of(ride_at))
            names, pre = MLA, "mla_"
            G["mla_q_norm"][j], G["mla_kv_norm"][j] = gm["q_norm"], gm["kv_norm"]
        else:
            dh1, gm, recv2 = _fox_bwd(dy, rec["h1"], rec["ms"], mw, carry_of(ride_at))
            names, pre = FOX, "fox_"
            G["fox_b_f"][j] = gm["b_f"]
        finish(ride_at[0], ride_at[1], recv2, i)
        if n_at < 3:
            queue.append((FFN[2:], ffn_parts[2:], i))
        queue.append((names, half_sums(pre[:-1], names, [gm[n[len(pre):]] for n in names]), j))
        if i > 0:
            p = saved[i - 1]
            dx_res, dy, dmod[i][1], dmod[i][0], dlg, dlb, dgate = _bwd_boundary(
                dx_res, dh1, p["z2"], p["y2"], mod[i, 1], mod[i - 1, 5], ln2_g[i - 1:i], ln2_b[i - 1:i], alpha)
        else:
            grad_x, dmod[i][1], dmod[i][0] = _first_bwd(dx_res, dh1, x0, mod[i, 1])

    small = jnp.concatenate([jnp.concatenate([g.reshape(-1) for g in G[n]]) for n in SMALL])
    dmod_v = jnp.concatenate([jnp.concatenate([d.reshape(-1) for d in row]) for row in dmod])
    n_small, n_dmod = small.shape[0], dmod_v.shape[0]
    wblk = -(-(n_small + n_dmod) // (8 * LANE)) * LANE
    blk = jnp.pad(jnp.concatenate([dmod_v, small]), (0, 8 * wblk - n_small - n_dmod)).reshape(8, wblk)
    parts = _ag_small("gather_small", blk).reshape(N_DEV, 8, wblk)
    tot = _sum_devices("sum_small", parts).reshape(-1)
    g_ada_b = tot[:n_dmod].reshape(L, 6 * D)
    off = n_dmod
    Gs = {}
    for n in SMALL:
        Gs[n] = tot[off:off + W[n].size].reshape(W[n].shape)
        off += W[n].size
    dmod_all = parts.reshape(N_DEV, 8 * wblk)[:, :n_dmod].reshape(N_DEV, L, N_CHIPS, w_ada)
    dmod_sh = lax.dynamic_index_in_dim(dmod_all, chip, axis=2, keepdims=False)
    dmod_sh = jnp.pad(dmod_sh, ((0, 8), (0, 0), (0, 0)))
    dmod_sh = dmod_sh.transpose(1, 0, 2)

    last = (sum((q[0] for q in queue), ()), sum((q[1] for q in queue), []))
    got = _run_carry("exchange_last", _exchange_carry(last[1], [LAYOUT[n] for n in last[0]]))
    (g_ada_w, d_ada_w, m_ada_w2, v_ada_w2), _ = _adamw_ada(c_act, dmod_sh, ada_w, m_ada_w, v_ada_w)
    for q in queue:
        finish(q[0], q[1], got[:len(q[0])], q[2])
        got = got[len(q[0]):]
    Gb = dict(zip(BIG, _join_halves([red[n] for n in BIG])))
    grads = dict(Gb)
    grads.update(Gs)
    grads["ada_b"] = g_ada_b
    delta, new_m, new_v = {}, {}, {}
    grads["ada_w"], delta["ada_w"], new_m["ada_w"], new_v["ada_w"] = g_ada_w, d_ada_w, m_ada_w2, v_ada_w2
    for n in WEIGHTS:
        if n in SMALL or n in ("ada_b", "ada_w"):
            continue
        shp = W[n].shape
        grads[n], delta[n], new_m[n], new_v[n] = [r.reshape(shp) for r in _adamw(
            "adamw_" + n, _flat(W[n]), _flat(grads[n]), _flat(Mo[n]), _flat(Vo[n]))]
    names_s = SMALL + ("ada_b",)
    cat = lambda d: jnp.concatenate([d[n].reshape(-1) for n in names_s])
    n_s = sum(W[n].size for n in names_s)
    ws = -(-n_s // (8 * LANE)) * LANE
    pk = lambda d: jnp.pad(cat(d), (0, 8 * ws - n_s)).reshape(8, ws)
    _, ds, ms_, vs = _adamw("adamw_small", pk(W), pk(grads), pk(Mo), pk(Vo))
    off = 0
    for n in names_s:
        sz, shp = W[n].size, W[n].shape
        delta[n] = ds.reshape(-1)[off:off + sz].reshape(shp)
        new_m[n] = ms_.reshape(-1)[off:off + sz].reshape(shp)
        new_v[n] = vs.reshape(-1)[off:off + sz].reshape(shp)
        off += sz

    return (loss, grad_x[None], *[grads[n].reshape(W[n].shape) for n in WEIGHTS], *[delta[n] for n in WEIGHTS],
            *[new_m[n] for n in WEIGHTS], *[new_v[n] for n in WEIGHTS])
```

```python
import functools

import numpy as np
import jax
import jax.numpy as jnp
from jax import lax
from jax.experimental import pallas as pl
from jax.experimental.pallas import tpu as pltpu

F32 = jnp.float32
BF16 = jnp.bfloat16
MXU_DTYPE = jnp.bfloat16
WIRE_DTYPE = jnp.bfloat16

HEAD_DIM = 128
ROPE_DIM = 64
CHUNK = 64
ROPE_THETA = 10000.0
LN_EPS = 1e-5
RMS_EPS = 1e-6
ADAM_LR, ADAM_B1, ADAM_B2, ADAM_EPS, ADAM_WD, ADAM_STEP = 0.001, 0.9, 0.999, 1e-08, 0.01, 10

N_CHIPS = 4
N_DEV = 8
LANE = 128
VMEM_LIMIT = 56 * 1024 * 1024
MESH = pl.DeviceIdType.MESH
ANY = pl.BlockSpec(memory_space=pl.ANY)
NEG = -1e30


def _params(**kw):
    return pltpu.CompilerParams(vmem_limit_bytes=VMEM_LIMIT, **kw)


def _pick(n, cands):
    for c in cands:
        if n % c == 0:
            return c
    return n


def _sigmoid(x):
    return 1.0 / (1.0 + jnp.exp(-x))


def _mm(name, terms, M, N, out_dtypes, epilogue=None, extras=(), row_extras=(), tm=512, tn=512, carry=None):
    tm = _pick(M, (tm, 256, 128))
    tn = _pick(N, (tn, 896, 768, 640, 384, 256, 128))
    extras = tuple(extras) + tuple(row_extras)
    n_row = len(row_extras)
    n_terms, n_ex, n_out = len(terms), len(extras), len(out_dtypes)
    n_acc = 1 + max(t[4] for t in terms)
    flags = [(t[2], t[3], t[4]) for t in terms]

    gi, gj = M // tm, N // tn

    def body(*refs):
        refs, c_in, c_out, c_sems = _carry_split(refs, 2 * n_terms + n_ex, n_out, 0, carry)
        pi, pj = pl.program_id(0), pl.program_id(1)
        _carry_run(carry, c_in, c_out, c_sems, jnp.logical_and(pi == 0, pj == 0), True)
        accs = [None] * n_acc
        for k, (ta, tb, ai) in enumerate(flags):
            a = refs[2 * k][...].astype(MXU_DTYPE)
            b = refs[2 * k + 1][...].astype(MXU_DTYPE)
            dn = (((0 if ta else 1,), (1 if tb else 0,)), ((), ()))
            r = lax.dot_general(a, b, dn, preferred_element_type=F32)
            accs[ai] = r if accs[ai] is None else accs[ai] + r
        ex = [refs[2 * n_terms + k][...] for k in range(n_ex)]
        outs = epilogue(accs, *ex) if epilogue is not None else (accs[0],)
        for k in range(n_out):
            o_ref = refs[2 * n_terms + n_ex + k]
            o_ref[...] = outs[k].astype(o_ref.dtype)
        _carry_run(carry, c_in, c_out, c_sems, jnp.logical_and(pi == gi - 1, pj == gj - 1), False)

    in_specs, args = [], []
    for (a, b, ta, tb, _, bcol) in terms:
        K = a.shape[0] if ta else a.shape[1]
        in_specs.append(pl.BlockSpec((K, tm), lambda i, j: (0, i)) if ta
                        else pl.BlockSpec((tm, K), lambda i, j: (i, 0)))
        if isinstance(b, tuple):
            b, lyr = b
            in_specs.append(pl.BlockSpec((None, tn, K), lambda i, j, o=bcol, l=lyr: (l, j + o, 0)) if tb
                            else pl.BlockSpec((None, K, tn), lambda i, j, o=bcol, l=lyr: (l, 0, j + o)))
        else:
            in_specs.append(pl.BlockSpec((tn, K), lambda i, j, o=bcol: (j + o, 0)) if tb
                            else pl.BlockSpec((K, tn), lambda i, j, o=bcol: (0, j + o)))
        args += [a, b]
    for k, e in enumerate(extras):
        in_specs.append(pl.BlockSpec((tm, tn), (lambda i, j: (i, 0)) if k >= n_ex - n_row else (lambda i, j: (i, j))))
        args.append(e)
    c_is, c_os, c_shape, c_sems, c_alias = _carry_call_args(carry, len(args), n_out)
    outs = pl.pallas_call(
        body, name=name, grid=(gi, gj), in_specs=in_specs + c_is,
        out_specs=[pl.BlockSpec((tm, tn), lambda i, j: (i, j)) for _ in out_dtypes] + c_os,
        out_shape=[jax.ShapeDtypeStruct((M, N), d) for d in out_dtypes] + c_shape,
        input_output_aliases=c_alias, scratch_shapes=c_sems,
        compiler_params=_params(),
    )(*args, *(carry["ins"] if carry else ()))
    return (outs[:n_out], list(outs[n_out:])) if carry else outs


def _wdim(b, axis):
    return b[0].shape[1 + axis] if isinstance(b, tuple) else b.shape[axis]


def _mm1(name, a, b, ta=False, tb=False, out_dtype=F32, bcol=0, N=None, **kw):
    M = a.shape[1] if ta else a.shape[0]
    if N is None:
        N = _wdim(b, 0 if tb else 1)
    res = _mm(name, [(a, b, ta, tb, 0, bcol)], M, N, [out_dtype], **kw)
    return (res[0][0], res[1]) if kw.get("carry") else res[0]


def _rowwise(name, fn, tiled, vecs, outs, reds=(), tr=128):
    R = tiled[0].shape[0]
    tr = _pick(R, (tr, 64, 32, 16, 8))
    nt, nv, no, nr = len(tiled), len(vecs), len(outs), len(reds)

    def body(*refs):
        vals = [r[...] for r in refs[:nt + nv]]
        res = fn(*vals)
        for k in range(no):
            o_ref = refs[nt + nv + k]
            o_ref[...] = res[k].astype(o_ref.dtype)
        if nr:
            first = pl.program_id(0) == 0
            for k in range(nr):
                r_ref = refs[nt + nv + no + k]

                @pl.when(first)
                def _(r_ref=r_ref, v=res[no + k]):
                    r_ref[...] = v

                @pl.when(jnp.logical_not(first))
                def _(r_ref=r_ref, v=res[no + k]):
                    r_ref[...] += v

    in_specs = [pl.BlockSpec((tr, t.shape[1]), lambda i: (i, 0)) for t in tiled]
    in_specs += [pl.BlockSpec(v.shape, lambda i, n=v.ndim: (0,) * n) for v in vecs]
    out_specs = [pl.BlockSpec((tr, w), lambda i: (i, 0)) for (w, _) in outs]
    out_specs += [pl.BlockSpec((1, w), lambda i: (0, 0)) for w in reds]
    out_shape = [jax.ShapeDtypeStruct((R, w), d) for (w, d) in outs]
    out_shape += [jax.ShapeDtypeStruct((1, w), F32) for w in reds]
    return pl.pallas_call(
        body, name=name, grid=(R // tr,), in_specs=in_specs, out_specs=out_specs, out_shape=out_shape,
        compiler_params=_params(),
    )(*tiled, *vecs)


def _colsum(v):
    return jnp.sum(v, axis=0, keepdims=True)


def _ln_stats(z):
    mu = jnp.mean(z, axis=-1, keepdims=True)
    zc = z - mu
    var = jnp.mean(zc * zc, axis=-1, keepdims=True)
    rstd = lax.rsqrt(var + LN_EPS)
    return zc * rstd, rstd


def _ln_bwd(dout, xhat, rstd, lg):
    dxh = dout * lg
    m1 = jnp.mean(dxh, axis=-1, keepdims=True)
    m2 = jnp.mean(dxh * xhat, axis=-1, keepdims=True)
    return rstd * (dxh - m1 - xhat * m2)


def _modulate(x, sc, sh):
    D = x.shape[1]
    return _rowwise("modulate", lambda x, sc, sh: ((x * (1.0 + sc) + sh),), [x], [sc, sh], [(D, MXU_DTYPE)])[0]


def _resid_ln_mod(x, y, g, lg, lb, sc_n, sh_n, alpha):
    D = x.shape[1]

    def fn(x, y, g, lg, lb, sc, sh):
        z = alpha * x + (1.0 + g) * y
        xhat, _ = _ln_stats(z)
        xo = xhat * lg + lb
        return z, xo, xo * (1.0 + sc) + sh

    return _rowwise("resid_ln_mod", fn, [x, y], [g, lg, lb, sc_n, sh_n], [(D, F32), (D, F32), (D, MXU_DTYPE)])


def _final_ln_loss(x, y, tgt, g, lg, lb, alpha):
    D = x.shape[1]

    def fn(x, y, t, g, lg, lb):
        z = alpha * x + (1.0 + g) * y
        xhat, rstd = _ln_stats(z)
        out = xhat * lg + lb
        err = out - t
        loss = jnp.sum(jnp.sum(err * err, axis=-1, keepdims=True), axis=0, keepdims=True)
        dout = err * (1.0 / D)
        dz = _ln_bwd(dout, xhat, rstd, lg)
        return (alpha * dz, (1.0 + g) * dz, jnp.broadcast_to(loss, (1, LANE)),
                _colsum(dout * xhat), _colsum(dout), _colsum(dz * y))

    return _rowwise("final_ln_loss", fn, [x, y, tgt], [g, lg, lb], [(D, F32), (D, MXU_DTYPE)], [LANE, D, D, D])


def _bwd_boundary(dx_res, dh, z_p, y_p, sc, g_p, lg_p, lb_p, alpha):
    D = dh.shape[1]

    def fn(dxr, dh, z, y, sc, g, lg, lb):
        xhat, rstd = _ln_stats(z)
        x_in = xhat * lg + lb
        dx = dxr + dh * (1.0 + sc)
        dz = _ln_bwd(dx, xhat, rstd, lg)
        return (alpha * dz, (1.0 + g) * dz,
                _colsum(dh * x_in), _colsum(dh), _colsum(dx * xhat), _colsum(dx), _colsum(dz * y))

    return _rowwise("bwd_boundary", fn, [dx_res, dh, z_p, y_p], [sc, g_p, lg_p, lb_p],
                    [(D, F32), (D, MXU_DTYPE)], [D, D, D, D, D])


def _first_bwd(dx_res, dh, x, sc):
    D = dh.shape[1]

    def fn(dxr, dh, x, sc):
        return dxr + dh * (1.0 + sc), _colsum(dh * x), _colsum(dh)

    return _rowwise("first_bwd", fn, [dx_res, dh, x], [sc], [(D, F32)], [D, D])


def _ride(ride, phase, bufs=None):
    if ride is None:
        return None
    bufs = ride["bufs"] if bufs is None else bufs
    n = ride.get("n_ici", len(bufs)) if phase == "ici" else len(bufs)
    return _gather_carry(bufs[:n], ride["lays"][:n], [0] * n, phase)


def _ride_on(ride, got):
    return list(got) + list(ride["bufs"][len(got):]) if ride else []


def _ffn_fwd(h, w1, w3, w2, ride=None, early=None):
    S, F = h.shape[0], _wdim(w1, 1)

    def epi(accs):
        a, b = accs
        return a, b, a * _sigmoid(a) * b

    res = _mm("ffn_up", [(h, w1, False, False, 0, 0), (h, w3, False, False, 1, 0)], S, F,
              [MXU_DTYPE, MXU_DTYPE, MXU_DTYPE], epilogue=epi, carry=_ride(ride, "ici"))
    (a, b, u), bufs = res if ride else (res, [])
    c_d2d, c_ici = _ride(ride, "d2d", bufs), _ride(early, "ici")
    carry = _merge_carries(c_d2d, c_ici)
    res = _mm1("ffn_down", u, w2, carry=carry)
    y, got = res if carry else (res, [])
    return y, (a, b, u), got[:len(bufs)], got[len(bufs):]


def _ffn_bwd(dy, h, saved, w1, w3, w2, carry_du=None, carry_dh=None, swap_of=None):
    a, b, u = saved
    S, F = a.shape
    D = h.shape[1]

    def epi(accs, a, b):
        du = accs[0]
        a = a.astype(F32)
        b = b.astype(F32)
        sg = _sigmoid(a)
        return du * b * (sg * (1.0 + a * (1.0 - sg))), du * (a * sg)

    res = _mm("ffn_du", [(dy, w2, False, True, 0, 0)], S, F, [MXU_DTYPE, MXU_DTYPE], epilogue=epi, extras=(a, b),
              carry=carry_du)
    (da, db), got_du = res if carry_du else (res, [])
    dw2 = _mm1("ffn_dw2", u, dy, ta=True, out_dtype=WIRE_DTYPE)
    dw1, dw3 = _mm("ffn_dw13", [(h, da, True, False, 0, 0), (h, db, True, False, 1, 0)], D, F,
                   [WIRE_DTYPE, WIRE_DTYPE], epilogue=lambda accs: (accs[0], accs[1]))
    carry = _merge_carries(carry_dh, swap_of((dw1, dw3, dw2)) if swap_of else None)
    res = _mm("ffn_dh", [(da, w1, False, True, 0, 0), (db, w3, False, True, 0, 0)], S, D, [F32], tn=256, carry=carry)
    (dh,), got = res if carry else (res, [])
    n_dh = len(carry_dh["out_shape"]) if carry_dh else 0
    return dh, dw1, dw3, dw2, got_du, got[:n_dh], got[n_dh:]


def _rope_tables(pos):
    j = np.arange(LANE)
    invf = ROPE_THETA ** (-jnp.arange(0, ROPE_DIM, 2, dtype=F32) / ROPE_DIM)
    invf = invf[(j % ROPE_DIM) // 2].reshape(1, LANE)
    sgn = jnp.asarray(np.where(j % 2 == 0, -1.0, 1.0).reshape(1, LANE), F32)

    def fn(pos, invf, sgn):
        ang = pos.astype(F32) * invf
        return jnp.cos(ang), jnp.sin(ang) * sgn

    return _rowwise("rope_tables", fn, [pos], [invf, sgn], [(LANE, F32), (LANE, F32)], tr=256)


def _pair_swap(x):
    w = x.shape[1]
    even = (lax.broadcasted_iota(jnp.int32, x.shape, 1) % 2) == 0
    return jnp.where(even, pltpu.roll(x, w - 1, 1), pltpu.roll(x, 1, 1))


def _rope_fwd(x, c, s):
    return x * c + _pair_swap(x) * s


def _rope_bwd(d, c, s):
    return d * c + _pair_swap(d * s)


ATT_T = 512


def _dot_nt(a, b):
    return lax.dot_general(a, b, (((1,), (1,)), ((), ())), preferred_element_type=F32)


def _dot_tn(a, b):
    return lax.dot_general(a, b, (((0,), (0,)), ((), ())), preferred_element_type=F32)


def _dot_nn(a, b):
    return lax.dot_general(a, b, (((1,), (0,)), ((), ())), preferred_element_type=F32)


def _diag_mask(T, gran):
    r = lax.broadcasted_iota(jnp.int32, (T, T), 0)
    c = lax.broadcasted_iota(jnp.int32, (T, T), 1)
    if gran > 1:
        sh = int(np.log2(gran))
        r, c = lax.shift_right_logical(r, sh), lax.shift_right_logical(c, sh)
    return r >= c


def _attn_specs(S, H, T, mla, col_q, col_k, col_v):
    W = 2 * HEAD_DIM
    specs = [pl.BlockSpec((T, W), lambda p, i: (i, col_q + p))]
    if mla:
        specs.append(pl.BlockSpec((T, 2 * ROPE_DIM), lambda p, i: (i, p)))
    specs.append(pl.BlockSpec((S, W), lambda p, i: (0, col_k + p)))
    if mla:
        specs.append(pl.BlockSpec((S, ROPE_DIM), lambda p, i: (0, 0)))
    specs.append(pl.BlockSpec((S, W), lambda p, i: (0, col_v + p)))
    if not mla:
        specs.append(pl.BlockSpec((2, T, 1), lambda p, i: (p, i, 0)))
        specs.append(pl.BlockSpec((2, 1, S), lambda p, i: (p, 0, 0)))
    return specs


def _attn_fwd(name, S, H, mla, q, k, v, q_pe=None, k_pe=None, cum_col=None, cum_row=None, cols=(0, 0, 0),
              carry=None):
    T = _pick(S, (ATT_T, 128))
    nq = S // T
    scale = (HEAD_DIM + ROPE_DIM) ** -0.5 if mla else HEAD_DIM ** -0.5
    gran = CHUNK if mla else 1

    def body(*refs):
        refs, c_in, c_out, c_sems = _carry_split(refs, 5, 3, 3, carry)
        if mla:
            q_ref, qpe_ref, k_ref, kpe_ref, v_ref, o_ref, of_ref, lse_ref, m_s, l_s, acc_s = refs
        else:
            q_ref, k_ref, v_ref, cc_ref, cr_ref, o_ref, of_ref, lse_ref, m_s, l_s, acc_s = refs
        hp, qi = pl.program_id(0), pl.program_id(1)
        _carry_run(carry, c_in, c_out, c_sems, jnp.logical_and(hp == 0, qi == 0), True)
        hls = [slice(hh * HEAD_DIM, (hh + 1) * HEAD_DIM) for hh in range(2)]
        qn = [q_ref[:, hl] for hl in hls]
        qp = [qpe_ref[:, hh * ROPE_DIM:(hh + 1) * ROPE_DIM] for hh in range(2)] if mla else None
        m_s[...] = jnp.full(m_s.shape, NEG, F32)
        l_s[...] = jnp.zeros(l_s.shape, F32)
        acc_s[...] = jnp.zeros(acc_s.shape, F32)

        def step(j, masked):
            rows = pl.ds(pl.multiple_of(j * T, T), T)
            for hh, hl in enumerate(hls):
                s = _dot_nt(qn[hh], k_ref[rows, hl])
                if mla:
                    s = s + _dot_nt(qp[hh], kpe_ref[rows, :])
                s = s * scale
                if not mla:
                    s = s + (cc_ref[hh] - cr_ref[hh, :, rows])
                if masked:
                    s = jnp.where(_diag_mask(T, gran), s, NEG)
                m_old = m_s[hh, :, 0:1]
                m_new = jnp.maximum(m_old, jnp.max(s, axis=-1, keepdims=True))
                p = jnp.exp(s - m_new)
                corr = jnp.exp(m_old - m_new)
                l_s[hh] = jnp.broadcast_to(corr * l_s[hh, :, 0:1] + jnp.sum(p, axis=-1, keepdims=True), (T, LANE))
                p_hi = p.astype(MXU_DTYPE)
                p_lo = (p - p_hi.astype(F32)).astype(MXU_DTYPE)
                vf = v_ref[rows, hl]
                acc_s[hh] = corr * acc_s[hh] + (_dot_nn(p_hi, vf) + _dot_nn(p_lo, vf))
                m_s[hh] = jnp.broadcast_to(m_new, (T, LANE))

        lax.fori_loop(0, qi, lambda j, c: (step(j, False), c)[1], 0)
        step(qi, True)
        for hh, hl in enumerate(hls):
            l = l_s[hh, :, 0:1]
            of = acc_s[hh] / l
            of_ref[:, hl] = of
            o_ref[:, hl] = of.astype(o_ref.dtype)
            lse_ref[hh] = jnp.broadcast_to(m_s[hh, :, 0:1] + jnp.log(l), (T, LANE))
        _carry_run(carry, c_in, c_out, c_sems, jnp.logical_and(hp == H // 2 - 1, qi == nq - 1), False)

    args = [q] + ([q_pe] if mla else []) + [k] + ([k_pe] if mla else []) + [v]
    if not mla:
        args += [cum_col, cum_row]
    c_is, c_os, c_shape, c_sems, c_alias = _carry_call_args(carry, 5, 3)
    res = pl.pallas_call(
        body, name=name, grid=(H // 2, nq),
        in_specs=_attn_specs(S, H, T, mla, *cols) + c_is,
        out_specs=[pl.BlockSpec((T, 2 * HEAD_DIM), lambda p, i: (i, p)),
                   pl.BlockSpec((T, 2 * HEAD_DIM), lambda p, i: (i, p)),
                   pl.BlockSpec((2, T, LANE), lambda p, i: (p, i, 0))] + c_os,
        out_shape=[jax.ShapeDtypeStruct((S, H * HEAD_DIM), MXU_DTYPE), jax.ShapeDtypeStruct((S, H * HEAD_DIM), F32),
                   jax.ShapeDtypeStruct((H, S, LANE), F32)] + c_shape,
        input_output_aliases=c_alias,
        scratch_shapes=[pltpu.VMEM((2, T, LANE), F32), pltpu.VMEM((2, T, LANE), F32),
                        pltpu.VMEM((2, T, HEAD_DIM), F32)] + c_sems,
        compiler_params=_params(),
    )(*args, *(carry["ins"] if carry else ()))
    return res[0], res[1], res[2], list(res[3:])


def _carry_split(refs, n_in, n_out, n_scr, carry):
    if carry is None:
        return refs, (), (), ()
    ci, co = len(carry["ins"]), len(carry["out_shape"])
    own = refs[:n_in] + refs[n_in + ci:n_in + ci + n_out] + refs[n_in + ci + n_out + co:n_in + ci + n_out + co + n_scr]
    return (own, refs[n_in:n_in + ci], refs[n_in + ci + n_out:n_in + ci + n_out + co],
            refs[n_in + ci + n_out + co + n_scr:])


def _carry_run(carry, c_in, c_out, c_sems, when, start):
    if carry is None:
        return

    @pl.when(when)
    def _():
        for cp in carry["copies"](c_in, c_out, *c_sems):
            if start:
                cp.start()
            else:
                cp.wait()


def _carry_call_args(carry, n_in, n_out):
    if carry is None:
        return [], [], [], [], {}
    sems = [pltpu.SemaphoreType.DMA(carry["sems"]), pltpu.SemaphoreType.DMA(carry["sems"])]
    alias = {n_in + i: n_out + o for i, o in carry.get("alias", {}).items()}
    return [ANY] * len(carry["ins"]), [ANY] * len(carry["out_shape"]), list(carry["out_shape"]), sems, alias


def _attn_bwd(name, S, H, mla, q, k, v, do, of, lse, q_pe=None, k_pe=None, cum_col=None, cum_row=None,
              cols=(0, 0, 0), carry=None):
    T = _pick(S, (ATT_T, 128))
    nq = S // T
    scale = (HEAD_DIM + ROPE_DIM) ** -0.5 if mla else HEAD_DIM ** -0.5
    gran = CHUNK if mla else 1
    dqk = HEAD_DIM + (ROPE_DIM if mla else 0)

    def body(*refs):
        refs, c_in, c_out, c_sems = _carry_split(refs, 8, 5, 2, carry)
        if mla:
            (q_ref, qpe_ref, k_ref, kpe_ref, v_ref, do_ref, of_ref, lse_ref,
             dq_ref, dk_ref, dv_ref, dqpe_ref, dkpe_ref, dq_s, r_s) = refs
        else:
            (q_ref, k_ref, v_ref, cc_ref, cr_ref, do_ref, of_ref, lse_ref,
             dq_ref, dk_ref, dv_ref, dck_ref, dcq_ref, dq_s, r_s) = refs
        hp, qi = pl.program_id(0), pl.program_id(1)
        _carry_run(carry, c_in, c_out, c_sems, jnp.logical_and(hp == 0, qi == 0), True)

        @pl.when(qi == 0)
        def _():
            dk_ref[...] = jnp.zeros(dk_ref.shape, F32)
            dv_ref[...] = jnp.zeros(dv_ref.shape, F32)
            if not mla:
                dck_ref[...] = jnp.zeros(dck_ref.shape, F32)

        if mla:
            @pl.when(jnp.logical_and(qi == 0, hp == 0))
            def _():
                dkpe_ref[...] = jnp.zeros(dkpe_ref.shape, F32)

        hls = [slice(hh * HEAD_DIM, (hh + 1) * HEAD_DIM) for hh in range(2)]
        qn = [q_ref[:, hl] for hl in hls]
        qp = [qpe_ref[:, hh * ROPE_DIM:(hh + 1) * ROPE_DIM] for hh in range(2)] if mla else None
        dof = [do_ref[:, hl] for hl in hls]
        delta = [jnp.sum(dof[hh].astype(F32) * of_ref[:, hl], axis=-1, keepdims=True) for hh, hl in enumerate(hls)]
        lse = [lse_ref[hh][:, 0:1] for hh in range(2)]
        dq_s[...] = jnp.zeros(dq_s.shape, F32)
        r_s[...] = jnp.zeros(r_s.shape, F32)

        def step(j, masked):
            rows = pl.ds(pl.multiple_of(j * T, T), T)
            for hh, hl in enumerate(hls):
                kn = k_ref[rows, hl]
                s = _dot_nt(qn[hh], kn)
                if mla:
                    kp = kpe_ref[rows, :]
                    s = s + _dot_nt(qp[hh], kp)
                s = s * scale
                if not mla:
                    s = s + (cc_ref[hh] - cr_ref[hh, :, rows])
                if masked:
                    s = jnp.where(_diag_mask(T, gran), s, NEG)
                p = jnp.exp(s - lse[hh])
                dp = _dot_nt(dof[hh], v_ref[rows, hl])
                ds = p * (dp - delta[hh])
                dv_ref[rows, hl] += _dot_tn(p.astype(MXU_DTYPE), dof[hh])
                dsb = (ds * scale).astype(MXU_DTYPE)
                dk_ref[rows, hl] += _dot_tn(dsb, qn[hh])
                dq_s[hh, :, :HEAD_DIM] += _dot_nn(dsb, kn)
                if mla:
                    dkpe_ref[rows, :] += _dot_tn(dsb, qp[hh])
                    dq_s[hh, :, HEAD_DIM:] += _dot_nn(dsb, kp)
                else:
                    dck_ref[hh, :, rows] -= jnp.sum(ds, axis=0, keepdims=True)
                    r_s[hh] += jnp.broadcast_to(jnp.sum(ds, axis=-1, keepdims=True), (T, LANE))

        lax.fori_loop(0, qi, lambda j, c: (step(j, False), c)[1], 0)
        step(qi, True)
        for hh, hl in enumerate(hls):
            dq_ref[:, hl] = dq_s[hh, :, :HEAD_DIM].astype(dq_ref.dtype)
            if mla:
                dqpe_ref[:, hh * ROPE_DIM:(hh + 1) * ROPE_DIM] = dq_s[hh, :, HEAD_DIM:]
            else:
                dcq_ref[hh] = r_s[hh, :, 0:1]
        _carry_run(carry, c_in, c_out, c_sems, jnp.logical_and(hp == H // 2 - 1, qi == nq - 1), False)

    W = 2 * HEAD_DIM
    args = [q] + ([q_pe] if mla else []) + [k] + ([k_pe] if mla else []) + [v]
    if not mla:
        args += [cum_col, cum_row]
    args += [do, of, lse]
    in_specs = _attn_specs(S, H, T, mla, *cols)
    in_specs += [pl.BlockSpec((T, W), lambda p, i: (i, p)), pl.BlockSpec((T, W), lambda p, i: (i, p)),
                 pl.BlockSpec((2, T, LANE), lambda p, i: (p, i, 0))]
    out_specs = [pl.BlockSpec((T, W), lambda p, i: (i, p)), pl.BlockSpec((S, W), lambda p, i: (0, p)),
                 pl.BlockSpec((S, W), lambda p, i: (0, p))]
    out_shape = [jax.ShapeDtypeStruct((S, H * HEAD_DIM), MXU_DTYPE), jax.ShapeDtypeStruct((S, H * HEAD_DIM), F32),
                 jax.ShapeDtypeStruct((S, H * HEAD_DIM), F32)]
    if mla:
        out_specs += [pl.BlockSpec((T, 2 * ROPE_DIM), lambda p, i: (i, p)),
                      pl.BlockSpec((S, ROPE_DIM), lambda p, i: (0, 0))]
        out_shape += [jax.ShapeDtypeStruct((S, H * ROPE_DIM), F32), jax.ShapeDtypeStruct((S, ROPE_DIM), F32)]
    else:
        out_specs += [pl.BlockSpec((2, 1, S), lambda p, i: (p, 0, 0)), pl.BlockSpec((2, T, 1), lambda p, i: (p, i, 0))]
        out_shape += [jax.ShapeDtypeStruct((H, 1, S), F32), jax.ShapeDtypeStruct((H, S, 1), F32)]
    assert len(args) == 8 and len(out_shape) == 5
    c_is, c_os, c_shape, c_sems, c_alias = _carry_call_args(carry, 8, 5)
    res = pl.pallas_call(
        body, name=name, grid=(H // 2, nq), in_specs=in_specs + c_is, out_specs=out_specs + c_os,
        out_shape=out_shape + c_shape, input_output_aliases=c_alias,
        scratch_shapes=[pltpu.VMEM((2, T, dqk), F32), pltpu.VMEM((2, T, LANE), F32)] + c_sems,
        compiler_params=_params(),
    )(*args, *(carry["ins"] if carry else ()))
    return (res[:5], res[5:]) if carry else (res, [])


def _mla_prep(lat, cos, sin, qn, kvn, ql, kvl):
    def fn(lat, c, s, qn, kvn):
        ql_ = lat[:, :ql]
        kv_ = lat[:, ql:ql + kvl]
        kp = lat[:, ql + kvl:]
        cq = ql_ * lax.rsqrt(jnp.mean(ql_ * ql_, axis=-1, keepdims=True) + RMS_EPS) * qn
        ckv = kv_ * lax.rsqrt(jnp.mean(kv_ * kv_, axis=-1, keepdims=True) + RMS_EPS) * kvn
        kp2 = jnp.concatenate([kp, jnp.zeros_like(kp)], axis=-1)
        kr = _rope_fwd(kp2, c, s)[:, :ROPE_DIM]
        return cq, ckv, kr

    return _rowwise("mla_prep", fn, [lat, cos, sin], [qn, kvn],
                    [(ql, MXU_DTYPE), (kvl, MXU_DTYPE), (ROPE_DIM, MXU_DTYPE)])


def _mla_prep_bwd(lat, cos, sin, qn, kvn, dcq, dckv, dkr, ql, kvl):
    def fn(lat, c, s, dcq, dckv, dkr, qn, kvn):
        outs, reds = [], []
        for (x, g, d) in ((lat[:, :ql], qn, dcq), (lat[:, ql:ql + kvl], kvn, dckv)):
            r = lax.rsqrt(jnp.mean(x * x, axis=-1, keepdims=True) + RMS_EPS)
            n = x * r
            dn = d * g
            outs.append(r * (dn - n * jnp.mean(dn * n, axis=-1, keepdims=True)))
            reds.append(_colsum(d * n))
        d2 = jnp.concatenate([dkr, jnp.zeros_like(dkr)], axis=-1)
        outs.append(_rope_bwd(d2, c, s)[:, :ROPE_DIM])
        return (jnp.concatenate(outs, axis=-1), *reds)

    return _rowwise("mla_prep_bwd", fn, [lat, cos, sin, dcq, dckv, dkr], [qn, kvn],
                    [(ql + kvl + ROPE_DIM, MXU_DTYPE)], [ql, kvl])


def _mla_fwd(h, w, rc, ride=None):
    S = h.shape[0]
    ql, kvl = w["q_norm"].shape[1], w["kv_norm"].shape[1]
    H = _wdim(w["w_uk"], 1) // HEAD_DIM
    n_nope, n_pe = H * HEAD_DIM, H * ROPE_DIM
    lat = _mm1("mla_down", h, w["w_down"])
    cq, ckv, kr = _mla_prep(lat, rc[0], rc[1], w["q_norm"], w["kv_norm"], ql, kvl)
    q_nope = _mm1("mla_uq_nope", cq, w["w_uq"], out_dtype=MXU_DTYPE, N=n_nope)
    q_pe = _mm("mla_uq_pe", [(cq, w["w_uq"], False, False, 0, n_nope // LANE)], S, n_pe, [MXU_DTYPE],
               epilogue=lambda accs, c, s: (_rope_fwd(accs[0], c, s),), row_extras=rc, tn=LANE)[0]
    k_nope, v = _mm("mla_ukv", [(ckv, w["w_uk"], False, False, 0, 0), (ckv, w["w_uv"], False, False, 1, 0)],
                    S, n_nope, [MXU_DTYPE, MXU_DTYPE], epilogue=lambda accs: (accs[0], accs[1]))
    o, of, lse, bufs = _attn_fwd("mla_attn_fwd", S, H, True, q_nope, k_nope, v, q_pe=q_pe, k_pe=kr,
                                 carry=_ride(ride, "ici"))
    res = _mm1("mla_wo", o, w["w_o"], carry=_ride(ride, "d2d", _ride_on(ride, bufs)))
    y, bufs = res if ride else (res, [])
    return y, (lat, cq, ckv, kr, q_nope, q_pe, k_nope, v, o, of, lse), bufs


def _mla_bwd(dy, h, saved, w, rc, carry):
    lat, cq, ckv, kr, q_nope, q_pe, k_nope, v, o, of, lse = saved
    S = h.shape[0]
    ql, kvl = w["q_norm"].shape[1], w["kv_norm"].shape[1]
    H = _wdim(w["w_uk"], 1) // HEAD_DIM
    n_nope, n_pe = H * HEAD_DIM, H * ROPE_DIM
    do = _mm1("mla_do", dy, w["w_o"], tb=True, out_dtype=MXU_DTYPE)
    dw_o = _mm1("mla_dwo", o, dy, ta=True, out_dtype=WIRE_DTYPE)
    (dq_nope, dk_nope, dv, dq_pe_r, dk_pe_r), carried = _attn_bwd(
        "mla_attn_bwd", S, H, True, q_nope, k_nope, v, do, of, lse, q_pe=q_pe, k_pe=kr, carry=carry)

    def unrope(d, c, s):
        reps = (1, n_pe // LANE)
        return (_rope_bwd(d, jnp.tile(c, reps), jnp.tile(s, reps)),)

    dq_pe = _rowwise("mla_unrope_q", unrope, [dq_pe_r, rc[0], rc[1]], [], [(n_pe, MXU_DTYPE)])[0]
    dq = jnp.concatenate([dq_nope, dq_pe], axis=1)
    dw_uq = _mm1("mla_dwuq", cq, dq, ta=True, out_dtype=WIRE_DTYPE)
    dcq = _mm1("mla_dcq", dq, w["w_uq"], tb=True)
    dw_uk, dw_uv = _mm("mla_dwukv", [(ckv, dk_nope, True, False, 0, 0), (ckv, dv, True, False, 1, 0)], kvl, n_nope,
                       [WIRE_DTYPE, WIRE_DTYPE], epilogue=lambda accs: (accs[0], accs[1]))
    dckv = _mm("mla_dckv", [(dk_nope, w["w_uk"], False, True, 0, 0), (dv, w["w_uv"], False, True, 0, 0)],
               S, kvl, [F32])[0]
    dlat, dqn, dkvn = _mla_prep_bwd(lat, rc[0], rc[1], w["q_norm"], w["kv_norm"], dcq, dckv, dk_pe_r, ql, kvl)
    dw_down = _mm1("mla_dwdown", h, dlat, ta=True, out_dtype=WIRE_DTYPE)
    dh = _mm1("mla_dh", dlat, w["w_down"], tb=True)
    return dh, dict(w_down=dw_down, q_norm=dqn, w_uq=dw_uq, kv_norm=dkvn, w_uk=dw_uk, w_uv=dw_uv, w_o=dw_o), carried


def _log_sigmoid(z):
    return jnp.minimum(z, 0.0) - jnp.log(1.0 + jnp.exp(-jnp.abs(z)))


def _fox_gate_fwd(f, bf):
    S = f.shape[0]
    B = LANE

    def body(f_ref, b_ref, cum_ref):
        r = lax.broadcasted_iota(jnp.int32, (B, B), 0)
        c = lax.broadcasted_iota(jnp.int32, (B, B), 1)
        tri = (r >= c).astype(F32)
        carry = jnp.zeros((1, LANE), F32)
        for blk in range(S // B):
            rows = slice(blk * B, (blk + 1) * B)
            lf = _log_sigmoid(f_ref[rows, :] + b_ref[...])
            cs = jnp.dot(tri, lf, precision=lax.Precision.HIGHEST, preferred_element_type=F32) + carry
            cum_ref[rows, :] = cs
            carry = cs[B - 1:B, :]

    return pl.pallas_call(body, name="fox_gate_fwd", out_shape=jax.ShapeDtypeStruct((S, LANE), F32),
                          compiler_params=_params())(f, bf)


def _fox_gate_bwd(dcum, f, bf):
    S = f.shape[0]
    B = LANE

    def body(d_ref, f_ref, b_ref, df_ref, db_ref):
        r = lax.broadcasted_iota(jnp.int32, (B, B), 0)
        c = lax.broadcasted_iota(jnp.int32, (B, B), 1)
        tri = (r <= c).astype(F32)
        carry = jnp.zeros((1, LANE), F32)
        db = jnp.zeros((1, LANE), F32)
        for blk in reversed(range(S // B)):
            rows = slice(blk * B, (blk + 1) * B)
            dlf = jnp.dot(tri, d_ref[rows, :], precision=lax.Precision.HIGHEST, preferred_element_type=F32) + carry
            carry = dlf[0:1, :]
            z = f_ref[rows, :] + b_ref[...]
            dz = dlf * _sigmoid(-z)
            df_ref[rows, :] = dz.astype(df_ref.dtype)
            db = db + jnp.sum(dz, axis=0, keepdims=True)
        db_ref[...] = db

    return pl.pallas_call(body, name="fox_gate_bwd",
                          out_shape=[jax.ShapeDtypeStruct((S, LANE), MXU_DTYPE), jax.ShapeDtypeStruct((1, LANE), F32)],
                          compiler_params=_params())(dcum, f, bf)


def _fox_fwd(h, w, ride=None):
    S, D = h.shape
    H = D // HEAD_DIM
    qkv = _mm1("fox_qkv", h, w["w_in"], out_dtype=MXU_DTYPE, N=3 * D)
    f = _mm1("fox_f", h, w["w_in"], bcol=3 * D // LANE, N=LANE, tn=LANE)
    cum = _fox_gate_fwd(f, w["b_f"])
    cumT = cum[:, :H].T
    cum_col, cum_row = cumT.reshape(H, S, 1), cumT.reshape(H, 1, S)
    nb = D // (2 * HEAD_DIM)
    o, of, lse, bufs = _attn_fwd("fox_attn_fwd", S, H, False, qkv, qkv, qkv, cum_col=cum_col, cum_row=cum_row,
                                 cols=(0, nb, 2 * nb), carry=_ride(ride, "ici"))
    res = _mm1("fox_wo", o, w["w_o"], carry=_ride(ride, "d2d", _ride_on(ride, bufs)))
    y, bufs = res if ride else (res, [])
    return y, (qkv, f, cum_col, cum_row, o, of, lse), bufs


def _fox_bwd(dy, h, saved, w, carry):
    qkv, f, cum_col, cum_row, o, of, lse = saved
    S, D = h.shape
    H = D // HEAD_DIM
    nb = D // (2 * HEAD_DIM)
    do = _mm1("fox_do", dy, w["w_o"], tb=True, out_dtype=MXU_DTYPE)
    dw_o = _mm1("fox_dwo", o, dy, ta=True, out_dtype=WIRE_DTYPE)
    (dq, dk, dv, dck, dcq), carried = _attn_bwd("fox_attn_bwd", S, H, False, qkv, qkv, qkv, do, of, lse,
                                                cum_col=cum_col, cum_row=cum_row, cols=(0, nb, 2 * nb), carry=carry)
    dcum = jnp.pad((dck.reshape(H, S) + dcq.reshape(H, S)).T, ((0, 0), (0, LANE - H)))
    df, dbf = _fox_gate_bwd(dcum, f, w["b_f"])
    dproj = jnp.concatenate([dq, dk.astype(MXU_DTYPE), dv.astype(MXU_DTYPE), df], axis=1)
    dw_in = _mm1("fox_dwin", h, dproj, ta=True, out_dtype=WIRE_DTYPE)
    dh = _mm1("fox_dh", dproj, w["w_in"], tb=True, tn=256)
    return dh, dict(w_in=dw_in, b_f=dbf[:, :H], w_o=dw_o), carried


def _place():
    x, y, c = lax.axis_index("x"), lax.axis_index("y"), lax.axis_index("c")
    return x, y, c, [(1 - x, y), (x, 1 - y), (1 - x, 1 - y)]


def _ag_small(name, blk):
    m, n = blk.shape

    def body(x_ref, out_ref, send_sems, recv_sems, local_sem):
        x, y, c, chips = _place()
        me, sibling = (x, y, c), (x, y, 1 - c)

        def rows(px, py, pc):
            return out_ref.at[pl.ds((4 * px + 2 * py + pc) * m, m), :]

        def copy(k, block, to, src=None):
            return pltpu.make_async_remote_copy(
                src_ref=rows(*block) if src is None else src, dst_ref=rows(*block),
                send_sem=send_sems.at[k], recv_sem=recv_sems.at[k], device_id=to, device_id_type=MESH)

        mine = pltpu.make_async_copy(x_ref, rows(*me), local_sem)
        mine.start()
        first = [copy(0, me, sibling, src=x_ref)]
        first += [copy(1 + j, me, (*chip, c), src=x_ref) for j, chip in enumerate(chips)]
        for cp in first:
            cp.start()
        passed = [copy(4 + j, (*chip, c), sibling) for j, chip in enumerate(chips)]
        for j, chip in enumerate(chips):
            copy(1 + j, (*chip, c), me).wait_recv()
            passed[j].start()
        copy(0, sibling, me).wait_recv()
        for j, chip in enumerate(chips):
            copy(4 + j, (*chip, 1 - c), me).wait_recv()
        for cp in first + passed:
            cp.wait_send()
        mine.wait()

    return pl.pallas_call(
        body, name=name, out_shape=jax.ShapeDtypeStruct((N_DEV * m, n), blk.dtype),
        in_specs=[pl.BlockSpec(memory_space=pltpu.VMEM)], out_specs=pl.BlockSpec(memory_space=pltpu.VMEM),
        scratch_shapes=[pltpu.SemaphoreType.DMA((7,)), pltpu.SemaphoreType.DMA((7,)), pltpu.SemaphoreType.DMA],
        compiler_params=_params(),
    )(blk)


def _half(ref, row_axis, c, rows):
    idx = [slice(None)] * len(ref.shape)
    idx[row_axis] = pl.ds(pl.multiple_of(c * rows, 16), rows)
    return ref.at[tuple(idx)]


def _shard(ref, layout, k):
    if layout == "row":
        return ref.at[:, k]
    w = ref.shape[2] // N_CHIPS
    return ref.at[:, :, pl.ds(pl.multiple_of(k * w, LANE), w)]


def _full_shape(shape, layout):
    L, r, w = shape
    return (L, N_CHIPS, r, w) if layout == "row" else (L, r, N_CHIPS * w)


def _cast_full(name, a, layout, k_idx, lyr):
    _, r, C = a.shape
    tr = _pick(r, (256, 128, 64, 32, 16))

    def body(k_ref, a_ref, o_ref):
        o_ref[...] = a_ref[...].astype(o_ref.dtype)

    if layout == "row":
        o_spec = pl.BlockSpec((None, None, tr, C), lambda l, i, k: (0, k[0], i, 0))
    else:
        o_spec = pl.BlockSpec((None, tr, C), lambda l, i, k: (0, i, k[0]))
    return pl.pallas_call(
        body, name=name,
        grid_spec=pltpu.PrefetchScalarGridSpec(
            num_scalar_prefetch=1, grid=(1, r // tr),
            in_specs=[pl.BlockSpec((None, tr, C), lambda l, i, k: (lyr, i, 0))], out_specs=o_spec),
        out_shape=jax.ShapeDtypeStruct(_full_shape((1, r, C), layout), WIRE_DTYPE),
        compiler_params=_params(),
    )(k_idx, a)


def _gather_carry(fulls, layouts, lyrs, phase):
    n = len(fulls)
    half_rows = [f.shape[2 if lay == "row" else 1] // 2 for f, lay in zip(fulls, layouts)]

    def copies(ins, outs, send_sems, recv_sems, off=0):
        x, y, c, chips = _place()
        cps = []
        for i in range(n):
            for j, chip in enumerate(chips):
                who, to = ((x, y), (*chip, c)) if phase == "ici" else (chip, (x, y, 1 - c))
                w = _half(_shard(outs[i].at[pl.ds(lyrs[i], 1)], layouts[i], 2 * who[0] + who[1]), 1, c, half_rows[i])
                cps.append(pltpu.make_async_remote_copy(
                    src_ref=w, dst_ref=w, send_sem=send_sems.at[off + i, j], recv_sem=recv_sems.at[off + i, j],
                    device_id=to, device_id_type=MESH))
        return cps

    return dict(ins=list(fulls), sems=(n, 3), copies=copies, alias={i: i for i in range(n)},
                out_shape=[jax.ShapeDtypeStruct(f.shape, f.dtype) for f in fulls])


def _run_carry(name, carry):
    n = len(carry["ins"])

    def body(*refs):
        cps = carry["copies"](refs[:n], refs[n:2 * n], *refs[2 * n:])
        for cp in cps:
            cp.start()
        for cp in cps:
            cp.wait()

    return pl.pallas_call(
        body, name=name, in_specs=[ANY] * n, out_specs=[ANY] * len(carry["out_shape"]), out_shape=carry["out_shape"],
        input_output_aliases=dict(carry.get("alias", {})),
        scratch_shapes=[pltpu.SemaphoreType.DMA(carry["sems"]), pltpu.SemaphoreType.DMA(carry["sems"])],
        compiler_params=_params(),
    )(*carry["ins"])


def _gather_weights(fulls, layouts, lyrs):
    n = len(fulls)
    half_rows = [f.shape[2 if lay == "row" else 1] // 2 for f, lay in zip(fulls, layouts)]

    def body(*refs):
        outs = refs[n:2 * n]
        send_sems, recv_sems = refs[2 * n:]
        x, y, c, chips = _place()
        sibling = (x, y, 1 - c)

        def window(i, kx, ky, half):
            return _half(_shard(outs[i].at[pl.ds(lyrs[i], 1)], layouts[i], 2 * kx + ky), 1, half, half_rows[i])

        first, passed = [], []
        for i in range(n):
            mine = window(i, x, y, c)
            for j, chip in enumerate(chips):
                cp = pltpu.make_async_remote_copy(
                    src_ref=mine, dst_ref=mine, send_sem=send_sems.at[i, j], recv_sem=recv_sems.at[i, j],
                    device_id=(*chip, c), device_id_type=MESH)
                cp.start()
                first.append(cp)
        for i in range(n):
            for j, chip in enumerate(chips):
                got = window(i, *chip, c)
                pltpu.make_async_remote_copy(
                    src_ref=got, dst_ref=got, send_sem=send_sems.at[i, j], recv_sem=recv_sems.at[i, j],
                    device_id=(*chip, c), device_id_type=MESH).wait_recv()
                cp = pltpu.make_async_remote_copy(
                    src_ref=got, dst_ref=got, send_sem=send_sems.at[i, 3 + j], recv_sem=recv_sems.at[i, 3 + j],
                    device_id=sibling, device_id_type=MESH)
                cp.start()
                passed.append(cp)
        for i in range(n):
            for j, chip in enumerate(chips):
                got = window(i, *chip, 1 - c)
                pltpu.make_async_remote_copy(
                    src_ref=got, dst_ref=got, send_sem=send_sems.at[i, 3 + j], recv_sem=recv_sems.at[i, 3 + j],
                    device_id=sibling, device_id_type=MESH).wait_recv()
        for cp in first + passed:
            cp.wait_send()

    return pl.pallas_call(
        body, name="gather_weights", in_specs=[ANY] * n, out_specs=[ANY] * n,
        out_shape=[jax.ShapeDtypeStruct(f.shape, f.dtype) for f in fulls],
        input_output_aliases={i: i for i in range(n)},
        scratch_shapes=[pltpu.SemaphoreType.DMA((n, 6)), pltpu.SemaphoreType.DMA((n, 6))],
        compiler_params=_params(),
    )(*fulls)


def _half_shape(shape, layout):
    s = list(shape)
    s[2 if layout == "row" else 1] //= 2
    return tuple(s)


def _swap_carry(grads, layouts):
    n = len(grads)
    row_axis = [2 if lay == "row" else 1 for lay in layouts]
    half_rows = [g.shape[ra] // 2 for g, ra in zip(grads, row_axis)]

    def copies(ins, outs, send_sems, recv_sems, off=0):
        x, y, c, _ = _place()
        return [pltpu.make_async_remote_copy(
            src_ref=_half(ins[i], row_axis[i], 1 - c, half_rows[i]), dst_ref=outs[i], send_sem=send_sems.at[off + i, 0],
            recv_sem=recv_sems.at[off + i, 0], device_id=(x, y, 1 - c), device_id_type=MESH) for i in range(n)]

    return dict(ins=list(grads), sems=(n, 3), copies=copies,
                out_shape=[jax.ShapeDtypeStruct(_half_shape(g.shape, lay), g.dtype) for g, lay in zip(grads, layouts)])


def _merge_carries(a, b):
    if a is None or b is None:
        return a if b is None else b
    na, ia, oa = a["sems"][0], len(a["ins"]), len(a["out_shape"])

    def copies(ins, outs, send_sems, recv_sems, off=0):
        return (a["copies"](ins[:ia], outs[:oa], send_sems, recv_sems, off)
                + b["copies"](ins[ia:], outs[oa:], send_sems, recv_sems, off + na))

    alias = dict(a.get("alias", {}))
    alias.update({ia + i: oa + o for i, o in b.get("alias", {}).items()})
    return dict(ins=a["ins"] + b["ins"], out_shape=a["out_shape"] + b["out_shape"], sems=(na + b["sems"][0], 3),
                copies=copies, alias=alias)


def _add_half(name, g, r, layout, c_idx):
    L = g.shape[0]
    if layout == "row":
        A, rows, W = L * N_CHIPS, g.shape[2] // 2, g.shape[3]
    else:
        A, rows, W = L, g.shape[1] // 2, g.shape[2]
    g3 = g.reshape(A, 2 * rows, W)
    r3 = r.reshape(A, rows, W)
    tr = _pick(rows, (256, 128, 64, 32, 16))
    nb = rows // tr

    def body(c_ref, g_ref, r_ref, o_ref):
        o_ref[...] = (g_ref[...].astype(F32) + r_ref[...].astype(F32)).astype(o_ref.dtype)

    out = pl.pallas_call(
        body, name=name,
        grid_spec=pltpu.PrefetchScalarGridSpec(
            num_scalar_prefetch=1, grid=(A, nb),
            in_specs=[pl.BlockSpec((None, tr, W), lambda a, i, c: (a, c[0] * nb + i, 0)),
                      pl.BlockSpec((None, tr, W), lambda a, i, c: (a, i, 0))],
            out_specs=pl.BlockSpec((None, tr, W), lambda a, i, c: (a, i, 0))),
        out_shape=jax.ShapeDtypeStruct((A, rows, W), WIRE_DTYPE),
        compiler_params=_params(),
    )(c_idx, g3, r3)
    return out.reshape(r.shape)


def _exchange_carry(parts, layouts):
    n = len(parts)

    def shard_half_shape(p, lay):
        if lay == "row":
            return (p.shape[0],) + p.shape[2:]
        return (p.shape[0], p.shape[1], p.shape[2] // N_CHIPS)

    def copies(ins, outs, send_sems, recv_sems, off=0):
        x, y, c, chips = _place()
        return [pltpu.make_async_remote_copy(
            src_ref=_shard(ins[i], layouts[i], 2 * kx + ky), dst_ref=outs[i].at[j], send_sem=send_sems.at[off + i, j],
            recv_sem=recv_sems.at[off + i, j], device_id=(kx, ky, c), device_id_type=MESH)
            for i in range(n) for j, (kx, ky) in enumerate(chips)]

    return dict(ins=list(parts), sems=(n, 3), copies=copies,
                out_shape=[jax.ShapeDtypeStruct((3,) + shard_half_shape(p, lay), p.dtype)
                           for p, lay in zip(parts, layouts)])


def _sum_shards(name, p, r, layout, kc_idx, dst, lyr, n_lyr):
    rows, W = r.shape[2], r.shape[3]
    tr = _pick(rows, (256, 128, 64, 32, 16))
    nb = rows // tr

    def body(kc_ref, p_ref, r_ref, *rest):
        acc = p_ref[...].astype(F32)
        for j in range(3):
            acc = acc + r_ref[j].astype(F32)
        rest[-1][...] = acc

    if layout == "row":
        p_spec = pl.BlockSpec((None, None, tr, W), lambda a, i, kc: (0, kc[0], i, 0))
    else:
        p_spec = pl.BlockSpec((None, tr, W), lambda a, i, kc: (0, i, kc[0]))
    in_specs = [p_spec, pl.BlockSpec((3, None, tr, W), lambda a, i, kc: (0, 0, i, 0))]
    args = [kc_idx, p, r]
    if dst is not None:
        in_specs.append(ANY)
        args.append(dst)
    return pl.pallas_call(
        body, name=name,
        grid_spec=pltpu.PrefetchScalarGridSpec(
            num_scalar_prefetch=1, grid=(1, nb), in_specs=in_specs,
            out_specs=pl.BlockSpec((None, tr, W), lambda a, i, kc: (lyr, kc[1] * nb + i, 0))),
        out_shape=jax.ShapeDtypeStruct((n_lyr, 2 * rows, W), F32),
        input_output_aliases={3: 0} if dst is not None else {},
        compiler_params=_params(),
    )(*args)


def _join_halves(shards):
    n = len(shards)

    def body(*refs):
        outs = refs[n:2 * n]
        send_sems, recv_sems = refs[2 * n:]
        x, y, c, _ = _place()
        cps = []
        for i in range(n):
            mine = _half(outs[i], 1, c, outs[i].shape[1] // 2)
            cp = pltpu.make_async_remote_copy(
                src_ref=mine, dst_ref=mine, send_sem=send_sems.at[i], recv_sem=recv_sems.at[i],
                device_id=(x, y, 1 - c), device_id_type=MESH)
            cp.start()
            cps.append(cp)
        for cp in cps:
            cp.wait()

    return pl.pallas_call(
        body, name="join_halves", in_specs=[ANY] * n, out_specs=[ANY] * n,
        out_shape=[jax.ShapeDtypeStruct(s.shape, s.dtype) for s in shards],
        input_output_aliases={i: i for i in range(n)},
        scratch_shapes=[pltpu.SemaphoreType.DMA((n,)), pltpu.SemaphoreType.DMA((n,))],
        compiler_params=_params(),
    )(*shards)


def _ada_fwd(c_all, ada_w, ada_b):
    L, D, w = ada_w.shape
    tn = _pick(w, (512, 256, 128))

    def body(c_ref, w_ref, b_ref, o_ref, a_ref):
        c = c_ref[...]
        act = (c * _sigmoid(c)).astype(MXU_DTYPE)
        a_ref[...] = act
        o_ref[...] = jnp.dot(act, w_ref[...].astype(MXU_DTYPE), preferred_element_type=F32) + b_ref[...]

    return pl.pallas_call(
        body, name="ada_fwd", grid=(L, w // tn),
        in_specs=[pl.BlockSpec((16, D), lambda l, j: (0, 0)), pl.BlockSpec((None, D, tn), lambda l, j: (l, 0, j)),
                  pl.BlockSpec((None, 1, tn), lambda l, j: (l, 0, j))],
        out_specs=[pl.BlockSpec((None, 16, tn), lambda l, j: (l, 0, j)), pl.BlockSpec((16, D), lambda l, j: (0, 0))],
        out_shape=[jax.ShapeDtypeStruct((L, 16, w), F32), jax.ShapeDtypeStruct((16, D), MXU_DTYPE)],
        compiler_params=_params(),
    )(c_all, ada_w, ada_b)


def _sum_devices(name, parts):
    n, R, W = parts.shape
    tw = _pick(W, (2048, 1024, 512, 256, 128))

    def body(p_ref, o_ref):
        acc = p_ref[0]
        for d in range(1, n):
            acc = acc + p_ref[d]
        o_ref[...] = acc

    return pl.pallas_call(
        body, name=name, grid=(W // tw,), in_specs=[pl.BlockSpec((n, R, tw), lambda j: (0, 0, j))],
        out_specs=pl.BlockSpec((R, tw), lambda j: (0, j)), out_shape=jax.ShapeDtypeStruct((R, W), F32),
        compiler_params=_params(),
    )(parts)


def _adamw_math(w, g, m, v):
    m2 = ADAM_B1 * m + (1.0 - ADAM_B1) * g
    v2 = ADAM_B2 * v + (1.0 - ADAM_B2) * (g * g)
    m_hat = m2 / (1.0 - ADAM_B1 ** ADAM_STEP)
    v_hat = v2 / (1.0 - ADAM_B2 ** ADAM_STEP)
    return -ADAM_LR * (m_hat / (jnp.sqrt(v_hat) + ADAM_EPS) + ADAM_WD * w), m2, v2


def _adamw(name, w, g, m, v):
    W = w.shape[1]
    tr = 256 if W <= 1024 else (128 if W <= 2048 else 64)
    return _rowwise(name, lambda w, g, m, v: (g,) + _adamw_math(w, g, m, v), [w, g, m, v], [], [(W, F32)] * 4, tr=tr)


def _adamw_ada(c_act, dmod, w, m, v, carry=None):
    L, D, wd = w.shape
    tr = LANE

    def body(*refs):
        refs, c_in, c_out, c_sems = _carry_split(refs, 5, 4, 0, carry)
        c_ref, d_ref, w_ref, m_ref, v_ref, g_ref, dl_ref, m2_ref, v2_ref = refs
        pl_, pi = pl.program_id(0), pl.program_id(1)
        _carry_run(carry, c_in, c_out, c_sems, jnp.logical_and(pl_ == 0, pi == 0), True)
        g = _dot_tn(c_ref[...], d_ref[...].astype(MXU_DTYPE))
        g_ref[...] = g
        dl_ref[...], m2_ref[...], v2_ref[...] = _adamw_math(w_ref[...], g, m_ref[...], v_ref[...])
        _carry_run(carry, c_in, c_out, c_sems, jnp.logical_and(pl_ == L - 1, pi == D // tr - 1), False)

    big = pl.BlockSpec((None, tr, wd), lambda l, i: (l, i, 0))
    c_is, c_os, c_shape, c_sems, c_alias = _carry_call_args(carry, 5, 4)
    res = pl.pallas_call(
        body, name="adamw_ada_w", grid=(L, D // tr),
        in_specs=[pl.BlockSpec((16, tr), lambda l, i: (0, i)), pl.BlockSpec((None, 16, wd), lambda l, i: (l, 0, 0)),
                  big, big, big] + c_is,
        out_specs=[big] * 4 + c_os, out_shape=[jax.ShapeDtypeStruct(w.shape, F32)] * 4 + c_shape,
        input_output_aliases=c_alias, scratch_shapes=c_sems,
        compiler_params=_params(),
    )(c_act, dmod, w, m, v, *(carry["ins"] if carry else ()))
    return res[:4], list(res[4:])


def _flat(a):
    return a.reshape(-1, a.shape[-1])


BIG = ("ffn_w1", "ffn_w3", "ffn_w2", "mla_w_down", "mla_w_uq", "mla_w_uk", "mla_w_uv", "mla_w_o", "fox_w_in",
       "fox_w_o")
LAYOUT = dict(ffn_w1="col", ffn_w3="col", ffn_w2="row", mla_w_down="row", mla_w_uq="col", mla_w_uk="col",
              mla_w_uv="col", mla_w_o="row", fox_w_in="row", fox_w_o="row")
FFN = ("ffn_w1", "ffn_w3", "ffn_w2")
MLA = ("mla_w_down", "mla_w_uq", "mla_w_uk", "mla_w_uv", "mla_w_o")
FOX = ("fox_w_in", "fox_w_o")
SMALL = ("ln1_g", "ln1_b", "ln2_g", "ln2_b", "mla_q_norm", "mla_kv_norm", "fox_b_f")
WEIGHTS = ("ada_w", "ada_b", "ln1_g", "ln1_b", "ln2_g", "ln2_b", "ffn_w1", "ffn_w3", "ffn_w2", "mla_w_down",
           "mla_q_norm", "mla_w_uq", "mla_kv_norm", "mla_w_uk", "mla_w_uv", "mla_w_o", "fox_w_in", "fox_b_f",
           "fox_w_o")


def _uq_perm(H):
    d = HEAD_DIM + ROPE_DIM
    nope = (np.arange(H)[:, None] * d + np.arange(HEAD_DIM)[None, :]).reshape(-1)
    pe = (np.arange(H)[:, None] * d + HEAD_DIM + np.arange(ROPE_DIM)[None, :]).reshape(-1)
    return np.concatenate([nope, pe])


def kernel(x, c, positions, ada_w, ada_b, ln1_g, ln1_b, ln2_g, ln2_b, ffn_w1, ffn_w3, ffn_w2, mla_w_down, mla_q_norm, mla_w_uq, mla_kv_norm, mla_w_uk, mla_w_uv, mla_w_o, fox_w_in, fox_b_f, fox_w_o, loss_target, m_ada_w, m_ada_b, m_ln1_g, m_ln1_b, m_ln2_g, m_ln2_b, m_ffn_w1, m_ffn_w3, m_ffn_w2, m_mla_w_down, m_mla_q_norm, m_mla_w_uq, m_mla_kv_norm, m_mla_w_uk, m_mla_w_uv, m_mla_w_o, m_fox_w_in, m_fox_b_f, m_fox_w_o, v_ada_w, v_ada_b, v_ln1_g, v_ln1_b, v_ln2_g, v_ln2_b, v_ffn_w1, v_ffn_w3, v_ffn_w2, v_mla_w_down, v_mla_q_norm, v_mla_w_uq, v_mla_kv_norm, v_mla_w_uk, v_mla_w_uv, v_mla_w_o, v_fox_w_in, v_fox_b_f, v_fox_w_o):
    W = dict(ada_w=ada_w, ada_b=ada_b, ln1_g=ln1_g, ln1_b=ln1_b, ln2_g=ln2_g, ln2_b=ln2_b, ffn_w1=ffn_w1,
             ffn_w3=ffn_w3, ffn_w2=ffn_w2, mla_w_down=mla_w_down, mla_q_norm=mla_q_norm, mla_w_uq=mla_w_uq,
             mla_kv_norm=mla_kv_norm, mla_w_uk=mla_w_uk, mla_w_uv=mla_w_uv, mla_w_o=mla_w_o, fox_w_in=fox_w_in,
             fox_b_f=fox_b_f, fox_w_o=fox_w_o)
    Mo = dict(ada_w=m_ada_w, ada_b=m_ada_b, ln1_g=m_ln1_g, ln1_b=m_ln1_b, ln2_g=m_ln2_g, ln2_b=m_ln2_b,
              ffn_w1=m_ffn_w1, ffn_w3=m_ffn_w3, ffn_w2=m_ffn_w2, mla_w_down=m_mla_w_down, mla_q_norm=m_mla_q_norm,
              mla_w_uq=m_mla_w_uq, mla_kv_norm=m_mla_kv_norm, mla_w_uk=m_mla_w_uk, mla_w_uv=m_mla_w_uv,
              mla_w_o=m_mla_w_o, fox_w_in=m_fox_w_in, fox_b_f=m_fox_b_f, fox_w_o=m_fox_w_o)
    Vo = dict(ada_w=v_ada_w, ada_b=v_ada_b, ln1_g=v_ln1_g, ln1_b=v_ln1_b, ln2_g=v_ln2_g, ln2_b=v_ln2_b,
              ffn_w1=v_ffn_w1, ffn_w3=v_ffn_w3, ffn_w2=v_ffn_w2, mla_w_down=v_mla_w_down, mla_q_norm=v_mla_q_norm,
              mla_w_uq=v_mla_w_uq, mla_kv_norm=v_mla_kv_norm, mla_w_uk=v_mla_w_uk, mla_w_uv=v_mla_w_uv,
              mla_w_o=v_mla_w_o, fox_w_in=v_fox_w_in, fox_b_f=v_fox_b_f, fox_w_o=v_fox_w_o)

    S, D = x.shape[1], x.shape[2]
    L = ada_w.shape[0]
    alpha = float((2 * L) ** 0.25)
    H_mla = mla_w_uk.shape[2] * N_CHIPS // HEAD_DIM
    H_fox = D // HEAD_DIM
    xi, yi, ci = lax.axis_index("x"), lax.axis_index("y"), lax.axis_index("c")
    chip = 2 * xi + yi
    dev = 2 * chip + ci
    c_idx = jnp.reshape(ci, (1,)).astype(jnp.int32)
    x0, tgt = x[0], loss_target[0]
    pos = positions.reshape(S, 1)

    c_all = _ag_small("gather_c", jnp.pad(c, ((0, 7), (0, 0)))).reshape(N_DEV, 8, D)[:, 0]
    w_ada = ada_w.shape[2]
    ada_b_sh = lax.dynamic_slice_in_dim(ada_b, chip * w_ada, w_ada, axis=1).reshape(L, 1, w_ada)
    mod_sh, c_act = _ada_fwd(jnp.pad(c_all, ((0, 8), (0, 0))), ada_w, ada_b_sh)
    mod_all = _ag_small("gather_mod", mod_sh.transpose(1, 0, 2).reshape(16, L * w_ada))
    mod_all = mod_all.reshape(N_CHIPS, 2, 16, L, w_ada)[:, 0]
    mod = lax.dynamic_index_in_dim(mod_all, dev, axis=1, keepdims=False)
    mod = mod.transpose(1, 0, 2).reshape(L, 6, 1, D)

    kc_idx = jnp.stack([chip, ci]).astype(jnp.int32)
    raw = {n: [_cast_full("cast_" + n, W[n], LAYOUT[n], kc_idx, l) for l in range(W[n].shape[0])] for n in BIG}
    full = {n: [None] * W[n].shape[0] for n in BIG}
    perm = _uq_perm(H_mla)
    n_in = fox_w_in.shape[2] * N_CHIPS

    def group(i):
        return FFN + (MLA if i % 2 == 0 else FOX)

    def ride_of(names, i):
        return dict(bufs=[raw[n][i if n in FFN else i // 2] for n in names], lays=[LAYOUT[n] for n in names])

    def land(names, i, bufs):
        for n, f in zip(names, bufs):
            if n == "fox_w_in":
                fw = f.reshape(N_CHIPS, D, -1).transpose(1, 0, 2).reshape(D, n_in)
                f = jnp.pad(fw, ((0, 0), (0, 3 * D + LANE - n_in)))[None]
            elif n == "mla_w_uq":
                f = f[:, :, perm]
            elif LAYOUT[n] == "row":
                f = f.reshape(1, f.shape[1] * f.shape[2], f.shape[3])
            full[n][i if n in FFN else i // 2] = f

    r0 = ride_of(group(0)[3:], 0)
    land(group(0)[3:], 0, _gather_weights(r0["bufs"], r0["lays"], [0] * len(r0["bufs"])))

    rc = tuple(_rope_tables(pos))

    def mixer_w(i):
        j = i // 2
        if i % 2 == 0:
            return dict(w_down=(full["mla_w_down"][j], 0), q_norm=mla_q_norm[j:j + 1], w_uq=(full["mla_w_uq"][j], 0),
                        kv_norm=mla_kv_norm[j:j + 1], w_uk=(full["mla_w_uk"][j], 0), w_uv=(full["mla_w_uv"][j], 0),
                        w_o=(full["mla_w_o"][j], 0))
        return dict(w_in=(full["fox_w_in"][j], 0), b_f=jnp.pad(fox_b_f[j:j + 1], ((0, 0), (0, LANE - H_fox))),
                    w_o=(full["fox_w_o"][j], 0))

    saved = []
    xc = x0
    h = _modulate(x0, mod[0, 1], mod[0, 0])
    for i in range(L):
        mw = mixer_w(i)
        nxt = i + 1 < L
        mix_n = group(i + 1)[3:]
        ride = dict(ride_of(FFN, i), n_ici=3 if i == 0 else 2)
        if i % 2 == 0:
            y1, ms, got = _mla_fwd(h, mw, rc, ride)
        else:
            y1, ms, got = _fox_fwd(h, mw, ride)
        land(FFN, i, got)
        z1, x1, h2 = _resid_ln_mod(xc, y1, mod[i, 2], ln1_g[i:i + 1], ln1_b[i:i + 1], mod[i, 4], mod[i, 3], alpha)
        y2, fs, got, early = _ffn_fwd(h2, (full["ffn_w1"][i], 0), (full["ffn_w3"][i], 0), (full["ffn_w2"][i], 0),
                                      ride_of(mix_n, i + 1) if nxt else None, ride_of(FFN[2:], i + 1) if nxt else None)
        land(mix_n, i + 1, got)
        if nxt:
            raw["ffn_w2"][i + 1] = early[0]
        rec = dict(h1=h, ms=ms, y1=y1, z1=z1, h2=h2, fs=fs, y2=y2)
        if i + 1 < L:
            z2, xc, h = _resid_ln_mod(x1, y2, mod[i, 5], ln2_g[i:i + 1], ln2_b[i:i + 1], mod[i + 1, 1],
                                      mod[i + 1, 0], alpha)
            rec["z2"] = z2
        else:
            dx_res, dy, loss_v, dlg, dlb, dgate = _final_ln_loss(x1, y2, tgt, mod[i, 5], ln2_g[i:i + 1],
                                                                 ln2_b[i:i + 1], alpha)
        saved.append(rec)
    loss = lax.psum(loss_v[0, 0] * (0.5 / D), ("x", "y", "c"))

    G = {n: [None] * W[n].shape[0] for n in SMALL}
    dmod = [[None] * 6 for _ in range(L)]
    red = {n: None for n in BIG}
    inv_perm = np.argsort(perm)

    def rs_view(n, g):
        if n == "mla_w_uq":
            g = g[:, inv_perm]
        if n == "fox_w_in":
            g = g[:, :n_in].reshape(D, N_CHIPS, n_in // N_CHIPS).transpose(1, 0, 2)
        elif LAYOUT[n] == "row":
            g = g.reshape(N_CHIPS, g.shape[0] // N_CHIPS, g.shape[1])
        return g[None]

    def add_halves(names, gs, recv):
        return [_add_half("add_half_" + n, g, r, LAYOUT[n], c_idx) for n, g, r in zip(names, gs, recv)]

    def half_sums(tag, names, grads):
        gs = [rs_view(n, g) for n, g in zip(names, grads)]
        return add_halves(names, gs, _run_carry("swap_" + tag, _swap_carry(gs, [LAYOUT[n] for n in names])))

    def finish(names, parts, recv2, lyr):
        for n, p, r in zip(names, parts, recv2):
            red[n] = _sum_shards("sum_shards_" + n, p, r, LAYOUT[n], kc_idx, red[n], lyr, W[n].shape[0])

    queue = []

    def carry_of(item):
        return _exchange_carry(item[1], [LAYOUT[n] for n in item[0]]) if item else None

    for i in reversed(range(L)):
        rec = saved[i]
        mw = mixer_w(i)
        j = i // 2
        G["ln2_g"][i], G["ln2_b"][i], dmod[i][5] = dlg, dlb, dgate
        ride_du = queue.pop(0) if queue else None
        ride_dh = queue.pop(0) if queue else None
        views = []

        def swap_of(grads):
            views.extend(rs_view(n, g) for n, g in zip(FFN, grads))
            return _swap_carry(views, [LAYOUT[n] for n in FFN])

        dh2, dw1, dw3, dw2, got_du, got_dh, got_swap = _ffn_bwd(
            dy, rec["h2"], rec["fs"], (full["ffn_w1"][i], 0), (full["ffn_w3"][i], 0), (full["ffn_w2"][i], 0),
            carry_of(ride_du), carry_of(ride_dh), swap_of)
        for item, got in ((ride_du, got_du), (ride_dh, got_dh)):
            if item:
                finish(item[0], item[1], got, item[2])
        ffn_parts = add_halves(FFN, views, got_swap)
        dx_res, dy, dmod[i][4], dmod[i][3], G["ln1_g"][i], G["ln1_b"][i], dmod[i][2] = _bwd_boundary(
            dx_res, dh2, rec["z1"], rec["y1"], mod[i, 4], mod[i, 2], ln1_g[i:i + 1], ln1_b[i:i + 1], alpha)
        n_at = 2 if i > 0 else 3
        ride_at = (FFN[:n_at], ffn_parts[:n_at], i)
        if i % 2 == 0:
            dh1, gm, recv2 = _mla_bwd(dy, rec["h1"], rec["ms"], mw, rc, carry_of(ride_at))
            names, pre = MLA, "mla_"
            G["mla_q_norm"][j], G["mla_kv_norm"][j] = gm["q_norm"], gm["kv_norm"]
        else:
            dh1, gm, recv2 = _fox_bwd(dy, rec["h1"], rec["ms"], mw, carry_of(ride_at))
            names, pre = FOX, "fox_"
            G["fox_b_f"][j] = gm["b_f"]
        finish(ride_at[0], ride_at[1], recv2, i)
        if n_at < 3:
            queue.append((FFN[2:], ffn_parts[2:], i))
        queue.append((names, half_sums(pre[:-1], names, [gm[n[len(pre):]] for n in names]), j))
        if i > 0:
            p = saved[i - 1]
            dx_res, dy, dmod[i][1], dmod[i][0], dlg, dlb, dgate = _bwd_boundary(
                dx_res, dh1, p["z2"], p["y2"], mod[i, 1], mod[i - 1, 5], ln2_g[i - 1:i], ln2_b[i - 1:i], alpha)
        else:
            grad_x, dmod[i][1], dmod[i][0] = _first_bwd(dx_res, dh1, x0, mod[i, 1])

    small = jnp.concatenate([jnp.concatenate([g.reshape(-1) for g in G[n]]) for n in SMALL])
    dmod_v = jnp.concatenate([jnp.concatenate([d.reshape(-1) for d in row]) for row in dmod])
    n_small, n_dmod = small.shape[0], dmod_v.shape[0]
    wblk = -(-(n_small + n_dmod) // (8 * LANE)) * LANE
    blk = jnp.pad(jnp.concatenate([dmod_v, small]), (0, 8 * wblk - n_small - n_dmod)).reshape(8, wblk)
    parts = _ag_small("gather_small", blk).reshape(N_DEV, 8, wblk)
    tot = _sum_devices("sum_small", parts).reshape(-1)
    g_ada_b = tot[:n_dmod].reshape(L, 6 * D)
    off = n_dmod
    Gs = {}
    for n in SMALL:
        Gs[n] = tot[off:off + W[n].size].reshape(W[n].shape)
        off += W[n].size
    dmod_all = parts.reshape(N_DEV, 8 * wblk)[:, :n_dmod].reshape(N_DEV, L, N_CHIPS, w_ada)
    dmod_sh = lax.dynamic_index_in_dim(dmod_all, chip, axis=2, keepdims=False)
    dmod_sh = jnp.pad(dmod_sh, ((0, 8), (0, 0), (0, 0)))
    dmod_sh = dmod_sh.transpose(1, 0, 2)

    last = (sum((q[0] for q in queue), ()), sum((q[1] for q in queue), []))
    got = _run_carry("exchange_last", _exchange_carry(last[1], [LAYOUT[n] for n in last[0]]))
    (g_ada_w, d_ada_w, m_ada_w2, v_ada_w2), _ = _adamw_ada(c_act, dmod_sh, ada_w, m_ada_w, v_ada_w)
    for q in queue:
        finish(q[0], q[1], got[:len(q[0])], q[2])
        got = got[len(q[0]):]
    Gb = dict(zip(BIG, _join_halves([red[n] for n in BIG])))
    grads = dict(Gb)
    grads.update(Gs)
    grads["ada_b"] = g_ada_b
    delta, new_m, new_v = {}, {}, {}
    grads["ada_w"], delta["ada_w"], new_m["ada_w"], new_v["ada_w"] = g_ada_w, d_ada_w, m_ada_w2, v_ada_w2
    for n in WEIGHTS:
        if n in SMALL or n in ("ada_b", "ada_w"):
            continue
        shp = W[n].shape
        grads[n], delta[n], new_m[n], new_v[n] = [r.reshape(shp) for r in _adamw(
            "adamw_" + n, _flat(W[n]), _flat(grads[n]), _flat(Mo[n]), _flat(Vo[n]))]
    names_s = SMALL + ("ada_b",)
    cat = lambda d: jnp.concatenate([d[n].reshape(-1) for n in names_s])
    n_s = sum(W[n].size for n in names_s)
    ws = -(-n_s // (8 * LANE)) * LANE
    pk = lambda d: jnp.pad(cat(d), (0, 8 * ws - n_s)).reshape(8, ws)
    _, ds, ms_, vs = _adamw("adamw_small", pk(W), pk(grads), pk(Mo), pk(Vo))
    off = 0
    for n in names_s:
        sz, shp = W[n].size, W[n].shape
        delta[n] = ds.reshape(-1)[off:off + sz].reshape(shp)
        new_m[n] = ms_.reshape(-1)[off:off + sz].reshape(shp)
        new_v[n] = vs.reshape(-1)[off:off + sz].reshape(shp)
        off += sz

    return (loss, grad_x[None], *[grads[n].reshape(W[n].shape) for n in WEIGHTS], *[delta[n] for n in WEIGHTS],
            *[new_m[n] for n in WEIGHTS], *[new_v[n] for n in WEIGHTS])
```

```python
import functools

import numpy as np
import jax
import jax.numpy as jnp
from jax import lax
from jax.experimental import pallas as pl
from jax.experimental.pallas import tpu as pltpu

F32 = jnp.float32
BF16 = jnp.bfloat16
MXU_DTYPE = jnp.bfloat16
WIRE_DTYPE = jnp.bfloat16

HEAD_DIM = 128
ROPE_DIM = 64
CHUNK = 64
ROPE_THETA = 10000.0
LN_EPS = 1e-5
RMS_EPS = 1e-6
ADAM_LR, ADAM_B1, ADAM_B2, ADAM_EPS, ADAM_WD, ADAM_STEP = 0.001, 0.9, 0.999, 1e-08, 0.01, 10

N_CHIPS = 4
N_DEV = 8
LANE = 128
VMEM_LIMIT = 56 * 1024 * 1024
MESH = pl.DeviceIdType.MESH
ANY = pl.BlockSpec(memory_space=pl.ANY)
NEG = -1e30


def _params(**kw):
    return pltpu.CompilerParams(vmem_limit_bytes=VMEM_LIMIT, **kw)


def _pick(n, cands):
    for c in cands:
        if n % c == 0:
            return c
    return n


def _sigmoid(x):
    return 1.0 / (1.0 + jnp.exp(-x))


def _mm(name, terms, M, N, out_dtypes, epilogue=None, extras=(), row_extras=(), tm=512, tn=512, carry=None):
    tm = _pick(M, (tm, 256, 128))
    tn = _pick(N, (tn, 896, 768, 640, 384, 256, 128))
    extras = tuple(extras) + tuple(row_extras)
    n_row = len(row_extras)
    n_terms, n_ex, n_out = len(terms), len(extras), len(out_dtypes)
    n_acc = 1 + max(t[4] for t in terms)
    flags = [(t[2], t[3], t[4]) for t in terms]

    gi, gj = M // tm, N // tn

    def body(*refs):
        refs, c_in, c_out, c_sems = _carry_split(refs, 2 * n_terms + n_ex, n_out, 0, carry)
        pi, pj = pl.program_id(0), pl.program_id(1)
        _carry_run(carry, c_in, c_out, c_sems, jnp.logical_and(pi == 0, pj == 0), True)
        accs = [None] * n_acc
        for k, (ta, tb, ai) in enumerate(flags):
            a = refs[2 * k][...].astype(MXU_DTYPE)
            b = refs[2 * k + 1][...].astype(MXU_DTYPE)
            dn = (((0 if ta else 1,), (1 if tb else 0,)), ((), ()))
            r = lax.dot_general(a, b, dn, preferred_element_type=F32)
            accs[ai] = r if accs[ai] is None else accs[ai] + r
        ex = [refs[2 * n_terms + k][...] for k in range(n_ex)]
        outs = epilogue(accs, *ex) if epilogue is not None else (accs[0],)
        for k in range(n_out):
            o_ref = refs[2 * n_terms + n_ex + k]
            o_ref[...] = outs[k].astype(o_ref.dtype)
        _carry_run(carry, c_in, c_out, c_sems, jnp.logical_and(pi == gi - 1, pj == gj - 1), False)

    in_specs, args = [], []
    for (a, b, ta, tb, _, bcol) in terms:
        K = a.shape[0] if ta else a.shape[1]
        in_specs.append(pl.BlockSpec((K, tm), lambda i, j: (0, i)) if ta
                        else pl.BlockSpec((tm, K), lambda i, j: (i, 0)))
        if isinstance(b, tuple):
            b, lyr = b
            in_specs.append(pl.BlockSpec((None, tn, K), lambda i, j, o=bcol, l=lyr: (l, j + o, 0)) if tb
                            else pl.BlockSpec((None, K, tn), lambda i, j, o=bcol, l=lyr: (l, 0, j + o)))
        else:
            in_specs.append(pl.BlockSpec((tn, K), lambda i, j, o=bcol: (j + o, 0)) if tb
                            else pl.BlockSpec((K, tn), lambda i, j, o=bcol: (0, j + o)))
        args += [a, b]
    for k, e in enumerate(extras):
        in_specs.append(pl.BlockSpec((tm, tn), (lambda i, j: (i, 0)) if k >= n_ex - n_row else (lambda i, j: (i, j))))
        args.append(e)
    c_is, c_os, c_shape, c_sems, c_alias = _carry_call_args(carry, len(args), n_out)
    outs = pl.pallas_call(
        body, name=name, grid=(gi, gj), in_specs=in_specs + c_is,
        out_specs=[pl.BlockSpec((tm, tn), lambda i, j: (i, j)) for _ in out_dtypes] + c_os,
        out_shape=[jax.ShapeDtypeStruct((M, N), d) for d in out_dtypes] + c_shape,
        input_output_aliases=c_alias, scratch_shapes=c_sems,
        compiler_params=_params(),
    )(*args, *(carry["ins"] if carry else ()))
    return (outs[:n_out], list(outs[n_out:])) if carry else outs


def _wdim(b, axis):
    return b[0].shape[1 + axis] if isinstance(b, tuple) else b.shape[axis]


def _mm1(name, a, b, ta=False, tb=False, out_dtype=F32, bcol=0, N=None, **kw):
    M = a.shape[1] if ta else a.shape[0]
    if N is None:
        N = _wdim(b, 0 if tb else 1)
    res = _mm(name, [(a, b, ta, tb, 0, bcol)], M, N, [out_dtype], **kw)
    return (res[0][0], res[1]) if kw.get("carry") else res[0]


def _rowwise(name, fn, tiled, vecs, outs, reds=(), tr=128, carry=None):
    R = tiled[0].shape[0]
    tr = _pick(R, (tr, 64, 32, 16, 8))
    nt, nv, no, nr = len(tiled), len(vecs), len(outs), len(reds)

    def body(*refs):
        refs, c_in, c_out, c_sems = _carry_split(refs, nt + nv, no + nr, 0, carry)
        _carry_run(carry, c_in, c_out, c_sems, pl.program_id(0) == 0, True)
        vals = [r[...] for r in refs[:nt + nv]]
        res = fn(*vals)
        for k in range(no):
            o_ref = refs[nt + nv + k]
            o_ref[...] = res[k].astype(o_ref.dtype)
        if nr:
            first = pl.program_id(0) == 0
            for k in range(nr):
                r_ref = refs[nt + nv + no + k]

                @pl.when(first)
                def _(r_ref=r_ref, v=res[no + k]):
                    r_ref[...] = v

                @pl.when(jnp.logical_not(first))
                def _(r_ref=r_ref, v=res[no + k]):
                    r_ref[...] += v
        _carry_run(carry, c_in, c_out, c_sems, pl.program_id(0) == R // tr - 1, False)

    in_specs = [pl.BlockSpec((tr, t.shape[1]), lambda i: (i, 0)) for t in tiled]
    in_specs += [pl.BlockSpec(v.shape, lambda i, n=v.ndim: (0,) * n) for v in vecs]
    out_specs = [pl.BlockSpec((tr, w), lambda i: (i, 0)) for (w, _) in outs]
    out_specs += [pl.BlockSpec((1, w), lambda i: (0, 0)) for w in reds]
    out_shape = [jax.ShapeDtypeStruct((R, w), d) for (w, d) in outs]
    out_shape += [jax.ShapeDtypeStruct((1, w), F32) for w in reds]
    c_is, c_os, c_shape, c_sems, c_alias = _carry_call_args(carry, nt + nv, no + nr)
    res = pl.pallas_call(
        body, name=name, grid=(R // tr,), in_specs=in_specs + c_is, out_specs=out_specs + c_os,
        out_shape=out_shape + c_shape, input_output_aliases=c_alias, scratch_shapes=c_sems,
        compiler_params=_params(),
    )(*tiled, *vecs, *(carry["ins"] if carry else ()))
    return (res[:no + nr], list(res[no + nr:])) if carry else res


def _colsum(v):
    return jnp.sum(v, axis=0, keepdims=True)


def _ln_stats(z):
    mu = jnp.mean(z, axis=-1, keepdims=True)
    zc = z - mu
    var = jnp.mean(zc * zc, axis=-1, keepdims=True)
    rstd = lax.rsqrt(var + LN_EPS)
    return zc * rstd, rstd


def _ln_bwd(dout, xhat, rstd, lg):
    dxh = dout * lg
    m1 = jnp.mean(dxh, axis=-1, keepdims=True)
    m2 = jnp.mean(dxh * xhat, axis=-1, keepdims=True)
    return rstd * (dxh - m1 - xhat * m2)


def _modulate(x, sc, sh):
    D = x.shape[1]
    return _rowwise("modulate", lambda x, sc, sh: ((x * (1.0 + sc) + sh),), [x], [sc, sh], [(D, MXU_DTYPE)])[0]


def _resid_ln_mod(x, y, g, lg, lb, sc_n, sh_n, alpha):
    D = x.shape[1]

    def fn(x, y, g, lg, lb, sc, sh):
        z = alpha * x + (1.0 + g) * y
        xhat, _ = _ln_stats(z)
        xo = xhat * lg + lb
        return z, xo, xo * (1.0 + sc) + sh

    return _rowwise("resid_ln_mod", fn, [x, y], [g, lg, lb, sc_n, sh_n], [(D, F32), (D, F32), (D, MXU_DTYPE)])


def _final_ln_loss(x, y, tgt, g, lg, lb, alpha):
    D = x.shape[1]

    def fn(x, y, t, g, lg, lb):
        z = alpha * x + (1.0 + g) * y
        xhat, rstd = _ln_stats(z)
        out = xhat * lg + lb
        err = out - t
        loss = jnp.sum(jnp.sum(err * err, axis=-1, keepdims=True), axis=0, keepdims=True)
        dout = err * (1.0 / D)
        dz = _ln_bwd(dout, xhat, rstd, lg)
        return (alpha * dz, (1.0 + g) * dz, jnp.broadcast_to(loss, (1, LANE)),
                _colsum(dout * xhat), _colsum(dout), _colsum(dz * y))

    return _rowwise("final_ln_loss", fn, [x, y, tgt], [g, lg, lb], [(D, F32), (D, MXU_DTYPE)], [LANE, D, D, D])


def _bwd_boundary(dx_res, dh, z_p, y_p, sc, g_p, lg_p, lb_p, alpha, carry=None):
    D = dh.shape[1]

    def fn(dxr, dh, z, y, sc, g, lg, lb):
        xhat, rstd = _ln_stats(z)
        x_in = xhat * lg + lb
        dx = dxr + dh * (1.0 + sc)
        dz = _ln_bwd(dx, xhat, rstd, lg)
        return (alpha * dz, (1.0 + g) * dz,
                _colsum(dh * x_in), _colsum(dh), _colsum(dx * xhat), _colsum(dx), _colsum(dz * y))

    return _rowwise("bwd_boundary", fn, [dx_res, dh, z_p, y_p], [sc, g_p, lg_p, lb_p],
                    [(D, F32), (D, MXU_DTYPE)], [D, D, D, D, D], carry=carry)


def _first_bwd(dx_res, dh, x, sc, carry=None):
    D = dh.shape[1]

    def fn(dxr, dh, x, sc):
        return dxr + dh * (1.0 + sc), _colsum(dh * x), _colsum(dh)

    return _rowwise("first_bwd", fn, [dx_res, dh, x], [sc], [(D, F32)], [D, D], carry=carry)


def _ride(ride, phase, bufs=None):
    if ride is None:
        return None
    bufs = ride["bufs"] if bufs is None else bufs
    n = ride.get("n_ici", len(bufs)) if phase == "ici" else len(bufs)
    return _gather_carry(bufs[:n], ride["lays"][:n], [0] * n, phase)


def _ride_on(ride, got):
    return list(got) + list(ride["bufs"][len(got):]) if ride else []


def _ffn_fwd(h, w1, w3, w2, ride=None, early=None):
    S, F = h.shape[0], _wdim(w1, 1)

    def epi(accs):
        a, b = accs
        return a, b, a * _sigmoid(a) * b

    res = _mm("ffn_up", [(h, w1, False, False, 0, 0), (h, w3, False, False, 1, 0)], S, F,
              [MXU_DTYPE, MXU_DTYPE, MXU_DTYPE], epilogue=epi, carry=_ride(ride, "ici"))
    (a, b, u), bufs = res if ride else (res, [])
    c_d2d, c_ici = _ride(ride, "d2d", bufs), _ride(early, "ici")
    carry = _merge_carries(c_d2d, c_ici)
    res = _mm1("ffn_down", u, w2, carry=carry)
    y, got = res if carry else (res, [])
    return y, (a, b, u), got[:len(bufs)], got[len(bufs):]


def _ffn_bwd(dy, h, saved, w1, w3, w2, carry_du=None, carry_dh=None, swap_of=None):
    a, b, u = saved
    S, F = a.shape
    D = h.shape[1]

    def epi(accs, a, b):
        du = accs[0]
        a = a.astype(F32)
        b = b.astype(F32)
        sg = _sigmoid(a)
        return du * b * (sg * (1.0 + a * (1.0 - sg))), du * (a * sg)

    res = _mm("ffn_du", [(dy, w2, False, True, 0, 0)], S, F, [MXU_DTYPE, MXU_DTYPE], epilogue=epi, extras=(a, b),
              carry=carry_du)
    (da, db), got_du = res if carry_du else (res, [])
    dw2 = _mm1("ffn_dw2", u, dy, ta=True, out_dtype=WIRE_DTYPE)
    dw1, dw3 = _mm("ffn_dw13", [(h, da, True, False, 0, 0), (h, db, True, False, 1, 0)], D, F,
                   [WIRE_DTYPE, WIRE_DTYPE], epilogue=lambda accs: (accs[0], accs[1]))
    carry = _merge_carries(carry_dh, swap_of((dw1, dw3, dw2)) if swap_of else None)
    res = _mm("ffn_dh", [(da, w1, False, True, 0, 0), (db, w3, False, True, 0, 0)], S, D, [F32], tn=256, carry=carry)
    (dh,), got = res if carry else (res, [])
    n_dh = len(carry_dh["out_shape"]) if carry_dh else 0
    return dh, dw1, dw3, dw2, got_du, got[:n_dh], got[n_dh:]


def _rope_tables(pos):
    j = np.arange(LANE)
    invf = ROPE_THETA ** (-jnp.arange(0, ROPE_DIM, 2, dtype=F32) / ROPE_DIM)
    invf = invf[(j % ROPE_DIM) // 2].reshape(1, LANE)
    sgn = jnp.asarray(np.where(j % 2 == 0, -1.0, 1.0).reshape(1, LANE), F32)

    def fn(pos, invf, sgn):
        ang = pos.astype(F32) * invf
        return jnp.cos(ang), jnp.sin(ang) * sgn

    return _rowwise("rope_tables", fn, [pos], [invf, sgn], [(LANE, F32), (LANE, F32)], tr=256)


def _pair_swap(x):
    w = x.shape[1]
    even = (lax.broadcasted_iota(jnp.int32, x.shape, 1) % 2) == 0
    return jnp.where(even, pltpu.roll(x, w - 1, 1), pltpu.roll(x, 1, 1))


def _rope_fwd(x, c, s):
    return x * c + _pair_swap(x) * s


def _rope_bwd(d, c, s):
    return d * c + _pair_swap(d * s)


ATT_T = 512


def _dot_nt(a, b):
    return lax.dot_general(a, b, (((1,), (1,)), ((), ())), preferred_element_type=F32)


def _dot_tn(a, b):
    return lax.dot_general(a, b, (((0,), (0,)), ((), ())), preferred_element_type=F32)


def _dot_nn(a, b):
    return lax.dot_general(a, b, (((1,), (0,)), ((), ())), preferred_element_type=F32)


def _diag_mask(T, gran):
    r = lax.broadcasted_iota(jnp.int32, (T, T), 0)
    c = lax.broadcasted_iota(jnp.int32, (T, T), 1)
    if gran > 1:
        sh = int(np.log2(gran))
        r, c = lax.shift_right_logical(r, sh), lax.shift_right_logical(c, sh)
    return r >= c


def _attn_specs(S, H, T, mla, col_q, col_k, col_v):
    W = 2 * HEAD_DIM
    specs = [pl.BlockSpec((T, W), lambda p, i: (i, col_q + p))]
    if mla:
        specs.append(pl.BlockSpec((T, 2 * ROPE_DIM), lambda p, i: (i, p)))
    specs.append(pl.BlockSpec((S, W), lambda p, i: (0, col_k + p)))
    if mla:
        specs.append(pl.BlockSpec((S, ROPE_DIM), lambda p, i: (0, 0)))
    specs.append(pl.BlockSpec((S, W), lambda p, i: (0, col_v + p)))
    if not mla:
        specs.append(pl.BlockSpec((2, T, 1), lambda p, i: (p, i, 0)))
        specs.append(pl.BlockSpec((2, 1, S), lambda p, i: (p, 0, 0)))
    return specs


def _attn_fwd(name, S, H, mla, q, k, v, q_pe=None, k_pe=None, cum_col=None, cum_row=None, cols=(0, 0, 0),
              carry=None):
    T = _pick(S, (ATT_T, 128))
    nq = S // T
    scale = (HEAD_DIM + ROPE_DIM) ** -0.5 if mla else HEAD_DIM ** -0.5
    gran = CHUNK if mla else 1

    def body(*refs):
        refs, c_in, c_out, c_sems = _carry_split(refs, 5, 3, 3, carry)
        if mla:
            q_ref, qpe_ref, k_ref, kpe_ref, v_ref, o_ref, of_ref, lse_ref, m_s, l_s, acc_s = refs
        else:
            q_ref, k_ref, v_ref, cc_ref, cr_ref, o_ref, of_ref, lse_ref, m_s, l_s, acc_s = refs
        hp, qi = pl.program_id(0), pl.program_id(1)
        _carry_run(carry, c_in, c_out, c_sems, jnp.logical_and(hp == 0, qi == 0), True)
        hls = [slice(hh * HEAD_DIM, (hh + 1) * HEAD_DIM) for hh in range(2)]
        qn = [q_ref[:, hl] for hl in hls]
        qp = [qpe_ref[:, hh * ROPE_DIM:(hh + 1) * ROPE_DIM] for hh in range(2)] if mla else None
        m_s[...] = jnp.full(m_s.shape, NEG, F32)
        l_s[...] = jnp.zeros(l_s.shape, F32)
        acc_s[...] = jnp.zeros(acc_s.shape, F32)

        def step(j, masked):
            rows = pl.ds(pl.multiple_of(j * T, T), T)
            for hh, hl in enumerate(hls):
                s = _dot_nt(qn[hh], k_ref[rows, hl])
                if mla:
                    s = s + _dot_nt(qp[hh], kpe_ref[rows, :])
                s = s * scale
                if not mla:
                    s = s + (cc_ref[hh] - cr_ref[hh, :, rows])
                if masked:
                    s = jnp.where(_diag_mask(T, gran), s, NEG)
                m_old = m_s[hh, :, 0:1]
                m_new = jnp.maximum(m_old, jnp.max(s, axis=-1, keepdims=True))
                p = jnp.exp(s - m_new)
                corr = jnp.exp(m_old - m_new)
                l_s[hh] = jnp.broadcast_to(corr * l_s[hh, :, 0:1] + jnp.sum(p, axis=-1, keepdims=True), (T, LANE))
                p_hi = p.astype(MXU_DTYPE)
                p_lo = (p - p_hi.astype(F32)).astype(MXU_DTYPE)
                vf = v_ref[rows, hl]
                acc_s[hh] = corr * acc_s[hh] + (_dot_nn(p_hi, vf) + _dot_nn(p_lo, vf))
                m_s[hh] = jnp.broadcast_to(m_new, (T, LANE))

        lax.fori_loop(0, qi, lambda j, c: (step(j, False), c)[1], 0)
        step(qi, True)
        for hh, hl in enumerate(hls):
            l = l_s[hh, :, 0:1]
            of = acc_s[hh] / l
            of_ref[:, hl] = of
            o_ref[:, hl] = of.astype(o_ref.dtype)
            lse_ref[hh] = jnp.broadcast_to(m_s[hh, :, 0:1] + jnp.log(l), (T, LANE))
        _carry_run(carry, c_in, c_out, c_sems, jnp.logical_and(hp == H // 2 - 1, qi == nq - 1), False)

    args = [q] + ([q_pe] if mla else []) + [k] + ([k_pe] if mla else []) + [v]
    if not mla:
        args += [cum_col, cum_row]
    c_is, c_os, c_shape, c_sems, c_alias = _carry_call_args(carry, 5, 3)
    res = pl.pallas_call(
        body, name=name, grid=(H // 2, nq),
        in_specs=_attn_specs(S, H, T, mla, *cols) + c_is,
        out_specs=[pl.BlockSpec((T, 2 * HEAD_DIM), lambda p, i: (i, p)),
                   pl.BlockSpec((T, 2 * HEAD_DIM), lambda p, i: (i, p)),
                   pl.BlockSpec((2, T, LANE), lambda p, i: (p, i, 0))] + c_os,
        out_shape=[jax.ShapeDtypeStruct((S, H * HEAD_DIM), MXU_DTYPE), jax.ShapeDtypeStruct((S, H * HEAD_DIM), F32),
                   jax.ShapeDtypeStruct((H, S, LANE), F32)] + c_shape,
        input_output_aliases=c_alias,
        scratch_shapes=[pltpu.VMEM((2, T, LANE), F32), pltpu.VMEM((2, T, LANE), F32),
                        pltpu.VMEM((2, T, HEAD_DIM), F32)] + c_sems,
        compiler_params=_params(),
    )(*args, *(carry["ins"] if carry else ()))
    return res[0], res[1], res[2], list(res[3:])


def _carry_split(refs, n_in, n_out, n_scr, carry):
    if carry is None:
        return refs, (), (), ()
    ci, co = len(carry["ins"]), len(carry["out_shape"])
    own = refs[:n_in] + refs[n_in + ci:n_in + ci + n_out] + refs[n_in + ci + n_out + co:n_in + ci + n_out + co + n_scr]
    return (own, refs[n_in:n_in + ci], refs[n_in + ci + n_out:n_in + ci + n_out + co],
            refs[n_in + ci + n_out + co + n_scr:])


def _carry_run(carry, c_in, c_out, c_sems, when, start):
    if carry is None:
        return

    @pl.when(when)
    def _():
        for cp in carry["copies"](c_in, c_out, *c_sems):
            if start:
                cp.start()
            else:
                cp.wait()


def _carry_call_args(carry, n_in, n_out):
    if carry is None:
        return [], [], [], [], {}
    sems = [pltpu.SemaphoreType.DMA(carry["sems"]), pltpu.SemaphoreType.DMA(carry["sems"])]
    alias = {n_in + i: n_out + o for i, o in carry.get("alias", {}).items()}
    return [ANY] * len(carry["ins"]), [ANY] * len(carry["out_shape"]), list(carry["out_shape"]), sems, alias


def _attn_bwd(name, S, H, mla, q, k, v, do, of, lse, q_pe=None, k_pe=None, cum_col=None, cum_row=None,
              cols=(0, 0, 0), carry=None):
    T = _pick(S, (ATT_T, 128))
    nq = S // T
    scale = (HEAD_DIM + ROPE_DIM) ** -0.5 if mla else HEAD_DIM ** -0.5
    gran = CHUNK if mla else 1
    dqk = HEAD_DIM + (ROPE_DIM if mla else 0)

    def body(*refs):
        refs, c_in, c_out, c_sems = _carry_split(refs, 8, 5, 2, carry)
        if mla:
            (q_ref, qpe_ref, k_ref, kpe_ref, v_ref, do_ref, of_ref, lse_ref,
             dq_ref, dk_ref, dv_ref, dqpe_ref, dkpe_ref, dq_s, r_s) = refs
        else:
            (q_ref, k_ref, v_ref, cc_ref, cr_ref, do_ref, of_ref, lse_ref,
             dq_ref, dk_ref, dv_ref, dck_ref, dcq_ref, dq_s, r_s) = refs
        hp, qi = pl.program_id(0), pl.program_id(1)
        _carry_run(carry, c_in, c_out, c_sems, jnp.logical_and(hp == 0, qi == 0), True)

        @pl.when(qi == 0)
        def _():
            dk_ref[...] = jnp.zeros(dk_ref.shape, F32)
            dv_ref[...] = jnp.zeros(dv_ref.shape, F32)
            if not mla:
                dck_ref[...] = jnp.zeros(dck_ref.shape, F32)

        if mla:
            @pl.when(jnp.logical_and(qi == 0, hp == 0))
            def _():
                dkpe_ref[...] = jnp.zeros(dkpe_ref.shape, F32)

        hls = [slice(hh * HEAD_DIM, (hh + 1) * HEAD_DIM) for hh in range(2)]
        qn = [q_ref[:, hl] for hl in hls]
        qp = [qpe_ref[:, hh * ROPE_DIM:(hh + 1) * ROPE_DIM] for hh in range(2)] if mla else None
        dof = [do_ref[:, hl] for hl in hls]
        delta = [jnp.sum(dof[hh].astype(F32) * of_ref[:, hl], axis=-1, keepdims=True) for hh, hl in enumerate(hls)]
        lse = [lse_ref[hh][:, 0:1] for hh in range(2)]
        dq_s[...] = jnp.zeros(dq_s.shape, F32)
        r_s[...] = jnp.zeros(r_s.shape, F32)

        def step(j, masked):
            rows = pl.ds(pl.multiple_of(j * T, T), T)
            for hh, hl in enumerate(hls):
                kn = k_ref[rows, hl]
                s = _dot_nt(qn[hh], kn)
                if mla:
                    kp = kpe_ref[rows, :]
                    s = s + _dot_nt(qp[hh], kp)
                s = s * scale
                if not mla:
                    s = s + (cc_ref[hh] - cr_ref[hh, :, rows])
                if masked:
                    s = jnp.where(_diag_mask(T, gran), s, NEG)
                p = jnp.exp(s - lse[hh])
                dp = _dot_nt(dof[hh], v_ref[rows, hl])
                ds = p * (dp - delta[hh])
                dv_ref[rows, hl] += _dot_tn(p.astype(MXU_DTYPE), dof[hh])
                dsb = (ds * scale).astype(MXU_DTYPE)
                dk_ref[rows, hl] += _dot_tn(dsb, qn[hh])
                dq_s[hh, :, :HEAD_DIM] += _dot_nn(dsb, kn)
                if mla:
                    dkpe_ref[rows, :] += _dot_tn(dsb, qp[hh])
                    dq_s[hh, :, HEAD_DIM:] += _dot_nn(dsb, kp)
                else:
                    dck_ref[hh, :, rows] -= jnp.sum(ds, axis=0, keepdims=True)
                    r_s[hh] += jnp.broadcast_to(jnp.sum(ds, axis=-1, keepdims=True), (T, LANE))

        lax.fori_loop(0, qi, lambda j, c: (step(j, False), c)[1], 0)
        step(qi, True)
        for hh, hl in enumerate(hls):
            dq_ref[:, hl] = dq_s[hh, :, :HEAD_DIM].astype(dq_ref.dtype)
            if mla:
                dqpe_ref[:, hh * ROPE_DIM:(hh + 1) * ROPE_DIM] = dq_s[hh, :, HEAD_DIM:]
            else:
                dcq_ref[hh] = r_s[hh, :, 0:1]
        _carry_run(carry, c_in, c_out, c_sems, jnp.logical_and(hp == H // 2 - 1, qi == nq - 1), False)

    W = 2 * HEAD_DIM
    args = [q] + ([q_pe] if mla else []) + [k] + ([k_pe] if mla else []) + [v]
    if not mla:
        args += [cum_col, cum_row]
    args += [do, of, lse]
    in_specs = _attn_specs(S, H, T, mla, *cols)
    in_specs += [pl.BlockSpec((T, W), lambda p, i: (i, p)), pl.BlockSpec((T, W), lambda p, i: (i, p)),
                 pl.BlockSpec((2, T, LANE), lambda p, i: (p, i, 0))]
    out_specs = [pl.BlockSpec((T, W), lambda p, i: (i, p)), pl.BlockSpec((S, W), lambda p, i: (0, p)),
                 pl.BlockSpec((S, W), lambda p, i: (0, p))]
    out_shape = [jax.ShapeDtypeStruct((S, H * HEAD_DIM), MXU_DTYPE), jax.ShapeDtypeStruct((S, H * HEAD_DIM), F32),
                 jax.ShapeDtypeStruct((S, H * HEAD_DIM), F32)]
    if mla:
        out_specs += [pl.BlockSpec((T, 2 * ROPE_DIM), lambda p, i: (i, p)),
                      pl.BlockSpec((S, ROPE_DIM), lambda p, i: (0, 0))]
        out_shape += [jax.ShapeDtypeStruct((S, H * ROPE_DIM), F32), jax.ShapeDtypeStruct((S, ROPE_DIM), F32)]
    else:
        out_specs += [pl.BlockSpec((2, 1, S), lambda p, i: (p, 0, 0)), pl.BlockSpec((2, T, 1), lambda p, i: (p, i, 0))]
        out_shape += [jax.ShapeDtypeStruct((H, 1, S), F32), jax.ShapeDtypeStruct((H, S, 1), F32)]
    assert len(args) == 8 and len(out_shape) == 5
    c_is, c_os, c_shape, c_sems, c_alias = _carry_call_args(carry, 8, 5)
    res = pl.pallas_call(
        body, name=name, grid=(H // 2, nq), in_specs=in_specs + c_is, out_specs=out_specs + c_os,
        out_shape=out_shape + c_shape, input_output_aliases=c_alias,
        scratch_shapes=[pltpu.VMEM((2, T, dqk), F32), pltpu.VMEM((2, T, LANE), F32)] + c_sems,
        compiler_params=_params(),
    )(*args, *(carry["ins"] if carry else ()))
    return (res[:5], res[5:]) if carry else (res, [])


def _mla_prep(lat, cos, sin, qn, kvn, ql, kvl):
    def fn(lat, c, s, qn, kvn):
        ql_ = lat[:, :ql]
        kv_ = lat[:, ql:ql + kvl]
        kp = lat[:, ql + kvl:]
        cq = ql_ * lax.rsqrt(jnp.mean(ql_ * ql_, axis=-1, keepdims=True) + RMS_EPS) * qn
        ckv = kv_ * lax.rsqrt(jnp.mean(kv_ * kv_, axis=-1, keepdims=True) + RMS_EPS) * kvn
        kp2 = jnp.concatenate([kp, jnp.zeros_like(kp)], axis=-1)
        kr = _rope_fwd(kp2, c, s)[:, :ROPE_DIM]
        return cq, ckv, kr

    return _rowwise("mla_prep", fn, [lat, cos, sin], [qn, kvn],
                    [(ql, MXU_DTYPE), (kvl, MXU_DTYPE), (ROPE_DIM, MXU_DTYPE)])


def _mla_prep_bwd(lat, cos, sin, qn, kvn, dcq, dckv, dkr, ql, kvl):
    def fn(lat, c, s, dcq, dckv, dkr, qn, kvn):
        outs, reds = [], []
        for (x, g, d) in ((lat[:, :ql], qn, dcq), (lat[:, ql:ql + kvl], kvn, dckv)):
            r = lax.rsqrt(jnp.mean(x * x, axis=-1, keepdims=True) + RMS_EPS)
            n = x * r
            dn = d * g
            outs.append(r * (dn - n * jnp.mean(dn * n, axis=-1, keepdims=True)))
            reds.append(_colsum(d * n))
        d2 = jnp.concatenate([dkr, jnp.zeros_like(dkr)], axis=-1)
        outs.append(_rope_bwd(d2, c, s)[:, :ROPE_DIM])
        return (jnp.concatenate(outs, axis=-1), *reds)

    return _rowwise("mla_prep_bwd", fn, [lat, cos, sin, dcq, dckv, dkr], [qn, kvn],
                    [(ql + kvl + ROPE_DIM, MXU_DTYPE)], [ql, kvl])


def _mla_fwd(h, w, rc, ride=None):
    S = h.shape[0]
    ql, kvl = w["q_norm"].shape[1], w["kv_norm"].shape[1]
    H = _wdim(w["w_uk"], 1) // HEAD_DIM
    n_nope, n_pe = H * HEAD_DIM, H * ROPE_DIM
    lat = _mm1("mla_down", h, w["w_down"])
    cq, ckv, kr = _mla_prep(lat, rc[0], rc[1], w["q_norm"], w["kv_norm"], ql, kvl)
    q_nope = _mm1("mla_uq_nope", cq, w["w_uq"], out_dtype=MXU_DTYPE, N=n_nope)
    q_pe = _mm("mla_uq_pe", [(cq, w["w_uq"], False, False, 0, n_nope // LANE)], S, n_pe, [MXU_DTYPE],
               epilogue=lambda accs, c, s: (_rope_fwd(accs[0], c, s),), row_extras=rc, tn=LANE)[0]
    k_nope, v = _mm("mla_ukv", [(ckv, w["w_uk"], False, False, 0, 0), (ckv, w["w_uv"], False, False, 1, 0)],
                    S, n_nope, [MXU_DTYPE, MXU_DTYPE], epilogue=lambda accs: (accs[0], accs[1]))
    o, of, lse, bufs = _attn_fwd("mla_attn_fwd", S, H, True, q_nope, k_nope, v, q_pe=q_pe, k_pe=kr,
                                 carry=_ride(ride, "ici"))
    res = _mm1("mla_wo", o, w["w_o"], carry=_ride(ride, "d2d", _ride_on(ride, bufs)))
    y, bufs = res if ride else (res, [])
    return y, (lat, cq, ckv, kr, q_nope, q_pe, k_nope, v, o, of, lse), bufs


def _mla_bwd(dy, h, saved, w, rc, carry):
    lat, cq, ckv, kr, q_nope, q_pe, k_nope, v, o, of, lse = saved
    S = h.shape[0]
    ql, kvl = w["q_norm"].shape[1], w["kv_norm"].shape[1]
    H = _wdim(w["w_uk"], 1) // HEAD_DIM
    n_nope, n_pe = H * HEAD_DIM, H * ROPE_DIM
    do = _mm1("mla_do", dy, w["w_o"], tb=True, out_dtype=MXU_DTYPE)
    dw_o = _mm1("mla_dwo", o, dy, ta=True, out_dtype=WIRE_DTYPE)
    (dq_nope, dk_nope, dv, dq_pe_r, dk_pe_r), carried = _attn_bwd(
        "mla_attn_bwd", S, H, True, q_nope, k_nope, v, do, of, lse, q_pe=q_pe, k_pe=kr, carry=carry)

    def unrope(d, c, s):
        reps = (1, n_pe // LANE)
        return (_rope_bwd(d, jnp.tile(c, reps), jnp.tile(s, reps)),)

    dq_pe = _rowwise("mla_unrope_q", unrope, [dq_pe_r, rc[0], rc[1]], [], [(n_pe, MXU_DTYPE)])[0]
    dq = jnp.concatenate([dq_nope, dq_pe], axis=1)
    dw_uq = _mm1("mla_dwuq", cq, dq, ta=True, out_dtype=WIRE_DTYPE)
    dcq = _mm1("mla_dcq", dq, w["w_uq"], tb=True)
    dw_uk, dw_uv = _mm("mla_dwukv", [(ckv, dk_nope, True, False, 0, 0), (ckv, dv, True, False, 1, 0)], kvl, n_nope,
                       [WIRE_DTYPE, WIRE_DTYPE], epilogue=lambda accs: (accs[0], accs[1]))
    dckv = _mm("mla_dckv", [(dk_nope, w["w_uk"], False, True, 0, 0), (dv, w["w_uv"], False, True, 0, 0)],
               S, kvl, [F32])[0]
    dlat, dqn, dkvn = _mla_prep_bwd(lat, rc[0], rc[1], w["q_norm"], w["kv_norm"], dcq, dckv, dk_pe_r, ql, kvl)
    dw_down = _mm1("mla_dwdown", h, dlat, ta=True, out_dtype=WIRE_DTYPE)
    dh = _mm1("mla_dh", dlat, w["w_down"], tb=True)
    return dh, dict(w_down=dw_down, q_norm=dqn, w_uq=dw_uq, kv_norm=dkvn, w_uk=dw_uk, w_uv=dw_uv, w_o=dw_o), carried


def _log_sigmoid(z):
    return jnp.minimum(z, 0.0) - jnp.log(1.0 + jnp.exp(-jnp.abs(z)))


def _fox_gate_fwd(f, bf):
    S = f.shape[0]
    B = LANE

    def body(f_ref, b_ref, cum_ref):
        r = lax.broadcasted_iota(jnp.int32, (B, B), 0)
        c = lax.broadcasted_iota(jnp.int32, (B, B), 1)
        tri = (r >= c).astype(F32)
        carry = jnp.zeros((1, LANE), F32)
        for blk in range(S // B):
            rows = slice(blk * B, (blk + 1) * B)
            lf = _log_sigmoid(f_ref[rows, :] + b_ref[...])
            cs = jnp.dot(tri, lf, precision=lax.Precision.HIGHEST, preferred_element_type=F32) + carry
            cum_ref[rows, :] = cs
            carry = cs[B - 1:B, :]

    return pl.pallas_call(body, name="fox_gate_fwd", out_shape=jax.ShapeDtypeStruct((S, LANE), F32),
                          compiler_params=_params())(f, bf)


def _fox_gate_bwd(dcum, f, bf):
    S = f.shape[0]
    B = LANE

    def body(d_ref, f_ref, b_ref, df_ref, db_ref):
        r = lax.broadcasted_iota(jnp.int32, (B, B), 0)
        c = lax.broadcasted_iota(jnp.int32, (B, B), 1)
        tri = (r <= c).astype(F32)
        carry = jnp.zeros((1, LANE), F32)
        db = jnp.zeros((1, LANE), F32)
        for blk in reversed(range(S // B)):
            rows = slice(blk * B, (blk + 1) * B)
            dlf = jnp.dot(tri, d_ref[rows, :], precision=lax.Precision.HIGHEST, preferred_element_type=F32) + carry
            carry = dlf[0:1, :]
            z = f_ref[rows, :] + b_ref[...]
            dz = dlf * _sigmoid(-z)
            df_ref[rows, :] = dz.astype(df_ref.dtype)
            db = db + jnp.sum(dz, axis=0, keepdims=True)
        db_ref[...] = db

    return pl.pallas_call(body, name="fox_gate_bwd",
                          out_shape=[jax.ShapeDtypeStruct((S, LANE), MXU_DTYPE), jax.ShapeDtypeStruct((1, LANE), F32)],
                          compiler_params=_params())(dcum, f, bf)


def _fox_fwd(h, w, ride=None):
    S, D = h.shape
    H = D // HEAD_DIM
    qkv = _mm1("fox_qkv", h, w["w_in"], out_dtype=MXU_DTYPE, N=3 * D)
    f = _mm1("fox_f", h, w["w_in"], bcol=3 * D // LANE, N=LANE, tn=LANE)
    cum = _fox_gate_fwd(f, w["b_f"])
    cumT = cum[:, :H].T
    cum_col, cum_row = cumT.reshape(H, S, 1), cumT.reshape(H, 1, S)
    nb = D // (2 * HEAD_DIM)
    o, of, lse, bufs = _attn_fwd("fox_attn_fwd", S, H, False, qkv, qkv, qkv, cum_col=cum_col, cum_row=cum_row,
                                 cols=(0, nb, 2 * nb), carry=_ride(ride, "ici"))
    res = _mm1("fox_wo", o, w["w_o"], carry=_ride(ride, "d2d", _ride_on(ride, bufs)))
    y, bufs = res if ride else (res, [])
    return y, (qkv, f, cum_col, cum_row, o, of, lse), bufs


def _fox_bwd(dy, h, saved, w, carry):
    qkv, f, cum_col, cum_row, o, of, lse = saved
    S, D = h.shape
    H = D // HEAD_DIM
    nb = D // (2 * HEAD_DIM)
    do = _mm1("fox_do", dy, w["w_o"], tb=True, out_dtype=MXU_DTYPE)
    dw_o = _mm1("fox_dwo", o, dy, ta=True, out_dtype=WIRE_DTYPE)
    (dq, dk, dv, dck, dcq), carried = _attn_bwd("fox_attn_bwd", S, H, False, qkv, qkv, qkv, do, of, lse,
                                                cum_col=cum_col, cum_row=cum_row, cols=(0, nb, 2 * nb), carry=carry)
    dcum = jnp.pad((dck.reshape(H, S) + dcq.reshape(H, S)).T, ((0, 0), (0, LANE - H)))
    df, dbf = _fox_gate_bwd(dcum, f, w["b_f"])
    dproj = jnp.concatenate([dq, dk.astype(MXU_DTYPE), dv.astype(MXU_DTYPE), df], axis=1)
    dw_in = _mm1("fox_dwin", h, dproj, ta=True, out_dtype=WIRE_DTYPE)
    dh = _mm1("fox_dh", dproj, w["w_in"], tb=True, tn=256)
    return dh, dict(w_in=dw_in, b_f=dbf[:, :H], w_o=dw_o), carried


def _place():
    x, y, c = lax.axis_index("x"), lax.axis_index("y"), lax.axis_index("c")
    return x, y, c, [(1 - x, y), (x, 1 - y), (1 - x, 1 - y)]


def _ag_small(name, blk):
    m, n = blk.shape

    def body(x_ref, out_ref, send_sems, recv_sems, local_sem):
        x, y, c, chips = _place()
        me, sibling = (x, y, c), (x, y, 1 - c)

        def rows(px, py, pc):
            return out_ref.at[pl.ds((4 * px + 2 * py + pc) * m, m), :]

        def copy(k, block, to, src=None):
            return pltpu.make_async_remote_copy(
                src_ref=rows(*block) if src is None else src, dst_ref=rows(*block),
                send_sem=send_sems.at[k], recv_sem=recv_sems.at[k], device_id=to, device_id_type=MESH)

        mine = pltpu.make_async_copy(x_ref, rows(*me), local_sem)
        mine.start()
        first = [copy(0, me, sibling, src=x_ref)]
        first += [copy(1 + j, me, (*chip, c), src=x_ref) for j, chip in enumerate(chips)]
        for cp in first:
            cp.start()
        passed = [copy(4 + j, (*chip, c), sibling) for j, chip in enumerate(chips)]
        for j, chip in enumerate(chips):
            copy(1 + j, (*chip, c), me).wait_recv()
            passed[j].start()
        copy(0, sibling, me).wait_recv()
        for j, chip in enumerate(chips):
            copy(4 + j, (*chip, 1 - c), me).wait_recv()
        for cp in first + passed:
            cp.wait_send()
        mine.wait()

    return pl.pallas_call(
        body, name=name, out_shape=jax.ShapeDtypeStruct((N_DEV * m, n), blk.dtype),
        in_specs=[pl.BlockSpec(memory_space=pltpu.VMEM)], out_specs=pl.BlockSpec(memory_space=pltpu.VMEM),
        scratch_shapes=[pltpu.SemaphoreType.DMA((7,)), pltpu.SemaphoreType.DMA((7,)), pltpu.SemaphoreType.DMA],
        compiler_params=_params(),
    )(blk)


def _half(ref, row_axis, c, rows):
    idx = [slice(None)] * len(ref.shape)
    idx[row_axis] = pl.ds(pl.multiple_of(c * rows, 16), rows)
    return ref.at[tuple(idx)]


def _shard(ref, layout, k):
    if layout == "row":
        return ref.at[:, k]
    w = ref.shape[2] // N_CHIPS
    return ref.at[:, :, pl.ds(pl.multiple_of(k * w, LANE), w)]


def _full_shape(shape, layout):
    L, r, w = shape
    return (L, N_CHIPS, r, w) if layout == "row" else (L, r, N_CHIPS * w)


def _cast_full(name, a, layout, k_idx, lyr):
    _, r, C = a.shape
    tr = _pick(r, (256, 128, 64, 32, 16))

    def body(k_ref, a_ref, o_ref):
        o_ref[...] = a_ref[...].astype(o_ref.dtype)

    if layout == "row":
        o_spec = pl.BlockSpec((None, None, tr, C), lambda l, i, k: (0, k[0], i, 0))
    else:
        o_spec = pl.BlockSpec((None, tr, C), lambda l, i, k: (0, i, k[0]))
    return pl.pallas_call(
        body, name=name,
        grid_spec=pltpu.PrefetchScalarGridSpec(
            num_scalar_prefetch=1, grid=(1, r // tr),
            in_specs=[pl.BlockSpec((None, tr, C), lambda l, i, k: (lyr, i, 0))], out_specs=o_spec),
        out_shape=jax.ShapeDtypeStruct(_full_shape((1, r, C), layout), WIRE_DTYPE),
        compiler_params=_params(),
    )(k_idx, a)


def _gather_carry(fulls, layouts, lyrs, phase):
    n = len(fulls)
    half_rows = [f.shape[2 if lay == "row" else 1] // 2 for f, lay in zip(fulls, layouts)]

    def copies(ins, outs, send_sems, recv_sems, off=0):
        x, y, c, chips = _place()
        cps = []
        for i in range(n):
            for j, chip in enumerate(chips):
                who, to = ((x, y), (*chip, c)) if phase == "ici" else (chip, (x, y, 1 - c))
                w = _half(_shard(outs[i].at[pl.ds(lyrs[i], 1)], layouts[i], 2 * who[0] + who[1]), 1, c, half_rows[i])
                cps.append(pltpu.make_async_remote_copy(
                    src_ref=w, dst_ref=w, send_sem=send_sems.at[off + i, j], recv_sem=recv_sems.at[off + i, j],
                    device_id=to, device_id_type=MESH))
        return cps

    return dict(ins=list(fulls), sems=(n, 3), copies=copies, alias={i: i for i in range(n)},
                out_shape=[jax.ShapeDtypeStruct(f.shape, f.dtype) for f in fulls])


def _run_carry(name, carry):
    n = len(carry["ins"])

    def body(*refs):
        cps = carry["copies"](refs[:n], refs[n:2 * n], *refs[2 * n:])
        for cp in cps:
            cp.start()
        for cp in cps:
            cp.wait()

    return pl.pallas_call(
        body, name=name, in_specs=[ANY] * n, out_specs=[ANY] * len(carry["out_shape"]), out_shape=carry["out_shape"],
        input_output_aliases=dict(carry.get("alias", {})),
        scratch_shapes=[pltpu.SemaphoreType.DMA(carry["sems"]), pltpu.SemaphoreType.DMA(carry["sems"])],
        compiler_params=_params(),
    )(*carry["ins"])


def _gather_weights(fulls, layouts, lyrs):
    n = len(fulls)
    half_rows = [f.shape[2 if lay == "row" else 1] // 2 for f, lay in zip(fulls, layouts)]

    def body(*refs):
        outs = refs[n:2 * n]
        send_sems, recv_sems = refs[2 * n:]
        x, y, c, chips = _place()
        sibling = (x, y, 1 - c)

        def window(i, kx, ky, half):
            return _half(_shard(outs[i].at[pl.ds(lyrs[i], 1)], layouts[i], 2 * kx + ky), 1, half, half_rows[i])

        first, passed = [], []
        for i in range(n):
            mine = window(i, x, y, c)
            for j, chip in enumerate(chips):
                cp = pltpu.make_async_remote_copy(
                    src_ref=mine, dst_ref=mine, send_sem=send_sems.at[i, j], recv_sem=recv_sems.at[i, j],
                    device_id=(*chip, c), device_id_type=MESH)
                cp.start()
                first.append(cp)
        for i in range(n):
            for j, chip in enumerate(chips):
                got = window(i, *chip, c)
                pltpu.make_async_remote_copy(
                    src_ref=got, dst_ref=got, send_sem=send_sems.at[i, j], recv_sem=recv_sems.at[i, j],
                    device_id=(*chip, c), device_id_type=MESH).wait_recv()
                cp = pltpu.make_async_remote_copy(
                    src_ref=got, dst_ref=got, send_sem=send_sems.at[i, 3 + j], recv_sem=recv_sems.at[i, 3 + j],
                    device_id=sibling, device_id_type=MESH)
                cp.start()
                passed.append(cp)
        for i in range(n):
            for j, chip in enumerate(chips):
                got = window(i, *chip, 1 - c)
                pltpu.make_async_remote_copy(
                    src_ref=got, dst_ref=got, send_sem=send_sems.at[i, 3 + j], recv_sem=recv_sems.at[i, 3 + j],
                    device_id=sibling, device_id_type=MESH).wait_recv()
        for cp in first + passed:
            cp.wait_send()

    return pl.pallas_call(
        body, name="gather_weights", in_specs=[ANY] * n, out_specs=[ANY] * n,
        out_shape=[jax.ShapeDtypeStruct(f.shape, f.dtype) for f in fulls],
        input_output_aliases={i: i for i in range(n)},
        scratch_shapes=[pltpu.SemaphoreType.DMA((n, 6)), pltpu.SemaphoreType.DMA((n, 6))],
        compiler_params=_params(),
    )(*fulls)


def _half_shape(shape, layout):
    s = list(shape)
    s[2 if layout == "row" else 1] //= 2
    return tuple(s)


def _swap_carry(grads, layouts):
    n = len(grads)
    row_axis = [2 if lay == "row" else 1 for lay in layouts]
    half_rows = [g.shape[ra] // 2 for g, ra in zip(grads, row_axis)]

    def copies(ins, outs, send_sems, recv_sems, off=0):
        x, y, c, _ = _place()
        return [pltpu.make_async_remote_copy(
            src_ref=_half(ins[i], row_axis[i], 1 - c, half_rows[i]), dst_ref=outs[i], send_sem=send_sems.at[off + i, 0],
            recv_sem=recv_sems.at[off + i, 0], device_id=(x, y, 1 - c), device_id_type=MESH) for i in range(n)]

    return dict(ins=list(grads), sems=(n, 3), copies=copies,
                out_shape=[jax.ShapeDtypeStruct(_half_shape(g.shape, lay), g.dtype) for g, lay in zip(grads, layouts)])


def _merge_carries(a, b):
    if a is None or b is None:
        return a if b is None else b
    na, ia, oa = a["sems"][0], len(a["ins"]), len(a["out_shape"])

    def copies(ins, outs, send_sems, recv_sems, off=0):
        return (a["copies"](ins[:ia], outs[:oa], send_sems, recv_sems, off)
                + b["copies"](ins[ia:], outs[oa:], send_sems, recv_sems, off + na))

    alias = dict(a.get("alias", {}))
    alias.update({ia + i: oa + o for i, o in b.get("alias", {}).items()})
    return dict(ins=a["ins"] + b["ins"], out_shape=a["out_shape"] + b["out_shape"], sems=(na + b["sems"][0], 3),
                copies=copies, alias=alias)


def _add_half(name, g, r, layout, c_idx):
    L = g.shape[0]
    if layout == "row":
        A, rows, W = L * N_CHIPS, g.shape[2] // 2, g.shape[3]
    else:
        A, rows, W = L, g.shape[1] // 2, g.shape[2]
    g3 = g.reshape(A, 2 * rows, W)
    r3 = r.reshape(A, rows, W)
    tr = _pick(rows, (256, 128, 64, 32, 16))
    nb = rows // tr

    def body(c_ref, g_ref, r_ref, o_ref):
        o_ref[...] = (g_ref[...].astype(F32) + r_ref[...].astype(F32)).astype(o_ref.dtype)

    out = pl.pallas_call(
        body, name=name,
        grid_spec=pltpu.PrefetchScalarGridSpec(
            num_scalar_prefetch=1, grid=(A, nb),
            in_specs=[pl.BlockSpec((None, tr, W), lambda a, i, c: (a, c[0] * nb + i, 0)),
                      pl.BlockSpec((None, tr, W), lambda a, i, c: (a, i, 0))],
            out_specs=pl.BlockSpec((None, tr, W), lambda a, i, c: (a, i, 0))),
        out_shape=jax.ShapeDtypeStruct((A, rows, W), WIRE_DTYPE),
        compiler_params=_params(),
    )(c_idx, g3, r3)
    return out.reshape(r.shape)


def _exchange_carry(parts, layouts):
    n = len(parts)

    def shard_half_shape(p, lay):
        if lay == "row":
            return (p.shape[0],) + p.shape[2:]
        return (p.shape[0], p.shape[1], p.shape[2] // N_CHIPS)

    def copies(ins, outs, send_sems, recv_sems, off=0):
        x, y, c, chips = _place()
        return [pltpu.make_async_remote_copy(
            src_ref=_shard(ins[i], layouts[i], 2 * kx + ky), dst_ref=outs[i].at[j], send_sem=send_sems.at[off + i, j],
            recv_sem=recv_sems.at[off + i, j], device_id=(kx, ky, c), device_id_type=MESH)
            for i in range(n) for j, (kx, ky) in enumerate(chips)]

    return dict(ins=list(parts), sems=(n, 3), copies=copies,
                out_shape=[jax.ShapeDtypeStruct((3,) + shard_half_shape(p, lay), p.dtype)
                           for p, lay in zip(parts, layouts)])


def _sum_shards(name, p, r, layout, kc_idx, dst, lyr, n_lyr):
    rows, W = r.shape[2], r.shape[3]
    tr = _pick(rows, (256, 128, 64, 32, 16))
    nb = rows // tr

    def body(kc_ref, p_ref, r_ref, *rest):
        acc = p_ref[...].astype(F32)
        for j in range(3):
            acc = acc + r_ref[j].astype(F32)
        rest[-1][...] = acc

    if layout == "row":
        p_spec = pl.BlockSpec((None, None, tr, W), lambda a, i, kc: (0, kc[0], i, 0))
    else:
        p_spec = pl.BlockSpec((None, tr, W), lambda a, i, kc: (0, i, kc[0]))
    in_specs = [p_spec, pl.BlockSpec((3, None, tr, W), lambda a, i, kc: (0, 0, i, 0))]
    args = [kc_idx, p, r]
    if dst is not None:
        in_specs.append(ANY)
        args.append(dst)
    return pl.pallas_call(
        body, name=name,
        grid_spec=pltpu.PrefetchScalarGridSpec(
            num_scalar_prefetch=1, grid=(1, nb), in_specs=in_specs,
            out_specs=pl.BlockSpec((None, tr, W), lambda a, i, kc: (lyr, kc[1] * nb + i, 0))),
        out_shape=jax.ShapeDtypeStruct((n_lyr, 2 * rows, W), F32),
        input_output_aliases={3: 0} if dst is not None else {},
        compiler_params=_params(),
    )(*args)


def _join_halves(shards):
    n = len(shards)

    def body(*refs):
        outs = refs[n:2 * n]
        send_sems, recv_sems = refs[2 * n:]
        x, y, c, _ = _place()
        cps = []
        for i in range(n):
            mine = _half(outs[i], 1, c, outs[i].shape[1] // 2)
            cp = pltpu.make_async_remote_copy(
                src_ref=mine, dst_ref=mine, send_sem=send_sems.at[i], recv_sem=recv_sems.at[i],
                device_id=(x, y, 1 - c), device_id_type=MESH)
            cp.start()
            cps.append(cp)
        for cp in cps:
            cp.wait()

    return pl.pallas_call(
        body, name="join_halves", in_specs=[ANY] * n, out_specs=[ANY] * n,
        out_shape=[jax.ShapeDtypeStruct(s.shape, s.dtype) for s in shards],
        input_output_aliases={i: i for i in range(n)},
        scratch_shapes=[pltpu.SemaphoreType.DMA((n,)), pltpu.SemaphoreType.DMA((n,))],
        compiler_params=_params(),
    )(*shards)


def _ada_fwd(c_all, ada_w, ada_b):
    L, D, w = ada_w.shape
    tn = _pick(w, (512, 256, 128))

    def body(c_ref, w_ref, b_ref, o_ref, a_ref):
        c = c_ref[...]
        act = (c * _sigmoid(c)).astype(MXU_DTYPE)
        a_ref[...] = act
        o_ref[...] = jnp.dot(act, w_ref[...].astype(MXU_DTYPE), preferred_element_type=F32) + b_ref[...]

    return pl.pallas_call(
        body, name="ada_fwd", grid=(L, w // tn),
        in_specs=[pl.BlockSpec((16, D), lambda l, j: (0, 0)), pl.BlockSpec((None, D, tn), lambda l, j: (l, 0, j)),
                  pl.BlockSpec((None, 1, tn), lambda l, j: (l, 0, j))],
        out_specs=[pl.BlockSpec((None, 16, tn), lambda l, j: (l, 0, j)), pl.BlockSpec((16, D), lambda l, j: (0, 0))],
        out_shape=[jax.ShapeDtypeStruct((L, 16, w), F32), jax.ShapeDtypeStruct((16, D), MXU_DTYPE)],
        compiler_params=_params(),
    )(c_all, ada_w, ada_b)


def _sum_devices(name, parts):
    n, R, W = parts.shape
    tw = _pick(W, (2048, 1024, 512, 256, 128))

    def body(p_ref, o_ref):
        acc = p_ref[0]
        for d in range(1, n):
            acc = acc + p_ref[d]
        o_ref[...] = acc

    return pl.pallas_call(
        body, name=name, grid=(W // tw,), in_specs=[pl.BlockSpec((n, R, tw), lambda j: (0, 0, j))],
        out_specs=pl.BlockSpec((R, tw), lambda j: (0, j)), out_shape=jax.ShapeDtypeStruct((R, W), F32),
        compiler_params=_params(),
    )(parts)


def _adamw_math(w, g, m, v):
    m2 = ADAM_B1 * m + (1.0 - ADAM_B1) * g
    v2 = ADAM_B2 * v + (1.0 - ADAM_B2) * (g * g)
    m_hat = m2 / (1.0 - ADAM_B1 ** ADAM_STEP)
    v_hat = v2 / (1.0 - ADAM_B2 ** ADAM_STEP)
    return -ADAM_LR * (m_hat / (jnp.sqrt(v_hat) + ADAM_EPS) + ADAM_WD * w), m2, v2


def _adamw(name, w, g, m, v):
    W = w.shape[1]
    tr = 256 if W <= 1024 else (128 if W <= 2048 else 64)
    return _rowwise(name, lambda w, g, m, v: (g,) + _adamw_math(w, g, m, v), [w, g, m, v], [], [(W, F32)] * 4, tr=tr)


def _adamw_ada(c_act, dmod, w, m, v, carry=None):
    L, D, wd = w.shape
    tr = LANE

    def body(*refs):
        refs, c_in, c_out, c_sems = _carry_split(refs, 5, 4, 0, carry)
        c_ref, d_ref, w_ref, m_ref, v_ref, g_ref, dl_ref, m2_ref, v2_ref = refs
        pl_, pi = pl.program_id(0), pl.program_id(1)
        _carry_run(carry, c_in, c_out, c_sems, jnp.logical_and(pl_ == 0, pi == 0), True)
        g = _dot_tn(c_ref[...], d_ref[...].astype(MXU_DTYPE))
        g_ref[...] = g
        dl_ref[...], m2_ref[...], v2_ref[...] = _adamw_math(w_ref[...], g, m_ref[...], v_ref[...])
        _carry_run(carry, c_in, c_out, c_sems, jnp.logical_and(pl_ == L - 1, pi == D // tr - 1), False)

    big = pl.BlockSpec((None, tr, wd), lambda l, i: (l, i, 0))
    c_is, c_os, c_shape, c_sems, c_alias = _carry_call_args(carry, 5, 4)
    res = pl.pallas_call(
        body, name="adamw_ada_w", grid=(L, D // tr),
        in_specs=[pl.BlockSpec((16, tr), lambda l, i: (0, i)), pl.BlockSpec((None, 16, wd), lambda l, i: (l, 0, 0)),
                  big, big, big] + c_is,
        out_specs=[big] * 4 + c_os, out_shape=[jax.ShapeDtypeStruct(w.shape, F32)] * 4 + c_shape,
        input_output_aliases=c_alias, scratch_shapes=c_sems,
        compiler_params=_params(),
    )(c_act, dmod, w, m, v, *(carry["ins"] if carry else ()))
    return res[:4], list(res[4:])


def _flat(a):
    return a.reshape(-1, a.shape[-1])


BIG = ("ffn_w1", "ffn_w3", "ffn_w2", "mla_w_down", "mla_w_uq", "mla_w_uk", "mla_w_uv", "mla_w_o", "fox_w_in",
       "fox_w_o")
LAYOUT = dict(ffn_w1="col", ffn_w3="col", ffn_w2="row", mla_w_down="row", mla_w_uq="col", mla_w_uk="col",
              mla_w_uv="col", mla_w_o="row", fox_w_in="row", fox_w_o="row")
FFN = ("ffn_w1", "ffn_w3", "ffn_w2")
MLA = ("mla_w_down", "mla_w_uq", "mla_w_uk", "mla_w_uv", "mla_w_o")
FOX = ("fox_w_in", "fox_w_o")
SMALL = ("ln1_g", "ln1_b", "ln2_g", "ln2_b", "mla_q_norm", "mla_kv_norm", "fox_b_f")
WEIGHTS = ("ada_w", "ada_b", "ln1_g", "ln1_b", "ln2_g", "ln2_b", "ffn_w1", "ffn_w3", "ffn_w2", "mla_w_down",
           "mla_q_norm", "mla_w_uq", "mla_kv_norm", "mla_w_uk", "mla_w_uv", "mla_w_o", "fox_w_in", "fox_b_f",
           "fox_w_o")


def _uq_perm(H):
    d = HEAD_DIM + ROPE_DIM
    nope = (np.arange(H)[:, None] * d + np.arange(HEAD_DIM)[None, :]).reshape(-1)
    pe = (np.arange(H)[:, None] * d + HEAD_DIM + np.arange(ROPE_DIM)[None, :]).reshape(-1)
    return np.concatenate([nope, pe])


def kernel(x, c, positions, ada_w, ada_b, ln1_g, ln1_b, ln2_g, ln2_b, ffn_w1, ffn_w3, ffn_w2, mla_w_down, mla_q_norm, mla_w_uq, mla_kv_norm, mla_w_uk, mla_w_uv, mla_w_o, fox_w_in, fox_b_f, fox_w_o, loss_target, m_ada_w, m_ada_b, m_ln1_g, m_ln1_b, m_ln2_g, m_ln2_b, m_ffn_w1, m_ffn_w3, m_ffn_w2, m_mla_w_down, m_mla_q_norm, m_mla_w_uq, m_mla_kv_norm, m_mla_w_uk, m_mla_w_uv, m_mla_w_o, m_fox_w_in, m_fox_b_f, m_fox_w_o, v_ada_w, v_ada_b, v_ln1_g, v_ln1_b, v_ln2_g, v_ln2_b, v_ffn_w1, v_ffn_w3, v_ffn_w2, v_mla_w_down, v_mla_q_norm, v_mla_w_uq, v_mla_kv_norm, v_mla_w_uk, v_mla_w_uv, v_mla_w_o, v_fox_w_in, v_fox_b_f, v_fox_w_o):
    W = dict(ada_w=ada_w, ada_b=ada_b, ln1_g=ln1_g, ln1_b=ln1_b, ln2_g=ln2_g, ln2_b=ln2_b, ffn_w1=ffn_w1,
             ffn_w3=ffn_w3, ffn_w2=ffn_w2, mla_w_down=mla_w_down, mla_q_norm=mla_q_norm, mla_w_uq=mla_w_uq,
             mla_kv_norm=mla_kv_norm, mla_w_uk=mla_w_uk, mla_w_uv=mla_w_uv, mla_w_o=mla_w_o, fox_w_in=fox_w_in,
             fox_b_f=fox_b_f, fox_w_o=fox_w_o)
    Mo = dict(ada_w=m_ada_w, ada_b=m_ada_b, ln1_g=m_ln1_g, ln1_b=m_ln1_b, ln2_g=m_ln2_g, ln2_b=m_ln2_b,
              ffn_w1=m_ffn_w1, ffn_w3=m_ffn_w3, ffn_w2=m_ffn_w2, mla_w_down=m_mla_w_down, mla_q_norm=m_mla_q_norm,
              mla_w_uq=m_mla_w_uq, mla_kv_norm=m_mla_kv_norm, mla_w_uk=m_mla_w_uk, mla_w_uv=m_mla_w_uv,
              mla_w_o=m_mla_w_o, fox_w_in=m_fox_w_in, fox_b_f=m_fox_b_f, fox_w_o=m_fox_w_o)
    Vo = dict(ada_w=v_ada_w, ada_b=v_ada_b, ln1_g=v_ln1_g, ln1_b=v_ln1_b, ln2_g=v_ln2_g, ln2_b=v_ln2_b,
              ffn_w1=v_ffn_w1, ffn_w3=v_ffn_w3, ffn_w2=v_ffn_w2, mla_w_down=v_mla_w_down, mla_q_norm=v_mla_q_norm,
              mla_w_uq=v_mla_w_uq, mla_kv_norm=v_mla_kv_norm, mla_w_uk=v_mla_w_uk, mla_w_uv=v_mla_w_uv,
              mla_w_o=v_mla_w_o, fox_w_in=v_fox_w_in, fox_b_f=v_fox_b_f, fox_w_o=v_fox_w_o)

    S, D = x.shape[1], x.shape[2]
    L = ada_w.shape[0]
    alpha = float((2 * L) ** 0.25)
    H_mla = mla_w_uk.shape[2] * N_CHIPS // HEAD_DIM
    H_fox = D // HEAD_DIM
    xi, yi, ci = lax.axis_index("x"), lax.axis_index("y"), lax.axis_index("c")
    chip = 2 * xi + yi
    dev = 2 * chip + ci
    c_idx = jnp.reshape(ci, (1,)).astype(jnp.int32)
    x0, tgt = x[0], loss_target[0]
    pos = positions.reshape(S, 1)

    c_all = _ag_small("gather_c", jnp.pad(c, ((0, 7), (0, 0)))).reshape(N_DEV, 8, D)[:, 0]
    w_ada = ada_w.shape[2]
    ada_b_sh = lax.dynamic_slice_in_dim(ada_b, chip * w_ada, w_ada, axis=1).reshape(L, 1, w_ada)
    mod_sh, c_act = _ada_fwd(jnp.pad(c_all, ((0, 8), (0, 0))), ada_w, ada_b_sh)
    mod_all = _ag_small("gather_mod", mod_sh.transpose(1, 0, 2).reshape(16, L * w_ada))
    mod_all = mod_all.reshape(N_CHIPS, 2, 16, L, w_ada)[:, 0]
    mod = lax.dynamic_index_in_dim(mod_all, dev, axis=1, keepdims=False)
    mod = mod.transpose(1, 0, 2).reshape(L, 6, 1, D)

    kc_idx = jnp.stack([chip, ci]).astype(jnp.int32)
    raw = {n: [_cast_full("cast_" + n, W[n], LAYOUT[n], kc_idx, l) for l in range(W[n].shape[0])] for n in BIG}
    full = {n: [None] * W[n].shape[0] for n in BIG}
    perm = _uq_perm(H_mla)
    n_in = fox_w_in.shape[2] * N_CHIPS

    def group(i):
        return FFN + (MLA if i % 2 == 0 else FOX)

    def ride_of(names, i):
        return dict(bufs=[raw[n][i if n in FFN else i // 2] for n in names], lays=[LAYOUT[n] for n in names])

    def land(names, i, bufs):
        for n, f in zip(names, bufs):
            if n == "fox_w_in":
                fw = f.reshape(N_CHIPS, D, -1).transpose(1, 0, 2).reshape(D, n_in)
                f = jnp.pad(fw, ((0, 0), (0, 3 * D + LANE - n_in)))[None]
            elif n == "mla_w_uq":
                f = f[:, :, perm]
            elif LAYOUT[n] == "row":
                f = f.reshape(1, f.shape[1] * f.shape[2], f.shape[3])
            full[n][i if n in FFN else i // 2] = f

    r0 = ride_of(group(0)[3:], 0)
    land(group(0)[3:], 0, _gather_weights(r0["bufs"], r0["lays"], [0] * len(r0["bufs"])))

    rc = tuple(_rope_tables(pos))

    def mixer_w(i):
        j = i // 2
        if i % 2 == 0:
            return dict(w_down=(full["mla_w_down"][j], 0), q_norm=mla_q_norm[j:j + 1], w_uq=(full["mla_w_uq"][j], 0),
                        kv_norm=mla_kv_norm[j:j + 1], w_uk=(full["mla_w_uk"][j], 0), w_uv=(full["mla_w_uv"][j], 0),
                        w_o=(full["mla_w_o"][j], 0))
        return dict(w_in=(full["fox_w_in"][j], 0), b_f=jnp.pad(fox_b_f[j:j + 1], ((0, 0), (0, LANE - H_fox))),
                    w_o=(full["fox_w_o"][j], 0))

    saved = []
    xc = x0
    h = _modulate(x0, mod[0, 1], mod[0, 0])
    for i in range(L):
        mw = mixer_w(i)
        nxt = i + 1 < L
        mix_n = group(i + 1)[3:]
        ride = dict(ride_of(FFN, i), n_ici=3 if i == 0 else 2)
        if i % 2 == 0:
            y1, ms, got = _mla_fwd(h, mw, rc, ride)
        else:
            y1, ms, got = _fox_fwd(h, mw, ride)
        land(FFN, i, got)
        z1, x1, h2 = _resid_ln_mod(xc, y1, mod[i, 2], ln1_g[i:i + 1], ln1_b[i:i + 1], mod[i, 4], mod[i, 3], alpha)
        y2, fs, got, early = _ffn_fwd(h2, (full["ffn_w1"][i], 0), (full["ffn_w3"][i], 0), (full["ffn_w2"][i], 0),
                                      ride_of(mix_n, i + 1) if nxt else None, ride_of(FFN[2:], i + 1) if nxt else None)
        land(mix_n, i + 1, got)
        if nxt:
            raw["ffn_w2"][i + 1] = early[0]
        rec = dict(h1=h, ms=ms, y1=y1, z1=z1, h2=h2, fs=fs, y2=y2)
        if i + 1 < L:
            z2, xc, h = _resid_ln_mod(x1, y2, mod[i, 5], ln2_g[i:i + 1], ln2_b[i:i + 1], mod[i + 1, 1],
                                      mod[i + 1, 0], alpha)
            rec["z2"] = z2
        else:
            dx_res, dy, loss_v, dlg, dlb, dgate = _final_ln_loss(x1, y2, tgt, mod[i, 5], ln2_g[i:i + 1],
                                                                 ln2_b[i:i + 1], alpha)
        saved.append(rec)
    loss = lax.psum(loss_v[0, 0] * (0.5 / D), ("x", "y", "c"))

    G = {n: [None] * W[n].shape[0] for n in SMALL}
    dmod = [[None] * 6 for _ in range(L)]
    red = {n: None for n in BIG}
    inv_perm = np.argsort(perm)

    def rs_view(n, g):
        if n == "mla_w_uq":
            g = g[:, inv_perm]
        if n == "fox_w_in":
            g = g[:, :n_in].reshape(D, N_CHIPS, n_in // N_CHIPS).transpose(1, 0, 2)
        elif LAYOUT[n] == "row":
            g = g.reshape(N_CHIPS, g.shape[0] // N_CHIPS, g.shape[1])
        return g[None]

    def add_halves(names, gs, recv):
        return [_add_half("add_half_" + n, g, r, LAYOUT[n], c_idx) for n, g, r in zip(names, gs, recv)]

    def finish(names, parts, recv2, lyr):
        for n, p, r in zip(names, parts, recv2):
            red[n] = _sum_shards("sum_shards_" + n, p, r, LAYOUT[n], kc_idx, red[n], lyr, W[n].shape[0])

    queue = []

    def carry_of(item):
        return _exchange_carry(item[1], [LAYOUT[n] for n in item[0]]) if item else None

    for i in reversed(range(L)):
        rec = saved[i]
        mw = mixer_w(i)
        j = i // 2
        G["ln2_g"][i], G["ln2_b"][i], dmod[i][5] = dlg, dlb, dgate
        ride_du = queue.pop(0) if queue else None
        ride_dh = queue.pop(0) if queue else None
        views = []

        def swap_of(grads):
            views.extend(rs_view(n, g) for n, g in zip(FFN, grads))
            return _swap_carry(views, [LAYOUT[n] for n in FFN])

        dh2, dw1, dw3, dw2, got_du, got_dh, got_swap = _ffn_bwd(
            dy, rec["h2"], rec["fs"], (full["ffn_w1"][i], 0), (full["ffn_w3"][i], 0), (full["ffn_w2"][i], 0),
            carry_of(ride_du), carry_of(ride_dh), swap_of)
        for item, got in ((ride_du, got_du), (ride_dh, got_dh)):
            if item:
                finish(item[0], item[1], got, item[2])
        ffn_parts = add_halves(FFN, views, got_swap)
        dx_res, dy, dmod[i][4], dmod[i][3], G["ln1_g"][i], G["ln1_b"][i], dmod[i][2] = _bwd_boundary(
            dx_res, dh2, rec["z1"], rec["y1"], mod[i, 4], mod[i, 2], ln1_g[i:i + 1], ln1_b[i:i + 1], alpha)
        n_at = 2 if i > 0 else 3
        ride_at = (FFN[:n_at], ffn_parts[:n_at], i)
        if i % 2 == 0:
            dh1, gm, recv2 = _mla_bwd(dy, rec["h1"], rec["ms"], mw, rc, carry_of(ride_at))
            names, pre = MLA, "mla_"
            G["mla_q_norm"][j], G["mla_kv_norm"][j] = gm["q_norm"], gm["kv_norm"]
        else:
            dh1, gm, recv2 = _fox_bwd(dy, rec["h1"], rec["ms"], mw, carry_of(ride_at))
            names, pre = FOX, "fox_"
            G["fox_b_f"][j] = gm["b_f"]
        finish(ride_at[0], ride_at[1], recv2, i)
        if n_at < 3:
            queue.append((FFN[2:], ffn_parts[2:], i))
        gs = [rs_view(n, gm[n[len(pre):]]) for n in names]
        swap = _swap_carry(gs, [LAYOUT[n] for n in names])
        if i > 0:
            p = saved[i - 1]
            (dx_res, dy, dmod[i][1], dmod[i][0], dlg, dlb, dgate), recv = _bwd_boundary(
                dx_res, dh1, p["z2"], p["y2"], mod[i, 1], mod[i - 1, 5], ln2_g[i - 1:i], ln2_b[i - 1:i], alpha, swap)
        else:
            (grad_x, dmod[i][1], dmod[i][0]), recv = _first_bwd(dx_res, dh1, x0, mod[i, 1], swap)
        queue.append((names, add_halves(names, gs, recv), j))

    small = jnp.concatenate([jnp.concatenate([g.reshape(-1) for g in G[n]]) for n in SMALL])
    dmod_v = jnp.concatenate([jnp.concatenate([d.reshape(-1) for d in row]) for row in dmod])
    n_small, n_dmod = small.shape[0], dmod_v.shape[0]
    wblk = -(-(n_small + n_dmod) // (8 * LANE)) * LANE
    blk = jnp.pad(jnp.concatenate([dmod_v, small]), (0, 8 * wblk - n_small - n_dmod)).reshape(8, wblk)
    parts = _ag_small("gather_small", blk).reshape(N_DEV, 8, wblk)
    tot = _sum_devices("sum_small", parts).reshape(-1)
    g_ada_b = tot[:n_dmod].reshape(L, 6 * D)
    off = n_dmod
    Gs = {}
    for n in SMALL:
        Gs[n] = tot[off:off + W[n].size].reshape(W[n].shape)
        off += W[n].size
    dmod_all = parts.reshape(N_DEV, 8 * wblk)[:, :n_dmod].reshape(N_DEV, L, N_CHIPS, w_ada)
    dmod_sh = lax.dynamic_index_in_dim(dmod_all, chip, axis=2, keepdims=False)
    dmod_sh = jnp.pad(dmod_sh, ((0, 8), (0, 0), (0, 0)))
    dmod_sh = dmod_sh.transpose(1, 0, 2)

    last = (sum((q[0] for q in queue), ()), sum((q[1] for q in queue), []))
    got = _run_carry("exchange_last", _exchange_carry(last[1], [LAYOUT[n] for n in last[0]]))
    (g_ada_w, d_ada_w, m_ada_w2, v_ada_w2), _ = _adamw_ada(c_act, dmod_sh, ada_w, m_ada_w, v_ada_w)
    for q in queue:
        finish(q[0], q[1], got[:len(q[0])], q[2])
        got = got[len(q[0]):]
    Gb = dict(zip(BIG, _join_halves([red[n] for n in BIG])))
    grads = dict(Gb)
    grads.update(Gs)
    grads["ada_b"] = g_ada_b
    delta, new_m, new_v = {}, {}, {}
    grads["ada_w"], delta["ada_w"], new_m["ada_w"], new_v["ada_w"] = g_ada_w, d_ada_w, m_ada_w2, v_ada_w2
    for n in WEIGHTS:
        if n in SMALL or n in ("ada_b", "ada_w"):
            continue
        shp = W[n].shape
        grads[n], delta[n], new_m[n], new_v[n] = [r.reshape(shp) for r in _adamw(
            "adamw_" + n, _flat(W[n]), _flat(grads[n]), _flat(Mo[n]), _flat(Vo[n]))]
    names_s = SMALL + ("ada_b",)
    cat = lambda d: jnp.concatenate([d[n].reshape(-1) for n in names_s])
    n_s = sum(W[n].size for n in names_s)
    ws = -(-n_s // (8 * LANE)) * LANE
    pk = lambda d: jnp.pad(cat(d), (0, 8 * ws - n_s)).reshape(8, ws)
    _, ds, ms_, vs = _adamw("adamw_small", pk(W), pk(grads), pk(Mo), pk(Vo))
    off = 0
    for n in names_s:
        sz, shp = W[n].size, W[n].shape
        delta[n] = ds.reshape(-1)[off:off + sz].reshape(shp)
        new_m[n] = ms_.reshape(-1)[off:off + sz].reshape(shp)
        new_v[n] = vs.reshape(-1)[off:off + sz].reshape(shp)
        off += sz

    return (loss, grad_x[None], *[grads[n].reshape(W[n].shape) for n in WEIGHTS], *[delta[n] for n in WEIGHTS],
            *[new_m[n] for n in WEIGHTS], *[new_v[n] for n in WEIGHTS])
```

```python
import functools

import numpy as np
import jax
import jax.numpy as jnp
from jax import lax
from jax.experimental import pallas as pl
from jax.experimental.pallas import tpu as pltpu

F32 = jnp.float32
BF16 = jnp.bfloat16
MXU_DTYPE = jnp.bfloat16
WIRE_DTYPE = jnp.bfloat16

HEAD_DIM = 128
ROPE_DIM = 64
CHUNK = 64
ROPE_THETA = 10000.0
LN_EPS = 1e-5
RMS_EPS = 1e-6
ADAM_LR, ADAM_B1, ADAM_B2, ADAM_EPS, ADAM_WD, ADAM_STEP = 0.001, 0.9, 0.999, 1e-08, 0.01, 10

N_CHIPS = 4
N_DEV = 8
LANE = 128
VMEM_LIMIT = 56 * 1024 * 1024
MESH = pl.DeviceIdType.MESH
ANY = pl.BlockSpec(memory_space=pl.ANY)
NEG = -1e30


def _params(**kw):
    return pltpu.CompilerParams(vmem_limit_bytes=VMEM_LIMIT, **kw)


def _pick(n, cands):
    for c in cands:
        if n % c == 0:
            return c
    return n


def _sigmoid(x):
    return 1.0 / (1.0 + jnp.exp(-x))


def _mm(name, terms, M, N, out_dtypes, epilogue=None, extras=(), row_extras=(), tm=512, tn=512, carry=None):
    tm = _pick(M, (tm, 256, 128))
    tn = _pick(N, (tn, 896, 768, 640, 384, 256, 128))
    extras = tuple(extras) + tuple(row_extras)
    n_row = len(row_extras)
    n_terms, n_ex, n_out = len(terms), len(extras), len(out_dtypes)
    n_acc = 1 + max(t[4] for t in terms)
    flags = [(t[2], t[3], t[4]) for t in terms]

    gi, gj = M // tm, N // tn

    def body(*refs):
        refs, c_in, c_out, c_sems = _carry_split(refs, 2 * n_terms + n_ex, n_out, 0, carry)
        pi, pj = pl.program_id(0), pl.program_id(1)
        _carry_run(carry, c_in, c_out, c_sems, jnp.logical_and(pi == 0, pj == 0), True)
        accs = [None] * n_acc
        for k, (ta, tb, ai) in enumerate(flags):
            a = refs[2 * k][...].astype(MXU_DTYPE)
            b = refs[2 * k + 1][...].astype(MXU_DTYPE)
            dn = (((0 if ta else 1,), (1 if tb else 0,)), ((), ()))
            r = lax.dot_general(a, b, dn, preferred_element_type=F32)
            accs[ai] = r if accs[ai] is None else accs[ai] + r
        ex = [refs[2 * n_terms + k][...] for k in range(n_ex)]
        outs = epilogue(accs, *ex) if epilogue is not None else (accs[0],)
        for k in range(n_out):
            o_ref = refs[2 * n_terms + n_ex + k]
            o_ref[...] = outs[k].astype(o_ref.dtype)
        _carry_run(carry, c_in, c_out, c_sems, jnp.logical_and(pi == gi - 1, pj == gj - 1), False)

    in_specs, args = [], []
    for (a, b, ta, tb, _, bcol) in terms:
        K = a.shape[0] if ta else a.shape[1]
        in_specs.append(pl.BlockSpec((K, tm), lambda i, j: (0, i)) if ta
                        else pl.BlockSpec((tm, K), lambda i, j: (i, 0)))
        if isinstance(b, tuple):
            b, lyr = b
            in_specs.append(pl.BlockSpec((None, tn, K), lambda i, j, o=bcol, l=lyr: (l, j + o, 0)) if tb
                            else pl.BlockSpec((None, K, tn), lambda i, j, o=bcol, l=lyr: (l, 0, j + o)))
        else:
            in_specs.append(pl.BlockSpec((tn, K), lambda i, j, o=bcol: (j + o, 0)) if tb
                            else pl.BlockSpec((K, tn), lambda i, j, o=bcol: (0, j + o)))
        args += [a, b]
    for k, e in enumerate(extras):
        in_specs.append(pl.BlockSpec((tm, tn), (lambda i, j: (i, 0)) if k >= n_ex - n_row else (lambda i, j: (i, j))))
        args.append(e)
    c_is, c_os, c_shape, c_sems, c_alias = _carry_call_args(carry, len(args), n_out)
    outs = pl.pallas_call(
        body, name=name, grid=(gi, gj), in_specs=in_specs + c_is,
        out_specs=[pl.BlockSpec((tm, tn), lambda i, j: (i, j)) for _ in out_dtypes] + c_os,
        out_shape=[jax.ShapeDtypeStruct((M, N), d) for d in out_dtypes] + c_shape,
        input_output_aliases=c_alias, scratch_shapes=c_sems,
        compiler_params=_params(),
    )(*args, *(carry["ins"] if carry else ()))
    return (outs[:n_out], list(outs[n_out:])) if carry else outs


def _wdim(b, axis):
    return b[0].shape[1 + axis] if isinstance(b, tuple) else b.shape[axis]


def _mm1(name, a, b, ta=False, tb=False, out_dtype=F32, bcol=0, N=None, **kw):
    M = a.shape[1] if ta else a.shape[0]
    if N is None:
        N = _wdim(b, 0 if tb else 1)
    res = _mm(name, [(a, b, ta, tb, 0, bcol)], M, N, [out_dtype], **kw)
    return (res[0][0], res[1]) if kw.get("carry") else res[0]


def _rowwise(name, fn, tiled, vecs, outs, reds=(), tr=128, carry=None):
    R = tiled[0].shape[0]
    tr = _pick(R, (tr, 64, 32, 16, 8))
    nt, nv, no, nr = len(tiled), len(vecs), len(outs), len(reds)

    def body(*refs):
        refs, c_in, c_out, c_sems = _carry_split(refs, nt + nv, no + nr, 0, carry)
        _carry_run(carry, c_in, c_out, c_sems, pl.program_id(0) == 0, True)
        vals = [r[...] for r in refs[:nt + nv]]
        res = fn(*vals)
        for k in range(no):
            o_ref = refs[nt + nv + k]
            o_ref[...] = res[k].astype(o_ref.dtype)
        if nr:
            first = pl.program_id(0) == 0
            for k in range(nr):
                r_ref = refs[nt + nv + no + k]

                @pl.when(first)
                def _(r_ref=r_ref, v=res[no + k]):
                    r_ref[...] = v

                @pl.when(jnp.logical_not(first))
                def _(r_ref=r_ref, v=res[no + k]):
                    r_ref[...] += v
        _carry_run(carry, c_in, c_out, c_sems, pl.program_id(0) == R // tr - 1, False)

    in_specs = [pl.BlockSpec((tr, t.shape[1]), lambda i: (i, 0)) for t in tiled]
    in_specs += [pl.BlockSpec(v.shape, lambda i, n=v.ndim: (0,) * n) for v in vecs]
    out_specs = [pl.BlockSpec((tr, w), lambda i: (i, 0)) for (w, _) in outs]
    out_specs += [pl.BlockSpec((1, w), lambda i: (0, 0)) for w in reds]
    out_shape = [jax.ShapeDtypeStruct((R, w), d) for (w, d) in outs]
    out_shape += [jax.ShapeDtypeStruct((1, w), F32) for w in reds]
    c_is, c_os, c_shape, c_sems, c_alias = _carry_call_args(carry, nt + nv, no + nr)
    res = pl.pallas_call(
        body, name=name, grid=(R // tr,), in_specs=in_specs + c_is, out_specs=out_specs + c_os,
        out_shape=out_shape + c_shape, input_output_aliases=c_alias, scratch_shapes=c_sems,
        compiler_params=_params(),
    )(*tiled, *vecs, *(carry["ins"] if carry else ()))
    return (res[:no + nr], list(res[no + nr:])) if carry else res


def _colsum(v):
    return jnp.sum(v, axis=0, keepdims=True)


def _ln_stats(z):
    mu = jnp.mean(z, axis=-1, keepdims=True)
    zc = z - mu
    var = jnp.mean(zc * zc, axis=-1, keepdims=True)
    rstd = lax.rsqrt(var + LN_EPS)
    return zc * rstd, rstd


def _ln_bwd(dout, xhat, rstd, lg):
    dxh = dout * lg
    m1 = jnp.mean(dxh, axis=-1, keepdims=True)
    m2 = jnp.mean(dxh * xhat, axis=-1, keepdims=True)
    return rstd * (dxh - m1 - xhat * m2)


def _modulate(x, sc, sh):
    D = x.shape[1]
    return _rowwise("modulate", lambda x, sc, sh: ((x * (1.0 + sc) + sh),), [x], [sc, sh], [(D, MXU_DTYPE)])[0]


def _resid_ln_mod(x, y, g, lg, lb, sc_n, sh_n, alpha):
    D = x.shape[1]

    def fn(x, y, g, lg, lb, sc, sh):
        z = alpha * x + (1.0 + g) * y
        xhat, _ = _ln_stats(z)
        xo = xhat * lg + lb
        return z, xo, xo * (1.0 + sc) + sh

    return _rowwise("resid_ln_mod", fn, [x, y], [g, lg, lb, sc_n, sh_n], [(D, F32), (D, F32), (D, MXU_DTYPE)])


def _final_ln_loss(x, y, tgt, g, lg, lb, alpha):
    D = x.shape[1]

    def fn(x, y, t, g, lg, lb):
        z = alpha * x + (1.0 + g) * y
        xhat, rstd = _ln_stats(z)
        out = xhat * lg + lb
        err = out - t
        loss = jnp.sum(jnp.sum(err * err, axis=-1, keepdims=True), axis=0, keepdims=True)
        dout = err * (1.0 / D)
        dz = _ln_bwd(dout, xhat, rstd, lg)
        return (alpha * dz, (1.0 + g) * dz, jnp.broadcast_to(loss, (1, LANE)),
                _colsum(dout * xhat), _colsum(dout), _colsum(dz * y))

    return _rowwise("final_ln_loss", fn, [x, y, tgt], [g, lg, lb], [(D, F32), (D, MXU_DTYPE)], [LANE, D, D, D])


def _bwd_boundary(dx_res, dh, z_p, y_p, sc, g_p, lg_p, lb_p, alpha, carry=None):
    D = dh.shape[1]

    def fn(dxr, dh, z, y, sc, g, lg, lb):
        xhat, rstd = _ln_stats(z)
        x_in = xhat * lg + lb
        dx = dxr + dh * (1.0 + sc)
        dz = _ln_bwd(dx, xhat, rstd, lg)
        return (alpha * dz, (1.0 + g) * dz,
                _colsum(dh * x_in), _colsum(dh), _colsum(dx * xhat), _colsum(dx), _colsum(dz * y))

    return _rowwise("bwd_boundary", fn, [dx_res, dh, z_p, y_p], [sc, g_p, lg_p, lb_p],
                    [(D, F32), (D, MXU_DTYPE)], [D, D, D, D, D], carry=carry)


def _first_bwd(dx_res, dh, x, sc, carry=None):
    D = dh.shape[1]

    def fn(dxr, dh, x, sc):
        return dxr + dh * (1.0 + sc), _colsum(dh * x), _colsum(dh)

    return _rowwise("first_bwd", fn, [dx_res, dh, x], [sc], [(D, F32)], [D, D], carry=carry)


def _ride(ride, phase, bufs=None):
    if ride is None:
        return None
    bufs = ride["bufs"] if bufs is None else bufs
    n = ride.get("n_ici", len(bufs)) if phase == "ici" else len(bufs)
    return _gather_carry(bufs[:n], ride["lays"][:n], [0] * n, phase)


def _ride_on(ride, got):
    return list(got) + list(ride["bufs"][len(got):]) if ride else []


def _ffn_fwd(h, w1, w3, w2, ride=None, early=None):
    S, F = h.shape[0], _wdim(w1, 1)

    def epi(accs):
        a, b = accs
        return a, b, a * _sigmoid(a) * b

    res = _mm("ffn_up", [(h, w1, False, False, 0, 0), (h, w3, False, False, 1, 0)], S, F,
              [MXU_DTYPE, MXU_DTYPE, MXU_DTYPE], epilogue=epi, carry=_ride(ride, "ici"))
    (a, b, u), bufs = res if ride else (res, [])
    c_d2d, c_ici = _ride(ride, "d2d", bufs), _ride(early, "ici")
    carry = _merge_carries(c_d2d, c_ici)
    res = _mm1("ffn_down", u, w2, carry=carry)
    y, got = res if carry else (res, [])
    return y, (a, b, u), got[:len(bufs)], got[len(bufs):]


def _ffn_bwd(dy, h, saved, w1, w3, w2, carry_du=None, carry_dw=None, swap_of=None):
    a, b, u = saved
    S, F = a.shape
    D = h.shape[1]

    def epi(accs, a, b):
        du = accs[0]
        a = a.astype(F32)
        b = b.astype(F32)
        sg = _sigmoid(a)
        return du * b * (sg * (1.0 + a * (1.0 - sg))), du * (a * sg)

    res = _mm("ffn_du", [(dy, w2, False, True, 0, 0)], S, F, [MXU_DTYPE, MXU_DTYPE], epilogue=epi, extras=(a, b),
              carry=carry_du)
    (da, db), got_du = res if carry_du else (res, [])
    dw2 = _mm1("ffn_dw2", u, dy, ta=True, out_dtype=WIRE_DTYPE)
    res = _mm("ffn_dw13", [(h, da, True, False, 0, 0), (h, db, True, False, 1, 0)], D, F,
              [WIRE_DTYPE, WIRE_DTYPE], epilogue=lambda accs: (accs[0], accs[1]), carry=carry_dw)
    (dw1, dw3), got_dw = res if carry_dw else (res, [])
    carry = swap_of((dw1, dw3, dw2)) if swap_of else None
    res = _mm("ffn_dh", [(da, w1, False, True, 0, 0), (db, w3, False, True, 0, 0)], S, D, [F32], tn=256, carry=carry)
    (dh,), got = res if carry else (res, [])
    return dh, dw1, dw3, dw2, got_du, got_dw, got


def _rope_tables(pos):
    j = np.arange(LANE)
    invf = ROPE_THETA ** (-jnp.arange(0, ROPE_DIM, 2, dtype=F32) / ROPE_DIM)
    invf = invf[(j % ROPE_DIM) // 2].reshape(1, LANE)
    sgn = jnp.asarray(np.where(j % 2 == 0, -1.0, 1.0).reshape(1, LANE), F32)

    def fn(pos, invf, sgn):
        ang = pos.astype(F32) * invf
        return jnp.cos(ang), jnp.sin(ang) * sgn

    return _rowwise("rope_tables", fn, [pos], [invf, sgn], [(LANE, F32), (LANE, F32)], tr=256)


def _pair_swap(x):
    w = x.shape[1]
    even = (lax.broadcasted_iota(jnp.int32, x.shape, 1) % 2) == 0
    return jnp.where(even, pltpu.roll(x, w - 1, 1), pltpu.roll(x, 1, 1))


def _rope_fwd(x, c, s):
    return x * c + _pair_swap(x) * s


def _rope_bwd(d, c, s):
    return d * c + _pair_swap(d * s)


ATT_T = 512


def _dot_nt(a, b):
    return lax.dot_general(a, b, (((1,), (1,)), ((), ())), preferred_element_type=F32)


def _dot_tn(a, b):
    return lax.dot_general(a, b, (((0,), (0,)), ((), ())), preferred_element_type=F32)


def _dot_nn(a, b):
    return lax.dot_general(a, b, (((1,), (0,)), ((), ())), preferred_element_type=F32)


def _diag_mask(T, gran):
    r = lax.broadcasted_iota(jnp.int32, (T, T), 0)
    c = lax.broadcasted_iota(jnp.int32, (T, T), 1)
    if gran > 1:
        sh = int(np.log2(gran))
        r, c = lax.shift_right_logical(r, sh), lax.shift_right_logical(c, sh)
    return r >= c


def _attn_specs(S, H, T, mla, col_q, col_k, col_v):
    W = 2 * HEAD_DIM
    specs = [pl.BlockSpec((T, W), lambda p, i: (i, col_q + p))]
    if mla:
        specs.append(pl.BlockSpec((T, 2 * ROPE_DIM), lambda p, i: (i, p)))
    specs.append(pl.BlockSpec((S, W), lambda p, i: (0, col_k + p)))
    if mla:
        specs.append(pl.BlockSpec((S, ROPE_DIM), lambda p, i: (0, 0)))
    specs.append(pl.BlockSpec((S, W), lambda p, i: (0, col_v + p)))
    if not mla:
        specs.append(pl.BlockSpec((2, T, 1), lambda p, i: (p, i, 0)))
        specs.append(pl.BlockSpec((2, 1, S), lambda p, i: (p, 0, 0)))
    return specs


def _attn_fwd(name, S, H, mla, q, k, v, q_pe=None, k_pe=None, cum_col=None, cum_row=None, cols=(0, 0, 0),
              carry=None):
    T = _pick(S, (ATT_T, 128))
    nq = S // T
    scale = (HEAD_DIM + ROPE_DIM) ** -0.5 if mla else HEAD_DIM ** -0.5
    gran = CHUNK if mla else 1

    def body(*refs):
        refs, c_in, c_out, c_sems = _carry_split(refs, 5, 3, 3, carry)
        if mla:
            q_ref, qpe_ref, k_ref, kpe_ref, v_ref, o_ref, of_ref, lse_ref, m_s, l_s, acc_s = refs
        else:
            q_ref, k_ref, v_ref, cc_ref, cr_ref, o_ref, of_ref, lse_ref, m_s, l_s, acc_s = refs
        hp, qi = pl.program_id(0), pl.program_id(1)
        _carry_run(carry, c_in, c_out, c_sems, jnp.logical_and(hp == 0, qi == 0), True)
        hls = [slice(hh * HEAD_DIM, (hh + 1) * HEAD_DIM) for hh in range(2)]
        qn = [q_ref[:, hl] for hl in hls]
        qp = [qpe_ref[:, hh * ROPE_DIM:(hh + 1) * ROPE_DIM] for hh in range(2)] if mla else None
        m_s[...] = jnp.full(m_s.shape, NEG, F32)
        l_s[...] = jnp.zeros(l_s.shape, F32)
        acc_s[...] = jnp.zeros(acc_s.shape, F32)

        def step(j, masked):
            rows = pl.ds(pl.multiple_of(j * T, T), T)
            for hh, hl in enumerate(hls):
                s = _dot_nt(qn[hh], k_ref[rows, hl])
                if mla:
                    s = s + _dot_nt(qp[hh], kpe_ref[rows, :])
                s = s * scale
                if not mla:
                    s = s + (cc_ref[hh] - cr_ref[hh, :, rows])
                if masked:
                    s = jnp.where(_diag_mask(T, gran), s, NEG)
                m_old = m_s[hh, :, 0:1]
                m_new = jnp.maximum(m_old, jnp.max(s, axis=-1, keepdims=True))
                p = jnp.exp(s - m_new)
                corr = jnp.exp(m_old - m_new)
                l_s[hh] = jnp.broadcast_to(corr * l_s[hh, :, 0:1] + jnp.sum(p, axis=-1, keepdims=True), (T, LANE))
                p_hi = p.astype(MXU_DTYPE)
                p_lo = (p - p_hi.astype(F32)).astype(MXU_DTYPE)
                vf = v_ref[rows, hl]
                acc_s[hh] = corr * acc_s[hh] + (_dot_nn(p_hi, vf) + _dot_nn(p_lo, vf))
                m_s[hh] = jnp.broadcast_to(m_new, (T, LANE))

        lax.fori_loop(0, qi, lambda j, c: (step(j, False), c)[1], 0)
        step(qi, True)
        for hh, hl in enumerate(hls):
            l = l_s[hh, :, 0:1]
            of = acc_s[hh] / l
            of_ref[:, hl] = of
            o_ref[:, hl] = of.astype(o_ref.dtype)
            lse_ref[hh] = jnp.broadcast_to(m_s[hh, :, 0:1] + jnp.log(l), (T, LANE))
        _carry_run(carry, c_in, c_out, c_sems, jnp.logical_and(hp == H // 2 - 1, qi == nq - 1), False)

    args = [q] + ([q_pe] if mla else []) + [k] + ([k_pe] if mla else []) + [v]
    if not mla:
        args += [cum_col, cum_row]
    c_is, c_os, c_shape, c_sems, c_alias = _carry_call_args(carry, 5, 3)
    res = pl.pallas_call(
        body, name=name, grid=(H // 2, nq),
        in_specs=_attn_specs(S, H, T, mla, *cols) + c_is,
        out_specs=[pl.BlockSpec((T, 2 * HEAD_DIM), lambda p, i: (i, p)),
                   pl.BlockSpec((T, 2 * HEAD_DIM), lambda p, i: (i, p)),
                   pl.BlockSpec((2, T, LANE), lambda p, i: (p, i, 0))] + c_os,
        out_shape=[jax.ShapeDtypeStruct((S, H * HEAD_DIM), MXU_DTYPE), jax.ShapeDtypeStruct((S, H * HEAD_DIM), F32),
                   jax.ShapeDtypeStruct((H, S, LANE), F32)] + c_shape,
        input_output_aliases=c_alias,
        scratch_shapes=[pltpu.VMEM((2, T, LANE), F32), pltpu.VMEM((2, T, LANE), F32),
                        pltpu.VMEM((2, T, HEAD_DIM), F32)] + c_sems,
        compiler_params=_params(),
    )(*args, *(carry["ins"] if carry else ()))
    return res[0], res[1], res[2], list(res[3:])


def _carry_split(refs, n_in, n_out, n_scr, carry):
    if carry is None:
        return refs, (), (), ()
    ci, co = len(carry["ins"]), len(carry["out_shape"])
    own = refs[:n_in] + refs[n_in + ci:n_in + ci + n_out] + refs[n_in + ci + n_out + co:n_in + ci + n_out + co + n_scr]
    return (own, refs[n_in:n_in + ci], refs[n_in + ci + n_out:n_in + ci + n_out + co],
            refs[n_in + ci + n_out + co + n_scr:])


def _carry_run(carry, c_in, c_out, c_sems, when, start):
    if carry is None:
        return

    @pl.when(when)
    def _():
        for cp in carry["copies"](c_in, c_out, *c_sems):
            if start:
                cp.start()
            else:
                cp.wait()


def _carry_call_args(carry, n_in, n_out):
    if carry is None:
        return [], [], [], [], {}
    sems = [pltpu.SemaphoreType.DMA(carry["sems"]), pltpu.SemaphoreType.DMA(carry["sems"])]
    alias = {n_in + i: n_out + o for i, o in carry.get("alias", {}).items()}
    return [ANY] * len(carry["ins"]), [ANY] * len(carry["out_shape"]), list(carry["out_shape"]), sems, alias


def _attn_bwd(name, S, H, mla, q, k, v, do, of, lse, q_pe=None, k_pe=None, cum_col=None, cum_row=None,
              cols=(0, 0, 0), carry=None):
    T = _pick(S, (ATT_T, 128))
    nq = S // T
    scale = (HEAD_DIM + ROPE_DIM) ** -0.5 if mla else HEAD_DIM ** -0.5
    gran = CHUNK if mla else 1
    dqk = HEAD_DIM + (ROPE_DIM if mla else 0)

    def body(*refs):
        refs, c_in, c_out, c_sems = _carry_split(refs, 8, 5, 2, carry)
        if mla:
            (q_ref, qpe_ref, k_ref, kpe_ref, v_ref, do_ref, of_ref, lse_ref,
             dq_ref, dk_ref, dv_ref, dqpe_ref, dkpe_ref, dq_s, r_s) = refs
        else:
            (q_ref, k_ref, v_ref, cc_ref, cr_ref, do_ref, of_ref, lse_ref,
             dq_ref, dk_ref, dv_ref, dck_ref, dcq_ref, dq_s, r_s) = refs
        hp, qi = pl.program_id(0), pl.program_id(1)
        _carry_run(carry, c_in, c_out, c_sems, jnp.logical_and(hp == 0, qi == 0), True)

        @pl.when(qi == 0)
        def _():
            dk_ref[...] = jnp.zeros(dk_ref.shape, F32)
            dv_ref[...] = jnp.zeros(dv_ref.shape, F32)
            if not mla:
                dck_ref[...] = jnp.zeros(dck_ref.shape, F32)

        if mla:
            @pl.when(jnp.logical_and(qi == 0, hp == 0))
            def _():
                dkpe_ref[...] = jnp.zeros(dkpe_ref.shape, F32)

        hls = [slice(hh * HEAD_DIM, (hh + 1) * HEAD_DIM) for hh in range(2)]
        qn = [q_ref[:, hl] for hl in hls]
        qp = [qpe_ref[:, hh * ROPE_DIM:(hh + 1) * ROPE_DIM] for hh in range(2)] if mla else None
        dof = [do_ref[:, hl] for hl in hls]
        delta = [jnp.sum(dof[hh].astype(F32) * of_ref[:, hl], axis=-1, keepdims=True) for hh, hl in enumerate(hls)]
        lse = [lse_ref[hh][:, 0:1] for hh in range(2)]
        dq_s[...] = jnp.zeros(dq_s.shape, F32)
        r_s[...] = jnp.zeros(r_s.shape, F32)

        def step(j, masked):
            rows = pl.ds(pl.multiple_of(j * T, T), T)
            for hh, hl in enumerate(hls):
                kn = k_ref[rows, hl]
                s = _dot_nt(qn[hh], kn)
                if mla:
                    kp = kpe_ref[rows, :]
                    s = s + _dot_nt(qp[hh], kp)
                s = s * scale
                if not mla:
                    s = s + (cc_ref[hh] - cr_ref[hh, :, rows])
                if masked:
                    s = jnp.where(_diag_mask(T, gran), s, NEG)
                p = jnp.exp(s - lse[hh])
                dp = _dot_nt(dof[hh], v_ref[rows, hl])
                ds = p * (dp - delta[hh])
                dv_ref[rows, hl] += _dot_tn(p.astype(MXU_DTYPE), dof[hh])
                dsb = (ds * scale).astype(MXU_DTYPE)
                dk_ref[rows, hl] += _dot_tn(dsb, qn[hh])
                dq_s[hh, :, :HEAD_DIM] += _dot_nn(dsb, kn)
                if mla:
                    dkpe_ref[rows, :] += _dot_tn(dsb, qp[hh])
                    dq_s[hh, :, HEAD_DIM:] += _dot_nn(dsb, kp)
                else:
                    dck_ref[hh, :, rows] -= jnp.sum(ds, axis=0, keepdims=True)
                    r_s[hh] += jnp.broadcast_to(jnp.sum(ds, axis=-1, keepdims=True), (T, LANE))

        lax.fori_loop(0, qi, lambda j, c: (step(j, False), c)[1], 0)
        step(qi, True)
        for hh, hl in enumerate(hls):
            dq_ref[:, hl] = dq_s[hh, :, :HEAD_DIM].astype(dq_ref.dtype)
            if mla:
                dqpe_ref[:, hh * ROPE_DIM:(hh + 1) * ROPE_DIM] = dq_s[hh, :, HEAD_DIM:]
            else:
                dcq_ref[hh] = r_s[hh, :, 0:1]
        _carry_run(carry, c_in, c_out, c_sems, jnp.logical_and(hp == H // 2 - 1, qi == nq - 1), False)

    W = 2 * HEAD_DIM
    args = [q] + ([q_pe] if mla else []) + [k] + ([k_pe] if mla else []) + [v]
    if not mla:
        args += [cum_col, cum_row]
    args += [do, of, lse]
    in_specs = _attn_specs(S, H, T, mla, *cols)
    in_specs += [pl.BlockSpec((T, W), lambda p, i: (i, p)), pl.BlockSpec((T, W), lambda p, i: (i, p)),
                 pl.BlockSpec((2, T, LANE), lambda p, i: (p, i, 0))]
    out_specs = [pl.BlockSpec((T, W), lambda p, i: (i, p)), pl.BlockSpec((S, W), lambda p, i: (0, p)),
                 pl.BlockSpec((S, W), lambda p, i: (0, p))]
    out_shape = [jax.ShapeDtypeStruct((S, H * HEAD_DIM), MXU_DTYPE), jax.ShapeDtypeStruct((S, H * HEAD_DIM), F32),
                 jax.ShapeDtypeStruct((S, H * HEAD_DIM), F32)]
    if mla:
        out_specs += [pl.BlockSpec((T, 2 * ROPE_DIM), lambda p, i: (i, p)),
                      pl.BlockSpec((S, ROPE_DIM), lambda p, i: (0, 0))]
        out_shape += [jax.ShapeDtypeStruct((S, H * ROPE_DIM), F32), jax.ShapeDtypeStruct((S, ROPE_DIM), F32)]
    else:
        out_specs += [pl.BlockSpec((2, 1, S), lambda p, i: (p, 0, 0)), pl.BlockSpec((2, T, 1), lambda p, i: (p, i, 0))]
        out_shape += [jax.ShapeDtypeStruct((H, 1, S), F32), jax.ShapeDtypeStruct((H, S, 1), F32)]
    assert len(args) == 8 and len(out_shape) == 5
    c_is, c_os, c_shape, c_sems, c_alias = _carry_call_args(carry, 8, 5)
    res = pl.pallas_call(
        body, name=name, grid=(H // 2, nq), in_specs=in_specs + c_is, out_specs=out_specs + c_os,
        out_shape=out_shape + c_shape, input_output_aliases=c_alias,
        scratch_shapes=[pltpu.VMEM((2, T, dqk), F32), pltpu.VMEM((2, T, LANE), F32)] + c_sems,
        compiler_params=_params(),
    )(*args, *(carry["ins"] if carry else ()))
    return (res[:5], res[5:]) if carry else (res, [])


def _mla_prep(lat, cos, sin, qn, kvn, ql, kvl):
    def fn(lat, c, s, qn, kvn):
        ql_ = lat[:, :ql]
        kv_ = lat[:, ql:ql + kvl]
        kp = lat[:, ql + kvl:]
        cq = ql_ * lax.rsqrt(jnp.mean(ql_ * ql_, axis=-1, keepdims=True) + RMS_EPS) * qn
        ckv = kv_ * lax.rsqrt(jnp.mean(kv_ * kv_, axis=-1, keepdims=True) + RMS_EPS) * kvn
        kp2 = jnp.concatenate([kp, jnp.zeros_like(kp)], axis=-1)
        kr = _rope_fwd(kp2, c, s)[:, :ROPE_DIM]
        return cq, ckv, kr

    return _rowwise("mla_prep", fn, [lat, cos, sin], [qn, kvn],
                    [(ql, MXU_DTYPE), (kvl, MXU_DTYPE), (ROPE_DIM, MXU_DTYPE)])


def _mla_prep_bwd(lat, cos, sin, qn, kvn, dcq, dckv, dkr, ql, kvl):
    def fn(lat, c, s, dcq, dckv, dkr, qn, kvn):
        outs, reds = [], []
        for (x, g, d) in ((lat[:, :ql], qn, dcq), (lat[:, ql:ql + kvl], kvn, dckv)):
            r = lax.rsqrt(jnp.mean(x * x, axis=-1, keepdims=True) + RMS_EPS)
            n = x * r
            dn = d * g
            outs.append(r * (dn - n * jnp.mean(dn * n, axis=-1, keepdims=True)))
            reds.append(_colsum(d * n))
        d2 = jnp.concatenate([dkr, jnp.zeros_like(dkr)], axis=-1)
        outs.append(_rope_bwd(d2, c, s)[:, :ROPE_DIM])
        return (jnp.concatenate(outs, axis=-1), *reds)

    return _rowwise("mla_prep_bwd", fn, [lat, cos, sin, dcq, dckv, dkr], [qn, kvn],
                    [(ql + kvl + ROPE_DIM, MXU_DTYPE)], [ql, kvl])


def _mla_fwd(h, w, rc, ride=None):
    S = h.shape[0]
    ql, kvl = w["q_norm"].shape[1], w["kv_norm"].shape[1]
    H = _wdim(w["w_uk"], 1) // HEAD_DIM
    n_nope, n_pe = H * HEAD_DIM, H * ROPE_DIM
    lat = _mm1("mla_down", h, w["w_down"])
    cq, ckv, kr = _mla_prep(lat, rc[0], rc[1], w["q_norm"], w["kv_norm"], ql, kvl)
    q_nope = _mm1("mla_uq_nope", cq, w["w_uq"], out_dtype=MXU_DTYPE, N=n_nope)
    q_pe = _mm("mla_uq_pe", [(cq, w["w_uq"], False, False, 0, n_nope // LANE)], S, n_pe, [MXU_DTYPE],
               epilogue=lambda accs, c, s: (_rope_fwd(accs[0], c, s),), row_extras=rc, tn=LANE)[0]
    k_nope, v = _mm("mla_ukv", [(ckv, w["w_uk"], False, False, 0, 0), (ckv, w["w_uv"], False, False, 1, 0)],
                    S, n_nope, [MXU_DTYPE, MXU_DTYPE], epilogue=lambda accs: (accs[0], accs[1]))
    o, of, lse, bufs = _attn_fwd("mla_attn_fwd", S, H, True, q_nope, k_nope, v, q_pe=q_pe, k_pe=kr,
                                 carry=_ride(ride, "ici"))
    res = _mm1("mla_wo", o, w["w_o"], carry=_ride(ride, "d2d", _ride_on(ride, bufs)))
    y, bufs = res if ride else (res, [])
    return y, (lat, cq, ckv, kr, q_nope, q_pe, k_nope, v, o, of, lse), bufs


def _mla_bwd(dy, h, saved, w, rc, carry):
    lat, cq, ckv, kr, q_nope, q_pe, k_nope, v, o, of, lse = saved
    S = h.shape[0]
    ql, kvl = w["q_norm"].shape[1], w["kv_norm"].shape[1]
    H = _wdim(w["w_uk"], 1) // HEAD_DIM
    n_nope, n_pe = H * HEAD_DIM, H * ROPE_DIM
    do = _mm1("mla_do", dy, w["w_o"], tb=True, out_dtype=MXU_DTYPE)
    dw_o = _mm1("mla_dwo", o, dy, ta=True, out_dtype=WIRE_DTYPE)
    (dq_nope, dk_nope, dv, dq_pe_r, dk_pe_r), carried = _attn_bwd(
        "mla_attn_bwd", S, H, True, q_nope, k_nope, v, do, of, lse, q_pe=q_pe, k_pe=kr, carry=carry)

    def unrope(d, c, s):
        reps = (1, n_pe // LANE)
        return (_rope_bwd(d, jnp.tile(c, reps), jnp.tile(s, reps)),)

    dq_pe = _rowwise("mla_unrope_q", unrope, [dq_pe_r, rc[0], rc[1]], [], [(n_pe, MXU_DTYPE)])[0]
    dq = jnp.concatenate([dq_nope, dq_pe], axis=1)
    dw_uq = _mm1("mla_dwuq", cq, dq, ta=True, out_dtype=WIRE_DTYPE)
    dcq = _mm1("mla_dcq", dq, w["w_uq"], tb=True)
    dw_uk, dw_uv = _mm("mla_dwukv", [(ckv, dk_nope, True, False, 0, 0), (ckv, dv, True, False, 1, 0)], kvl, n_nope,
                       [WIRE_DTYPE, WIRE_DTYPE], epilogue=lambda accs: (accs[0], accs[1]))
    dckv = _mm("mla_dckv", [(dk_nope, w["w_uk"], False, True, 0, 0), (dv, w["w_uv"], False, True, 0, 0)],
               S, kvl, [F32])[0]
    dlat, dqn, dkvn = _mla_prep_bwd(lat, rc[0], rc[1], w["q_norm"], w["kv_norm"], dcq, dckv, dk_pe_r, ql, kvl)
    dw_down = _mm1("mla_dwdown", h, dlat, ta=True, out_dtype=WIRE_DTYPE)
    dh = _mm1("mla_dh", dlat, w["w_down"], tb=True)
    return dh, dict(w_down=dw_down, q_norm=dqn, w_uq=dw_uq, kv_norm=dkvn, w_uk=dw_uk, w_uv=dw_uv, w_o=dw_o), carried


def _log_sigmoid(z):
    return jnp.minimum(z, 0.0) - jnp.log(1.0 + jnp.exp(-jnp.abs(z)))


def _fox_gate_fwd(f, bf):
    S = f.shape[0]
    B = LANE

    def body(f_ref, b_ref, cum_ref):
        r = lax.broadcasted_iota(jnp.int32, (B, B), 0)
        c = lax.broadcasted_iota(jnp.int32, (B, B), 1)
        tri = (r >= c).astype(F32)
        carry = jnp.zeros((1, LANE), F32)
        for blk in range(S // B):
            rows = slice(blk * B, (blk + 1) * B)
            lf = _log_sigmoid(f_ref[rows, :] + b_ref[...])
            cs = jnp.dot(tri, lf, precision=lax.Precision.HIGHEST, preferred_element_type=F32) + carry
            cum_ref[rows, :] = cs
            carry = cs[B - 1:B, :]

    return pl.pallas_call(body, name="fox_gate_fwd", out_shape=jax.ShapeDtypeStruct((S, LANE), F32),
                          compiler_params=_params())(f, bf)


def _fox_gate_bwd(dcum, f, bf):
    S = f.shape[0]
    B = LANE

    def body(d_ref, f_ref, b_ref, df_ref, db_ref):
        r = lax.broadcasted_iota(jnp.int32, (B, B), 0)
        c = lax.broadcasted_iota(jnp.int32, (B, B), 1)
        tri = (r <= c).astype(F32)
        carry = jnp.zeros((1, LANE), F32)
        db = jnp.zeros((1, LANE), F32)
        for blk in reversed(range(S // B)):
            rows = slice(blk * B, (blk + 1) * B)
            dlf = jnp.dot(tri, d_ref[rows, :], precision=lax.Precision.HIGHEST, preferred_element_type=F32) + carry
            carry = dlf[0:1, :]
            z = f_ref[rows, :] + b_ref[...]
            dz = dlf * _sigmoid(-z)
            df_ref[rows, :] = dz.astype(df_ref.dtype)
            db = db + jnp.sum(dz, axis=0, keepdims=True)
        db_ref[...] = db

    return pl.pallas_call(body, name="fox_gate_bwd",
                          out_shape=[jax.ShapeDtypeStruct((S, LANE), MXU_DTYPE), jax.ShapeDtypeStruct((1, LANE), F32)],
                          compiler_params=_params())(dcum, f, bf)


def _fox_fwd(h, w, ride=None):
    S, D = h.shape
    H = D // HEAD_DIM
    qkv = _mm1("fox_qkv", h, w["w_in"], out_dtype=MXU_DTYPE, N=3 * D)
    f = _mm1("fox_f", h, w["w_in"], bcol=3 * D // LANE, N=LANE, tn=LANE)
    cum = _fox_gate_fwd(f, w["b_f"])
    cumT = cum[:, :H].T
    cum_col, cum_row = cumT.reshape(H, S, 1), cumT.reshape(H, 1, S)
    nb = D // (2 * HEAD_DIM)
    o, of, lse, bufs = _attn_fwd("fox_attn_fwd", S, H, False, qkv, qkv, qkv, cum_col=cum_col, cum_row=cum_row,
                                 cols=(0, nb, 2 * nb), carry=_ride(ride, "ici"))
    res = _mm1("fox_wo", o, w["w_o"], carry=_ride(ride, "d2d", _ride_on(ride, bufs)))
    y, bufs = res if ride else (res, [])
    return y, (qkv, f, cum_col, cum_row, o, of, lse), bufs


def _fox_bwd(dy, h, saved, w, carry):
    qkv, f, cum_col, cum_row, o, of, lse = saved
    S, D = h.shape
    H = D // HEAD_DIM
    nb = D // (2 * HEAD_DIM)
    do = _mm1("fox_do", dy, w["w_o"], tb=True, out_dtype=MXU_DTYPE)
    dw_o = _mm1("fox_dwo", o, dy, ta=True, out_dtype=WIRE_DTYPE)
    (dq, dk, dv, dck, dcq), carried = _attn_bwd("fox_attn_bwd", S, H, False, qkv, qkv, qkv, do, of, lse,
                                                cum_col=cum_col, cum_row=cum_row, cols=(0, nb, 2 * nb), carry=carry)
    dcum = jnp.pad((dck.reshape(H, S) + dcq.reshape(H, S)).T, ((0, 0), (0, LANE - H)))
    df, dbf = _fox_gate_bwd(dcum, f, w["b_f"])
    dproj = jnp.concatenate([dq, dk.astype(MXU_DTYPE), dv.astype(MXU_DTYPE), df], axis=1)
    dw_in = _mm1("fox_dwin", h, dproj, ta=True, out_dtype=WIRE_DTYPE)
    dh = _mm1("fox_dh", dproj, w["w_in"], tb=True, tn=256)
    return dh, dict(w_in=dw_in, b_f=dbf[:, :H], w_o=dw_o), carried


def _place():
    x, y, c = lax.axis_index("x"), lax.axis_index("y"), lax.axis_index("c")
    return x, y, c, [(1 - x, y), (x, 1 - y), (1 - x, 1 - y)]


def _ag_small(name, blk):
    m, n = blk.shape

    def body(x_ref, out_ref, send_sems, recv_sems, local_sem):
        x, y, c, chips = _place()
        me, sibling = (x, y, c), (x, y, 1 - c)

        def rows(px, py, pc):
            return out_ref.at[pl.ds((4 * px + 2 * py + pc) * m, m), :]

        def copy(k, block, to, src=None):
            return pltpu.make_async_remote_copy(
                src_ref=rows(*block) if src is None else src, dst_ref=rows(*block),
                send_sem=send_sems.at[k], recv_sem=recv_sems.at[k], device_id=to, device_id_type=MESH)

        mine = pltpu.make_async_copy(x_ref, rows(*me), local_sem)
        mine.start()
        first = [copy(0, me, sibling, src=x_ref)]
        first += [copy(1 + j, me, (*chip, c), src=x_ref) for j, chip in enumerate(chips)]
        for cp in first:
            cp.start()
        passed = [copy(4 + j, (*chip, c), sibling) for j, chip in enumerate(chips)]
        for j, chip in enumerate(chips):
            copy(1 + j, (*chip, c), me).wait_recv()
            passed[j].start()
        copy(0, sibling, me).wait_recv()
        for j, chip in enumerate(chips):
            copy(4 + j, (*chip, 1 - c), me).wait_recv()
        for cp in first + passed:
            cp.wait_send()
        mine.wait()

    return pl.pallas_call(
        body, name=name, out_shape=jax.ShapeDtypeStruct((N_DEV * m, n), blk.dtype),
        in_specs=[pl.BlockSpec(memory_space=pltpu.VMEM)], out_specs=pl.BlockSpec(memory_space=pltpu.VMEM),
        scratch_shapes=[pltpu.SemaphoreType.DMA((7,)), pltpu.SemaphoreType.DMA((7,)), pltpu.SemaphoreType.DMA],
        compiler_params=_params(),
    )(blk)


def _half(ref, row_axis, c, rows):
    idx = [slice(None)] * len(ref.shape)
    idx[row_axis] = pl.ds(pl.multiple_of(c * rows, 16), rows)
    return ref.at[tuple(idx)]


def _shard(ref, layout, k):
    if layout == "row":
        return ref.at[:, k]
    w = ref.shape[2] // N_CHIPS
    return ref.at[:, :, pl.ds(pl.multiple_of(k * w, LANE), w)]


def _full_shape(shape, layout):
    L, r, w = shape
    return (L, N_CHIPS, r, w) if layout == "row" else (L, r, N_CHIPS * w)


def _cast_full(name, a, layout, k_idx, lyr):
    _, r, C = a.shape
    tr = _pick(r, (256, 128, 64, 32, 16))

    def body(k_ref, a_ref, o_ref):
        o_ref[...] = a_ref[...].astype(o_ref.dtype)

    if layout == "row":
        o_spec = pl.BlockSpec((None, None, tr, C), lambda l, i, k: (0, k[0], i, 0))
    else:
        o_spec = pl.BlockSpec((None, tr, C), lambda l, i, k: (0, i, k[0]))
    return pl.pallas_call(
        body, name=name,
        grid_spec=pltpu.PrefetchScalarGridSpec(
            num_scalar_prefetch=1, grid=(1, r // tr),
            in_specs=[pl.BlockSpec((None, tr, C), lambda l, i, k: (lyr, i, 0))], out_specs=o_spec),
        out_shape=jax.ShapeDtypeStruct(_full_shape((1, r, C), layout), WIRE_DTYPE),
        compiler_params=_params(),
    )(k_idx, a)


def _gather_carry(fulls, layouts, lyrs, phase):
    n = len(fulls)
    half_rows = [f.shape[2 if lay == "row" else 1] // 2 for f, lay in zip(fulls, layouts)]

    def copies(ins, outs, send_sems, recv_sems, off=0):
        x, y, c, chips = _place()
        cps = []
        for i in range(n):
            for j, chip in enumerate(chips):
                who, to = ((x, y), (*chip, c)) if phase == "ici" else (chip, (x, y, 1 - c))
                w = _half(_shard(outs[i].at[pl.ds(lyrs[i], 1)], layouts[i], 2 * who[0] + who[1]), 1, c, half_rows[i])
                cps.append(pltpu.make_async_remote_copy(
                    src_ref=w, dst_ref=w, send_sem=send_sems.at[off + i, j], recv_sem=recv_sems.at[off + i, j],
                    device_id=to, device_id_type=MESH))
        return cps

    return dict(ins=list(fulls), sems=(n, 3), copies=copies, alias={i: i for i in range(n)},
                out_shape=[jax.ShapeDtypeStruct(f.shape, f.dtype) for f in fulls])


def _run_carry(name, carry):
    n = len(carry["ins"])

    def body(*refs):
        cps = carry["copies"](refs[:n], refs[n:2 * n], *refs[2 * n:])
        for cp in cps:
            cp.start()
        for cp in cps:
            cp.wait()

    return pl.pallas_call(
        body, name=name, in_specs=[ANY] * n, out_specs=[ANY] * len(carry["out_shape"]), out_shape=carry["out_shape"],
        input_output_aliases=dict(carry.get("alias", {})),
        scratch_shapes=[pltpu.SemaphoreType.DMA(carry["sems"]), pltpu.SemaphoreType.DMA(carry["sems"])],
        compiler_params=_params(),
    )(*carry["ins"])


def _gather_weights(fulls, layouts, lyrs):
    n = len(fulls)
    half_rows = [f.shape[2 if lay == "row" else 1] // 2 for f, lay in zip(fulls, layouts)]

    def body(*refs):
        outs = refs[n:2 * n]
        send_sems, recv_sems = refs[2 * n:]
        x, y, c, chips = _place()
        sibling = (x, y, 1 - c)

        def window(i, kx, ky, half):
            return _half(_shard(outs[i].at[pl.ds(lyrs[i], 1)], layouts[i], 2 * kx + ky), 1, half, half_rows[i])

        first, passed = [], []
        for i in range(n):
            mine = window(i, x, y, c)
            for j, chip in enumerate(chips):
                cp = pltpu.make_async_remote_copy(
                    src_ref=mine, dst_ref=mine, send_sem=send_sems.at[i, j], recv_sem=recv_sems.at[i, j],
                    device_id=(*chip, c), device_id_type=MESH)
                cp.start()
                first.append(cp)
        for i in range(n):
            for j, chip in enumerate(chips):
                got = window(i, *chip, c)
                pltpu.make_async_remote_copy(
                    src_ref=got, dst_ref=got, send_sem=send_sems.at[i, j], recv_sem=recv_sems.at[i, j],
                    device_id=(*chip, c), device_id_type=MESH).wait_recv()
                cp = pltpu.make_async_remote_copy(
                    src_ref=got, dst_ref=got, send_sem=send_sems.at[i, 3 + j], recv_sem=recv_sems.at[i, 3 + j],
                    device_id=sibling, device_id_type=MESH)
                cp.start()
                passed.append(cp)
        for i in range(n):
            for j, chip in enumerate(chips):
                got = window(i, *chip, 1 - c)
                pltpu.make_async_remote_copy(
                    src_ref=got, dst_ref=got, send_sem=send_sems.at[i, 3 + j], recv_sem=recv_sems.at[i, 3 + j],
                    device_id=sibling, device_id_type=MESH).wait_recv()
        for cp in first + passed:
            cp.wait_send()

    return pl.pallas_call(
        body, name="gather_weights", in_specs=[ANY] * n, out_specs=[ANY] * n,
        out_shape=[jax.ShapeDtypeStruct(f.shape, f.dtype) for f in fulls],
        input_output_aliases={i: i for i in range(n)},
        scratch_shapes=[pltpu.SemaphoreType.DMA((n, 6)), pltpu.SemaphoreType.DMA((n, 6))],
        compiler_params=_params(),
    )(*fulls)


def _half_shape(shape, layout):
    s = list(shape)
    s[2 if layout == "row" else 1] //= 2
    return tuple(s)


def _swap_carry(grads, layouts):
    n = len(grads)
    row_axis = [2 if lay == "row" else 1 for lay in layouts]
    half_rows = [g.shape[ra] // 2 for g, ra in zip(grads, row_axis)]

    def copies(ins, outs, send_sems, recv_sems, off=0):
        x, y, c, _ = _place()
        return [pltpu.make_async_remote_copy(
            src_ref=_half(ins[i], row_axis[i], 1 - c, half_rows[i]), dst_ref=outs[i], send_sem=send_sems.at[off + i, 0],
            recv_sem=recv_sems.at[off + i, 0], device_id=(x, y, 1 - c), device_id_type=MESH) for i in range(n)]

    return dict(ins=list(grads), sems=(n, 3), copies=copies,
                out_shape=[jax.ShapeDtypeStruct(_half_shape(g.shape, lay), g.dtype) for g, lay in zip(grads, layouts)])


def _merge_carries(a, b):
    if a is None or b is None:
        return a if b is None else b
    na, ia, oa = a["sems"][0], len(a["ins"]), len(a["out_shape"])

    def copies(ins, outs, send_sems, recv_sems, off=0):
        return (a["copies"](ins[:ia], outs[:oa], send_sems, recv_sems, off)
                + b["copies"](ins[ia:], outs[oa:], send_sems, recv_sems, off + na))

    alias = dict(a.get("alias", {}))
    alias.update({ia + i: oa + o for i, o in b.get("alias", {}).items()})
    return dict(ins=a["ins"] + b["ins"], out_shape=a["out_shape"] + b["out_shape"], sems=(na + b["sems"][0], 3),
                copies=copies, alias=alias)


def _add_half(name, g, r, layout, c_idx):
    L = g.shape[0]
    if layout == "row":
        A, rows, W = L * N_CHIPS, g.shape[2] // 2, g.shape[3]
    else:
        A, rows, W = L, g.shape[1] // 2, g.shape[2]
    g3 = g.reshape(A, 2 * rows, W)
    r3 = r.reshape(A, rows, W)
    tr = _pick(rows, (256, 128, 64, 32, 16))
    nb = rows // tr

    def body(c_ref, g_ref, r_ref, o_ref):
        o_ref[...] = (g_ref[...].astype(F32) + r_ref[...].astype(F32)).astype(o_ref.dtype)

    out = pl.pallas_call(
        body, name=name,
        grid_spec=pltpu.PrefetchScalarGridSpec(
            num_scalar_prefetch=1, grid=(A, nb),
            in_specs=[pl.BlockSpec((None, tr, W), lambda a, i, c: (a, c[0] * nb + i, 0)),
                      pl.BlockSpec((None, tr, W), lambda a, i, c: (a, i, 0))],
            out_specs=pl.BlockSpec((None, tr, W), lambda a, i, c: (a, i, 0))),
        out_shape=jax.ShapeDtypeStruct((A, rows, W), WIRE_DTYPE),
        compiler_params=_params(),
    )(c_idx, g3, r3)
    return out.reshape(r.shape)


def _exchange_carry(parts, layouts):
    n = len(parts)

    def shard_half_shape(p, lay):
        if lay == "row":
            return (p.shape[0],) + p.shape[2:]
        return (p.shape[0], p.shape[1], p.shape[2] // N_CHIPS)

    def copies(ins, outs, send_sems, recv_sems, off=0):
        x, y, c, chips = _place()
        return [pltpu.make_async_remote_copy(
            src_ref=_shard(ins[i], layouts[i], 2 * kx + ky), dst_ref=outs[i].at[j], send_sem=send_sems.at[off + i, j],
            recv_sem=recv_sems.at[off + i, j], device_id=(kx, ky, c), device_id_type=MESH)
            for i in range(n) for j, (kx, ky) in enumerate(chips)]

    return dict(ins=list(parts), sems=(n, 3), copies=copies,
                out_shape=[jax.ShapeDtypeStruct((3,) + shard_half_shape(p, lay), p.dtype)
                           for p, lay in zip(parts, layouts)])


def _sum_shards(name, p, r, layout, kc_idx, dst, lyr, n_lyr):
    rows, W = r.shape[2], r.shape[3]
    tr = _pick(rows, (256, 128, 64, 32, 16))
    nb = rows // tr

    def body(kc_ref, p_ref, r_ref, *rest):
        acc = p_ref[...].astype(F32)
        for j in range(3):
            acc = acc + r_ref[j].astype(F32)
        rest[-1][...] = acc

    if layout == "row":
        p_spec = pl.BlockSpec((None, None, tr, W), lambda a, i, kc: (0, kc[0], i, 0))
    else:
        p_spec = pl.BlockSpec((None, tr, W), lambda a, i, kc: (0, i, kc[0]))
    in_specs = [p_spec, pl.BlockSpec((3, None, tr, W), lambda a, i, kc: (0, 0, i, 0))]
    args = [kc_idx, p, r]
    if dst is not None:
        in_specs.append(ANY)
        args.append(dst)
    return pl.pallas_call(
        body, name=name,
        grid_spec=pltpu.PrefetchScalarGridSpec(
            num_scalar_prefetch=1, grid=(1, nb), in_specs=in_specs,
            out_specs=pl.BlockSpec((None, tr, W), lambda a, i, kc: (lyr, kc[1] * nb + i, 0))),
        out_shape=jax.ShapeDtypeStruct((n_lyr, 2 * rows, W), F32),
        input_output_aliases={3: 0} if dst is not None else {},
        compiler_params=_params(),
    )(*args)


def _join_halves(shards):
    n = len(shards)

    def body(*refs):
        outs = refs[n:2 * n]
        send_sems, recv_sems = refs[2 * n:]
        x, y, c, _ = _place()
        cps = []
        for i in range(n):
            mine = _half(outs[i], 1, c, outs[i].shape[1] // 2)
            cp = pltpu.make_async_remote_copy(
                src_ref=mine, dst_ref=mine, send_sem=send_sems.at[i], recv_sem=recv_sems.at[i],
                device_id=(x, y, 1 - c), device_id_type=MESH)
            cp.start()
            cps.append(cp)
        for cp in cps:
            cp.wait()

    return pl.pallas_call(
        body, name="join_halves", in_specs=[ANY] * n, out_specs=[ANY] * n,
        out_shape=[jax.ShapeDtypeStruct(s.shape, s.dtype) for s in shards],
        input_output_aliases={i: i for i in range(n)},
        scratch_shapes=[pltpu.SemaphoreType.DMA((n,)), pltpu.SemaphoreType.DMA((n,))],
        compiler_params=_params(),
    )(*shards)


def _ada_fwd(c_all, ada_w, ada_b):
    L, D, w = ada_w.shape
    tn = _pick(w, (512, 256, 128))

    def body(c_ref, w_ref, b_ref, o_ref, a_ref):
        c = c_ref[...]
        act = (c * _sigmoid(c)).astype(MXU_DTYPE)
        a_ref[...] = act
        o_ref[...] = jnp.dot(act, w_ref[...].astype(MXU_DTYPE), preferred_element_type=F32) + b_ref[...]

    return pl.pallas_call(
        body, name="ada_fwd", grid=(L, w // tn),
        in_specs=[pl.BlockSpec((16, D), lambda l, j: (0, 0)), pl.BlockSpec((None, D, tn), lambda l, j: (l, 0, j)),
                  pl.BlockSpec((None, 1, tn), lambda l, j: (l, 0, j))],
        out_specs=[pl.BlockSpec((None, 16, tn), lambda l, j: (l, 0, j)), pl.BlockSpec((16, D), lambda l, j: (0, 0))],
        out_shape=[jax.ShapeDtypeStruct((L, 16, w), F32), jax.ShapeDtypeStruct((16, D), MXU_DTYPE)],
        compiler_params=_params(),
    )(c_all, ada_w, ada_b)


def _sum_devices(name, parts):
    n, R, W = parts.shape
    tw = _pick(W, (2048, 1024, 512, 256, 128))

    def body(p_ref, o_ref):
        acc = p_ref[0]
        for d in range(1, n):
            acc = acc + p_ref[d]
        o_ref[...] = acc

    return pl.pallas_call(
        body, name=name, grid=(W // tw,), in_specs=[pl.BlockSpec((n, R, tw), lambda j: (0, 0, j))],
        out_specs=pl.BlockSpec((R, tw), lambda j: (0, j)), out_shape=jax.ShapeDtypeStruct((R, W), F32),
        compiler_params=_params(),
    )(parts)


def _adamw_math(w, g, m, v):
    m2 = ADAM_B1 * m + (1.0 - ADAM_B1) * g
    v2 = ADAM_B2 * v + (1.0 - ADAM_B2) * (g * g)
    m_hat = m2 / (1.0 - ADAM_B1 ** ADAM_STEP)
    v_hat = v2 / (1.0 - ADAM_B2 ** ADAM_STEP)
    return -ADAM_LR * (m_hat / (jnp.sqrt(v_hat) + ADAM_EPS) + ADAM_WD * w), m2, v2


def _adamw(name, w, g, m, v):
    W = w.shape[1]
    tr = 256 if W <= 1024 else (128 if W <= 2048 else 64)
    return _rowwise(name, lambda w, g, m, v: (g,) + _adamw_math(w, g, m, v), [w, g, m, v], [], [(W, F32)] * 4, tr=tr)


def _adamw_ada(c_act, dmod, w, m, v, carry=None):
    L, D, wd = w.shape
    tr = LANE

    def body(*refs):
        refs, c_in, c_out, c_sems = _carry_split(refs, 5, 4, 0, carry)
        c_ref, d_ref, w_ref, m_ref, v_ref, g_ref, dl_ref, m2_ref, v2_ref = refs
        pl_, pi = pl.program_id(0), pl.program_id(1)
        _carry_run(carry, c_in, c_out, c_sems, jnp.logical_and(pl_ == 0, pi == 0), True)
        g = _dot_tn(c_ref[...], d_ref[...].astype(MXU_DTYPE))
        g_ref[...] = g
        dl_ref[...], m2_ref[...], v2_ref[...] = _adamw_math(w_ref[...], g, m_ref[...], v_ref[...])
        _carry_run(carry, c_in, c_out, c_sems, jnp.logical_and(pl_ == L - 1, pi == D // tr - 1), False)

    big = pl.BlockSpec((None, tr, wd), lambda l, i: (l, i, 0))
    c_is, c_os, c_shape, c_sems, c_alias = _carry_call_args(carry, 5, 4)
    res = pl.pallas_call(
        body, name="adamw_ada_w", grid=(L, D // tr),
        in_specs=[pl.BlockSpec((16, tr), lambda l, i: (0, i)), pl.BlockSpec((None, 16, wd), lambda l, i: (l, 0, 0)),
                  big, big, big] + c_is,
        out_specs=[big] * 4 + c_os, out_shape=[jax.ShapeDtypeStruct(w.shape, F32)] * 4 + c_shape,
        input_output_aliases=c_alias, scratch_shapes=c_sems,
        compiler_params=_params(),
    )(c_act, dmod, w, m, v, *(carry["ins"] if carry else ()))
    return res[:4], list(res[4:])


def _flat(a):
    return a.reshape(-1, a.shape[-1])


BIG = ("ffn_w1", "ffn_w3", "ffn_w2", "mla_w_down", "mla_w_uq", "mla_w_uk", "mla_w_uv", "mla_w_o", "fox_w_in",
       "fox_w_o")
LAYOUT = dict(ffn_w1="col", ffn_w3="col", ffn_w2="row", mla_w_down="row", mla_w_uq="col", mla_w_uk="col",
              mla_w_uv="col", mla_w_o="row", fox_w_in="row", fox_w_o="row")
FFN = ("ffn_w1", "ffn_w3", "ffn_w2")
MLA = ("mla_w_down", "mla_w_uq", "mla_w_uk", "mla_w_uv", "mla_w_o")
FOX = ("fox_w_in", "fox_w_o")
SMALL = ("ln1_g", "ln1_b", "ln2_g", "ln2_b", "mla_q_norm", "mla_kv_norm", "fox_b_f")
WEIGHTS = ("ada_w", "ada_b", "ln1_g", "ln1_b", "ln2_g", "ln2_b", "ffn_w1", "ffn_w3", "ffn_w2", "mla_w_down",
           "mla_q_norm", "mla_w_uq", "mla_kv_norm", "mla_w_uk", "mla_w_uv", "mla_w_o", "fox_w_in", "fox_b_f",
           "fox_w_o")


def _uq_perm(H):
    d = HEAD_DIM + ROPE_DIM
    nope = (np.arange(H)[:, None] * d + np.arange(HEAD_DIM)[None, :]).reshape(-1)
    pe = (np.arange(H)[:, None] * d + HEAD_DIM + np.arange(ROPE_DIM)[None, :]).reshape(-1)
    return np.concatenate([nope, pe])


def kernel(x, c, positions, ada_w, ada_b, ln1_g, ln1_b, ln2_g, ln2_b, ffn_w1, ffn_w3, ffn_w2, mla_w_down, mla_q_norm, mla_w_uq, mla_kv_norm, mla_w_uk, mla_w_uv, mla_w_o, fox_w_in, fox_b_f, fox_w_o, loss_target, m_ada_w, m_ada_b, m_ln1_g, m_ln1_b, m_ln2_g, m_ln2_b, m_ffn_w1, m_ffn_w3, m_ffn_w2, m_mla_w_down, m_mla_q_norm, m_mla_w_uq, m_mla_kv_norm, m_mla_w_uk, m_mla_w_uv, m_mla_w_o, m_fox_w_in, m_fox_b_f, m_fox_w_o, v_ada_w, v_ada_b, v_ln1_g, v_ln1_b, v_ln2_g, v_ln2_b, v_ffn_w1, v_ffn_w3, v_ffn_w2, v_mla_w_down, v_mla_q_norm, v_mla_w_uq, v_mla_kv_norm, v_mla_w_uk, v_mla_w_uv, v_mla_w_o, v_fox_w_in, v_fox_b_f, v_fox_w_o):
    W = dict(ada_w=ada_w, ada_b=ada_b, ln1_g=ln1_g, ln1_b=ln1_b, ln2_g=ln2_g, ln2_b=ln2_b, ffn_w1=ffn_w1,
             ffn_w3=ffn_w3, ffn_w2=ffn_w2, mla_w_down=mla_w_down, mla_q_norm=mla_q_norm, mla_w_uq=mla_w_uq,
             mla_kv_norm=mla_kv_norm, mla_w_uk=mla_w_uk, mla_w_uv=mla_w_uv, mla_w_o=mla_w_o, fox_w_in=fox_w_in,
             fox_b_f=fox_b_f, fox_w_o=fox_w_o)
    Mo = dict(ada_w=m_ada_w, ada_b=m_ada_b, ln1_g=m_ln1_g, ln1_b=m_ln1_b, ln2_g=m_ln2_g, ln2_b=m_ln2_b,
              ffn_w1=m_ffn_w1, ffn_w3=m_ffn_w3, ffn_w2=m_ffn_w2, mla_w_down=m_mla_w_down, mla_q_norm=m_mla_q_norm,
              mla_w_uq=m_mla_w_uq, mla_kv_norm=m_mla_kv_norm, mla_w_uk=m_mla_w_uk, mla_w_uv=m_mla_w_uv,
              mla_w_o=m_mla_w_o, fox_w_in=m_fox_w_in, fox_b_f=m_fox_b_f, fox_w_o=m_fox_w_o)
    Vo = dict(ada_w=v_ada_w, ada_b=v_ada_b, ln1_g=v_ln1_g, ln1_b=v_ln1_b, ln2_g=v_ln2_g, ln2_b=v_ln2_b,
              ffn_w1=v_ffn_w1, ffn_w3=v_ffn_w3, ffn_w2=v_ffn_w2, mla_w_down=v_mla_w_down, mla_q_norm=v_mla_q_norm,
              mla_w_uq=v_mla_w_uq, mla_kv_norm=v_mla_kv_norm, mla_w_uk=v_mla_w_uk, mla_w_uv=v_mla_w_uv,
              mla_w_o=v_mla_w_o, fox_w_in=v_fox_w_in, fox_b_f=v_fox_b_f, fox_w_o=v_fox_w_o)

    S, D = x.shape[1], x.shape[2]
    L = ada_w.shape[0]
    alpha = float((2 * L) ** 0.25)
    H_mla = mla_w_uk.shape[2] * N_CHIPS // HEAD_DIM
    H_fox = D // HEAD_DIM
    xi, yi, ci = lax.axis_index("x"), lax.axis_index("y"), lax.axis_index("c")
    chip = 2 * xi + yi
    dev = 2 * chip + ci
    c_idx = jnp.reshape(ci, (1,)).astype(jnp.int32)
    x0, tgt = x[0], loss_target[0]
    pos = positions.reshape(S, 1)

    c_all = _ag_small("gather_c", jnp.pad(c, ((0, 7), (0, 0)))).reshape(N_DEV, 8, D)[:, 0]
    w_ada = ada_w.shape[2]
    ada_b_sh = lax.dynamic_slice_in_dim(ada_b, chip * w_ada, w_ada, axis=1).reshape(L, 1, w_ada)
    mod_sh, c_act = _ada_fwd(jnp.pad(c_all, ((0, 8), (0, 0))), ada_w, ada_b_sh)
    mod_all = _ag_small("gather_mod", mod_sh.transpose(1, 0, 2).reshape(16, L * w_ada))
    mod_all = mod_all.reshape(N_CHIPS, 2, 16, L, w_ada)[:, 0]
    mod = lax.dynamic_index_in_dim(mod_all, dev, axis=1, keepdims=False)
    mod = mod.transpose(1, 0, 2).reshape(L, 6, 1, D)

    kc_idx = jnp.stack([chip, ci]).astype(jnp.int32)
    raw = {n: [_cast_full("cast_" + n, W[n], LAYOUT[n], kc_idx, l) for l in range(W[n].shape[0])] for n in BIG}
    full = {n: [None] * W[n].shape[0] for n in BIG}
    perm = _uq_perm(H_mla)
    n_in = fox_w_in.shape[2] * N_CHIPS

    def group(i):
        return FFN + (MLA if i % 2 == 0 else FOX)

    def ride_of(names, i):
        return dict(bufs=[raw[n][i if n in FFN else i // 2] for n in names], lays=[LAYOUT[n] for n in names])

    def land(names, i, bufs):
        for n, f in zip(names, bufs):
            if n == "fox_w_in":
                fw = f.reshape(N_CHIPS, D, -1).transpose(1, 0, 2).reshape(D, n_in)
                f = jnp.pad(fw, ((0, 0), (0, 3 * D + LANE - n_in)))[None]
            elif n == "mla_w_uq":
                f = f[:, :, perm]
            elif LAYOUT[n] == "row":
                f = f.reshape(1, f.shape[1] * f.shape[2], f.shape[3])
            full[n][i if n in FFN else i // 2] = f

    r0 = ride_of(group(0)[3:], 0)
    land(group(0)[3:], 0, _gather_weights(r0["bufs"], r0["lays"], [0] * len(r0["bufs"])))

    rc = tuple(_rope_tables(pos))

    def mixer_w(i):
        j = i // 2
        if i % 2 == 0:
            return dict(w_down=(full["mla_w_down"][j], 0), q_norm=mla_q_norm[j:j + 1], w_uq=(full["mla_w_uq"][j], 0),
                        kv_norm=mla_kv_norm[j:j + 1], w_uk=(full["mla_w_uk"][j], 0), w_uv=(full["mla_w_uv"][j], 0),
                        w_o=(full["mla_w_o"][j], 0))
        return dict(w_in=(full["fox_w_in"][j], 0), b_f=jnp.pad(fox_b_f[j:j + 1], ((0, 0), (0, LANE - H_fox))),
                    w_o=(full["fox_w_o"][j], 0))

    saved = []
    xc = x0
    h = _modulate(x0, mod[0, 1], mod[0, 0])
    for i in range(L):
        mw = mixer_w(i)
        nxt = i + 1 < L
        mix_n = group(i + 1)[3:]
        ride = dict(ride_of(FFN, i), n_ici=3 if i == 0 else 2)
        if i % 2 == 0:
            y1, ms, got = _mla_fwd(h, mw, rc, ride)
        else:
            y1, ms, got = _fox_fwd(h, mw, ride)
        land(FFN, i, got)
        z1, x1, h2 = _resid_ln_mod(xc, y1, mod[i, 2], ln1_g[i:i + 1], ln1_b[i:i + 1], mod[i, 4], mod[i, 3], alpha)
        y2, fs, got, early = _ffn_fwd(h2, (full["ffn_w1"][i], 0), (full["ffn_w3"][i], 0), (full["ffn_w2"][i], 0),
                                      ride_of(mix_n, i + 1) if nxt else None, ride_of(FFN[2:], i + 1) if nxt else None)
        land(mix_n, i + 1, got)
        if nxt:
            raw["ffn_w2"][i + 1] = early[0]
        rec = dict(h1=h, ms=ms, y1=y1, z1=z1, h2=h2, fs=fs, y2=y2)
        if i + 1 < L:
            z2, xc, h = _resid_ln_mod(x1, y2, mod[i, 5], ln2_g[i:i + 1], ln2_b[i:i + 1], mod[i + 1, 1],
                                      mod[i + 1, 0], alpha)
            rec["z2"] = z2
        else:
            dx_res, dy, loss_v, dlg, dlb, dgate = _final_ln_loss(x1, y2, tgt, mod[i, 5], ln2_g[i:i + 1],
                                                                 ln2_b[i:i + 1], alpha)
        saved.append(rec)
    loss = lax.psum(loss_v[0, 0] * (0.5 / D), ("x", "y", "c"))

    G = {n: [None] * W[n].shape[0] for n in SMALL}
    dmod = [[None] * 6 for _ in range(L)]
    red = {n: None for n in BIG}
    inv_perm = np.argsort(perm)

    def rs_view(n, g):
        if n == "mla_w_uq":
            g = g[:, inv_perm]
        if n == "fox_w_in":
            g = g[:, :n_in].reshape(D, N_CHIPS, n_in // N_CHIPS).transpose(1, 0, 2)
        elif LAYOUT[n] == "row":
            g = g.reshape(N_CHIPS, g.shape[0] // N_CHIPS, g.shape[1])
        return g[None]

    def add_halves(names, gs, recv):
        return [_add_half("add_half_" + n, g, r, LAYOUT[n], c_idx) for n, g, r in zip(names, gs, recv)]

    def finish(names, parts, recv2, lyr):
        for n, p, r in zip(names, parts, recv2):
            red[n] = _sum_shards("sum_shards_" + n, p, r, LAYOUT[n], kc_idx, red[n], lyr, W[n].shape[0])

    queue = []

    def carry_of(item):
        return _exchange_carry(item[1], [LAYOUT[n] for n in item[0]]) if item else None

    for i in reversed(range(L)):
        rec = saved[i]
        mw = mixer_w(i)
        j = i // 2
        G["ln2_g"][i], G["ln2_b"][i], dmod[i][5] = dlg, dlb, dgate
        ride_du = queue.pop(0) if queue else None
        ride_dh = queue.pop(0) if queue else None
        views = []

        def swap_of(grads):
            views.extend(rs_view(n, g) for n, g in zip(FFN, grads))
            return _swap_carry(views, [LAYOUT[n] for n in FFN])

        dh2, dw1, dw3, dw2, got_du, got_dh, got_swap = _ffn_bwd(
            dy, rec["h2"], rec["fs"], (full["ffn_w1"][i], 0), (full["ffn_w3"][i], 0), (full["ffn_w2"][i], 0),
            carry_of(ride_du), carry_of(ride_dh), swap_of)
        for item, got in ((ride_du, got_du), (ride_dh, got_dh)):
            if item:
                finish(item[0], item[1], got, item[2])
        ffn_parts = add_halves(FFN, views, got_swap)
        dx_res, dy, dmod[i][4], dmod[i][3], G["ln1_g"][i], G["ln1_b"][i], dmod[i][2] = _bwd_boundary(
            dx_res, dh2, rec["z1"], rec["y1"], mod[i, 4], mod[i, 2], ln1_g[i:i + 1], ln1_b[i:i + 1], alpha)
        n_at = 2 if i > 0 else 3
        ride_at = (FFN[:n_at], ffn_parts[:n_at], i)
        if i % 2 == 0:
            dh1, gm, recv2 = _mla_bwd(dy, rec["h1"], rec["ms"], mw, rc, carry_of(ride_at))
            names, pre = MLA, "mla_"
            G["mla_q_norm"][j], G["mla_kv_norm"][j] = gm["q_norm"], gm["kv_norm"]
        else:
            dh1, gm, recv2 = _fox_bwd(dy, rec["h1"], rec["ms"], mw, carry_of(ride_at))
            names, pre = FOX, "fox_"
            G["fox_b_f"][j] = gm["b_f"]
        finish(ride_at[0], ride_at[1], recv2, i)
        if n_at < 3:
            queue.append((FFN[2:], ffn_parts[2:], i))
        gs = [rs_view(n, gm[n[len(pre):]]) for n in names]
        swap = _swap_carry(gs, [LAYOUT[n] for n in names])
        if i > 0:
            p = saved[i - 1]
            (dx_res, dy, dmod[i][1], dmod[i][0], dlg, dlb, dgate), recv = _bwd_boundary(
                dx_res, dh1, p["z2"], p["y2"], mod[i, 1], mod[i - 1, 5], ln2_g[i - 1:i], ln2_b[i - 1:i], alpha, swap)
        else:
            (grad_x, dmod[i][1], dmod[i][0]), recv = _first_bwd(dx_res, dh1, x0, mod[i, 1], swap)
        queue.append((names, add_halves(names, gs, recv), j))

    small = jnp.concatenate([jnp.concatenate([g.reshape(-1) for g in G[n]]) for n in SMALL])
    dmod_v = jnp.concatenate([jnp.concatenate([d.reshape(-1) for d in row]) for row in dmod])
    n_small, n_dmod = small.shape[0], dmod_v.shape[0]
    wblk = -(-(n_small + n_dmod) // (8 * LANE)) * LANE
    blk = jnp.pad(jnp.concatenate([dmod_v, small]), (0, 8 * wblk - n_small - n_dmod)).reshape(8, wblk)
    parts = _ag_small("gather_small", blk).reshape(N_DEV, 8, wblk)
    tot = _sum_devices("sum_small", parts).reshape(-1)
    g_ada_b = tot[:n_dmod].reshape(L, 6 * D)
    off = n_dmod
    Gs = {}
    for n in SMALL:
        Gs[n] = tot[off:off + W[n].size].reshape(W[n].shape)
        off += W[n].size
    dmod_all = parts.reshape(N_DEV, 8 * wblk)[:, :n_dmod].reshape(N_DEV, L, N_CHIPS, w_ada)
    dmod_sh = lax.dynamic_index_in_dim(dmod_all, chip, axis=2, keepdims=False)
    dmod_sh = jnp.pad(dmod_sh, ((0, 8), (0, 0), (0, 0)))
    dmod_sh = dmod_sh.transpose(1, 0, 2)

    last = (sum((q[0] for q in queue), ()), sum((q[1] for q in queue), []))
    got = _run_carry("exchange_last", _exchange_carry(last[1], [LAYOUT[n] for n in last[0]]))
    (g_ada_w, d_ada_w, m_ada_w2, v_ada_w2), _ = _adamw_ada(c_act, dmod_sh, ada_w, m_ada_w, v_ada_w)
    for q in queue:
        finish(q[0], q[1], got[:len(q[0])], q[2])
        got = got[len(q[0]):]
    Gb = dict(zip(BIG, _join_halves([red[n] for n in BIG])))
    grads = dict(Gb)
    grads.update(Gs)
    grads["ada_b"] = g_ada_b
    delta, new_m, new_v = {}, {}, {}
    grads["ada_w"], delta["ada_w"], new_m["ada_w"], new_v["ada_w"] = g_ada_w, d_ada_w, m_ada_w2, v_ada_w2
    for n in WEIGHTS:
        if n in SMALL or n in ("ada_b", "ada_w"):
            continue
        shp = W[n].shape
        grads[n], delta[n], new_m[n], new_v[n] = [r.reshape(shp) for r in _adamw(
            "adamw_" + n, _flat(W[n]), _flat(grads[n]), _flat(Mo[n]), _flat(Vo[n]))]
    names_s = SMALL + ("ada_b",)
    cat = lambda d: jnp.concatenate([d[n].reshape(-1) for n in names_s])
    n_s = sum(W[n].size for n in names_s)
    ws = -(-n_s // (8 * LANE)) * LANE
    pk = lambda d: jnp.pad(cat(d), (0, 8 * ws - n_s)).reshape(8, ws)
    _, ds, ms_, vs = _adamw("adamw_small", pk(W), pk(grads), pk(Mo), pk(Vo))
    off = 0
    for n in names_s:
        sz, shp = W[n].size, W[n].shape
        delta[n] = ds.reshape(-1)[off:off + sz].reshape(shp)
        new_m[n] = ms_.reshape(-1)[off:off + sz].reshape(shp)
        new_v[n] = vs.reshape(-1)[off:off + sz].reshape(shp)
        off += sz

    return (loss, grad_x[None], *[grads[n].reshape(W[n].shape) for n in WEIGHTS], *[delta[n] for n in WEIGHTS],
            *[new_m[n] for n in WEIGHTS], *[new_v[n] for n in WEIGHTS])
```
